```python
import math
import jax, jax.numpy as jnp
from jax import lax
import numpy as np

D_MODEL = 1024
BATCH = 16
SEQ = 4096
DEPTH = 4

ATTN_HEADS = 8
ATTN_HEAD_DIM = 64
ATTN_WIDTH = ATTN_HEADS * ATTN_HEAD_DIM
Q_BLOCK = 128
SSM_GROUPS = 32
SSM_GROUP_CH = 16
SSM_WIDTH = SSM_GROUPS * SSM_GROUP_CH
SSM_STATE = 64
D_FF = 4 * D_MODEL
N_IN = 3 * ATTN_WIDTH + ATTN_HEADS + SSM_WIDTH + 2 * D_MODEL
RMS_EPS = 1e-6
DT_MIN = 1e-3
DT_MAX = 1e-1

kernel_name = 'fox_s5_gated_hybrid_trunk'


def rmsnorm(x, g):
    xf = x.astype(jnp.float32)
    xf = xf * lax.rsqrt(jnp.mean(xf * xf, axis=-1, keepdims=True) + RMS_EPS)
    return (xf * g.astype(jnp.float32)).astype(x.dtype)


def forgetting_attention(q, k, v, log_f):
    seq = q.shape[2]
    scale = ATTN_HEAD_DIM ** -0.5
    cum = jnp.cumsum(log_f, axis=-1)
    outs = []
    for i in range(seq // Q_BLOCK):
        lo, hi = i * Q_BLOCK, (i + 1) * Q_BLOCK
        s = jnp.einsum('bhqd,bhkd->bhqk', q[:, :, lo:hi], k[:, :, :hi]).astype(jnp.float32) * scale
        s = s + cum[:, :, lo:hi, None] - cum[:, :, None, :hi]
        causal = (lo + jnp.arange(Q_BLOCK))[:, None] >= jnp.arange(hi)[None, :]
        p = jax.nn.softmax(jnp.where(causal, s, -jnp.inf), axis=-1)
        outs.append(jnp.einsum('bhqk,bhkd->bhqd', p.astype(v.dtype), v[:, :, :hi]))
    return jnp.concatenate(outs, axis=2)


def _linear_recurrence(e1, e2):
    a1, b1 = e1
    a2, b2 = e2
    return a1 * a2, a2 * b1 + b2


def s5_ssm(u, lam_re, lam_im, log_dt, b_re, b_im, c_re, c_im, d_skip):
    bsz, seq, _ = u.shape
    f32 = jnp.float32
    ug = u.astype(f32).reshape(bsz, seq, SSM_GROUPS, SSM_GROUP_CH)
    lam = lax.complex(lam_re.astype(f32), lam_im.astype(f32))
    dt = jnp.exp(log_dt.astype(f32))[:, None]
    lam_bar = jnp.exp(lam * dt)
    b_mat = lax.complex(b_re.astype(f32), b_im.astype(f32))
    b_bar = ((lam_bar - 1.0) / lam)[:, :, None] * b_mat
    bu = jnp.einsum('bsgc,gpc->bsgp', ug.astype(jnp.complex64), b_bar)
    a = jnp.broadcast_to(lam_bar[None, None], (1, seq, SSM_GROUPS, SSM_STATE))
    _, states = lax.associative_scan(_linear_recurrence, (a, bu), axis=1)
    c_mat = lax.complex(c_re.astype(f32), c_im.astype(f32))
    y = jnp.einsum('bsgp,gcp->bsgc', states, c_mat).real
    y = y + d_skip.astype(f32).reshape(SSM_GROUPS, SSM_GROUP_CH) * ug
    return y.reshape(bsz, seq, SSM_WIDTH).astype(u.dtype)


def hybrid_layer(x, norm_mix, w_in, b_forget, lam_re, lam_im, log_dt, b_re, b_im,
                 c_re, c_im, d_skip, w_glu, b_glu, w_branch_a, w_branch_b, w_out,
                 norm_mlp, w_mlp_up, w_mlp_down):
    bsz, seq, _ = x.shape
    h = rmsnorm(x, norm_mix)
    proj = h @ w_in
    o1 = ATTN_WIDTH
    o2 = o1 + ATTN_WIDTH
    o3 = o2 + ATTN_WIDTH
    o4 = o3 + ATTN_HEADS
    o5 = o4 + SSM_WIDTH
    o6 = o5 + D_MODEL
    q, k, v, f_logit, u, gate_a, gate_b = jnp.split(proj, [o1, o2, o3, o4, o5, o6], axis=-1)

    def heads(t):
        return t.reshape(bsz, seq, ATTN_HEADS, ATTN_HEAD_DIM).transpose(0, 2, 1, 3)
    log_f = jax.nn.log_sigmoid((f_logit + b_forget).astype(jnp.float32)).transpose(0, 2, 1)
    y_a = forgetting_attention(heads(q), heads(k), heads(v), log_f)
    y_a = y_a.transpose(0, 2, 1, 3).reshape(bsz, seq, ATTN_WIDTH)

    y_b = jax.nn.gelu(s5_ssm(u, lam_re, lam_im, log_dt, b_re, b_im, c_re, c_im, d_skip))
    y_b = y_b * jax.nn.sigmoid(y_b @ w_glu + b_glu)

    mixed = jax.nn.sigmoid(gate_a) * (y_a @ w_branch_a) + jax.nn.sigmoid(gate_b) * (y_b @ w_branch_b)
    x = x + mixed @ w_out

    h = rmsnorm(x, norm_mlp)
    x = x + jnp.square(jax.nn.relu(h @ w_mlp_up)) @ w_mlp_down
    return x


def _fwd_setup_inputs(seed: int = 0) -> dict:
    key = jax.random.key(seed)
    ks = jax.random.split(key, 24)
    f32 = jnp.float32
    L, G, P, C = DEPTH, SSM_GROUPS, SSM_STATE, SSM_GROUP_CH

    def nrm(k, shape, scale):
        return jax.random.normal(k, shape, f32) * scale

    n_idx = jnp.arange(P, dtype=f32)
    return {
        'x': nrm(ks[0], (BATCH, SEQ, D_MODEL), 1.0),
        'norm_mix': 1.0 + nrm(ks[1], (L, D_MODEL), 0.02),
        'w_in': nrm(ks[2], (L, D_MODEL, N_IN), D_MODEL ** -0.5),
        'b_forget': jax.random.uniform(ks[3], (L, ATTN_HEADS), f32, 1.0, 5.0),
        'ssm_lambda_re': -0.5 + nrm(ks[4], (L, G, P), 0.01),
        'ssm_lambda_im': jnp.pi * n_idx + nrm(ks[5], (L, G, P), 0.01),
        'ssm_log_dt': jax.random.uniform(ks[6], (L, G), f32, math.log(DT_MIN), math.log(DT_MAX)),
        'ssm_b_re': nrm(ks[7], (L, G, P, C), (2 * C) ** -0.5),
        'ssm_b_im': nrm(ks[8], (L, G, P, C), (2 * C) ** -0.5),
        'ssm_c_re': nrm(ks[9], (L, G, C, P), P ** -0.5),
        'ssm_c_im': nrm(ks[10], (L, G, C, P), P ** -0.5),
        'ssm_d': nrm(ks[11], (L, SSM_WIDTH), 1.0),
        'w_glu': nrm(ks[12], (L, SSM_WIDTH, SSM_WIDTH), SSM_WIDTH ** -0.5),
        'b_glu': nrm(ks[13], (L, SSM_WIDTH), 0.01),
        'w_branch_a': nrm(ks[14], (L, ATTN_WIDTH, D_MODEL), ATTN_WIDTH ** -0.5),
        'w_branch_b': nrm(ks[15], (L, SSM_WIDTH, D_MODEL), SSM_WIDTH ** -0.5),
        'w_out': nrm(ks[16], (L, D_MODEL, D_MODEL), D_MODEL ** -0.5),
        'norm_mlp': 1.0 + nrm(ks[17], (L, D_MODEL), 0.02),
        'w_mlp_up': nrm(ks[18], (L, D_MODEL, D_FF), D_MODEL ** -0.5),
        'w_mlp_down': nrm(ks[19], (L, D_FF, D_MODEL), D_FF ** -0.5),
        'norm_final': 1.0 + nrm(ks[20], (D_MODEL,), 0.02),
    }


def _fwd_reference(x, norm_mix, w_in, b_forget, ssm_lambda_re, ssm_lambda_im, ssm_log_dt,
              ssm_b_re, ssm_b_im, ssm_c_re, ssm_c_im, ssm_d, w_glu, b_glu,
              w_branch_a, w_branch_b, w_out, norm_mlp, w_mlp_up, w_mlp_down, norm_final):
    for l in range(DEPTH):
        x = hybrid_layer(x, norm_mix[l], w_in[l], b_forget[l], ssm_lambda_re[l], ssm_lambda_im[l],
                         ssm_log_dt[l], ssm_b_re[l], ssm_b_im[l], ssm_c_re[l], ssm_c_im[l],
                         ssm_d[l], w_glu[l], b_glu[l], w_branch_a[l], w_branch_b[l], w_out[l],
                         norm_mlp[l], w_mlp_up[l], w_mlp_down[l])
    return rmsnorm(x, norm_final)


import jax as _jax
import jax.numpy as _jnp

TWIN_FORMAT = 'train_step'
FWD_PARAMS = ['x', 'norm_mix', 'w_in', 'b_forget', 'ssm_lambda_re', 'ssm_lambda_im', 'ssm_log_dt', 'ssm_b_re', 'ssm_b_im', 'ssm_c_re', 'ssm_c_im', 'ssm_d', 'w_glu', 'b_glu', 'w_branch_a', 'w_branch_b', 'w_out', 'norm_mlp', 'w_mlp_up', 'w_mlp_down', 'norm_final']
TWIN_WEIGHTS = ['norm_mix', 'w_in', 'b_forget', 'ssm_lambda_re', 'ssm_lambda_im', 'ssm_log_dt', 'ssm_b_re', 'ssm_b_im', 'ssm_c_re', 'ssm_c_im', 'ssm_d', 'w_glu', 'b_glu', 'w_branch_a', 'w_branch_b', 'w_out', 'norm_mlp', 'w_mlp_up', 'w_mlp_down', 'norm_final']
TWIN_DIFF_INPUT = 'x'
TWIN_INPUTS = ['x', 'norm_mix', 'w_in', 'b_forget', 'ssm_lambda_re', 'ssm_lambda_im', 'ssm_log_dt', 'ssm_b_re', 'ssm_b_im', 'ssm_c_re', 'ssm_c_im', 'ssm_d', 'w_glu', 'b_glu', 'w_branch_a', 'w_branch_b', 'w_out', 'norm_mlp', 'w_mlp_up', 'w_mlp_down', 'norm_final', 'loss_target', 'm_norm_mix', 'm_w_in', 'm_b_forget', 'm_ssm_lambda_re', 'm_ssm_lambda_im', 'm_ssm_log_dt', 'm_ssm_b_re', 'm_ssm_b_im', 'm_ssm_c_re', 'm_ssm_c_im', 'm_ssm_d', 'm_w_glu', 'm_b_glu', 'm_w_branch_a', 'm_w_branch_b', 'm_w_out', 'm_norm_mlp', 'm_w_mlp_up', 'm_w_mlp_down', 'm_norm_final', 'v_norm_mix', 'v_w_in', 'v_b_forget', 'v_ssm_lambda_re', 'v_ssm_lambda_im', 'v_ssm_log_dt', 'v_ssm_b_re', 'v_ssm_b_im', 'v_ssm_c_re', 'v_ssm_c_im', 'v_ssm_d', 'v_w_glu', 'v_b_glu', 'v_w_branch_a', 'v_w_branch_b', 'v_w_out', 'v_norm_mlp', 'v_w_mlp_up', 'v_w_mlp_down', 'v_norm_final']
TWIN_OUTPUTS = ['loss', 'grad_x', 'grad_norm_mix', 'grad_w_in', 'grad_b_forget', 'grad_ssm_lambda_re', 'grad_ssm_lambda_im', 'grad_ssm_log_dt', 'grad_ssm_b_re', 'grad_ssm_b_im', 'grad_ssm_c_re', 'grad_ssm_c_im', 'grad_ssm_d', 'grad_w_glu', 'grad_b_glu', 'grad_w_branch_a', 'grad_w_branch_b', 'grad_w_out', 'grad_norm_mlp', 'grad_w_mlp_up', 'grad_w_mlp_down', 'grad_norm_final', 'delta_norm_mix', 'delta_w_in', 'delta_b_forget', 'delta_ssm_lambda_re', 'delta_ssm_lambda_im', 'delta_ssm_log_dt', 'delta_ssm_b_re', 'delta_ssm_b_im', 'delta_ssm_c_re', 'delta_ssm_c_im', 'delta_ssm_d', 'delta_w_glu', 'delta_b_glu', 'delta_w_branch_a', 'delta_w_branch_b', 'delta_w_out', 'delta_norm_mlp', 'delta_w_mlp_up', 'delta_w_mlp_down', 'delta_norm_final', 'new_m_norm_mix', 'new_m_w_in', 'new_m_b_forget', 'new_m_ssm_lambda_re', 'new_m_ssm_lambda_im', 'new_m_ssm_log_dt', 'new_m_ssm_b_re', 'new_m_ssm_b_im', 'new_m_ssm_c_re', 'new_m_ssm_c_im', 'new_m_ssm_d', 'new_m_w_glu', 'new_m_b_glu', 'new_m_w_branch_a', 'new_m_w_branch_b', 'new_m_w_out', 'new_m_norm_mlp', 'new_m_w_mlp_up', 'new_m_w_mlp_down', 'new_m_norm_final', 'new_v_norm_mix', 'new_v_w_in', 'new_v_b_forget', 'new_v_ssm_lambda_re', 'new_v_ssm_lambda_im', 'new_v_ssm_log_dt', 'new_v_ssm_b_re', 'new_v_ssm_b_im', 'new_v_ssm_c_re', 'new_v_ssm_c_im', 'new_v_ssm_d', 'new_v_w_glu', 'new_v_b_glu', 'new_v_w_branch_a', 'new_v_w_branch_b', 'new_v_w_out', 'new_v_norm_mlp', 'new_v_w_mlp_up', 'new_v_w_mlp_down', 'new_v_norm_final']
TWIN_LEAF_KINDS = {'loss': 'loss', 'grad_x': 'grad_x', 'grad_norm_mix': 'grad_w', 'grad_w_in': 'grad_w', 'grad_b_forget': 'grad_w', 'grad_ssm_lambda_re': 'grad_w', 'grad_ssm_lambda_im': 'grad_w', 'grad_ssm_log_dt': 'grad_w', 'grad_ssm_b_re': 'grad_w', 'grad_ssm_b_im': 'grad_w', 'grad_ssm_c_re': 'grad_w', 'grad_ssm_c_im': 'grad_w', 'grad_ssm_d': 'grad_w', 'grad_w_glu': 'grad_w', 'grad_b_glu': 'grad_w', 'grad_w_branch_a': 'grad_w', 'grad_w_branch_b': 'grad_w', 'grad_w_out': 'grad_w', 'grad_norm_mlp': 'grad_w', 'grad_w_mlp_up': 'grad_w', 'grad_w_mlp_down': 'grad_w', 'grad_norm_final': 'grad_w', 'delta_norm_mix': 'delta_w', 'delta_w_in': 'delta_w', 'delta_b_forget': 'delta_w', 'delta_ssm_lambda_re': 'delta_w', 'delta_ssm_lambda_im': 'delta_w', 'delta_ssm_log_dt': 'delta_w', 'delta_ssm_b_re': 'delta_w', 'delta_ssm_b_im': 'delta_w', 'delta_ssm_c_re': 'delta_w', 'delta_ssm_c_im': 'delta_w', 'delta_ssm_d': 'delta_w', 'delta_w_glu': 'delta_w', 'delta_b_glu': 'delta_w', 'delta_w_branch_a': 'delta_w', 'delta_w_branch_b': 'delta_w', 'delta_w_out': 'delta_w', 'delta_norm_mlp': 'delta_w', 'delta_w_mlp_up': 'delta_w', 'delta_w_mlp_down': 'delta_w', 'delta_norm_final': 'delta_w', 'new_m_norm_mix': 'new_m', 'new_m_w_in': 'new_m', 'new_m_b_forget': 'new_m', 'new_m_ssm_lambda_re': 'new_m', 'new_m_ssm_lambda_im': 'new_m', 'new_m_ssm_log_dt': 'new_m', 'new_m_ssm_b_re': 'new_m', 'new_m_ssm_b_im': 'new_m', 'new_m_ssm_c_re': 'new_m', 'new_m_ssm_c_im': 'new_m', 'new_m_ssm_d': 'new_m', 'new_m_w_glu': 'new_m', 'new_m_b_glu': 'new_m', 'new_m_w_branch_a': 'new_m', 'new_m_w_branch_b': 'new_m', 'new_m_w_out': 'new_m', 'new_m_norm_mlp': 'new_m', 'new_m_w_mlp_up': 'new_m', 'new_m_w_mlp_down': 'new_m', 'new_m_norm_final': 'new_m', 'new_v_norm_mix': 'new_v', 'new_v_w_in': 'new_v', 'new_v_b_forget': 'new_v', 'new_v_ssm_lambda_re': 'new_v', 'new_v_ssm_lambda_im': 'new_v', 'new_v_ssm_log_dt': 'new_v', 'new_v_ssm_b_re': 'new_v', 'new_v_ssm_b_im': 'new_v', 'new_v_ssm_c_re': 'new_v', 'new_v_ssm_c_im': 'new_v', 'new_v_ssm_d': 'new_v', 'new_v_w_glu': 'new_v', 'new_v_b_glu': 'new_v', 'new_v_w_branch_a': 'new_v', 'new_v_w_branch_b': 'new_v', 'new_v_w_out': 'new_v', 'new_v_norm_mlp': 'new_v', 'new_v_w_mlp_up': 'new_v', 'new_v_w_mlp_down': 'new_v', 'new_v_norm_final': 'new_v'}


def _forward(args):
    return _fwd_reference(*[args[k] for k in FWD_PARAMS])


def _output_shape():
    out = _jax.eval_shape(lambda: _forward(_fwd_setup_inputs(0)))
    return out.shape, out.dtype

N_MICROBATCH = 1
ADAM_LR = 0.001
ADAM_B1 = 0.9
ADAM_B2 = 0.999
ADAM_EPS = 1e-08
ADAM_WD = 0.01
ADAM_STEP = 10
PER_EXAMPLE_BATCH_AXIS = {'x': 0, 'loss_target': 0}
SHARED_INPUTS = []
_WEIGHT_DTYPES = {'norm_mix': _jnp.float32, 'w_in': _jnp.float32, 'b_forget': _jnp.float32, 'ssm_lambda_re': _jnp.float32, 'ssm_lambda_im': _jnp.float32, 'ssm_log_dt': _jnp.float32, 'ssm_b_re': _jnp.float32, 'ssm_b_im': _jnp.float32, 'ssm_c_re': _jnp.float32, 'ssm_c_im': _jnp.float32, 'ssm_d': _jnp.float32, 'w_glu': _jnp.float32, 'b_glu': _jnp.float32, 'w_branch_a': _jnp.float32, 'w_branch_b': _jnp.float32, 'w_out': _jnp.float32, 'norm_mlp': _jnp.float32, 'w_mlp_up': _jnp.float32, 'w_mlp_down': _jnp.float32, 'norm_final': _jnp.float32}
MOMENT_SCALE = {'norm_mix': 7.735627e-02, 'w_in': 3.894031e-02, 'b_forget': 2.958893e-01, 'ssm_lambda_re': 6.855472e-03, 'ssm_lambda_im': 7.084395e-03, 'ssm_log_dt': 7.850850e+00, 'ssm_b_re': 3.830880e-03, 'ssm_b_im': 3.933425e-03, 'ssm_c_re': 5.394807e-03, 'ssm_c_im': 5.289192e-03, 'ssm_d': 6.759339e-02, 'w_glu': 1.594532e-02, 'b_glu': 2.434175e-02, 'w_branch_a': 5.240914e-02, 'w_branch_b': 4.446023e-02, 'w_out': 6.853639e-02, 'norm_mlp': 2.024522e-01, 'w_mlp_up': 9.763257e-02, 'w_mlp_down': 2.043103e-01, 'norm_final': 6.648340e+01}


def _to_microbatches(a, axis):
    t = _jnp.moveaxis(a, axis, 0)
    t = t.reshape((N_MICROBATCH, t.shape[0] // N_MICROBATCH) + t.shape[1:])
    return _jnp.moveaxis(t, 1, axis + 1)


def setup_inputs(seed: int = 0) -> dict:
    inp = _fwd_setup_inputs(seed)
    key = _jax.random.fold_in(_jax.random.key(seed), 7919)
    shape, _ = _output_shape()
    out = dict(inp)
    out["loss_target"] = _jax.random.normal(_jax.random.fold_in(key, 0), shape, _jnp.float32)
    for i, name in enumerate(TWIN_WEIGHTS):
        w = inp[name].astype(_jnp.float32)
        if MOMENT_SCALE is None:
            s = _jnp.sqrt(_jnp.mean(_jnp.square(w)) + 1e-30)
        else:
            s = MOMENT_SCALE[name]
        km, kv = _jax.random.split(_jax.random.fold_in(key, i + 1))
        out[name] = w
        out["m_" + name] = s * _jax.random.normal(km, w.shape, _jnp.float32)
        out["v_" + name] = (s * s) * _jax.random.uniform(kv, w.shape, _jnp.float32, 0.5, 1.5)
    if N_MICROBATCH > 1:
        for name, axis in PER_EXAMPLE_BATCH_AXIS.items():
            out[name] = _to_microbatches(out[name], axis)
    return {'x': out['x'], 'norm_mix': out['norm_mix'], 'w_in': out['w_in'], 'b_forget': out['b_forget'], 'ssm_lambda_re': out['ssm_lambda_re'], 'ssm_lambda_im': out['ssm_lambda_im'], 'ssm_log_dt': out['ssm_log_dt'], 'ssm_b_re': out['ssm_b_re'], 'ssm_b_im': out['ssm_b_im'], 'ssm_c_re': out['ssm_c_re'], 'ssm_c_im': out['ssm_c_im'], 'ssm_d': out['ssm_d'], 'w_glu': out['w_glu'], 'b_glu': out['b_glu'], 'w_branch_a': out['w_branch_a'], 'w_branch_b': out['w_branch_b'], 'w_out': out['w_out'], 'norm_mlp': out['norm_mlp'], 'w_mlp_up': out['w_mlp_up'], 'w_mlp_down': out['w_mlp_down'], 'norm_final': out['norm_final'], 'loss_target': out['loss_target'], 'm_norm_mix': out['m_norm_mix'], 'm_w_in': out['m_w_in'], 'm_b_forget': out['m_b_forget'], 'm_ssm_lambda_re': out['m_ssm_lambda_re'], 'm_ssm_lambda_im': out['m_ssm_lambda_im'], 'm_ssm_log_dt': out['m_ssm_log_dt'], 'm_ssm_b_re': out['m_ssm_b_re'], 'm_ssm_b_im': out['m_ssm_b_im'], 'm_ssm_c_re': out['m_ssm_c_re'], 'm_ssm_c_im': out['m_ssm_c_im'], 'm_ssm_d': out['m_ssm_d'], 'm_w_glu': out['m_w_glu'], 'm_b_glu': out['m_b_glu'], 'm_w_branch_a': out['m_w_branch_a'], 'm_w_branch_b': out['m_w_branch_b'], 'm_w_out': out['m_w_out'], 'm_norm_mlp': out['m_norm_mlp'], 'm_w_mlp_up': out['m_w_mlp_up'], 'm_w_mlp_down': out['m_w_mlp_down'], 'm_norm_final': out['m_norm_final'], 'v_norm_mix': out['v_norm_mix'], 'v_w_in': out['v_w_in'], 'v_b_forget': out['v_b_forget'], 'v_ssm_lambda_re': out['v_ssm_lambda_re'], 'v_ssm_lambda_im': out['v_ssm_lambda_im'], 'v_ssm_log_dt': out['v_ssm_log_dt'], 'v_ssm_b_re': out['v_ssm_b_re'], 'v_ssm_b_im': out['v_ssm_b_im'], 'v_ssm_c_re': out['v_ssm_c_re'], 'v_ssm_c_im': out['v_ssm_c_im'], 'v_ssm_d': out['v_ssm_d'], 'v_w_glu': out['v_w_glu'], 'v_b_glu': out['v_b_glu'], 'v_w_branch_a': out['v_w_branch_a'], 'v_w_branch_b': out['v_w_branch_b'], 'v_w_out': out['v_w_out'], 'v_norm_mlp': out['v_norm_mlp'], 'v_w_mlp_up': out['v_w_mlp_up'], 'v_w_mlp_down': out['v_w_mlp_down'], 'v_norm_final': out['v_norm_final']}


def _loss(weights, diff, rest, loss_target):
    with _jax.named_scope("forward"):
        args = {**rest, TWIN_DIFF_INPUT: diff, **{k: w.astype(_WEIGHT_DTYPES[k]) for k, w in weights.items()}}
        y = _forward(args)
    with _jax.named_scope("loss_head"):
        err = _jnp.square(y.astype(_jnp.float32) - loss_target)
        return 0.5 * _jnp.sum(_jnp.mean(err, axis=-1)) if err.ndim else 0.5 * err


def _adamw(w, g, m, v):
    m = ADAM_B1 * m + (1.0 - ADAM_B1) * g
    v = ADAM_B2 * v + (1.0 - ADAM_B2) * _jnp.square(g)
    m_hat = m / (1.0 - ADAM_B1 ** ADAM_STEP)
    v_hat = v / (1.0 - ADAM_B2 ** ADAM_STEP)
    delta = -ADAM_LR * (m_hat / (_jnp.sqrt(v_hat) + ADAM_EPS) + ADAM_WD * w)
    return delta, m, v


def reference(x, norm_mix, w_in, b_forget, ssm_lambda_re, ssm_lambda_im, ssm_log_dt, ssm_b_re, ssm_b_im, ssm_c_re, ssm_c_im, ssm_d, w_glu, b_glu, w_branch_a, w_branch_b, w_out, norm_mlp, w_mlp_up, w_mlp_down, norm_final, loss_target, m_norm_mix, m_w_in, m_b_forget, m_ssm_lambda_re, m_ssm_lambda_im, m_ssm_log_dt, m_ssm_b_re, m_ssm_b_im, m_ssm_c_re, m_ssm_c_im, m_ssm_d, m_w_glu, m_b_glu, m_w_branch_a, m_w_branch_b, m_w_out, m_norm_mlp, m_w_mlp_up, m_w_mlp_down, m_norm_final, v_norm_mix, v_w_in, v_b_forget, v_ssm_lambda_re, v_ssm_lambda_im, v_ssm_log_dt, v_ssm_b_re, v_ssm_b_im, v_ssm_c_re, v_ssm_c_im, v_ssm_d, v_w_glu, v_b_glu, v_w_branch_a, v_w_branch_b, v_w_out, v_norm_mlp, v_w_mlp_up, v_w_mlp_down, v_norm_final):
    given = dict(x=x, norm_mix=norm_mix, w_in=w_in, b_forget=b_forget, ssm_lambda_re=ssm_lambda_re, ssm_lambda_im=ssm_lambda_im, ssm_log_dt=ssm_log_dt, ssm_b_re=ssm_b_re, ssm_b_im=ssm_b_im, ssm_c_re=ssm_c_re, ssm_c_im=ssm_c_im, ssm_d=ssm_d, w_glu=w_glu, b_glu=b_glu, w_branch_a=w_branch_a, w_branch_b=w_branch_b, w_out=w_out, norm_mlp=norm_mlp, w_mlp_up=w_mlp_up, w_mlp_down=w_mlp_down, norm_final=norm_final, loss_target=loss_target, m_norm_mix=m_norm_mix, m_w_in=m_w_in, m_b_forget=m_b_forget, m_ssm_lambda_re=m_ssm_lambda_re, m_ssm_lambda_im=m_ssm_lambda_im, m_ssm_log_dt=m_ssm_log_dt, m_ssm_b_re=m_ssm_b_re, m_ssm_b_im=m_ssm_b_im, m_ssm_c_re=m_ssm_c_re, m_ssm_c_im=m_ssm_c_im, m_ssm_d=m_ssm_d, m_w_glu=m_w_glu, m_b_glu=m_b_glu, m_w_branch_a=m_w_branch_a, m_w_branch_b=m_w_branch_b, m_w_out=m_w_out, m_norm_mlp=m_norm_mlp, m_w_mlp_up=m_w_mlp_up, m_w_mlp_down=m_w_mlp_down, m_norm_final=m_norm_final, v_norm_mix=v_norm_mix, v_w_in=v_w_in, v_b_forget=v_b_forget, v_ssm_lambda_re=v_ssm_lambda_re, v_ssm_lambda_im=v_ssm_lambda_im, v_ssm_log_dt=v_ssm_log_dt, v_ssm_b_re=v_ssm_b_re, v_ssm_b_im=v_ssm_b_im, v_ssm_c_re=v_ssm_c_re, v_ssm_c_im=v_ssm_c_im, v_ssm_d=v_ssm_d, v_w_glu=v_w_glu, v_b_glu=v_b_glu, v_w_branch_a=v_w_branch_a, v_w_branch_b=v_w_branch_b, v_w_out=v_w_out, v_norm_mlp=v_norm_mlp, v_w_mlp_up=v_w_mlp_up, v_w_mlp_down=v_w_mlp_down, v_norm_final=v_norm_final)
    weights = {n: given[n] for n in TWIN_WEIGHTS}
    shared = {n: given[n] for n in SHARED_INPUTS}
    per_example = {n: given[n] for n in ['x']}
    grad_fn = _jax.value_and_grad(_loss, argnums=(0, 1))

    def one_microbatch(ex, loss_target):
        ex = dict(ex)
        diff = ex.pop(TWIN_DIFF_INPUT)
        return grad_fn(weights, diff, {**shared, **ex}, loss_target)

    if N_MICROBATCH == 1:
        loss, (grad_w, grad_x) = one_microbatch(per_example, given["loss_target"])
    else:
        def body(carry, xs):
            loss_sum, grad_sum = carry
            l_k, (gw_k, gx_k) = one_microbatch(xs[0], xs[1])
            with _jax.named_scope("update"):
                return (loss_sum + l_k, _jax.tree.map(_jnp.add, grad_sum, gw_k)), gx_k

        init = (_jnp.zeros((), _jnp.float32), _jax.tree.map(_jnp.zeros_like, weights))
        (loss, grad_w), grad_x = _jax.lax.scan(body, init, (per_example, given["loss_target"]))
    with _jax.named_scope("update"):
        delta_w, new_m, new_v = {}, {}, {}
        for n in TWIN_WEIGHTS:
            delta_w[n], new_m[n], new_v[n] = _adamw(weights[n], grad_w[n], given["m_" + n], given["v_" + n])
    return (loss, grad_x, *[grad_w[n] for n in TWIN_WEIGHTS], *[delta_w[n] for n in TWIN_WEIGHTS],
            *[new_m[n] for n in TWIN_WEIGHTS], *[new_v[n] for n in TWIN_WEIGHTS])
```

```python
import functools

import jax
import jax.numpy as jnp
from jax import lax
from jax.experimental import pallas as pl
from jax.experimental.pallas import tpu as pltpu

F32 = jnp.float32
BF16 = jnp.bfloat16

N_DEV = 8
HEAD_DIM = 64
LANES = 128
SSM_CHUNK = 32
SLAB_GROUPS = 8
ATTN_BLOCK = 256
PACK_COLS = 1024
PACK_ROWS = 256
RMS_EPS = 1e-6
VMEM_LIMIT = 56 * 1024 * 1024
ADAM_LR, ADAM_B1, ADAM_B2, ADAM_EPS, ADAM_WD, ADAM_STEP = 0.001, 0.9, 0.999, 1e-08, 0.01, 10
MESH_AXES = ("x", "y", "c")
NEG = -1e30
NT = (((1,), (1,)), ((), ()))
TN = (((0,), (0,)), ((), ()))


def _cparams(*sem):
    return pltpu.CompilerParams(dimension_semantics=sem, vmem_limit_bytes=VMEM_LIMIT)


def _tile(dim, pref, unit=LANES):
    if dim <= pref:
        return dim
    best = None
    for t in range(unit, pref + 1, unit):
        if dim % t == 0:
            best = t
    assert best is not None, (dim, pref)
    return best


def _mm(a, b, *, name, ta=False, a_fn=None, epi=None, extras=(), out_dtype=BF16, tm=1024, tn=1024, tk=1024):
    if ta:
        K, M = a.shape
    else:
        M, K = a.shape
    Kb, N = b.shape
    assert K == Kb, (a.shape, b.shape)
    tm, tn, tk = _tile(M, tm), _tile(N, tn), _tile(K, tk)
    nk = K // tk
    ne = len(extras)

    def body(a_ref, b_ref, *rest):
        e_refs, o_ref, acc_ref = rest[:ne], rest[ne], rest[ne + 1]
        k = pl.program_id(2)
        av = a_ref[...]
        if a_fn is not None:
            av = a_fn(av)
        av = av.astype(BF16)
        bv = b_ref[...].astype(BF16)
        part = lax.dot_general(av, bv, TN if ta else (((1,), (0,)), ((), ())), preferred_element_type=F32)

        @pl.when(k == 0)
        def _():
            acc_ref[...] = part

        @pl.when(k > 0)
        def _():
            acc_ref[...] += part

        @pl.when(k == nk - 1)
        def _():
            r = acc_ref[...]
            if epi is not None:
                r = epi(r, *[e[...] for e in e_refs])
            o_ref[...] = r.astype(o_ref.dtype)

    a_spec = pl.BlockSpec((tk, tm), lambda i, j, k: (k, i)) if ta else pl.BlockSpec((tm, tk), lambda i, j, k: (i, k))
    return pl.pallas_call(
        body, name=name,
        out_shape=jax.ShapeDtypeStruct((M, N), out_dtype),
        grid=(M // tm, N // tn, nk),
        in_specs=[a_spec, pl.BlockSpec((tk, tn), lambda i, j, k: (k, j))]
        + [pl.BlockSpec((tm, tn), lambda i, j, k: (i, j)) for _ in extras],
        out_specs=pl.BlockSpec((tm, tn), lambda i, j, k: (i, j)),
        scratch_shapes=[pltpu.VMEM((tm, tn), F32)],
        compiler_params=_cparams("parallel", "parallel", "arbitrary"),
    )(a, b, *extras)


def _relu_sq(v):
    r = jnp.maximum(v.astype(F32), 0.0)
    return r * r


def _sigmoid(v):
    return 1.0 / (1.0 + jnp.exp(-v))


GELU_C = 0.7978845608028654
GELU_A = 0.044715


def _gelu(v):
    return 0.5 * v * (1.0 + jnp.tanh(GELU_C * (v + GELU_A * v * v * v)))


def _gelu_grad(v):
    t = jnp.tanh(GELU_C * (v + GELU_A * v * v * v))
    return 0.5 * (1.0 + t) + 0.5 * v * (1.0 - t * t) * GELU_C * (1.0 + 3.0 * GELU_A * v * v)


def _rmsnorm_fwd(x, g, *, name, tr=512):
    n, d = x.shape
    tr = _tile(n, tr, 8)

    def body(x_ref, g_ref, h_ref, r_ref):
        xv = x_ref[...]
        r = lax.rsqrt(jnp.mean(xv * xv, axis=-1, keepdims=True) + RMS_EPS)
        h_ref[...] = (xv * r * g_ref[...]).astype(BF16)
        r_ref[...] = r

    return pl.pallas_call(
        body, name=name,
        out_shape=(jax.ShapeDtypeStruct((n, d), BF16), jax.ShapeDtypeStruct((n, 1), F32)),
        grid=(n // tr,),
        in_specs=[pl.BlockSpec((tr, d), lambda i: (i, 0)), pl.BlockSpec((1, d), lambda i: (0, 0))],
        out_specs=(pl.BlockSpec((tr, d), lambda i: (i, 0)), pl.BlockSpec((tr, 1), lambda i: (i, 0))),
        compiler_params=_cparams("parallel"),
    )(x, g.reshape(1, d))


def _rmsnorm_bwd(dh, x, r, g, dres, *, name, tr=512):
    n, d = x.shape
    tr = _tile(n, tr, 8)

    def body(dh_ref, x_ref, r_ref, g_ref, dres_ref, dx_ref, dg_ref):
        i = pl.program_id(0)
        rv = r_ref[...]
        xh = x_ref[...] * rv
        dhv = dh_ref[...].astype(F32)
        dxh = dhv * g_ref[...]
        m = jnp.mean(dxh * xh, axis=-1, keepdims=True)
        dx_ref[...] = rv * (dxh - xh * m) + dres_ref[...]
        part = jnp.sum(dhv * xh, axis=0, keepdims=True)

        @pl.when(i == 0)
        def _():
            dg_ref[...] = part

        @pl.when(i > 0)
        def _():
            dg_ref[...] += part

    row = pl.BlockSpec((tr, d), lambda i: (i, 0))
    vec = pl.BlockSpec((1, d), lambda i: (0, 0))
    return pl.pallas_call(
        body, name=name,
        out_shape=(jax.ShapeDtypeStruct((n, d), F32), jax.ShapeDtypeStruct((1, d), F32)),
        grid=(n // tr,),
        in_specs=[row, row, pl.BlockSpec((tr, 1), lambda i: (i, 0)), vec, row],
        out_specs=(row, vec),
        compiler_params=_cparams("arbitrary"),
    )(dh, x, r, g.reshape(1, d), dres)


def _loss_head(x, g, target, *, name, tr=512):
    n, d = x.shape
    tr = _tile(n, tr, 8)

    def body(x_ref, g_ref, t_ref, dx_ref, dg_ref, loss_ref):
        i = pl.program_id(0)
        xv = x_ref[...]
        gv = g_ref[...]
        r = lax.rsqrt(jnp.mean(xv * xv, axis=-1, keepdims=True) + RMS_EPS)
        xh = xv * r
        err = xh * gv - t_ref[...]
        lpart = 0.5 * jnp.sum(jnp.mean(err * err, axis=-1, keepdims=True), axis=0, keepdims=True)
        dy = err * (1.0 / d)
        dxh = dy * gv
        m = jnp.mean(dxh * xh, axis=-1, keepdims=True)
        dx_ref[...] = r * (dxh - xh * m)
        gpart = jnp.sum(dy * xh, axis=0, keepdims=True)
        lrow = jnp.broadcast_to(lpart, (1, LANES))

        @pl.when(i == 0)
        def _():
            dg_ref[...] = gpart
            loss_ref[...] = lrow

        @pl.when(i > 0)
        def _():
            dg_ref[...] += gpart
            loss_ref[...] += lrow

    row = pl.BlockSpec((tr, d), lambda i: (i, 0))
    vec = pl.BlockSpec((1, d), lambda i: (0, 0))
    return pl.pallas_call(
        body, name=name,
        out_shape=(jax.ShapeDtypeStruct((n, d), F32), jax.ShapeDtypeStruct((1, d), F32),
                   jax.ShapeDtypeStruct((1, LANES), F32)),
        grid=(n // tr,),
        in_specs=[row, vec, row],
        out_specs=(row, vec, pl.BlockSpec((1, LANES), lambda i: (0, 0))),
        compiler_params=_cparams("arbitrary"),
    )(x, g.reshape(1, d), target)


def _tri_dot(v, tri):
    hi = v.astype(BF16)
    r1 = v - hi.astype(F32)
    mid = r1.astype(BF16)
    lo = (r1 - mid.astype(F32)).astype(BF16)
    d = functools.partial(jnp.dot, preferred_element_type=F32)
    return d(hi, tri) + d(mid, tri) + d(lo, tri)


def _fox_gate_fwd(ft, bf, *, name, blk=256):
    B, H, S = ft.shape
    blk = _tile(S, blk)
    nb = S // blk

    def body(f_ref, b_ref, o_ref):
        x = f_ref[0] + b_ref[...]
        logf = jnp.minimum(x, 0.0) - jnp.log(1.0 + jnp.exp(-jnp.abs(x)))
        rr = lax.broadcasted_iota(jnp.int32, (blk, blk), 0)
        cc = lax.broadcasted_iota(jnp.int32, (blk, blk), 1)
        tri = (rr <= cc).astype(BF16)
        carry = jnp.zeros((H, 1), F32)
        for n in range(nb):
            c = _tri_dot(logf[:, n * blk:(n + 1) * blk], tri) + carry
            o_ref[0, :, n * blk:(n + 1) * blk] = c
            carry = c[:, blk - 1:blk]

    return pl.pallas_call(
        body, name=name,
        out_shape=jax.ShapeDtypeStruct((B, H, S), F32),
        grid=(B,),
        in_specs=[pl.BlockSpec((1, H, S), lambda b: (b, 0, 0)), pl.BlockSpec((H, 1), lambda b: (0, 0))],
        out_specs=pl.BlockSpec((1, H, S), lambda b: (b, 0, 0)),
        compiler_params=_cparams("parallel"),
    )(ft, bf.reshape(H, 1))


def _fox_gate_bwd(dF, ft, bf, *, name, blk=256):
    B, H, S = ft.shape
    blk = _tile(S, blk)
    nb = S // blk

    def body(d_ref, f_ref, b_ref, o_ref, db_ref):
        b = pl.program_id(0)
        x = f_ref[0] + b_ref[...]
        sneg = 1.0 / (1.0 + jnp.exp(x))
        dv = d_ref[0]
        rr = lax.broadcasted_iota(jnp.int32, (blk, blk), 0)
        cc = lax.broadcasted_iota(jnp.int32, (blk, blk), 1)
        tri = (rr >= cc).astype(BF16)
        carry = jnp.zeros((H, 1), F32)
        tot = jnp.zeros((H, 1), F32)
        for n in reversed(range(nb)):
            sl = slice(n * blk, (n + 1) * blk)
            c = _tri_dot(dv[:, sl], tri) + carry
            g = c * sneg[:, sl]
            o_ref[0, :, sl] = g
            tot = tot + jnp.sum(g, axis=1, keepdims=True)
            carry = c[:, 0:1]

        @pl.when(b == 0)
        def _():
            db_ref[...] = tot

        @pl.when(b > 0)
        def _():
            db_ref[...] += tot

    blkspec = pl.BlockSpec((1, H, S), lambda b: (b, 0, 0))
    return pl.pallas_call(
        body, name=name,
        out_shape=(jax.ShapeDtypeStruct((B, H, S), F32), jax.ShapeDtypeStruct((H, 1), F32)),
        grid=(B,),
        in_specs=[blkspec, blkspec, pl.BlockSpec((H, 1), lambda b: (0, 0))],
        out_specs=(blkspec, pl.BlockSpec((H, 1), lambda b: (0, 0))),
        compiler_params=_cparams("arbitrary"),
    )(dF, ft, bf.reshape(H, 1))


def _head_masks():
    lane = lax.broadcasted_iota(jnp.int32, (1, LANES), 1)
    return [lane < HEAD_DIM, lane >= HEAD_DIM]


def _attn_fwd(proj, fcol, frow, *, name, H, tq):
    B, S, _ = proj.shape
    HP = H // 2
    nq = S // tq
    scale = HEAD_DIM ** -0.5

    def body(q_ref, k_ref, v_ref, fq_ref, fk_ref, o_ref, lse_ref):
        i = pl.program_id(2)
        q = q_ref[0]
        masks = _head_masks()
        rr = lax.broadcasted_iota(jnp.int32, (tq, tq), 0)
        cc = lax.broadcasted_iota(jnp.int32, (tq, tq), 1)
        causal = rr >= cc
        outs, lses = [], []
        for h in range(2):
            qm = jnp.where(masks[h], q, jnp.zeros_like(q)) * jnp.asarray(scale, BF16)
            fq = fq_ref[0, 0][:, h:h + 1]

            def block(j, carry, masked, h=h, qm=qm, fq=fq):
                m, l, acc = carry
                rows = pl.ds(pl.multiple_of(j * tq, tq), tq)
                kj = k_ref[0, rows, :]
                vj = v_ref[0, rows, :]
                s = lax.dot_general(qm, kj, NT, preferred_element_type=F32)
                s = s + (fq - fk_ref[0, 0, h, pl.ds(j, 1), :])
                if masked:
                    s = jnp.where(causal, s, NEG)
                m_new = jnp.maximum(m, jnp.max(s, axis=-1, keepdims=True))
                alpha = jnp.exp(m - m_new)
                p = jnp.exp(s - m_new)
                l = alpha * l + jnp.sum(p, axis=-1, keepdims=True)
                acc = alpha * acc + jnp.dot(p.astype(BF16), vj, preferred_element_type=F32)
                return m_new, l, acc

            init = (jnp.full((tq, 1), NEG, F32), jnp.zeros((tq, 1), F32), jnp.zeros((tq, LANES), F32))
            carry = lax.fori_loop(0, i, lambda j, c, block=block: block(j, c, False), init)
            m, l, acc = block(i, carry, True)
            outs.append(acc / l)
            lses.append(m + jnp.log(l))
        o_ref[0] = jnp.where(masks[0], outs[0], outs[1]).astype(BF16)
        two = lax.broadcasted_iota(jnp.int32, (1, 2), 1)
        lse_ref[0, 0] = jnp.where(two == 0, lses[0], lses[1])

    kv = lambda off: pl.BlockSpec((1, S, LANES), lambda b, hp, i: (b, 0, off + hp))
    return pl.pallas_call(
        body, name=name,
        out_shape=(jax.ShapeDtypeStruct((B, S, H * HEAD_DIM), BF16), jax.ShapeDtypeStruct((B, HP, S, 2), F32)),
        grid=(B, HP, nq),
        in_specs=[pl.BlockSpec((1, tq, LANES), lambda b, hp, i: (b, i, hp)), kv(HP), kv(2 * HP),
                  pl.BlockSpec((1, 1, tq, 2), lambda b, hp, i: (b, hp, i, 0)),
                  pl.BlockSpec((1, 1, 2, nq, tq), lambda b, hp, i: (b, hp, 0, 0, 0))],
        out_specs=(pl.BlockSpec((1, tq, LANES), lambda b, hp, i: (b, i, hp)),
                   pl.BlockSpec((1, 1, tq, 2), lambda b, hp, i: (b, hp, i, 0))),
        compiler_params=_cparams("parallel", "parallel", "arbitrary"),
    )(proj, proj, proj, fcol, frow)


def _attn_bwd(proj, ya, dya, lse, fcol, frow, *, name, H, tq):
    B, S, _ = proj.shape
    HP = H // 2
    nq = S // tq
    AW = H * HEAD_DIM
    scale = HEAD_DIM ** -0.5

    def body(q_ref, k_ref, v_ref, o_ref, do_ref, lse_ref, fq_ref, fk_ref,
             dq_ref, dk_ref, dv_ref, dfk_ref, dfq_ref, dq_acc, dk_acc, dv_acc, delta_ref, dfk_acc, dfq_acc):
        masks = _head_masks()
        rr = lax.broadcasted_iota(jnp.int32, (tq, tq), 0)
        cc = lax.broadcasted_iota(jnp.int32, (tq, tq), 1)
        causal = rr >= cc
        sc = jnp.asarray(scale, BF16)
        prod = do_ref[0].astype(F32) * o_ref[0].astype(F32)
        for h in range(2):
            delta_ref[h] = jnp.sum(jnp.where(masks[h], prod, 0.0), axis=-1, keepdims=True)
        dq_acc[...] = jnp.zeros_like(dq_acc)
        dfq_acc[...] = jnp.zeros_like(dfq_acc)

        def kv_block(j, carry):
            rows_j = pl.ds(pl.multiple_of(j * tq, tq), tq)
            kj = k_ref[0, rows_j, :]
            vj = v_ref[0, rows_j, :]
            dk_acc[...] = jnp.zeros_like(dk_acc)
            dv_acc[...] = jnp.zeros_like(dv_acc)
            dfk_acc[...] = jnp.zeros_like(dfk_acc)
            kms = [jnp.where(masks[h], kj, jnp.zeros_like(kj)) * sc for h in range(2)]

            def q_block(i, masked):
                rows_i = pl.ds(pl.multiple_of(i * tq, tq), tq)
                qi = q_ref[0, rows_i, :]
                doi = do_ref[0, rows_i, :]
                fqi = fq_ref[0, 0, rows_i, :]
                lsei = lse_ref[0, 0, rows_i, :]
                for h in range(2):
                    qm = jnp.where(masks[h], qi, jnp.zeros_like(qi)) * sc
                    dom = jnp.where(masks[h], doi, jnp.zeros_like(doi))
                    s = lax.dot_general(qm, kj, NT, preferred_element_type=F32)
                    s = s + (fqi[:, h:h + 1] - fk_ref[0, 0, h, pl.ds(j, 1), :])
                    p = jnp.exp(s - lsei[:, h:h + 1])
                    if masked:
                        p = jnp.where(causal, p, 0.0)
                    dp = lax.dot_general(dom, vj, NT, preferred_element_type=F32)
                    ds = p * (dp - delta_ref[h, rows_i, :])
                    pb, dsb = p.astype(BF16), ds.astype(BF16)
                    dv_acc[...] += lax.dot_general(pb, dom, TN, preferred_element_type=F32)
                    dk_acc[...] += lax.dot_general(dsb, qm, TN, preferred_element_type=F32)
                    dq_acc[rows_i, :] += jnp.dot(dsb, kms[h], preferred_element_type=F32)
                    dfk_acc[h:h + 1, :] -= jnp.sum(ds, axis=0, keepdims=True)
                    dfq_acc[h, rows_i, :] += jnp.sum(ds, axis=1, keepdims=True)

            q_block(j, True)

            def rest(i, c):
                q_block(i, False)
                return c

            lax.fori_loop(j + 1, nq, rest, 0)
            dk_ref[0, rows_j, :] = dk_acc[...].astype(BF16)
            dv_ref[0, rows_j, :] = dv_acc[...].astype(BF16)
            for h in range(2):
                dfk_ref[0, 0, h, pl.ds(j, 1), :] = dfk_acc[h:h + 1, :]
            return carry

        lax.fori_loop(0, nq, kv_block, 0)
        dq_ref[0] = dq_acc[...].astype(BF16)
        two = lax.broadcasted_iota(jnp.int32, (1, 2), 1)
        dfq_ref[0, 0] = jnp.where(two == 0, dfq_acc[0], dfq_acc[1])

    col = lambda off: pl.BlockSpec((1, S, LANES), lambda b, hp: (b, 0, off + hp))
    stat = pl.BlockSpec((1, 1, S, 2), lambda b, hp: (b, hp, 0, 0))
    rowf = pl.BlockSpec((1, 1, 2, nq, tq), lambda b, hp: (b, hp, 0, 0, 0))
    grad = jax.ShapeDtypeStruct((B, S, AW), BF16)
    return pl.pallas_call(
        body, name=name,
        out_shape=(grad, grad, grad, jax.ShapeDtypeStruct((B, HP, 2, nq, tq), F32),
                   jax.ShapeDtypeStruct((B, HP, S, 2), F32)),
        grid=(B, HP),
        in_specs=[col(0), col(HP), col(2 * HP), col(0), col(0), stat, stat, rowf],
        out_specs=(col(0), col(0), col(0), rowf, stat),
        scratch_shapes=[pltpu.VMEM((S, LANES), F32), pltpu.VMEM((tq, LANES), F32), pltpu.VMEM((tq, LANES), F32),
                        pltpu.VMEM((2, S, 1), F32), pltpu.VMEM((2, tq), F32), pltpu.VMEM((2, S, 1), F32)],
        compiler_params=_cparams("parallel", "parallel"),
    )(proj, proj, proj, ya, dya, lse, fcol, frow)


def _cmul(ar, ai, br, bi):
    return ar * br - ai * bi, ar * bi + ai * br


def _ssm_states(u_ref, bm, lam_ref, pw_ref, lamT_ref, hr_ref, hi_ref, inr_ref, ini_ref, T, NC, SP):
    lr, li = lam_ref[0, 0:1, :], lam_ref[0, 1:2, :]
    bu = jnp.dot(u_ref[0, 0], bm, preferred_element_type=F32)
    hr_ref[0] = bu[:, :SP]
    hi_ref[0] = bu[:, SP:]

    def step(t, c):
        bu = jnp.dot(u_ref[0, t], bm, preferred_element_type=F32)
        pr, pi = _cmul(hr_ref[t - 1], hi_ref[t - 1], lr, li)
        hr_ref[t] = pr + bu[:, :SP]
        hi_ref[t] = pi + bu[:, SP:]
        return c

    lax.fori_loop(1, T, step, 0)

    tr, ti = lamT_ref[0, 0:1, :], lamT_ref[0, 1:2, :]
    inr_ref[0:1, :] = jnp.zeros((1, SP), F32)
    ini_ref[0:1, :] = jnp.zeros((1, SP), F32)

    def chunk(n, c):
        prev = pl.ds(n - 1, 1)
        pr, pi = _cmul(inr_ref[prev, :], ini_ref[prev, :], tr, ti)
        inr_ref[pl.ds(n, 1), :] = pr + hr_ref[T - 1, prev, :]
        ini_ref[pl.ds(n, 1), :] = pi + hi_ref[T - 1, prev, :]
        return c

    lax.fori_loop(1, NC, chunk, 0)

    def fix(t, c):
        cr, ci = _cmul(inr_ref[...], ini_ref[...], pw_ref[0, 0, pl.ds(t, 1), :], pw_ref[0, 1, pl.ds(t, 1), :])
        hr_ref[t] += cr
        hi_ref[t] += ci
        return c

    lax.fori_loop(0, T, fix, 0)


def _ssm_fwd(u_tm, bmat, cmat, lam, pw, lamT, dskip, *, name):
    B, T, NC, W = u_tm.shape
    NS = W // LANES
    SP = bmat.shape[2] // 2

    def body(u_ref, b_ref, c_ref, lam_ref, pw_ref, lamT_ref, d_ref, y_ref, hr_ref, hi_ref, inr_ref, ini_ref):
        _ssm_states(u_ref, b_ref[0], lam_ref, pw_ref, lamT_ref, hr_ref, hi_ref, inr_ref, ini_ref, T, NC, SP)
        cm = c_ref[0]
        dv = d_ref[...]

        def out(t, c):
            hcat = jnp.concatenate([hr_ref[t], hi_ref[t]], axis=1).astype(BF16)
            y_ref[0, t] = jnp.dot(hcat, cm, preferred_element_type=F32) + dv * u_ref[0, t].astype(F32)
            return c

        lax.fori_loop(0, T, out, 0)

    slab = lambda *shape: pl.BlockSpec((1,) + shape, lambda b, s: (s,) + (0,) * len(shape))
    tok = pl.BlockSpec((1, T, NC, LANES), lambda b, s: (b, 0, 0, s))
    return pl.pallas_call(
        body, name=name,
        out_shape=jax.ShapeDtypeStruct((B, T, NC, W), F32),
        grid=(B, NS),
        in_specs=[tok, slab(LANES, 2 * SP), slab(2 * SP, LANES), slab(2, SP), slab(2, T, SP), slab(2, SP),
                  pl.BlockSpec((1, LANES), lambda b, s: (0, s))],
        out_specs=tok,
        scratch_shapes=[pltpu.VMEM((T, NC, SP), F32), pltpu.VMEM((T, NC, SP), F32),
                        pltpu.VMEM((NC, SP), F32), pltpu.VMEM((NC, SP), F32)],
        compiler_params=_cparams("parallel", "parallel"),
    )(u_tm, bmat, cmat, lam, pw, lamT, dskip)


def _ssm_bwd(u_tm, dy_tm, bmat, bmat_t, cmat_t, lam, pw, lamT, dskip, *, name):
    B, T, NC, W = u_tm.shape
    NS = W // LANES
    SP = bmat.shape[2] // 2

    def body(u_ref, dy_ref, b_ref, bt_ref, ct_ref, lam_ref, pw_ref, lamT_ref, d_ref,
             du_ref, gb_ref, gc_ref, glam_ref, gd_ref,
             hr_ref, hi_ref, ar_ref, ai_ref, inr_ref, ini_ref, anr_ref, ani_ref, glr_ref, gli_ref):
        b = pl.program_id(1)
        _ssm_states(u_ref, b_ref[0], lam_ref, pw_ref, lamT_ref, hr_ref, hi_ref, inr_ref, ini_ref, T, NC, SP)
        lr, li = lam_ref[0, 0:1, :], lam_ref[0, 1:2, :]
        ct = ct_ref[0]
        bt = bt_ref[0]
        dv = d_ref[...]

        gh = jnp.dot(dy_ref[0, T - 1].astype(BF16), ct, preferred_element_type=F32)
        ar_ref[T - 1] = gh[:, :SP]
        ai_ref[T - 1] = gh[:, SP:]

        def back(k, c):
            t = T - 2 - k
            gh = jnp.dot(dy_ref[0, t].astype(BF16), ct, preferred_element_type=F32)
            pr, pi = _cmul(ar_ref[t + 1], ai_ref[t + 1], lr, -li)
            ar_ref[t] = pr + gh[:, :SP]
            ai_ref[t] = pi + gh[:, SP:]
            return c

        lax.fori_loop(0, T - 1, back, 0)

        tr, ti = lamT_ref[0, 0:1, :], lamT_ref[0, 1:2, :]
        anr_ref[NC - 1:NC, :] = jnp.zeros((1, SP), F32)
        ani_ref[NC - 1:NC, :] = jnp.zeros((1, SP), F32)

        def chunk(k, c):
            n = NC - 2 - k
            nxt = pl.ds(n + 1, 1)
            pr, pi = _cmul(anr_ref[nxt, :], ani_ref[nxt, :], tr, -ti)
            anr_ref[pl.ds(n, 1), :] = pr + ar_ref[0, nxt, :]
            ani_ref[pl.ds(n, 1), :] = pi + ai_ref[0, nxt, :]
            return c

        lax.fori_loop(0, NC - 1, chunk, 0)

        @pl.when(b == 0)
        def _():
            gb_ref[...] = jnp.zeros_like(gb_ref)
            gc_ref[...] = jnp.zeros_like(gc_ref)
            glam_ref[...] = jnp.zeros_like(glam_ref)
            gd_ref[...] = jnp.zeros_like(gd_ref)

        glr_ref[...] = jnp.zeros_like(glr_ref)
        gli_ref[...] = jnp.zeros_like(gli_ref)

        def final(t, hpr, hpi):
            back_pow = pl.ds(T - 1 - t, 1)
            cr, ci = _cmul(anr_ref[...], ani_ref[...], pw_ref[0, 0, back_pow, :], -pw_ref[0, 1, back_pow, :])
            a_r = ar_ref[t] + cr
            a_i = ai_ref[t] + ci
            glr_ref[...] += a_r * hpr + a_i * hpi
            gli_ref[...] += a_i * hpr - a_r * hpi
            acat = jnp.concatenate([a_r, a_i], axis=1).astype(BF16)
            ut = u_ref[0, t]
            dyt = dy_ref[0, t]
            du_ref[0, t] = (jnp.dot(acat, bt, preferred_element_type=F32) + dv * dyt).astype(BF16)
            gb_ref[0] += lax.dot_general(acat, ut, TN, preferred_element_type=F32)
            hcat = jnp.concatenate([hr_ref[t], hi_ref[t]], axis=1).astype(BF16)
            gc_ref[0] += lax.dot_general(dyt.astype(BF16), hcat, TN, preferred_element_type=F32)
            gd_ref[0] += jnp.sum(dyt * ut.astype(F32), axis=0, keepdims=True)

        final(0, inr_ref[...], ini_ref[...])

        def rest(t, c):
            final(t, hr_ref[t - 1], hi_ref[t - 1])
            return c

        lax.fori_loop(1, T, rest, 0)
        glam_ref[0, 0:1, :] += jnp.sum(glr_ref[...], axis=0, keepdims=True)
        glam_ref[0, 1:2, :] += jnp.sum(gli_ref[...], axis=0, keepdims=True)

    slab = lambda *shape: pl.BlockSpec((1,) + shape, lambda s, b: (s,) + (0,) * len(shape))
    tok = pl.BlockSpec((1, T, NC, LANES), lambda s, b: (b, 0, 0, s))
    big = pltpu.VMEM((T, NC, SP), F32)
    small = pltpu.VMEM((NC, SP), F32)
    return pl.pallas_call(
        body, name=name,
        out_shape=(jax.ShapeDtypeStruct((B, T, NC, W), BF16),
                   jax.ShapeDtypeStruct((NS, 2 * SP, LANES), F32), jax.ShapeDtypeStruct((NS, LANES, 2 * SP), F32),
                   jax.ShapeDtypeStruct((NS, 2, SP), F32), jax.ShapeDtypeStruct((NS, 1, LANES), F32)),
        grid=(NS, B),
        in_specs=[tok, tok, slab(LANES, 2 * SP), slab(2 * SP, LANES), slab(LANES, 2 * SP), slab(2, SP),
                  slab(2, T, SP), slab(2, SP), pl.BlockSpec((1, LANES), lambda s, b: (0, s))],
        out_specs=(tok, slab(2 * SP, LANES), slab(LANES, 2 * SP), slab(2, SP), slab(1, LANES)),
        scratch_shapes=[big, big, big, big, small, small, small, small, small, small],
        compiler_params=_cparams("parallel", "arbitrary"),
    )(u_tm, dy_tm, bmat, bmat_t, cmat_t, lam, pw, lamT, dskip)


def _glu_fwd(ys, w, b, *, name, tr=512):
    n, wd = ys.shape
    tr = _tile(n, tr, 8)

    def body(y_ref, w_ref, b_ref, o_ref):
        yb = _gelu(y_ref[...])
        z = jnp.dot(yb.astype(BF16), w_ref[...], preferred_element_type=F32) + b_ref[...]
        o_ref[...] = (yb * _sigmoid(z)).astype(BF16)

    row = pl.BlockSpec((tr, wd), lambda i: (i, 0))
    return pl.pallas_call(
        body, name=name, out_shape=jax.ShapeDtypeStruct((n, wd), BF16), grid=(n // tr,),
        in_specs=[row, pl.BlockSpec((wd, wd), lambda i: (0, 0)), pl.BlockSpec((1, wd), lambda i: (0, 0))],
        out_specs=row, compiler_params=_cparams("parallel"),
    )(ys, w, b.reshape(1, wd))


def _glu_bwd(ys, dyb2, w, w_t, b, *, name, tr=512):
    n, wd = ys.shape
    tr = _tile(n, tr, 8)

    def body(y_ref, d_ref, w_ref, wt_ref, b_ref, dys_ref, dz_ref, yb_ref, db_ref):
        i = pl.program_id(0)
        yv = y_ref[...]
        yb = _gelu(yv)
        ybb = yb.astype(BF16)
        sg = _sigmoid(jnp.dot(ybb, w_ref[...], preferred_element_type=F32) + b_ref[...])
        dv = d_ref[...].astype(F32)
        dz = dv * yb * sg * (1.0 - sg)
        dzb = dz.astype(BF16)
        dyb = dv * sg + jnp.dot(dzb, wt_ref[...], preferred_element_type=F32)
        dys_ref[...] = dyb * _gelu_grad(yv)
        dz_ref[...] = dzb
        yb_ref[...] = ybb
        part = jnp.sum(dz, axis=0, keepdims=True)

        @pl.when(i == 0)
        def _():
            db_ref[...] = part

        @pl.when(i > 0)
        def _():
            db_ref[...] += part

    row = pl.BlockSpec((tr, wd), lambda i: (i, 0))
    mat = pl.BlockSpec((wd, wd), lambda i: (0, 0))
    vec = pl.BlockSpec((1, wd), lambda i: (0, 0))
    return pl.pallas_call(
        body, name=name,
        out_shape=(jax.ShapeDtypeStruct((n, wd), F32), jax.ShapeDtypeStruct((n, wd), BF16),
                   jax.ShapeDtypeStruct((n, wd), BF16), jax.ShapeDtypeStruct((1, wd), F32)),
        grid=(n // tr,), in_specs=[row, row, mat, mat, vec], out_specs=(row, row, row, vec),
        compiler_params=_cparams("arbitrary"),
    )(ys, dyb2, w, w_t, b.reshape(1, wd))


def _merge_fwd(ya, yb2, wa, wb, proj, gate_blk, *, name, tr=512):
    n, aw = ya.shape
    d = wa.shape[1]
    tr = _tile(n, tr, 8)

    def body(ya_ref, yb_ref, wa_ref, wb_ref, ga_ref, gb_ref, mix_ref, pa_ref, pb_ref):
        pa = jnp.dot(ya_ref[...], wa_ref[...], preferred_element_type=F32)
        pb = jnp.dot(yb_ref[...], wb_ref[...], preferred_element_type=F32)
        mix = _sigmoid(ga_ref[...].astype(F32)) * pa + _sigmoid(gb_ref[...].astype(F32)) * pb
        mix_ref[...] = mix.astype(BF16)
        pa_ref[...] = pa.astype(BF16)
        pb_ref[...] = pb.astype(BF16)

    row = lambda wdt: pl.BlockSpec((tr, wdt), lambda i: (i, 0))
    full = lambda r, c: pl.BlockSpec((r, c), lambda i: (0, 0))
    out = jax.ShapeDtypeStruct((n, d), BF16)
    return pl.pallas_call(
        body, name=name, out_shape=(out, out, out), grid=(n // tr,),
        in_specs=[row(aw), row(yb2.shape[1]), full(*wa.shape), full(*wb.shape),
                  pl.BlockSpec((tr, d), lambda i: (i, gate_blk)), pl.BlockSpec((tr, d), lambda i: (i, gate_blk + 1))],
        out_specs=(row(d), row(d), row(d)), compiler_params=_cparams("parallel"),
    )(ya, yb2, wa, wb, proj, proj)


def _merge_bwd(dmix, proj, pa, pb, gate_blk, *, name, tr=512):
    n, d = dmix.shape
    tr = _tile(n, tr, 8)

    def body(dm_ref, ga_ref, gb_ref, pa_ref, pb_ref, dpa_ref, dpb_ref, dga_ref, dgb_ref):
        dm = dm_ref[...].astype(F32)
        sa = _sigmoid(ga_ref[...].astype(F32))
        sb = _sigmoid(gb_ref[...].astype(F32))
        dpa_ref[...] = (dm * sa).astype(BF16)
        dpb_ref[...] = (dm * sb).astype(BF16)
        dga_ref[...] = (dm * pa_ref[...].astype(F32) * sa * (1.0 - sa)).astype(BF16)
        dgb_ref[...] = (dm * pb_ref[...].astype(F32) * sb * (1.0 - sb)).astype(BF16)

    row = pl.BlockSpec((tr, d), lambda i: (i, 0))
    out = jax.ShapeDtypeStruct((n, d), BF16)
    return pl.pallas_call(
        body, name=name, out_shape=(out, out, out, out), grid=(n // tr,),
        in_specs=[row, pl.BlockSpec((tr, d), lambda i: (i, gate_blk)), pl.BlockSpec((tr, d), lambda i: (i, gate_blk + 1)),
                  row, row],
        out_specs=(row, row, row, row), compiler_params=_cparams("parallel"),
    )(dmix, proj, proj, pa, pb)


def _outproj_fwd(mixed, w, x0, g, *, name, tr=512):
    n, d = x0.shape
    tr = _tile(n, tr, 8)

    def body(m_ref, w_ref, x_ref, g_ref, x1_ref, h_ref, r_ref):
        x1 = x_ref[...] + jnp.dot(m_ref[...], w_ref[...], preferred_element_type=F32)
        r = lax.rsqrt(jnp.mean(x1 * x1, axis=-1, keepdims=True) + RMS_EPS)
        x1_ref[...] = x1
        h_ref[...] = (x1 * r * g_ref[...]).astype(BF16)
        r_ref[...] = r

    row = pl.BlockSpec((tr, d), lambda i: (i, 0))
    return pl.pallas_call(
        body, name=name,
        out_shape=(jax.ShapeDtypeStruct((n, d), F32), jax.ShapeDtypeStruct((n, d), BF16),
                   jax.ShapeDtypeStruct((n, 1), F32)),
        grid=(n // tr,),
        in_specs=[row, pl.BlockSpec((d, d), lambda i: (0, 0)), row, pl.BlockSpec((1, d), lambda i: (0, 0))],
        out_specs=(row, row, pl.BlockSpec((tr, 1), lambda i: (i, 0))),
        compiler_params=_cparams("parallel"),
    )(mixed, w, x0, g.reshape(1, d))


def _adamw(w, g, m, v, *, name):
    shape = w.shape
    total = w.size
    if total % PACK_COLS == 0 and ((total // PACK_COLS) % 8 == 0 or total // PACK_COLS <= 512):
        rows, cols = total // PACK_COLS, PACK_COLS
    elif w.ndim >= 2:
        rows, cols = total // shape[-1], shape[-1]
    else:
        rows, cols = 1, total
    tr = _tile(rows, 512, 8)

    def body(w_ref, g_ref, m_ref, v_ref, d_ref, nm_ref, nv_ref):
        gv = g_ref[...]
        mn = ADAM_B1 * m_ref[...] + (1.0 - ADAM_B1) * gv
        vn = ADAM_B2 * v_ref[...] + (1.0 - ADAM_B2) * (gv * gv)
        m_hat = mn / (1.0 - ADAM_B1 ** ADAM_STEP)
        v_hat = vn / (1.0 - ADAM_B2 ** ADAM_STEP)
        d_ref[...] = -ADAM_LR * (m_hat / (jnp.sqrt(v_hat) + ADAM_EPS) + ADAM_WD * w_ref[...])
        nm_ref[...] = mn
        nv_ref[...] = vn

    blk = pl.BlockSpec((tr, cols), lambda i: (i, 0))
    out = jax.ShapeDtypeStruct((rows, cols), F32)
    outs = pl.pallas_call(
        body, name=name, out_shape=(out, out, out), grid=(rows // tr,),
        in_specs=[blk] * 4, out_specs=(blk, blk, blk), compiler_params=_cparams("parallel"),
    )(*[t.reshape(rows, cols) for t in (w, g, m, v)])
    return tuple(o.reshape(shape) for o in outs)


HBM = pl.BlockSpec(memory_space=pltpu.HBM)
MESH = pl.DeviceIdType.MESH


def _all_gather(blk, *, name):
    R, C = blk.shape

    def body(x_ref, out_ref, send_sems, recv_sems, local_sem):
        x, y, c = lax.axis_index("x"), lax.axis_index("y"), lax.axis_index("c")
        me, sibling = (x, y, c), (x, y, 1 - c)
        chips = [(1 - x, y), (x, 1 - y), (1 - x, 1 - y)]

        def slot(px, py, pc):
            return out_ref.at[4 * px + 2 * py + pc]

        def copy(k, block, to, src=None):
            return pltpu.make_async_remote_copy(
                src_ref=slot(*block) if src is None else src, dst_ref=slot(*block),
                send_sem=send_sems.at[k], recv_sem=recv_sems.at[k], device_id=to, device_id_type=MESH)

        mine = pltpu.make_async_copy(x_ref, slot(*me), local_sem)
        mine.start()
        first = [copy(0, me, sibling, src=x_ref)]
        first += [copy(1 + j, me, (*chip, c), src=x_ref) for j, chip in enumerate(chips)]
        for cp in first:
            cp.start()
        passed = [copy(4 + j, (*chip, c), sibling) for j, chip in enumerate(chips)]
        for j, chip in enumerate(chips):
            copy(1 + j, (*chip, c), me).wait_recv()
            passed[j].start()
        copy(0, sibling, me).wait_recv()
        for j, chip in enumerate(chips):
            copy(4 + j, (*chip, 1 - c), me).wait_recv()
        for cp in first + passed:
            cp.wait_send()
        mine.wait()

    return pl.pallas_call(
        body, name=name, out_shape=jax.ShapeDtypeStruct((N_DEV, R, C), blk.dtype),
        in_specs=[HBM], out_specs=HBM,
        scratch_shapes=[pltpu.SemaphoreType.DMA((7,)), pltpu.SemaphoreType.DMA((7,)), pltpu.SemaphoreType.DMA],
    )(blk)


def _all_to_all(blocks, *, name):
    _, R, C = blocks.shape

    def body(x_ref, out_ref, send_sems, recv_sems, local_sem):
        x, y, c = lax.axis_index("x"), lax.axis_index("y"), lax.axis_index("c")
        me = 4 * x + 2 * y + c
        mine = pltpu.make_async_copy(x_ref.at[me], out_ref.at[me], local_sem)
        mine.start()
        copies = []
        for k in range(1, N_DEV):
            px = x if not (k >> 2) & 1 else 1 - x
            py = y if not (k >> 1) & 1 else 1 - y
            pc = c if not k & 1 else 1 - c
            cp = pltpu.make_async_remote_copy(
                src_ref=x_ref.at[4 * px + 2 * py + pc], dst_ref=out_ref.at[me],
                send_sem=send_sems.at[k - 1], recv_sem=recv_sems.at[k - 1],
                device_id=(px, py, pc), device_id_type=MESH)
            cp.start()
            copies.append(cp)
        for cp in copies:
            cp.wait()
        mine.wait()

    return pl.pallas_call(
        body, name=name, out_shape=jax.ShapeDtypeStruct(blocks.shape, blocks.dtype),
        in_specs=[HBM], out_specs=HBM,
        scratch_shapes=[pltpu.SemaphoreType.DMA((7,)), pltpu.SemaphoreType.DMA((7,)), pltpu.SemaphoreType.DMA],
    )(blocks)


def _sum8(blocks, *, name, tr=PACK_ROWS):
    _, R, C = blocks.shape
    tr = _tile(R, tr, 16)

    def body(x_ref, o_ref):
        acc = x_ref[0].astype(F32)
        for i in range(1, N_DEV):
            acc = acc + x_ref[i].astype(F32)
        o_ref[...] = acc

    return pl.pallas_call(
        body, name=name, out_shape=jax.ShapeDtypeStruct((R, C), F32), grid=(R // tr,),
        in_specs=[pl.BlockSpec((N_DEV, tr, C), lambda i: (0, i, 0))],
        out_specs=pl.BlockSpec((tr, C), lambda i: (i, 0)), compiler_params=_cparams("parallel"),
    )(blocks)


def _pack_rows(flat_last):
    n = flat_last.shape[-1]
    unit = PACK_ROWS * PACK_COLS
    padded = -(-n // unit) * unit
    pad = [(0, 0)] * (flat_last.ndim - 1) + [(0, padded - n)]
    return jnp.pad(flat_last, pad).reshape(flat_last.shape[:-1] + (padded // PACK_COLS, PACK_COLS))


def _ssm_discretise(lre, lim, logdt, bre, bim):
    lam = lax.complex(lre, lim)
    dt = jnp.exp(logdt)[:, None]
    lam_bar = jnp.exp(lam * dt)
    b_bar = ((lam_bar - 1.0) / lam)[:, :, None] * lax.complex(bre, bim)
    return lam_bar.real, lam_bar.imag, b_bar.real, b_bar.imag


def _block_diag(a, rows_first):
    ns, g, r, c = a.shape
    eye = jnp.eye(g, dtype=a.dtype)
    return jnp.einsum("sgrc,gh->sgrhc", a, eye).reshape(ns, g * r, g * c)


def _diag_blocks(m, r, c):
    ns = m.shape[0]
    g = SLAB_GROUPS
    return jnp.einsum("sgrhc,gh->sgrc", m.reshape(ns, g, r, g, c), jnp.eye(g, dtype=m.dtype))


def _to_tm(a, T):
    b, s, w = a.shape
    return a.reshape(b, s // T, T, w).transpose(0, 2, 1, 3)


def _from_tm(a):
    b, t, nc, w = a.shape
    return a.transpose(0, 2, 1, 3).reshape(b, nc * t, w)


WEIGHTS = ["norm_mix", "w_in", "b_forget", "ssm_lambda_re", "ssm_lambda_im", "ssm_log_dt", "ssm_b_re", "ssm_b_im",
           "ssm_c_re", "ssm_c_im", "ssm_d", "w_glu", "b_glu", "w_branch_a", "w_branch_b", "w_out", "norm_mlp",
           "w_mlp_up", "w_mlp_down", "norm_final"]
SHARDED = {"w_in": 2, "w_glu": 1, "w_branch_a": 2, "w_branch_b": 2, "w_out": 1, "w_mlp_up": 2, "w_mlp_down": 1}


def _gather_weights(shards):
    names = list(SHARDED)
    flat = jnp.concatenate([shards[n].astype(BF16).reshape(-1) for n in names])
    got = _all_gather(_pack_rows(flat), name="gather_weights").reshape(N_DEV, -1)
    full, off = {}, 0
    for n in names:
        shp, ax = shards[n].shape, SHARDED[n]
        seg = got[:, off:off + shards[n].size].reshape((N_DEV,) + shp)
        off += shards[n].size
        seg = jnp.moveaxis(seg, 0, ax)
        full[n] = seg.reshape(shp[:ax] + (N_DEV * shp[ax],) + shp[ax + 1:])
    return full


def _scatter_grads(grads, shard_shapes):
    names = list(SHARDED)
    parts = []
    for n in names:
        shp, ax = shard_shapes[n], SHARDED[n]
        g = grads[n].reshape(shp[:ax] + (N_DEV, shp[ax]) + shp[ax + 1:])
        parts.append(jnp.moveaxis(g, ax, 0).reshape(N_DEV, -1))
    packed = _pack_rows(jnp.concatenate(parts, axis=1))
    summed = _sum8(_all_to_all(packed, name="exchange_grads"), name="sum_grads").reshape(-1)
    out, off = {}, 0
    for n in names:
        size = 1
        for s in shard_shapes[n]:
            size *= s
        out[n] = summed[off:off + size].reshape(shard_shapes[n])
        off += size
    return out


def _allreduce_small(grads):
    names = list(grads)
    flat = jnp.concatenate([grads[n].astype(F32).reshape(-1) for n in names])
    got = _all_gather(_pack_rows(flat), name="gather_small_grads")
    summed = _sum8(got, name="sum_small_grads").reshape(-1)
    out, off = {}, 0
    for n in names:
        out[n] = summed[off:off + grads[n].size].reshape(grads[n].shape)
        off += grads[n].size
    return out


def kernel(x, norm_mix, w_in, b_forget, ssm_lambda_re, ssm_lambda_im, ssm_log_dt, ssm_b_re, ssm_b_im, ssm_c_re, ssm_c_im, ssm_d, w_glu, b_glu, w_branch_a, w_branch_b, w_out, norm_mlp, w_mlp_up, w_mlp_down, norm_final, loss_target, m_norm_mix, m_w_in, m_b_forget, m_ssm_lambda_re, m_ssm_lambda_im, m_ssm_log_dt, m_ssm_b_re, m_ssm_b_im, m_ssm_c_re, m_ssm_c_im, m_ssm_d, m_w_glu, m_b_glu, m_w_branch_a, m_w_branch_b, m_w_out, m_norm_mlp, m_w_mlp_up, m_w_mlp_down, m_norm_final, v_norm_mix, v_w_in, v_b_forget, v_ssm_lambda_re, v_ssm_lambda_im, v_ssm_log_dt, v_ssm_b_re, v_ssm_b_im, v_ssm_c_re, v_ssm_c_im, v_ssm_d, v_w_glu, v_b_glu, v_w_branch_a, v_w_branch_b, v_w_out, v_norm_mlp, v_w_mlp_up, v_w_mlp_down, v_norm_final):
    args = dict(locals())
    w = {n: args[n] for n in WEIGHTS}
    Bl, S, D = x.shape
    L, H = b_forget.shape
    G, P, C = ssm_b_re.shape[1:]
    AW, W, HP = H * HEAD_DIM, G * C, H // 2
    N = Bl * S
    T = SSM_CHUNK
    NS = G // SLAB_GROUPS
    SP = SLAB_GROUPS * P
    tq = min(ATTN_BLOCK, S)
    nq = S // tq
    u_off = 3 * AW
    gate_blk = (u_off + W) // D
    assert (u_off + W) % D == 0 and W % LANES == 0 and AW % LANES == 0 and S % T == 0

    full = _gather_weights({n: w[n] for n in SHARDED})
    win = full["w_in"]
    wcat = jnp.concatenate([win[:, :, :3 * AW], win[:, :, 3 * AW + H:]], axis=2)
    wf = jnp.pad(win[:, :, 3 * AW:3 * AW + H], ((0, 0), (0, 0), (0, LANES - H)))
    wcat_t = jnp.swapaxes(jnp.concatenate([wcat, wf], axis=2), 1, 2)
    tr_ = lambda a: jnp.swapaxes(a, 1, 2)

    ssm = []
    for l in range(L):
        disc, disc_vjp = jax.vjp(_ssm_discretise, ssm_lambda_re[l], ssm_lambda_im[l], ssm_log_dt[l],
                                 ssm_b_re[l], ssm_b_im[l])
        lbr, lbi, bbr, bbi = disc
        z = lax.complex(ssm_lambda_re[l], ssm_lambda_im[l]) * jnp.exp(ssm_log_dt[l])[:, None]
        powers = jnp.exp(z[None] * jnp.arange(1, T + 1, dtype=F32)[:, None, None])
        slabs = lambda a: a.reshape(NS, SP)
        lam = jnp.stack([slabs(lbr), slabs(lbi)], axis=1)
        lam_t = jnp.stack([slabs(powers[T - 1].real), slabs(powers[T - 1].imag)], axis=1)
        pw = jnp.stack([powers.real.reshape(T, NS, SP), powers.imag.reshape(T, NS, SP)], axis=0).transpose(2, 0, 1, 3)
        to_rows = lambda a: jnp.swapaxes(a.reshape(NS, SLAB_GROUPS, P, C), 2, 3)
        bmat = jnp.concatenate([_block_diag(to_rows(bbr), True), _block_diag(to_rows(bbi), True)], axis=2)
        cre = ssm_c_re[l].reshape(NS, SLAB_GROUPS, C, P)
        cim = ssm_c_im[l].reshape(NS, SLAB_GROUPS, C, P)
        cmat_t = jnp.concatenate([_block_diag(cre, True), -_block_diag(cim, True)], axis=2)
        ssm.append(dict(vjp=disc_vjp, lam=lam, lam_t=lam_t, pw=pw, bmat=bmat.astype(BF16),
                        bmat_t=tr_(bmat).astype(BF16), cmat=tr_(cmat_t).astype(BF16), cmat_t=cmat_t.astype(BF16),
                        d=ssm_d[l].reshape(1, W)))

    xcur = x.reshape(N, D)
    saved = []
    for l in range(L):
        s_ = ssm[l]
        h, r0 = _rmsnorm_fwd(xcur, norm_mix[l], name="norm_mix_fwd")
        proj = _mm(h, wcat[l], name="in_proj")
        fl = _mm(h, wf[l], name="forget_proj", out_dtype=F32)
        ft = fl[:, :H].reshape(Bl, S, H).transpose(0, 2, 1)
        F = _fox_gate_fwd(ft, b_forget[l], name="forget_gate_fwd")
        fcol = F.reshape(Bl, HP, 2, S).transpose(0, 1, 3, 2)
        frow = F.reshape(Bl, HP, 2, nq, tq)
        proj3 = proj.reshape(Bl, S, -1)
        ya, lse = _attn_fwd(proj3, fcol, frow, name="attn_fwd", H=H, tq=tq)
        u_tm = _to_tm(proj3[:, :, u_off:u_off + W], T)
        ys = _from_tm(_ssm_fwd(u_tm, s_["bmat"], s_["cmat"], s_["lam"], s_["pw"], s_["lam_t"], s_["d"],
                               name="ssm_fwd")).reshape(N, W)
        yb2 = _glu_fwd(ys, full["w_glu"][l], b_glu[l], name="glu_fwd")
        ya2 = ya.reshape(N, AW)
        mixed, pa, pb = _merge_fwd(ya2, yb2, full["w_branch_a"][l], full["w_branch_b"][l], proj, gate_blk,
                                   name="merge_fwd")
        x1, h2, r1 = _outproj_fwd(mixed, full["w_out"][l], xcur, norm_mlp[l], name="out_proj")
        a = _mm(h2, full["w_mlp_up"][l], name="mlp_up")
        x2 = _mm(a, full["w_mlp_down"][l], name="mlp_down", a_fn=_relu_sq, epi=lambda acc, res: acc + res,
                 extras=(x1,), out_dtype=F32)
        saved.append(dict(x0=xcur, h=h, r0=r0, proj=proj, ft=ft, fcol=fcol, frow=frow, ya=ya, lse=lse, u_tm=u_tm,
                          ys=ys, yb2=yb2, mixed=mixed, pa=pa, pb=pb, x1=x1, h2=h2, r1=r1, a=a))
        xcur = x2

    dx, g_final, loss_row = _loss_head(xcur, norm_final, loss_target.reshape(N, D), name="loss_head")
    loss = lax.psum(loss_row[0, 0], MESH_AXES)

    big = {n: [None] * L for n in SHARDED}
    small = {n: [None] * L for n in WEIGHTS if n not in SHARDED and n != "norm_final"}
    for l in reversed(range(L)):
        sv, s_ = saved[l], ssm[l]
        a = sv["a"]
        d_a = _mm(dx, tr_(full["w_mlp_down"])[l], name="mlp_down_dx",
                  epi=lambda acc, av: acc * (2.0 * jnp.maximum(av.astype(F32), 0.0)), extras=(a,))
        big["w_mlp_down"][l] = _mm(a, dx, name="mlp_down_dw", ta=True, a_fn=_relu_sq)
        big["w_mlp_up"][l] = _mm(sv["h2"], d_a, name="mlp_up_dw", ta=True)
        dh2 = _mm(d_a, tr_(full["w_mlp_up"])[l], name="mlp_up_dx", out_dtype=F32)
        dx1, g = _rmsnorm_bwd(dh2, sv["x1"], sv["r1"], norm_mlp[l], dx, name="norm_mlp_bwd")
        small["norm_mlp"][l] = g[0]
        dmix = _mm(dx1, tr_(full["w_out"])[l], name="out_proj_dx")
        big["w_out"][l] = _mm(sv["mixed"], dx1, name="out_proj_dw", ta=True)
        dpa, dpb, dga, dgb = _merge_bwd(dmix, sv["proj"], sv["pa"], sv["pb"], gate_blk, name="merge_bwd")
        ya2 = sv["ya"].reshape(N, AW)
        big["w_branch_a"][l] = _mm(ya2, dpa, name="branch_a_dw", ta=True)
        dya = _mm(dpa, tr_(full["w_branch_a"])[l], name="branch_a_dx")
        big["w_branch_b"][l] = _mm(sv["yb2"], dpb, name="branch_b_dw", ta=True)
        dyb2 = _mm(dpb, tr_(full["w_branch_b"])[l], name="branch_b_dx")
        dys, dz, yb, g = _glu_bwd(sv["ys"], dyb2, full["w_glu"][l], tr_(full["w_glu"])[l], b_glu[l], name="glu_bwd")
        small["b_glu"][l] = g[0]
        big["w_glu"][l] = _mm(yb, dz, name="glu_dw", ta=True)

        du_tm, g_bt, g_ct, g_lam, g_d = _ssm_bwd(
            sv["u_tm"], _to_tm(dys.reshape(Bl, S, W), T), s_["bmat"], s_["bmat_t"], s_["cmat_t"], s_["lam"],
            s_["pw"], s_["lam_t"], s_["d"], name="ssm_bwd")
        du = _from_tm(du_tm).reshape(N, W)
        g_b = _diag_blocks(jnp.swapaxes(g_bt, 1, 2).reshape(NS, LANES, 2, SP).transpose(2, 0, 1, 3).reshape(
            2 * NS, LANES, SP), C, P).reshape(2, G, C, P)
        g_bbar = jnp.swapaxes(g_b, 2, 3)
        g_c = _diag_blocks(g_ct.reshape(NS, LANES, 2, SP).transpose(2, 0, 1, 3).reshape(2 * NS, LANES, SP),
                           C, P).reshape(2, G, C, P)
        g_lbar = g_lam.transpose(1, 0, 2).reshape(2, G, P)
        g_lre, g_lim, g_ldt, g_bre, g_bim = s_["vjp"]((g_lbar[0], g_lbar[1], g_bbar[0], g_bbar[1]))
        small["ssm_lambda_re"][l], small["ssm_lambda_im"][l], small["ssm_log_dt"][l] = g_lre, g_lim, g_ldt
        small["ssm_b_re"][l], small["ssm_b_im"][l] = g_bre, g_bim
        small["ssm_c_re"][l], small["ssm_c_im"][l] = g_c[0], -g_c[1]
        small["ssm_d"][l] = g_d.reshape(W)

        proj3 = sv["proj"].reshape(Bl, S, -1)
        dq, dk, dv, dfk, dfq = _attn_bwd(proj3, sv["ya"], dya.reshape(Bl, S, AW), sv["lse"], sv["fcol"],
                                         sv["frow"], name="attn_bwd", H=H, tq=tq)
        dF = dfk.reshape(Bl, H, S) + dfq.transpose(0, 1, 3, 2).reshape(Bl, H, S)
        dft, g = _fox_gate_bwd(dF, sv["ft"], b_forget[l], name="forget_gate_bwd")
        small["b_forget"][l] = g[:, 0]
        dfl = jnp.pad(dft.transpose(0, 2, 1).reshape(N, H), ((0, 0), (0, LANES - H))).astype(BF16)
        dproj = jnp.concatenate([dq.reshape(N, AW), dk.reshape(N, AW), dv.reshape(N, AW), du, dga, dgb, dfl], axis=1)
        gw = _mm(sv["h"], dproj, name="in_proj_dw", ta=True, tn=1408)
        ncat = wcat.shape[2]
        big["w_in"][l] = jnp.concatenate([gw[:, :3 * AW], gw[:, ncat:ncat + H], gw[:, 3 * AW:ncat]], axis=1)
        dh = _mm(dproj, wcat_t[l], name="in_proj_dx", out_dtype=F32, tk=1408)
        dx, g = _rmsnorm_bwd(dh, sv["x0"], sv["r0"], norm_mix[l], dx1, name="norm_mix_bwd")
        small["norm_mix"][l] = g[0]

    grads = _scatter_grads({n: jnp.stack(big[n]) for n in SHARDED}, {n: w[n].shape for n in SHARDED})
    small_stacked = {n: jnp.stack(small[n]) for n in small}
    small_stacked["norm_final"] = g_final[0]
    grads.update(_allreduce_small(small_stacked))

    deltas, new_m, new_v = {}, {}, {}
    for n in WEIGHTS:
        deltas[n], new_m[n], new_v[n] = _adamw(w[n], grads[n], args["m_" + n], args["v_" + n], name="adamw_" + n)
    return (loss, dx.reshape(Bl, S, D), *[grads[n] for n in WEIGHTS], *[deltas[n] for n in WEIGHTS],
            *[new_m[n] for n in WEIGHTS], *[new_v[n] for n in WEIGHTS])
```

```python
import functools

import jax
import jax.numpy as jnp
from jax import lax
from jax.experimental import pallas as pl
from jax.experimental.pallas import tpu as pltpu

F32 = jnp.float32
BF16 = jnp.bfloat16

N_DEV = 8
HEAD_DIM = 64
LANES = 128
SSM_CHUNK = 32
SLAB_GROUPS = 8
ATTN_BLOCK = 512
PACK_COLS = 1024
PACK_ROWS = 256
RMS_EPS = 1e-6
VMEM_LIMIT = 56 * 1024 * 1024
ADAM_LR, ADAM_B1, ADAM_B2, ADAM_EPS, ADAM_WD, ADAM_STEP = 0.001, 0.9, 0.999, 1e-08, 0.01, 10
MESH_AXES = ("x", "y", "c")
NEG = -1e30
NT = (((1,), (1,)), ((), ()))
TN = (((0,), (0,)), ((), ()))


def _cparams(*sem):
    return pltpu.CompilerParams(dimension_semantics=sem, vmem_limit_bytes=VMEM_LIMIT)


def _tile(dim, pref, unit=LANES):
    if dim <= pref:
        return dim
    best = None
    for t in range(unit, pref + 1, unit):
        if dim % t == 0:
            best = t
    assert best is not None, (dim, pref)
    return best


def _mm(a, b, *, name, ta=False, a_fn=None, epi=None, extras=(), out_dtype=BF16, tm=1024, tn=1024, tk=1024):
    if ta:
        K, M = a.shape
    else:
        M, K = a.shape
    Kb, N = b.shape
    assert K == Kb, (a.shape, b.shape)
    tm, tn, tk = _tile(M, tm), _tile(N, tn), _tile(K, tk)
    nk = K // tk
    ne = len(extras)

    def body(a_ref, b_ref, *rest):
        e_refs, o_ref, acc_ref = rest[:ne], rest[ne], rest[ne + 1]
        k = pl.program_id(2)
        av = a_ref[...]
        if a_fn is not None:
            av = a_fn(av)
        av = av.astype(BF16)
        bv = b_ref[...].astype(BF16)
        part = lax.dot_general(av, bv, TN if ta else (((1,), (0,)), ((), ())), preferred_element_type=F32)

        @pl.when(k == 0)
        def _():
            acc_ref[...] = part

        @pl.when(k > 0)
        def _():
            acc_ref[...] += part

        @pl.when(k == nk - 1)
        def _():
            r = acc_ref[...]
            if epi is not None:
                r = epi(r, *[e[...] for e in e_refs])
            o_ref[...] = r.astype(o_ref.dtype)

    a_spec = pl.BlockSpec((tk, tm), lambda i, j, k: (k, i)) if ta else pl.BlockSpec((tm, tk), lambda i, j, k: (i, k))
    return pl.pallas_call(
        body, name=name,
        out_shape=jax.ShapeDtypeStruct((M, N), out_dtype),
        grid=(M // tm, N // tn, nk),
        in_specs=[a_spec, pl.BlockSpec((tk, tn), lambda i, j, k: (k, j))]
        + [pl.BlockSpec((tm, tn), lambda i, j, k: (i, j)) for _ in extras],
        out_specs=pl.BlockSpec((tm, tn), lambda i, j, k: (i, j)),
        scratch_shapes=[pltpu.VMEM((tm, tn), F32)],
        compiler_params=_cparams("parallel", "parallel", "arbitrary"),
    )(a, b, *extras)


def _relu_sq(v):
    r = jnp.maximum(v.astype(F32), 0.0)
    return r * r


def _sigmoid(v):
    return 1.0 / (1.0 + jnp.exp(-v))


GELU_C = 0.7978845608028654
GELU_A = 0.044715


def _gelu(v):
    return 0.5 * v * (1.0 + jnp.tanh(GELU_C * (v + GELU_A * v * v * v)))


def _gelu_grad(v):
    t = jnp.tanh(GELU_C * (v + GELU_A * v * v * v))
    return 0.5 * (1.0 + t) + 0.5 * v * (1.0 - t * t) * GELU_C * (1.0 + 3.0 * GELU_A * v * v)


def _rmsnorm_fwd(x, g, *, name, tr=512):
    n, d = x.shape
    tr = _tile(n, tr, 8)

    def body(x_ref, g_ref, h_ref, r_ref):
        xv = x_ref[...]
        r = lax.rsqrt(jnp.mean(xv * xv, axis=-1, keepdims=True) + RMS_EPS)
        h_ref[...] = (xv * r * g_ref[...]).astype(BF16)
        r_ref[...] = r

    return pl.pallas_call(
        body, name=name,
        out_shape=(jax.ShapeDtypeStruct((n, d), BF16), jax.ShapeDtypeStruct((n, 1), F32)),
        grid=(n // tr,),
        in_specs=[pl.BlockSpec((tr, d), lambda i: (i, 0)), pl.BlockSpec((1, d), lambda i: (0, 0))],
        out_specs=(pl.BlockSpec((tr, d), lambda i: (i, 0)), pl.BlockSpec((tr, 1), lambda i: (i, 0))),
        compiler_params=_cparams("parallel"),
    )(x, g.reshape(1, d))


def _rmsnorm_bwd(dh, x, r, g, dres, *, name, tr=512):
    n, d = x.shape
    tr = _tile(n, tr, 8)

    def body(dh_ref, x_ref, r_ref, g_ref, dres_ref, dx_ref, dg_ref):
        i = pl.program_id(0)
        rv = r_ref[...]
        xh = x_ref[...] * rv
        dhv = dh_ref[...].astype(F32)
        dxh = dhv * g_ref[...]
        m = jnp.mean(dxh * xh, axis=-1, keepdims=True)
        dx_ref[...] = rv * (dxh - xh * m) + dres_ref[...]
        part = jnp.sum(dhv * xh, axis=0, keepdims=True)

        @pl.when(i == 0)
        def _():
            dg_ref[...] = part

        @pl.when(i > 0)
        def _():
            dg_ref[...] += part

    row = pl.BlockSpec((tr, d), lambda i: (i, 0))
    vec = pl.BlockSpec((1, d), lambda i: (0, 0))
    return pl.pallas_call(
        body, name=name,
        out_shape=(jax.ShapeDtypeStruct((n, d), F32), jax.ShapeDtypeStruct((1, d), F32)),
        grid=(n // tr,),
        in_specs=[row, row, pl.BlockSpec((tr, 1), lambda i: (i, 0)), vec, row],
        out_specs=(row, vec),
        compiler_params=_cparams("arbitrary"),
    )(dh, x, r, g.reshape(1, d), dres)


def _loss_head(x, g, target, *, name, tr=512):
    n, d = x.shape
    tr = _tile(n, tr, 8)

    def body(x_ref, g_ref, t_ref, dx_ref, dg_ref, loss_ref):
        i = pl.program_id(0)
        xv = x_ref[...]
        gv = g_ref[...]
        r = lax.rsqrt(jnp.mean(xv * xv, axis=-1, keepdims=True) + RMS_EPS)
        xh = xv * r
        err = xh * gv - t_ref[...]
        lpart = 0.5 * jnp.sum(jnp.mean(err * err, axis=-1, keepdims=True), axis=0, keepdims=True)
        dy = err * (1.0 / d)
        dxh = dy * gv
        m = jnp.mean(dxh * xh, axis=-1, keepdims=True)
        dx_ref[...] = r * (dxh - xh * m)
        gpart = jnp.sum(dy * xh, axis=0, keepdims=True)
        lrow = jnp.broadcast_to(lpart, (1, LANES))

        @pl.when(i == 0)
        def _():
            dg_ref[...] = gpart
            loss_ref[...] = lrow

        @pl.when(i > 0)
        def _():
            dg_ref[...] += gpart
            loss_ref[...] += lrow

    row = pl.BlockSpec((tr, d), lambda i: (i, 0))
    vec = pl.BlockSpec((1, d), lambda i: (0, 0))
    return pl.pallas_call(
        body, name=name,
        out_shape=(jax.ShapeDtypeStruct((n, d), F32), jax.ShapeDtypeStruct((1, d), F32),
                   jax.ShapeDtypeStruct((1, LANES), F32)),
        grid=(n // tr,),
        in_specs=[row, vec, row],
        out_specs=(row, vec, pl.BlockSpec((1, LANES), lambda i: (0, 0))),
        compiler_params=_cparams("arbitrary"),
    )(x, g.reshape(1, d), target)


def _tri_dot(v, tri):
    hi = v.astype(BF16)
    r1 = v - hi.astype(F32)
    mid = r1.astype(BF16)
    lo = (r1 - mid.astype(F32)).astype(BF16)
    d = functools.partial(jnp.dot, preferred_element_type=F32)
    return d(hi, tri) + d(mid, tri) + d(lo, tri)


def _fox_gate_fwd(ft, bf, *, name, blk=256):
    B, H, S = ft.shape
    blk = _tile(S, blk)
    nb = S // blk

    def body(f_ref, b_ref, o_ref):
        x = f_ref[0] + b_ref[...]
        logf = jnp.minimum(x, 0.0) - jnp.log(1.0 + jnp.exp(-jnp.abs(x)))
        rr = lax.broadcasted_iota(jnp.int32, (blk, blk), 0)
        cc = lax.broadcasted_iota(jnp.int32, (blk, blk), 1)
        tri = (rr <= cc).astype(BF16)
        carry = jnp.zeros((H, 1), F32)
        for n in range(nb):
            c = _tri_dot(logf[:, n * blk:(n + 1) * blk], tri) + carry
            o_ref[0, :, n * blk:(n + 1) * blk] = c
            carry = c[:, blk - 1:blk]

    return pl.pallas_call(
        body, name=name,
        out_shape=jax.ShapeDtypeStruct((B, H, S), F32),
        grid=(B,),
        in_specs=[pl.BlockSpec((1, H, S), lambda b: (b, 0, 0)), pl.BlockSpec((H, 1), lambda b: (0, 0))],
        out_specs=pl.BlockSpec((1, H, S), lambda b: (b, 0, 0)),
        compiler_params=_cparams("parallel"),
    )(ft, bf.reshape(H, 1))


def _fox_gate_bwd(dF, ft, bf, *, name, blk=256):
    B, H, S = ft.shape
    blk = _tile(S, blk)
    nb = S // blk

    def body(d_ref, f_ref, b_ref, o_ref, db_ref):
        b = pl.program_id(0)
        x = f_ref[0] + b_ref[...]
        sneg = 1.0 / (1.0 + jnp.exp(x))
        dv = d_ref[0]
        rr = lax.broadcasted_iota(jnp.int32, (blk, blk), 0)
        cc = lax.broadcasted_iota(jnp.int32, (blk, blk), 1)
        tri = (rr >= cc).astype(BF16)
        carry = jnp.zeros((H, 1), F32)
        tot = jnp.zeros((H, 1), F32)
        for n in reversed(range(nb)):
            sl = slice(n * blk, (n + 1) * blk)
            c = _tri_dot(dv[:, sl], tri) + carry
            g = c * sneg[:, sl]
            o_ref[0, :, sl] = g
            tot = tot + jnp.sum(g, axis=1, keepdims=True)
            carry = c[:, 0:1]

        @pl.when(b == 0)
        def _():
            db_ref[...] = tot

        @pl.when(b > 0)
        def _():
            db_ref[...] += tot

    blkspec = pl.BlockSpec((1, H, S), lambda b: (b, 0, 0))
    return pl.pallas_call(
        body, name=name,
        out_shape=(jax.ShapeDtypeStruct((B, H, S), F32), jax.ShapeDtypeStruct((H, 1), F32)),
        grid=(B,),
        in_specs=[blkspec, blkspec, pl.BlockSpec((H, 1), lambda b: (0, 0))],
        out_specs=(blkspec, pl.BlockSpec((H, 1), lambda b: (0, 0))),
        compiler_params=_cparams("arbitrary"),
    )(dF, ft, bf.reshape(H, 1))


def _head_masks():
    lane = lax.broadcasted_iota(jnp.int32, (1, LANES), 1)
    return [lane < HEAD_DIM, lane >= HEAD_DIM]


def _stack_heads(x, masks):
    zero = jnp.zeros_like(x)
    return jnp.concatenate([jnp.where(masks[0], x, zero), jnp.where(masks[1], x, zero)], axis=0)


def _attn_fwd(proj, frow, *, name, H, tq):
    B, S, _ = proj.shape
    HP = H // 2
    nq = S // tq
    scale = HEAD_DIM ** -0.5

    def body(q_ref, k_ref, v_ref, fk_ref, o_ref, lse_ref):
        i = pl.program_id(2)
        masks = _head_masks()
        q2 = _stack_heads(q_ref[0], masks) * jnp.asarray(scale, BF16)
        rr = lax.broadcasted_iota(jnp.int32, (tq, tq), 0)
        cc = lax.broadcasted_iota(jnp.int32, (tq, tq), 1)
        causal = rr >= cc

        def block(j, carry, masked):
            rows = pl.ds(pl.multiple_of(j * tq, tq), tq)
            kj = k_ref[0, rows, :]
            vj = v_ref[0, rows, :]
            s2 = lax.dot_general(q2, kj, NT, preferred_element_type=F32)
            new, ps = [], []
            for h in range(2):
                m, l, acc = carry[h]
                s = s2[h * tq:(h + 1) * tq] - fk_ref[0, 0, h, pl.ds(j, 1), :]
                if masked:
                    s = jnp.where(causal, s, NEG)
                m_new = jnp.maximum(m, jnp.max(s, axis=-1, keepdims=True))
                alpha = jnp.exp(m - m_new)
                p = jnp.exp(s - m_new)
                new.append((m_new, alpha * l + jnp.sum(p, axis=-1, keepdims=True), alpha, acc))
                ps.append(p.astype(BF16))
            pv = jnp.dot(jnp.concatenate(ps, axis=0), vj, preferred_element_type=F32)
            return tuple((m, l, alpha * acc + pv[h * tq:(h + 1) * tq]) for h, (m, l, alpha, acc) in enumerate(new))

        one = (jnp.full((tq, 1), NEG, F32), jnp.zeros((tq, 1), F32), jnp.zeros((tq, LANES), F32))
        carry = lax.fori_loop(0, i, lambda j, c: block(j, c, False), (one, one))
        (m0, l0, a0), (m1, l1, a1) = block(i, carry, True)
        o_ref[0] = jnp.where(masks[0], a0 / l0, a1 / l1).astype(BF16)
        two = lax.broadcasted_iota(jnp.int32, (1, 2), 1)
        lse_ref[0, 0] = jnp.where(two == 0, m0 + jnp.log(l0), m1 + jnp.log(l1))

    kv = lambda off: pl.BlockSpec((1, S, LANES), lambda b, hp, i: (b, 0, off + hp))
    return pl.pallas_call(
        body, name=name,
        out_shape=(jax.ShapeDtypeStruct((B, S, H * HEAD_DIM), BF16), jax.ShapeDtypeStruct((B, HP, S, 2), F32)),
        grid=(B, HP, nq),
        in_specs=[pl.BlockSpec((1, tq, LANES), lambda b, hp, i: (b, i, hp)), kv(HP), kv(2 * HP),
                  pl.BlockSpec((1, 1, 2, nq, tq), lambda b, hp, i: (b, hp, 0, 0, 0))],
        out_specs=(pl.BlockSpec((1, tq, LANES), lambda b, hp, i: (b, i, hp)),
                   pl.BlockSpec((1, 1, tq, 2), lambda b, hp, i: (b, hp, i, 0))),
        compiler_params=_cparams("parallel", "parallel", "arbitrary"),
    )(proj, proj, proj, frow)


def _attn_bwd(proj, ya, dya, lse, frow, *, name, H, tq):
    B, S, _ = proj.shape
    HP = H // 2
    nq = S // tq
    AW = H * HEAD_DIM
    scale = HEAD_DIM ** -0.5

    def body(q_ref, k_ref, v_ref, o_ref, do_ref, lse_ref, fk_ref, dq_ref, dk_ref, dv_ref, dfk_ref, dfq_ref,
             q2_ref, do2_ref, lse2_ref, delta2_ref, dq2_acc, dfq2_acc, dk_acc, dv_acc, dfk_acc):
        masks = _head_masks()
        rr = lax.broadcasted_iota(jnp.int32, (tq, tq), 0)
        cc = lax.broadcasted_iota(jnp.int32, (tq, tq), 1)
        causal = rr >= cc
        sc = jnp.asarray(scale, BF16)

        def stage(i, c):
            rows = pl.ds(pl.multiple_of(i * tq, tq), tq)
            dov = do_ref[0, rows, :]
            q2_ref[i] = _stack_heads(q_ref[0, rows, :], masks) * sc
            do2_ref[i] = _stack_heads(dov, masks)
            prod = dov.astype(F32) * o_ref[0, rows, :].astype(F32)
            delta2_ref[i] = jnp.concatenate(
                [jnp.sum(jnp.where(masks[h], prod, 0.0), axis=-1, keepdims=True) for h in range(2)], axis=0)
            lv = lse_ref[0, 0, rows, :]
            lse2_ref[i] = jnp.concatenate([lv[:, 0:1], lv[:, 1:2]], axis=0)
            return c

        lax.fori_loop(0, nq, stage, 0)
        dq2_acc[...] = jnp.zeros_like(dq2_acc)
        dfq2_acc[...] = jnp.zeros_like(dfq2_acc)

        def kv_block(j, carry):
            rows_j = pl.ds(pl.multiple_of(j * tq, tq), tq)
            kj = k_ref[0, rows_j, :]
            vj = v_ref[0, rows_j, :]
            ks = kj * sc
            dk_acc[...] = jnp.zeros_like(dk_acc)
            dv_acc[...] = jnp.zeros_like(dv_acc)
            dfk_acc[...] = jnp.zeros_like(dfk_acc)

            def logits(i):
                return (lax.dot_general(q2_ref[i], kj, NT, preferred_element_type=F32),
                        lax.dot_general(do2_ref[i], vj, NT, preferred_element_type=F32))

            def probs(i, s2, dp2, masked):
                lse2 = lse2_ref[i]
                delta2 = delta2_ref[i]
                ps, dss = [], []
                for h in range(2):
                    half = slice(h * tq, (h + 1) * tq)
                    p = jnp.exp(s2[half] - fk_ref[0, 0, h, pl.ds(j, 1), :] - lse2[half])
                    if masked:
                        p = jnp.where(causal, p, 0.0)
                    ds = p * (dp2[half] - delta2[half])
                    dfk_acc[h:h + 1, :] -= jnp.sum(ds, axis=0, keepdims=True)
                    dfq2_acc[i, half, :] += jnp.sum(ds, axis=1, keepdims=True)
                    ps.append(p.astype(BF16))
                    dss.append(ds.astype(BF16))
                return jnp.concatenate(ps, axis=0), jnp.concatenate(dss, axis=0)

            def grads(i, p2, ds2):
                dv_acc[...] += lax.dot_general(p2, do2_ref[i], TN, preferred_element_type=F32)
                dk_acc[...] += lax.dot_general(ds2, q2_ref[i], TN, preferred_element_type=F32)
                dq2_acc[i] += jnp.dot(ds2, ks, preferred_element_type=F32)

            grads(j, *probs(j, *logits(j), True))

            def rest(i, c):
                grads(i, *probs(i, *logits(i), False))
                return c

            lax.fori_loop(j + 1, nq, rest, 0)
            dk_ref[0, rows_j, :] = dk_acc[...].astype(BF16)
            dv_ref[0, rows_j, :] = dv_acc[...].astype(BF16)
            for h in range(2):
                dfk_ref[0, 0, h, pl.ds(j, 1), :] = dfk_acc[h:h + 1, :]
            return carry

        lax.fori_loop(0, nq, kv_block, 0)
        two = lax.broadcasted_iota(jnp.int32, (1, 2), 1)

        def finish(i, c):
            rows = pl.ds(pl.multiple_of(i * tq, tq), tq)
            dq2 = dq2_acc[i]
            dq_ref[0, rows, :] = jnp.where(masks[0], dq2[:tq], dq2[tq:]).astype(BF16)
            dfq2 = dfq2_acc[i]
            dfq_ref[0, 0, rows, :] = jnp.where(two == 0, dfq2[:tq], dfq2[tq:])
            return c

        lax.fori_loop(0, nq, finish, 0)

    col = lambda off: pl.BlockSpec((1, S, LANES), lambda b, hp: (b, 0, off + hp))
    stat = pl.BlockSpec((1, 1, S, 2), lambda b, hp: (b, hp, 0, 0))
    rowf = pl.BlockSpec((1, 1, 2, nq, tq), lambda b, hp: (b, hp, 0, 0, 0))
    grad = jax.ShapeDtypeStruct((B, S, AW), BF16)
    return pl.pallas_call(
        body, name=name,
        out_shape=(grad, grad, grad, jax.ShapeDtypeStruct((B, HP, 2, nq, tq), F32),
                   jax.ShapeDtypeStruct((B, HP, S, 2), F32)),
        grid=(B, HP),
        in_specs=[col(0), col(HP), col(2 * HP), col(0), col(0), stat, rowf],
        out_specs=(col(0), col(0), col(0), rowf, stat),
        scratch_shapes=[pltpu.VMEM((nq, 2 * tq, LANES), BF16), pltpu.VMEM((nq, 2 * tq, LANES), BF16),
                        pltpu.VMEM((nq, 2 * tq, 1), F32), pltpu.VMEM((nq, 2 * tq, 1), F32),
                        pltpu.VMEM((nq, 2 * tq, LANES), F32), pltpu.VMEM((nq, 2 * tq, 1), F32),
                        pltpu.VMEM((tq, LANES), F32), pltpu.VMEM((tq, LANES), F32), pltpu.VMEM((2, tq), F32)],
        compiler_params=_cparams("parallel", "parallel"),
    )(proj, proj, proj, ya, dya, lse, frow)


def _attn_fwd_old(proj, fcol, frow, *, name, H, tq):
    B, S, _ = proj.shape
    HP = H // 2
    nq = S // tq
    scale = HEAD_DIM ** -0.5

    def body(q_ref, k_ref, v_ref, fq_ref, fk_ref, o_ref, lse_ref):
        i = pl.program_id(2)
        q = q_ref[0]
        masks = _head_masks()
        rr = lax.broadcasted_iota(jnp.int32, (tq, tq), 0)
        cc = lax.broadcasted_iota(jnp.int32, (tq, tq), 1)
        causal = rr >= cc
        outs, lses = [], []
        for h in range(2):
            qm = jnp.where(masks[h], q, jnp.zeros_like(q)) * jnp.asarray(scale, BF16)
            fq = fq_ref[0, 0][:, h:h + 1]

            def block(j, carry, masked, h=h, qm=qm, fq=fq):
                m, l, acc = carry
                rows = pl.ds(pl.multiple_of(j * tq, tq), tq)
                kj = k_ref[0, rows, :]
                vj = v_ref[0, rows, :]
                s = lax.dot_general(qm, kj, NT, preferred_element_type=F32)
                s = s + (fq - fk_ref[0, 0, h, pl.ds(j, 1), :])
                if masked:
                    s = jnp.where(causal, s, NEG)
                m_new = jnp.maximum(m, jnp.max(s, axis=-1, keepdims=True))
                alpha = jnp.exp(m - m_new)
                p = jnp.exp(s - m_new)
                l = alpha * l + jnp.sum(p, axis=-1, keepdims=True)
                acc = alpha * acc + jnp.dot(p.astype(BF16), vj, preferred_element_type=F32)
                return m_new, l, acc

            init = (jnp.full((tq, 1), NEG, F32), jnp.zeros((tq, 1), F32), jnp.zeros((tq, LANES), F32))
            carry = lax.fori_loop(0, i, lambda j, c, block=block: block(j, c, False), init)
            m, l, acc = block(i, carry, True)
            outs.append(acc / l)
            lses.append(m + jnp.log(l))
        o_ref[0] = jnp.where(masks[0], outs[0], outs[1]).astype(BF16)
        two = lax.broadcasted_iota(jnp.int32, (1, 2), 1)
        lse_ref[0, 0] = jnp.where(two == 0, lses[0], lses[1])

    kv = lambda off: pl.BlockSpec((1, S, LANES), lambda b, hp, i: (b, 0, off + hp))
    return pl.pallas_call(
        body, name=name,
        out_shape=(jax.ShapeDtypeStruct((B, S, H * HEAD_DIM), BF16), jax.ShapeDtypeStruct((B, HP, S, 2), F32)),
        grid=(B, HP, nq),
        in_specs=[pl.BlockSpec((1, tq, LANES), lambda b, hp, i: (b, i, hp)), kv(HP), kv(2 * HP),
                  pl.BlockSpec((1, 1, tq, 2), lambda b, hp, i: (b, hp, i, 0)),
                  pl.BlockSpec((1, 1, 2, nq, tq), lambda b, hp, i: (b, hp, 0, 0, 0))],
        out_specs=(pl.BlockSpec((1, tq, LANES), lambda b, hp, i: (b, i, hp)),
                   pl.BlockSpec((1, 1, tq, 2), lambda b, hp, i: (b, hp, i, 0))),
        compiler_params=_cparams("parallel", "parallel", "arbitrary"),
    )(proj, proj, proj, fcol, frow)


def _attn_bwd_old(proj, ya, dya, lse, fcol, frow, *, name, H, tq):
    B, S, _ = proj.shape
    HP = H // 2
    nq = S // tq
    AW = H * HEAD_DIM
    scale = HEAD_DIM ** -0.5

    def body(q_ref, k_ref, v_ref, o_ref, do_ref, lse_ref, fq_ref, fk_ref,
             dq_ref, dk_ref, dv_ref, dfk_ref, dfq_ref, dq_acc, dk_acc, dv_acc, delta_ref, dfk_acc, dfq_acc):
        masks = _head_masks()
        rr = lax.broadcasted_iota(jnp.int32, (tq, tq), 0)
        cc = lax.broadcasted_iota(jnp.int32, (tq, tq), 1)
        causal = rr >= cc
        sc = jnp.asarray(scale, BF16)
        prod = do_ref[0].astype(F32) * o_ref[0].astype(F32)
        for h in range(2):
            delta_ref[h] = jnp.sum(jnp.where(masks[h], prod, 0.0), axis=-1, keepdims=True)
        dq_acc[...] = jnp.zeros_like(dq_acc)
        dfq_acc[...] = jnp.zeros_like(dfq_acc)

        def kv_block(j, carry):
            rows_j = pl.ds(pl.multiple_of(j * tq, tq), tq)
            kj = k_ref[0, rows_j, :]
            vj = v_ref[0, rows_j, :]
            dk_acc[...] = jnp.zeros_like(dk_acc)
            dv_acc[...] = jnp.zeros_like(dv_acc)
            dfk_acc[...] = jnp.zeros_like(dfk_acc)
            kms = [jnp.where(masks[h], kj, jnp.zeros_like(kj)) * sc for h in range(2)]

            def q_block(i, masked):
                rows_i = pl.ds(pl.multiple_of(i * tq, tq), tq)
                qi = q_ref[0, rows_i, :]
                doi = do_ref[0, rows_i, :]
                fqi = fq_ref[0, 0, rows_i, :]
                lsei = lse_ref[0, 0, rows_i, :]
                for h in range(2):
                    qm = jnp.where(masks[h], qi, jnp.zeros_like(qi)) * sc
                    dom = jnp.where(masks[h], doi, jnp.zeros_like(doi))
                    s = lax.dot_general(qm, kj, NT, preferred_element_type=F32)
                    s = s + (fqi[:, h:h + 1] - fk_ref[0, 0, h, pl.ds(j, 1), :])
                    p = jnp.exp(s - lsei[:, h:h + 1])
                    if masked:
                        p = jnp.where(causal, p, 0.0)
                    dp = lax.dot_general(dom, vj, NT, preferred_element_type=F32)
                    ds = p * (dp - delta_ref[h, rows_i, :])
                    pb, dsb = p.astype(BF16), ds.astype(BF16)
                    dv_acc[...] += lax.dot_general(pb, dom, TN, preferred_element_type=F32)
                    dk_acc[...] += lax.dot_general(dsb, qm, TN, preferred_element_type=F32)
                    dq_acc[rows_i, :] += jnp.dot(dsb, kms[h], preferred_element_type=F32)
                    dfk_acc[h:h + 1, :] -= jnp.sum(ds, axis=0, keepdims=True)
                    dfq_acc[h, rows_i, :] += jnp.sum(ds, axis=1, keepdims=True)

            q_block(j, True)

            def rest(i, c):
                q_block(i, False)
                return c

            lax.fori_loop(j + 1, nq, rest, 0)
            dk_ref[0, rows_j, :] = dk_acc[...].astype(BF16)
            dv_ref[0, rows_j, :] = dv_acc[...].astype(BF16)
            for h in range(2):
                dfk_ref[0, 0, h, pl.ds(j, 1), :] = dfk_acc[h:h + 1, :]
            return carry

        lax.fori_loop(0, nq, kv_block, 0)
        dq_ref[0] = dq_acc[...].astype(BF16)
        two = lax.broadcasted_iota(jnp.int32, (1, 2), 1)
        dfq_ref[0, 0] = jnp.where(two == 0, dfq_acc[0], dfq_acc[1])

    col = lambda off: pl.BlockSpec((1, S, LANES), lambda b, hp: (b, 0, off + hp))
    stat = pl.BlockSpec((1, 1, S, 2), lambda b, hp: (b, hp, 0, 0))
    rowf = pl.BlockSpec((1, 1, 2, nq, tq), lambda b, hp: (b, hp, 0, 0, 0))
    grad = jax.ShapeDtypeStruct((B, S, AW), BF16)
    return pl.pallas_call(
        body, name=name,
        out_shape=(grad, grad, grad, jax.ShapeDtypeStruct((B, HP, 2, nq, tq), F32),
                   jax.ShapeDtypeStruct((B, HP, S, 2), F32)),
        grid=(B, HP),
        in_specs=[col(0), col(HP), col(2 * HP), col(0), col(0), stat, stat, rowf],
        out_specs=(col(0), col(0), col(0), rowf, stat),
        scratch_shapes=[pltpu.VMEM((S, LANES), F32), pltpu.VMEM((tq, LANES), F32), pltpu.VMEM((tq, LANES), F32),
                        pltpu.VMEM((2, S, 1), F32), pltpu.VMEM((2, tq), F32), pltpu.VMEM((2, S, 1), F32)],
        compiler_params=_cparams("parallel", "parallel"),
    )(proj, proj, proj, ya, dya, lse, fcol, frow)


def _cmul(ar, ai, br, bi):
    return ar * br - ai * bi, ar * bi + ai * br


def _ssm_states(u_ref, bm, lam_ref, pw_ref, lamT_ref, hr_ref, hi_ref, inr_ref, ini_ref, T, NC, SP):
    lr, li = lam_ref[0, 0:1, :], lam_ref[0, 1:2, :]
    bu = jnp.dot(u_ref[0, 0], bm, preferred_element_type=F32)
    hr_ref[0] = bu[:, :SP]
    hi_ref[0] = bu[:, SP:]

    def step(t, c):
        bu = jnp.dot(u_ref[0, t], bm, preferred_element_type=F32)
        pr, pi = _cmul(hr_ref[t - 1], hi_ref[t - 1], lr, li)
        hr_ref[t] = pr + bu[:, :SP]
        hi_ref[t] = pi + bu[:, SP:]
        return c

    lax.fori_loop(1, T, step, 0)

    tr, ti = lamT_ref[0, 0:1, :], lamT_ref[0, 1:2, :]
    inr_ref[0:1, :] = jnp.zeros((1, SP), F32)
    ini_ref[0:1, :] = jnp.zeros((1, SP), F32)

    def chunk(n, c):
        prev = pl.ds(n - 1, 1)
        pr, pi = _cmul(inr_ref[prev, :], ini_ref[prev, :], tr, ti)
        inr_ref[pl.ds(n, 1), :] = pr + hr_ref[T - 1, prev, :]
        ini_ref[pl.ds(n, 1), :] = pi + hi_ref[T - 1, prev, :]
        return c

    lax.fori_loop(1, NC, chunk, 0)

    def fix(t, c):
        cr, ci = _cmul(inr_ref[...], ini_ref[...], pw_ref[0, 0, pl.ds(t, 1), :], pw_ref[0, 1, pl.ds(t, 1), :])
        hr_ref[t] += cr
        hi_ref[t] += ci
        return c

    lax.fori_loop(0, T, fix, 0)


def _ssm_fwd(u_tm, bmat, cmat, lam, pw, lamT, dskip, *, name):
    B, T, NC, W = u_tm.shape
    NS = W // LANES
    SP = bmat.shape[2] // 2

    def body(u_ref, b_ref, c_ref, lam_ref, pw_ref, lamT_ref, d_ref, y_ref, hr_ref, hi_ref, inr_ref, ini_ref):
        _ssm_states(u_ref, b_ref[0], lam_ref, pw_ref, lamT_ref, hr_ref, hi_ref, inr_ref, ini_ref, T, NC, SP)
        cm = c_ref[0]
        dv = d_ref[...]

        def out(t, c):
            hcat = jnp.concatenate([hr_ref[t], hi_ref[t]], axis=1).astype(BF16)
            y_ref[0, t] = jnp.dot(hcat, cm, preferred_element_type=F32) + dv * u_ref[0, t].astype(F32)
            return c

        lax.fori_loop(0, T, out, 0)

    slab = lambda *shape: pl.BlockSpec((1,) + shape, lambda b, s: (s,) + (0,) * len(shape))
    tok = pl.BlockSpec((1, T, NC, LANES), lambda b, s: (b, 0, 0, s))
    return pl.pallas_call(
        body, name=name,
        out_shape=jax.ShapeDtypeStruct((B, T, NC, W), F32),
        grid=(B, NS),
        in_specs=[tok, slab(LANES, 2 * SP), slab(2 * SP, LANES), slab(2, SP), slab(2, T, SP), slab(2, SP),
                  pl.BlockSpec((1, LANES), lambda b, s: (0, s))],
        out_specs=tok,
        scratch_shapes=[pltpu.VMEM((T, NC, SP), F32), pltpu.VMEM((T, NC, SP), F32),
                        pltpu.VMEM((NC, SP), F32), pltpu.VMEM((NC, SP), F32)],
        compiler_params=_cparams("parallel", "parallel"),
    )(u_tm, bmat, cmat, lam, pw, lamT, dskip)


def _ssm_bwd(u_tm, dy_tm, bmat, bmat_t, cmat_t, lam, pw, lamT, dskip, *, name):
    B, T, NC, W = u_tm.shape
    NS = W // LANES
    SP = bmat.shape[2] // 2

    def body(u_ref, dy_ref, b_ref, bt_ref, ct_ref, lam_ref, pw_ref, lamT_ref, d_ref,
             du_ref, gb_ref, gc_ref, glam_ref, gd_ref,
             hr_ref, hi_ref, ar_ref, ai_ref, inr_ref, ini_ref, anr_ref, ani_ref, glr_ref, gli_ref):
        b = pl.program_id(1)
        _ssm_states(u_ref, b_ref[0], lam_ref, pw_ref, lamT_ref, hr_ref, hi_ref, inr_ref, ini_ref, T, NC, SP)
        lr, li = lam_ref[0, 0:1, :], lam_ref[0, 1:2, :]
        ct = ct_ref[0]
        bt = bt_ref[0]
        dv = d_ref[...]

        gh = jnp.dot(dy_ref[0, T - 1].astype(BF16), ct, preferred_element_type=F32)
        ar_ref[T - 1] = gh[:, :SP]
        ai_ref[T - 1] = gh[:, SP:]

        def back(k, c):
            t = T - 2 - k
            gh = jnp.dot(dy_ref[0, t].astype(BF16), ct, preferred_element_type=F32)
            pr, pi = _cmul(ar_ref[t + 1], ai_ref[t + 1], lr, -li)
            ar_ref[t] = pr + gh[:, :SP]
            ai_ref[t] = pi + gh[:, SP:]
            return c

        lax.fori_loop(0, T - 1, back, 0)

        tr, ti = lamT_ref[0, 0:1, :], lamT_ref[0, 1:2, :]
        anr_ref[NC - 1:NC, :] = jnp.zeros((1, SP), F32)
        ani_ref[NC - 1:NC, :] = jnp.zeros((1, SP), F32)

        def chunk(k, c):
            n = NC - 2 - k
            nxt = pl.ds(n + 1, 1)
            pr, pi = _cmul(anr_ref[nxt, :], ani_ref[nxt, :], tr, -ti)
            anr_ref[pl.ds(n, 1), :] = pr + ar_ref[0, nxt, :]
            ani_ref[pl.ds(n, 1), :] = pi + ai_ref[0, nxt, :]
            return c

        lax.fori_loop(0, NC - 1, chunk, 0)

        @pl.when(b == 0)
        def _():
            gb_ref[...] = jnp.zeros_like(gb_ref)
            gc_ref[...] = jnp.zeros_like(gc_ref)
            glam_ref[...] = jnp.zeros_like(glam_ref)
            gd_ref[...] = jnp.zeros_like(gd_ref)

        glr_ref[...] = jnp.zeros_like(glr_ref)
        gli_ref[...] = jnp.zeros_like(gli_ref)

        def final(t, hpr, hpi):
            back_pow = pl.ds(T - 1 - t, 1)
            cr, ci = _cmul(anr_ref[...], ani_ref[...], pw_ref[0, 0, back_pow, :], -pw_ref[0, 1, back_pow, :])
            a_r = ar_ref[t] + cr
            a_i = ai_ref[t] + ci
            glr_ref[...] += a_r * hpr + a_i * hpi
            gli_ref[...] += a_i * hpr - a_r * hpi
            acat = jnp.concatenate([a_r, a_i], axis=1).astype(BF16)
            ut = u_ref[0, t]
            dyt = dy_ref[0, t]
            du_ref[0, t] = (jnp.dot(acat, bt, preferred_element_type=F32) + dv * dyt).astype(BF16)
            gb_ref[0] += lax.dot_general(acat, ut, TN, preferred_element_type=F32)
            hcat = jnp.concatenate([hr_ref[t], hi_ref[t]], axis=1).astype(BF16)
            gc_ref[0] += lax.dot_general(dyt.astype(BF16), hcat, TN, preferred_element_type=F32)
            gd_ref[0] += jnp.sum(dyt * ut.astype(F32), axis=0, keepdims=True)

        final(0, inr_ref[...], ini_ref[...])

        def rest(t, c):
            final(t, hr_ref[t - 1], hi_ref[t - 1])
            return c

        lax.fori_loop(1, T, rest, 0)
        glam_ref[0, 0:1, :] += jnp.sum(glr_ref[...], axis=0, keepdims=True)
        glam_ref[0, 1:2, :] += jnp.sum(gli_ref[...], axis=0, keepdims=True)

    slab = lambda *shape: pl.BlockSpec((1,) + shape, lambda s, b: (s,) + (0,) * len(shape))
    tok = pl.BlockSpec((1, T, NC, LANES), lambda s, b: (b, 0, 0, s))
    big = pltpu.VMEM((T, NC, SP), F32)
    small = pltpu.VMEM((NC, SP), F32)
    return pl.pallas_call(
        body, name=name,
        out_shape=(jax.ShapeDtypeStruct((B, T, NC, W), BF16),
                   jax.ShapeDtypeStruct((NS, 2 * SP, LANES), F32), jax.ShapeDtypeStruct((NS, LANES, 2 * SP), F32),
                   jax.ShapeDtypeStruct((NS, 2, SP), F32), jax.ShapeDtypeStruct((NS, 1, LANES), F32)),
        grid=(NS, B),
        in_specs=[tok, tok, slab(LANES, 2 * SP), slab(2 * SP, LANES), slab(LANES, 2 * SP), slab(2, SP),
                  slab(2, T, SP), slab(2, SP), pl.BlockSpec((1, LANES), lambda s, b: (0, s))],
        out_specs=(tok, slab(2 * SP, LANES), slab(LANES, 2 * SP), slab(2, SP), slab(1, LANES)),
        scratch_shapes=[big, big, big, big, small, small, small, small, small, small],
        compiler_params=_cparams("parallel", "arbitrary"),
    )(u_tm, dy_tm, bmat, bmat_t, cmat_t, lam, pw, lamT, dskip)


def _glu_fwd(ys, w, b, *, name, tr=512):
    n, wd = ys.shape
    tr = _tile(n, tr, 8)

    def body(y_ref, w_ref, b_ref, o_ref):
        yb = _gelu(y_ref[...])
        z = jnp.dot(yb.astype(BF16), w_ref[...], preferred_element_type=F32) + b_ref[...]
        o_ref[...] = (yb * _sigmoid(z)).astype(BF16)

    row = pl.BlockSpec((tr, wd), lambda i: (i, 0))
    return pl.pallas_call(
        body, name=name, out_shape=jax.ShapeDtypeStruct((n, wd), BF16), grid=(n // tr,),
        in_specs=[row, pl.BlockSpec((wd, wd), lambda i: (0, 0)), pl.BlockSpec((1, wd), lambda i: (0, 0))],
        out_specs=row, compiler_params=_cparams("parallel"),
    )(ys, w, b.reshape(1, wd))


def _glu_bwd(ys, dyb2, w, w_t, b, *, name, tr=512):
    n, wd = ys.shape
    tr = _tile(n, tr, 8)

    def body(y_ref, d_ref, w_ref, wt_ref, b_ref, dys_ref, dz_ref, yb_ref, db_ref):
        i = pl.program_id(0)
        yv = y_ref[...]
        yb = _gelu(yv)
        ybb = yb.astype(BF16)
        sg = _sigmoid(jnp.dot(ybb, w_ref[...], preferred_element_type=F32) + b_ref[...])
        dv = d_ref[...].astype(F32)
        dz = dv * yb * sg * (1.0 - sg)
        dzb = dz.astype(BF16)
        dyb = dv * sg + jnp.dot(dzb, wt_ref[...], preferred_element_type=F32)
        dys_ref[...] = dyb * _gelu_grad(yv)
        dz_ref[...] = dzb
        yb_ref[...] = ybb
        part = jnp.sum(dz, axis=0, keepdims=True)

        @pl.when(i == 0)
        def _():
            db_ref[...] = part

        @pl.when(i > 0)
        def _():
            db_ref[...] += part

    row = pl.BlockSpec((tr, wd), lambda i: (i, 0))
    mat = pl.BlockSpec((wd, wd), lambda i: (0, 0))
    vec = pl.BlockSpec((1, wd), lambda i: (0, 0))
    return pl.pallas_call(
        body, name=name,
        out_shape=(jax.ShapeDtypeStruct((n, wd), F32), jax.ShapeDtypeStruct((n, wd), BF16),
                   jax.ShapeDtypeStruct((n, wd), BF16), jax.ShapeDtypeStruct((1, wd), F32)),
        grid=(n // tr,), in_specs=[row, row, mat, mat, vec], out_specs=(row, row, row, vec),
        compiler_params=_cparams("arbitrary"),
    )(ys, dyb2, w, w_t, b.reshape(1, wd))


def _merge_fwd(ya, yb2, wa, wb, proj, gate_blk, *, name, tr=512):
    n, aw = ya.shape
    d = wa.shape[1]
    tr = _tile(n, tr, 8)

    def body(ya_ref, yb_ref, wa_ref, wb_ref, ga_ref, gb_ref, mix_ref, pa_ref, pb_ref):
        pa = jnp.dot(ya_ref[...], wa_ref[...], preferred_element_type=F32)
        pb = jnp.dot(yb_ref[...], wb_ref[...], preferred_element_type=F32)
        mix = _sigmoid(ga_ref[...].astype(F32)) * pa + _sigmoid(gb_ref[...].astype(F32)) * pb
        mix_ref[...] = mix.astype(BF16)
        pa_ref[...] = pa.astype(BF16)
        pb_ref[...] = pb.astype(BF16)

    row = lambda wdt: pl.BlockSpec((tr, wdt), lambda i: (i, 0))
    full = lambda r, c: pl.BlockSpec((r, c), lambda i: (0, 0))
    out = jax.ShapeDtypeStruct((n, d), BF16)
    return pl.pallas_call(
        body, name=name, out_shape=(out, out, out), grid=(n // tr,),
        in_specs=[row(aw), row(yb2.shape[1]), full(*wa.shape), full(*wb.shape),
                  pl.BlockSpec((tr, d), lambda i: (i, gate_blk)), pl.BlockSpec((tr, d), lambda i: (i, gate_blk + 1))],
        out_specs=(row(d), row(d), row(d)), compiler_params=_cparams("parallel"),
    )(ya, yb2, wa, wb, proj, proj)


def _merge_bwd(dmix, proj, pa, pb, gate_blk, *, name, tr=512):
    n, d = dmix.shape
    tr = _tile(n, tr, 8)

    def body(dm_ref, ga_ref, gb_ref, pa_ref, pb_ref, dpa_ref, dpb_ref, dga_ref, dgb_ref):
        dm = dm_ref[...].astype(F32)
        sa = _sigmoid(ga_ref[...].astype(F32))
        sb = _sigmoid(gb_ref[...].astype(F32))
        dpa_ref[...] = (dm * sa).astype(BF16)
        dpb_ref[...] = (dm * sb).astype(BF16)
        dga_ref[...] = (dm * pa_ref[...].astype(F32) * sa * (1.0 - sa)).astype(BF16)
        dgb_ref[...] = (dm * pb_ref[...].astype(F32) * sb * (1.0 - sb)).astype(BF16)

    row = pl.BlockSpec((tr, d), lambda i: (i, 0))
    out = jax.ShapeDtypeStruct((n, d), BF16)
    return pl.pallas_call(
        body, name=name, out_shape=(out, out, out, out), grid=(n // tr,),
        in_specs=[row, pl.BlockSpec((tr, d), lambda i: (i, gate_blk)), pl.BlockSpec((tr, d), lambda i: (i, gate_blk + 1)),
                  row, row],
        out_specs=(row, row, row, row), compiler_params=_cparams("parallel"),
    )(dmix, proj, proj, pa, pb)


def _outproj_fwd(mixed, w, x0, g, *, name, tr=512):
    n, d = x0.shape
    tr = _tile(n, tr, 8)

    def body(m_ref, w_ref, x_ref, g_ref, x1_ref, h_ref, r_ref):
        x1 = x_ref[...] + jnp.dot(m_ref[...], w_ref[...], preferred_element_type=F32)
        r = lax.rsqrt(jnp.mean(x1 * x1, axis=-1, keepdims=True) + RMS_EPS)
        x1_ref[...] = x1
        h_ref[...] = (x1 * r * g_ref[...]).astype(BF16)
        r_ref[...] = r

    row = pl.BlockSpec((tr, d), lambda i: (i, 0))
    return pl.pallas_call(
        body, name=name,
        out_shape=(jax.ShapeDtypeStruct((n, d), F32), jax.ShapeDtypeStruct((n, d), BF16),
                   jax.ShapeDtypeStruct((n, 1), F32)),
        grid=(n // tr,),
        in_specs=[row, pl.BlockSpec((d, d), lambda i: (0, 0)), row, pl.BlockSpec((1, d), lambda i: (0, 0))],
        out_specs=(row, row, pl.BlockSpec((tr, 1), lambda i: (i, 0))),
        compiler_params=_cparams("parallel"),
    )(mixed, w, x0, g.reshape(1, d))


def _adamw(w, g, m, v, *, name):
    shape = w.shape
    total = w.size
    if total % PACK_COLS == 0 and ((total // PACK_COLS) % 8 == 0 or total // PACK_COLS <= 512):
        rows, cols = total // PACK_COLS, PACK_COLS
    elif w.ndim >= 2:
        rows, cols = total // shape[-1], shape[-1]
    else:
        rows, cols = 1, total
    tr = _tile(rows, 512, 8)

    def body(w_ref, g_ref, m_ref, v_ref, d_ref, nm_ref, nv_ref):
        gv = g_ref[...]
        mn = ADAM_B1 * m_ref[...] + (1.0 - ADAM_B1) * gv
        vn = ADAM_B2 * v_ref[...] + (1.0 - ADAM_B2) * (gv * gv)
        m_hat = mn / (1.0 - ADAM_B1 ** ADAM_STEP)
        v_hat = vn / (1.0 - ADAM_B2 ** ADAM_STEP)
        d_ref[...] = -ADAM_LR * (m_hat / (jnp.sqrt(v_hat) + ADAM_EPS) + ADAM_WD * w_ref[...])
        nm_ref[...] = mn
        nv_ref[...] = vn

    blk = pl.BlockSpec((tr, cols), lambda i: (i, 0))
    out = jax.ShapeDtypeStruct((rows, cols), F32)
    outs = pl.pallas_call(
        body, name=name, out_shape=(out, out, out), grid=(rows // tr,),
        in_specs=[blk] * 4, out_specs=(blk, blk, blk), compiler_params=_cparams("parallel"),
    )(*[t.reshape(rows, cols) for t in (w, g, m, v)])
    return tuple(o.reshape(shape) for o in outs)


HBM = pl.BlockSpec(memory_space=pltpu.HBM)
MESH = pl.DeviceIdType.MESH


def _all_gather(blocks, *, name):
    n = len(blocks)

    def body(*refs):
        x_refs, out_refs = refs[:n], refs[n:2 * n]
        send_sems, recv_sems, local_sems = refs[2 * n:]
        x, y, c = lax.axis_index("x"), lax.axis_index("y"), lax.axis_index("c")
        me, sibling = (x, y, c), (x, y, 1 - c)
        chips = [(1 - x, y), (x, 1 - y), (1 - x, 1 - y)]

        def slot(a, px, py, pc):
            return out_refs[a].at[4 * px + 2 * py + pc]

        def copy(a, k, block, to, src=None):
            return pltpu.make_async_remote_copy(
                src_ref=slot(a, *block) if src is None else src, dst_ref=slot(a, *block),
                send_sem=send_sems.at[7 * a + k], recv_sem=recv_sems.at[7 * a + k], device_id=to,
                device_id_type=MESH)

        started = []
        for a in range(n):
            mine = pltpu.make_async_copy(x_refs[a], slot(a, *me), local_sems.at[a])
            mine.start()
            started.append(mine)
        sends = []
        for a in range(n):
            first = [copy(a, 0, me, sibling, src=x_refs[a])]
            first += [copy(a, 1 + j, me, (*chip, c), src=x_refs[a]) for j, chip in enumerate(chips)]
            for cp in first:
                cp.start()
            sends += first
        for a in range(n):
            for j, chip in enumerate(chips):
                copy(a, 1 + j, (*chip, c), me).wait_recv()
                onward = copy(a, 4 + j, (*chip, c), sibling)
                onward.start()
                sends.append(onward)
        for a in range(n):
            copy(a, 0, sibling, me).wait_recv()
            for j, chip in enumerate(chips):
                copy(a, 4 + j, (*chip, 1 - c), me).wait_recv()
        for cp in sends:
            cp.wait_send()
        for mine in started:
            mine.wait()

    return pl.pallas_call(
        body, name=name, out_shape=[jax.ShapeDtypeStruct((N_DEV,) + b.shape, b.dtype) for b in blocks],
        in_specs=[HBM] * n, out_specs=[HBM] * n,
        scratch_shapes=[pltpu.SemaphoreType.DMA((7 * n,)), pltpu.SemaphoreType.DMA((7 * n,)),
                        pltpu.SemaphoreType.DMA((n,))],
    )(*blocks)


def _all_to_all(blocks, *, name):
    n = len(blocks)

    def body(*refs):
        x_refs, out_refs = refs[:n], refs[n:2 * n]
        send_sems, recv_sems, local_sems = refs[2 * n:]
        x, y, c = lax.axis_index("x"), lax.axis_index("y"), lax.axis_index("c")
        me = 4 * x + 2 * y + c
        copies = []
        for a in range(n):
            mine = pltpu.make_async_copy(x_refs[a].at[me], out_refs[a].at[me], local_sems.at[a])
            mine.start()
            copies.append(mine)
        for k in range(1, N_DEV):
            px = x if not (k >> 2) & 1 else 1 - x
            py = y if not (k >> 1) & 1 else 1 - y
            pc = c if not k & 1 else 1 - c
            for a in range(n):
                cp = pltpu.make_async_remote_copy(
                    src_ref=x_refs[a].at[4 * px + 2 * py + pc], dst_ref=out_refs[a].at[me],
                    send_sem=send_sems.at[7 * a + k - 1], recv_sem=recv_sems.at[7 * a + k - 1],
                    device_id=(px, py, pc), device_id_type=MESH)
                cp.start()
                copies.append(cp)
        for cp in copies:
            cp.wait()

    return pl.pallas_call(
        body, name=name, out_shape=[jax.ShapeDtypeStruct(b.shape, b.dtype) for b in blocks],
        in_specs=[HBM] * n, out_specs=[HBM] * n,
        scratch_shapes=[pltpu.SemaphoreType.DMA((7 * n,)), pltpu.SemaphoreType.DMA((7 * n,)),
                        pltpu.SemaphoreType.DMA((n,))],
    )(*blocks)


def _sum8(blocks, *, name, tr=PACK_ROWS):
    _, R, C = blocks.shape
    tr = _tile(R, tr, 16)

    def body(x_ref, o_ref):
        acc = x_ref[0].astype(F32)
        for i in range(1, N_DEV):
            acc = acc + x_ref[i].astype(F32)
        o_ref[...] = acc

    return pl.pallas_call(
        body, name=name, out_shape=jax.ShapeDtypeStruct((R, C), F32), grid=(R // tr,),
        in_specs=[pl.BlockSpec((N_DEV, tr, C), lambda i: (0, i, 0))],
        out_specs=pl.BlockSpec((tr, C), lambda i: (i, 0)), compiler_params=_cparams("parallel"),
    )(blocks)


def _pack_rows(flat_last):
    n = flat_last.shape[-1]
    unit = PACK_ROWS * PACK_COLS
    padded = -(-n // unit) * unit
    pad = [(0, 0)] * (flat_last.ndim - 1) + [(0, padded - n)]
    return jnp.pad(flat_last, pad).reshape(flat_last.shape[:-1] + (padded // PACK_COLS, PACK_COLS))


def _ssm_discretise(lre, lim, logdt, bre, bim):
    lam = lax.complex(lre, lim)
    dt = jnp.exp(logdt)[:, None]
    lam_bar = jnp.exp(lam * dt)
    b_bar = ((lam_bar - 1.0) / lam)[:, :, None] * lax.complex(bre, bim)
    return lam_bar.real, lam_bar.imag, b_bar.real, b_bar.imag


def _block_diag(a, rows_first):
    ns, g, r, c = a.shape
    eye = jnp.eye(g, dtype=a.dtype)
    return jnp.einsum("sgrc,gh->sgrhc", a, eye).reshape(ns, g * r, g * c)


def _diag_blocks(m, r, c):
    ns = m.shape[0]
    g = SLAB_GROUPS
    return jnp.einsum("sgrhc,gh->sgrc", m.reshape(ns, g, r, g, c), jnp.eye(g, dtype=m.dtype))


def _to_tm(a, T):
    b, s, w = a.shape
    return a.reshape(b, s // T, T, w).transpose(0, 2, 1, 3)


def _from_tm(a):
    b, t, nc, w = a.shape
    return a.transpose(0, 2, 1, 3).reshape(b, nc * t, w)


WEIGHTS = ["norm_mix", "w_in", "b_forget", "ssm_lambda_re", "ssm_lambda_im", "ssm_log_dt", "ssm_b_re", "ssm_b_im",
           "ssm_c_re", "ssm_c_im", "ssm_d", "w_glu", "b_glu", "w_branch_a", "w_branch_b", "w_out", "norm_mlp",
           "w_mlp_up", "w_mlp_down", "norm_final"]
SHARDED = {"w_in": 2, "w_glu": 1, "w_branch_a": 2, "w_branch_b": 2, "w_out": 1, "w_mlp_up": 2, "w_mlp_down": 1}


def _gather_weights(shards):
    names = list(SHARDED)
    got = _all_gather([shards[n].astype(BF16) for n in names], name="gather_weights")
    full = {}
    for n, seg in zip(names, got):
        shp, ax = shards[n].shape, SHARDED[n]
        full[n] = jnp.moveaxis(seg, 0, ax).reshape(shp[:ax] + (N_DEV * shp[ax],) + shp[ax + 1:])
    return full


def _scatter_grads(grads, shard_shapes):
    names = list(SHARDED)
    parts = []
    for n in names:
        shp, ax = shard_shapes[n], SHARDED[n]
        g = grads[n].reshape(shp[:ax] + (N_DEV, shp[ax]) + shp[ax + 1:])
        parts.append(jnp.moveaxis(g, ax, 0))
    got = _all_to_all(parts, name="exchange_grads")
    out = {}
    for n, g in zip(names, got):
        shp = shard_shapes[n]
        out[n] = _sum8(g.reshape(N_DEV, -1, shp[-1]), name="sum_grads_" + n).reshape(shp)
    return out


def _allreduce_small(grads):
    names = list(grads)
    flat = jnp.concatenate([grads[n].astype(F32).reshape(-1) for n in names])
    got = _all_gather([_pack_rows(flat)], name="gather_small_grads")[0]
    summed = _sum8(got, name="sum_small_grads").reshape(-1)
    out, off = {}, 0
    for n in names:
        out[n] = summed[off:off + grads[n].size].reshape(grads[n].shape)
        off += grads[n].size
    return out


def kernel(x, norm_mix, w_in, b_forget, ssm_lambda_re, ssm_lambda_im, ssm_log_dt, ssm_b_re, ssm_b_im, ssm_c_re, ssm_c_im, ssm_d, w_glu, b_glu, w_branch_a, w_branch_b, w_out, norm_mlp, w_mlp_up, w_mlp_down, norm_final, loss_target, m_norm_mix, m_w_in, m_b_forget, m_ssm_lambda_re, m_ssm_lambda_im, m_ssm_log_dt, m_ssm_b_re, m_ssm_b_im, m_ssm_c_re, m_ssm_c_im, m_ssm_d, m_w_glu, m_b_glu, m_w_branch_a, m_w_branch_b, m_w_out, m_norm_mlp, m_w_mlp_up, m_w_mlp_down, m_norm_final, v_norm_mix, v_w_in, v_b_forget, v_ssm_lambda_re, v_ssm_lambda_im, v_ssm_log_dt, v_ssm_b_re, v_ssm_b_im, v_ssm_c_re, v_ssm_c_im, v_ssm_d, v_w_glu, v_b_glu, v_w_branch_a, v_w_branch_b, v_w_out, v_norm_mlp, v_w_mlp_up, v_w_mlp_down, v_norm_final):
    args = dict(locals())
    w = {n: args[n] for n in WEIGHTS}
    Bl, S, D = x.shape
    L, H = b_forget.shape
    G, P, C = ssm_b_re.shape[1:]
    AW, W, HP = H * HEAD_DIM, G * C, H // 2
    N = Bl * S
    T = SSM_CHUNK
    NS = G // SLAB_GROUPS
    SP = SLAB_GROUPS * P
    tq = min(ATTN_BLOCK, S)
    nq = S // tq
    u_off = 3 * AW
    gate_blk = (u_off + W) // D
    assert (u_off + W) % D == 0 and W % LANES == 0 and AW % LANES == 0 and S % T == 0

    full = _gather_weights({n: w[n] for n in SHARDED})
    win = full["w_in"]
    wcat = jnp.concatenate([win[:, :, :3 * AW], win[:, :, 3 * AW + H:]], axis=2)
    wf = jnp.pad(win[:, :, 3 * AW:3 * AW + H], ((0, 0), (0, 0), (0, LANES - H)))
    wcat_t = jnp.swapaxes(jnp.concatenate([wcat, wf], axis=2), 1, 2)
    tr_ = lambda a: jnp.swapaxes(a, 1, 2)

    ssm = []
    for l in range(L):
        disc, disc_vjp = jax.vjp(_ssm_discretise, ssm_lambda_re[l], ssm_lambda_im[l], ssm_log_dt[l],
                                 ssm_b_re[l], ssm_b_im[l])
        lbr, lbi, bbr, bbi = disc
        z = lax.complex(ssm_lambda_re[l], ssm_lambda_im[l]) * jnp.exp(ssm_log_dt[l])[:, None]
        powers = jnp.exp(z[None] * jnp.arange(1, T + 1, dtype=F32)[:, None, None])
        slabs = lambda a: a.reshape(NS, SP)
        lam = jnp.stack([slabs(lbr), slabs(lbi)], axis=1)
        lam_t = jnp.stack([slabs(powers[T - 1].real), slabs(powers[T - 1].imag)], axis=1)
        pw = jnp.stack([powers.real.reshape(T, NS, SP), powers.imag.reshape(T, NS, SP)], axis=0).transpose(2, 0, 1, 3)
        to_rows = lambda a: jnp.swapaxes(a.reshape(NS, SLAB_GROUPS, P, C), 2, 3)
        bmat = jnp.concatenate([_block_diag(to_rows(bbr), True), _block_diag(to_rows(bbi), True)], axis=2)
        cre = ssm_c_re[l].reshape(NS, SLAB_GROUPS, C, P)
        cim = ssm_c_im[l].reshape(NS, SLAB_GROUPS, C, P)
        cmat_t = jnp.concatenate([_block_diag(cre, True), -_block_diag(cim, True)], axis=2)
        ssm.append(dict(vjp=disc_vjp, lam=lam, lam_t=lam_t, pw=pw, bmat=bmat.astype(BF16),
                        bmat_t=tr_(bmat).astype(BF16), cmat=tr_(cmat_t).astype(BF16), cmat_t=cmat_t.astype(BF16),
                        d=ssm_d[l].reshape(1, W)))

    xcur = x.reshape(N, D)
    saved = []
    for l in range(L):
        s_ = ssm[l]
        h, r0 = _rmsnorm_fwd(xcur, norm_mix[l], name="norm_mix_fwd")
        proj = _mm(h, wcat[l], name="in_proj")
        fl = _mm(h, wf[l], name="forget_proj", out_dtype=F32)
        ft = fl[:, :H].reshape(Bl, S, H).transpose(0, 2, 1)
        F = _fox_gate_fwd(ft, b_forget[l], name="forget_gate_fwd")
        frow = F.reshape(Bl, HP, 2, nq, tq)
        proj3 = proj.reshape(Bl, S, -1)
        ya, lse = _attn_fwd(proj3, frow, name="attn_fwd", H=H, tq=tq)
        u_tm = _to_tm(proj3[:, :, u_off:u_off + W], T)
        ys = _from_tm(_ssm_fwd(u_tm, s_["bmat"], s_["cmat"], s_["lam"], s_["pw"], s_["lam_t"], s_["d"],
                               name="ssm_fwd")).reshape(N, W)
        yb2 = _glu_fwd(ys, full["w_glu"][l], b_glu[l], name="glu_fwd")
        ya2 = ya.reshape(N, AW)
        mixed, pa, pb = _merge_fwd(ya2, yb2, full["w_branch_a"][l], full["w_branch_b"][l], proj, gate_blk,
                                   name="merge_fwd")
        x1, h2, r1 = _outproj_fwd(mixed, full["w_out"][l], xcur, norm_mlp[l], name="out_proj")
        a = _mm(h2, full["w_mlp_up"][l], name="mlp_up")
        x2 = _mm(a, full["w_mlp_down"][l], name="mlp_down", a_fn=_relu_sq, epi=lambda acc, res: acc + res,
                 extras=(x1,), out_dtype=F32)
        saved.append(dict(x0=xcur, h=h, r0=r0, proj=proj, ft=ft, frow=frow, ya=ya, lse=lse, u_tm=u_tm,
                          ys=ys, yb2=yb2, mixed=mixed, pa=pa, pb=pb, x1=x1, h2=h2, r1=r1, a=a))
        xcur = x2

    dx, g_final, loss_row = _loss_head(xcur, norm_final, loss_target.reshape(N, D), name="loss_head")
    loss = lax.psum(loss_row[0, 0], MESH_AXES)

    big = {n: [None] * L for n in SHARDED}
    small = {n: [None] * L for n in WEIGHTS if n not in SHARDED and n != "norm_final"}
    for l in reversed(range(L)):
        sv, s_ = saved[l], ssm[l]
        a = sv["a"]
        d_a = _mm(dx, tr_(full["w_mlp_down"])[l], name="mlp_down_dx",
                  epi=lambda acc, av: acc * (2.0 * jnp.maximum(av.astype(F32), 0.0)), extras=(a,))
        big["w_mlp_down"][l] = _mm(a, dx, name="mlp_down_dw", ta=True, a_fn=_relu_sq)
        big["w_mlp_up"][l] = _mm(sv["h2"], d_a, name="mlp_up_dw", ta=True)
        dh2 = _mm(d_a, tr_(full["w_mlp_up"])[l], name="mlp_up_dx", out_dtype=F32)
        dx1, g = _rmsnorm_bwd(dh2, sv["x1"], sv["r1"], norm_mlp[l], dx, name="norm_mlp_bwd")
        small["norm_mlp"][l] = g[0]
        dmix = _mm(dx1, tr_(full["w_out"])[l], name="out_proj_dx")
        big["w_out"][l] = _mm(sv["mixed"], dx1, name="out_proj_dw", ta=True)
        dpa, dpb, dga, dgb = _merge_bwd(dmix, sv["proj"], sv["pa"], sv["pb"], gate_blk, name="merge_bwd")
        ya2 = sv["ya"].reshape(N, AW)
        big["w_branch_a"][l] = _mm(ya2, dpa, name="branch_a_dw", ta=True)
        dya = _mm(dpa, tr_(full["w_branch_a"])[l], name="branch_a_dx")
        big["w_branch_b"][l] = _mm(sv["yb2"], dpb, name="branch_b_dw", ta=True)
        dyb2 = _mm(dpb, tr_(full["w_branch_b"])[l], name="branch_b_dx")
        dys, dz, yb, g = _glu_bwd(sv["ys"], dyb2, full["w_glu"][l], tr_(full["w_glu"])[l], b_glu[l], name="glu_bwd")
        small["b_glu"][l] = g[0]
        big["w_glu"][l] = _mm(yb, dz, name="glu_dw", ta=True)

        du_tm, g_bt, g_ct, g_lam, g_d = _ssm_bwd(
            sv["u_tm"], _to_tm(dys.reshape(Bl, S, W), T), s_["bmat"], s_["bmat_t"], s_["cmat_t"], s_["lam"],
            s_["pw"], s_["lam_t"], s_["d"], name="ssm_bwd")
        du = _from_tm(du_tm).reshape(N, W)
        g_b = _diag_blocks(jnp.swapaxes(g_bt, 1, 2).reshape(NS, LANES, 2, SP).transpose(2, 0, 1, 3).reshape(
            2 * NS, LANES, SP), C, P).reshape(2, G, C, P)
        g_bbar = jnp.swapaxes(g_b, 2, 3)
        g_c = _diag_blocks(g_ct.reshape(NS, LANES, 2, SP).transpose(2, 0, 1, 3).reshape(2 * NS, LANES, SP),
                           C, P).reshape(2, G, C, P)
        g_lbar = g_lam.transpose(1, 0, 2).reshape(2, G, P)
        g_lre, g_lim, g_ldt, g_bre, g_bim = s_["vjp"]((g_lbar[0], g_lbar[1], g_bbar[0], g_bbar[1]))
        small["ssm_lambda_re"][l], small["ssm_lambda_im"][l], small["ssm_log_dt"][l] = g_lre, g_lim, g_ldt
        small["ssm_b_re"][l], small["ssm_b_im"][l] = g_bre, g_bim
        small["ssm_c_re"][l], small["ssm_c_im"][l] = g_c[0], -g_c[1]
        small["ssm_d"][l] = g_d.reshape(W)

        proj3 = sv["proj"].reshape(Bl, S, -1)
        dq, dk, dv, dfk, dfq = _attn_bwd(proj3, sv["ya"], dya.reshape(Bl, S, AW), sv["lse"], sv["frow"],
                                         name="attn_bwd", H=H, tq=tq)
        dF = dfk.reshape(Bl, H, S) + dfq.transpose(0, 1, 3, 2).reshape(Bl, H, S)
        dft, g = _fox_gate_bwd(dF, sv["ft"], b_forget[l], name="forget_gate_bwd")
        small["b_forget"][l] = g[:, 0]
        dfl = jnp.pad(dft.transpose(0, 2, 1).reshape(N, H), ((0, 0), (0, LANES - H))).astype(BF16)
        dproj = jnp.concatenate([dq.reshape(N, AW), dk.reshape(N, AW), dv.reshape(N, AW), du, dga, dgb, dfl], axis=1)
        gw = _mm(sv["h"], dproj, name="in_proj_dw", ta=True, tn=1408)
        ncat = wcat.shape[2]
        big["w_in"][l] = jnp.concatenate([gw[:, :3 * AW], gw[:, ncat:ncat + H], gw[:, 3 * AW:ncat]], axis=1)
        dh = _mm(dproj, wcat_t[l], name="in_proj_dx", out_dtype=F32, tk=1408)
        dx, g = _rmsnorm_bwd(dh, sv["x0"], sv["r0"], norm_mix[l], dx1, name="norm_mix_bwd")
        small["norm_mix"][l] = g[0]

    grads = _scatter_grads({n: jnp.stack(big[n]) for n in SHARDED}, {n: w[n].shape for n in SHARDED})
    small_stacked = {n: jnp.stack(small[n]) for n in small}
    small_stacked["norm_final"] = g_final[0]
    grads.update(_allreduce_small(small_stacked))

    deltas, new_m, new_v = {}, {}, {}
    for n in WEIGHTS:
        deltas[n], new_m[n], new_v[n] = _adamw(w[n], grads[n], args["m_" + n], args["v_" + n], name="adamw_" + n)
    return (loss, dx.reshape(Bl, S, D), *[grads[n] for n in WEIGHTS], *[deltas[n] for n in WEIGHTS],
            *[new_m[n] for n in WEIGHTS], *[new_v[n] for n in WEIGHTS])
```

```python
import functools

import jax
import jax.numpy as jnp
from jax import lax
from jax.experimental import pallas as pl
from jax.experimental.pallas import tpu as pltpu

F32 = jnp.float32
BF16 = jnp.bfloat16

N_DEV = 8
HEAD_DIM = 64
LANES = 128
SSM_CHUNK = 32
SLAB_GROUPS = 8
ATTN_BLOCK = 512
PACK_COLS = 1024
PACK_ROWS = 256
RMS_EPS = 1e-6
VMEM_LIMIT = 56 * 1024 * 1024
ADAM_LR, ADAM_B1, ADAM_B2, ADAM_EPS, ADAM_WD, ADAM_STEP = 0.001, 0.9, 0.999, 1e-08, 0.01, 10
MESH_AXES = ("x", "y", "c")
NEG = -1e30
NT = (((1,), (1,)), ((), ()))
TN = (((0,), (0,)), ((), ()))


def _cparams(*sem):
    return pltpu.CompilerParams(dimension_semantics=sem, vmem_limit_bytes=VMEM_LIMIT)


def _tile(dim, pref, unit=LANES):
    if dim <= pref:
        return dim
    best = None
    for t in range(unit, pref + 1, unit):
        if dim % t == 0:
            best = t
    assert best is not None, (dim, pref)
    return best


def _mm(a, b, *, name, ta=False, a_fn=None, epi=None, extras=(), out_dtype=BF16, tm=1024, tn=1024, tk=1024):
    if ta:
        K, M = a.shape
    else:
        M, K = a.shape
    Kb, N = b.shape
    assert K == Kb, (a.shape, b.shape)
    tm, tn, tk = _tile(M, tm), _tile(N, tn), _tile(K, tk)
    nk = K // tk
    ne = len(extras)

    def body(a_ref, b_ref, *rest):
        e_refs, o_ref, acc_ref = rest[:ne], rest[ne], rest[ne + 1]
        k = pl.program_id(2)
        av = a_ref[...]
        if a_fn is not None:
            av = a_fn(av)
        av = av.astype(BF16)
        bv = b_ref[...].astype(BF16)
        part = lax.dot_general(av, bv, TN if ta else (((1,), (0,)), ((), ())), preferred_element_type=F32)

        @pl.when(k == 0)
        def _():
            acc_ref[...] = part

        @pl.when(k > 0)
        def _():
            acc_ref[...] += part

        @pl.when(k == nk - 1)
        def _():
            r = acc_ref[...]
            if epi is not None:
                r = epi(r, *[e[...] for e in e_refs])
            o_ref[...] = r.astype(o_ref.dtype)

    a_spec = pl.BlockSpec((tk, tm), lambda i, j, k: (k, i)) if ta else pl.BlockSpec((tm, tk), lambda i, j, k: (i, k))
    return pl.pallas_call(
        body, name=name,
        out_shape=jax.ShapeDtypeStruct((M, N), out_dtype),
        grid=(M // tm, N // tn, nk),
        in_specs=[a_spec, pl.BlockSpec((tk, tn), lambda i, j, k: (k, j))]
        + [pl.BlockSpec((tm, tn), lambda i, j, k: (i, j)) for _ in extras],
        out_specs=pl.BlockSpec((tm, tn), lambda i, j, k: (i, j)),
        scratch_shapes=[pltpu.VMEM((tm, tn), F32)],
        compiler_params=_cparams("parallel", "parallel", "arbitrary"),
    )(a, b, *extras)


def _relu_sq(v):
    r = jnp.maximum(v.astype(F32), 0.0)
    return r * r


def _sigmoid(v):
    return 1.0 / (1.0 + jnp.exp(-v))


GELU_C = 0.7978845608028654
GELU_A = 0.044715


def _gelu(v):
    return 0.5 * v * (1.0 + jnp.tanh(GELU_C * (v + GELU_A * v * v * v)))


def _gelu_grad(v):
    t = jnp.tanh(GELU_C * (v + GELU_A * v * v * v))
    return 0.5 * (1.0 + t) + 0.5 * v * (1.0 - t * t) * GELU_C * (1.0 + 3.0 * GELU_A * v * v)


def _rmsnorm_fwd(x, g, *, name, tr=512):
    n, d = x.shape
    tr = _tile(n, tr, 8)

    def body(x_ref, g_ref, h_ref, r_ref):
        xv = x_ref[...]
        r = lax.rsqrt(jnp.mean(xv * xv, axis=-1, keepdims=True) + RMS_EPS)
        h_ref[...] = (xv * r * g_ref[...]).astype(BF16)
        r_ref[...] = r

    return pl.pallas_call(
        body, name=name,
        out_shape=(jax.ShapeDtypeStruct((n, d), BF16), jax.ShapeDtypeStruct((n, 1), F32)),
        grid=(n // tr,),
        in_specs=[pl.BlockSpec((tr, d), lambda i: (i, 0)), pl.BlockSpec((1, d), lambda i: (0, 0))],
        out_specs=(pl.BlockSpec((tr, d), lambda i: (i, 0)), pl.BlockSpec((tr, 1), lambda i: (i, 0))),
        compiler_params=_cparams("parallel"),
    )(x, g.reshape(1, d))


def _rmsnorm_bwd(dh, x, r, g, dres, *, name, tr=512):
    n, d = x.shape
    tr = _tile(n, tr, 8)

    def body(dh_ref, x_ref, r_ref, g_ref, dres_ref, dx_ref, dg_ref):
        i = pl.program_id(0)
        rv = r_ref[...]
        xh = x_ref[...] * rv
        dhv = dh_ref[...].astype(F32)
        dxh = dhv * g_ref[...]
        m = jnp.mean(dxh * xh, axis=-1, keepdims=True)
        dx_ref[...] = rv * (dxh - xh * m) + dres_ref[...]
        part = jnp.sum(dhv * xh, axis=0, keepdims=True)

        @pl.when(i == 0)
        def _():
            dg_ref[...] = part

        @pl.when(i > 0)
        def _():
            dg_ref[...] += part

    row = pl.BlockSpec((tr, d), lambda i: (i, 0))
    vec = pl.BlockSpec((1, d), lambda i: (0, 0))
    return pl.pallas_call(
        body, name=name,
        out_shape=(jax.ShapeDtypeStruct((n, d), F32), jax.ShapeDtypeStruct((1, d), F32)),
        grid=(n // tr,),
        in_specs=[row, row, pl.BlockSpec((tr, 1), lambda i: (i, 0)), vec, row],
        out_specs=(row, vec),
        compiler_params=_cparams("arbitrary"),
    )(dh, x, r, g.reshape(1, d), dres)


def _loss_head(x, g, target, *, name, tr=512):
    n, d = x.shape
    tr = _tile(n, tr, 8)

    def body(x_ref, g_ref, t_ref, dx_ref, dg_ref, loss_ref):
        i = pl.program_id(0)
        xv = x_ref[...]
        gv = g_ref[...]
        r = lax.rsqrt(jnp.mean(xv * xv, axis=-1, keepdims=True) + RMS_EPS)
        xh = xv * r
        err = xh * gv - t_ref[...]
        lpart = 0.5 * jnp.sum(jnp.mean(err * err, axis=-1, keepdims=True), axis=0, keepdims=True)
        dy = err * (1.0 / d)
        dxh = dy * gv
        m = jnp.mean(dxh * xh, axis=-1, keepdims=True)
        dx_ref[...] = r * (dxh - xh * m)
        gpart = jnp.sum(dy * xh, axis=0, keepdims=True)
        lrow = jnp.broadcast_to(lpart, (1, LANES))

        @pl.when(i == 0)
        def _():
            dg_ref[...] = gpart
            loss_ref[...] = lrow

        @pl.when(i > 0)
        def _():
            dg_ref[...] += gpart
            loss_ref[...] += lrow

    row = pl.BlockSpec((tr, d), lambda i: (i, 0))
    vec = pl.BlockSpec((1, d), lambda i: (0, 0))
    return pl.pallas_call(
        body, name=name,
        out_shape=(jax.ShapeDtypeStruct((n, d), F32), jax.ShapeDtypeStruct((1, d), F32),
                   jax.ShapeDtypeStruct((1, LANES), F32)),
        grid=(n // tr,),
        in_specs=[row, vec, row],
        out_specs=(row, vec, pl.BlockSpec((1, LANES), lambda i: (0, 0))),
        compiler_params=_cparams("arbitrary"),
    )(x, g.reshape(1, d), target)


def _tri_dot(v, tri):
    hi = v.astype(BF16)
    r1 = v - hi.astype(F32)
    mid = r1.astype(BF16)
    lo = (r1 - mid.astype(F32)).astype(BF16)
    d = functools.partial(jnp.dot, preferred_element_type=F32)
    return d(hi, tri) + d(mid, tri) + d(lo, tri)


def _fox_gate_fwd(ft, bf, *, name, blk=256):
    B, H, S = ft.shape
    blk = _tile(S, blk)
    nb = S // blk

    def body(f_ref, b_ref, o_ref):
        x = f_ref[0] + b_ref[...]
        logf = jnp.minimum(x, 0.0) - jnp.log(1.0 + jnp.exp(-jnp.abs(x)))
        rr = lax.broadcasted_iota(jnp.int32, (blk, blk), 0)
        cc = lax.broadcasted_iota(jnp.int32, (blk, blk), 1)
        tri = (rr <= cc).astype(BF16)
        carry = jnp.zeros((H, 1), F32)
        for n in range(nb):
            c = _tri_dot(logf[:, n * blk:(n + 1) * blk], tri) + carry
            o_ref[0, :, n * blk:(n + 1) * blk] = c
            carry = c[:, blk - 1:blk]

    return pl.pallas_call(
        body, name=name,
        out_shape=jax.ShapeDtypeStruct((B, H, S), F32),
        grid=(B,),
        in_specs=[pl.BlockSpec((1, H, S), lambda b: (b, 0, 0)), pl.BlockSpec((H, 1), lambda b: (0, 0))],
        out_specs=pl.BlockSpec((1, H, S), lambda b: (b, 0, 0)),
        compiler_params=_cparams("parallel"),
    )(ft, bf.reshape(H, 1))


def _fox_gate_bwd(dF, ft, bf, *, name, blk=256):
    B, H, S = ft.shape
    blk = _tile(S, blk)
    nb = S // blk

    def body(d_ref, f_ref, b_ref, o_ref, db_ref):
        b = pl.program_id(0)
        x = f_ref[0] + b_ref[...]
        sneg = 1.0 / (1.0 + jnp.exp(x))
        dv = d_ref[0]
        rr = lax.broadcasted_iota(jnp.int32, (blk, blk), 0)
        cc = lax.broadcasted_iota(jnp.int32, (blk, blk), 1)
        tri = (rr >= cc).astype(BF16)
        carry = jnp.zeros((H, 1), F32)
        tot = jnp.zeros((H, 1), F32)
        for n in reversed(range(nb)):
            sl = slice(n * blk, (n + 1) * blk)
            c = _tri_dot(dv[:, sl], tri) + carry
            g = c * sneg[:, sl]
            o_ref[0, :, sl] = g
            tot = tot + jnp.sum(g, axis=1, keepdims=True)
            carry = c[:, 0:1]

        @pl.when(b == 0)
        def _():
            db_ref[...] = tot

        @pl.when(b > 0)
        def _():
            db_ref[...] += tot

    blkspec = pl.BlockSpec((1, H, S), lambda b: (b, 0, 0))
    return pl.pallas_call(
        body, name=name,
        out_shape=(jax.ShapeDtypeStruct((B, H, S), F32), jax.ShapeDtypeStruct((H, 1), F32)),
        grid=(B,),
        in_specs=[blkspec, blkspec, pl.BlockSpec((H, 1), lambda b: (0, 0))],
        out_specs=(blkspec, pl.BlockSpec((H, 1), lambda b: (0, 0))),
        compiler_params=_cparams("arbitrary"),
    )(dF, ft, bf.reshape(H, 1))


def _head_masks():
    lane = lax.broadcasted_iota(jnp.int32, (1, LANES), 1)
    return [lane < HEAD_DIM, lane >= HEAD_DIM]


HBM = pl.BlockSpec(memory_space=pltpu.HBM)
MESH = pl.DeviceIdType.MESH


def _direct_copies(kind, x_refs, out_refs, send_sems, recv_sems, local_sems):
    x, y, c = lax.axis_index("x"), lax.axis_index("y"), lax.axis_index("c")
    me = 4 * x + 2 * y + c
    gather = kind == "gather"
    copies = []
    for a, (xr, outr) in enumerate(zip(x_refs, out_refs)):
        copies.append(pltpu.make_async_copy(xr if gather else xr.at[me], outr.at[me], local_sems.at[a]))
    for k in range(1, N_DEV):
        px = 1 - x if (k >> 2) & 1 else x
        py = 1 - y if (k >> 1) & 1 else y
        pc = 1 - c if k & 1 else c
        for a, (xr, outr) in enumerate(zip(x_refs, out_refs)):
            copies.append(pltpu.make_async_remote_copy(
                src_ref=xr if gather else xr.at[4 * px + 2 * py + pc], dst_ref=outr.at[me],
                send_sem=send_sems.at[7 * a + k - 1], recv_sem=recv_sems.at[7 * a + k - 1],
                device_id=(px, py, pc), device_id_type=MESH))
    return copies


class _Hosted:
    def __init__(self, kind, arrays):
        self.kind, self.arrays, self.n = kind, list(arrays), len(arrays)
        lead = (N_DEV,) if kind == "gather" else ()
        self.out_shape = [jax.ShapeDtypeStruct(lead + a.shape, a.dtype) for a in self.arrays]
        self.scratch = [pltpu.SemaphoreType.DMA((7 * self.n,)), pltpu.SemaphoreType.DMA((7 * self.n,)),
                        pltpu.SemaphoreType.DMA((self.n,))]

    def run(self, first, last, x_refs, out_refs, sems):
        def go(when, act):
            @pl.when(when)
            def _():
                for cp in _direct_copies(self.kind, x_refs, out_refs, *sems):
                    act(cp)
        return (lambda: go(first, lambda cp: cp.start())), (lambda: go(last, lambda cp: cp.wait()))


def _stack_heads(x, masks):
    zero = jnp.zeros_like(x)
    return jnp.concatenate([jnp.where(masks[0], x, zero), jnp.where(masks[1], x, zero)], axis=0)


def _attn_fwd(proj, frow, *, name, H, tq, hosted=None):
    B, S, _ = proj.shape
    HP = H // 2
    nq = S // tq
    scale = HEAD_DIM ** -0.5
    nh = hosted.n if hosted else 0

    def body(*refs):
        q_ref, k_ref, v_ref, fk_ref = refs[:4]
        o_ref, lse_ref = refs[4 + nh:6 + nh]
        i = pl.program_id(2)
        if hosted:
            b, hp = pl.program_id(0), pl.program_id(1)
            start, finish = hosted.run((b == 0) & (hp == 0) & (i == 0), (b == B - 1) & (hp == HP - 1) & (i == nq - 1),
                                       refs[4:4 + nh], refs[6 + nh:6 + 2 * nh], refs[6 + 2 * nh:])
            start()
        masks = _head_masks()
        q2 = _stack_heads(q_ref[0], masks) * jnp.asarray(scale, BF16)
        rr = lax.broadcasted_iota(jnp.int32, (tq, tq), 0)
        cc = lax.broadcasted_iota(jnp.int32, (tq, tq), 1)
        causal = rr >= cc

        def block(j, carry, masked):
            rows = pl.ds(pl.multiple_of(j * tq, tq), tq)
            kj = k_ref[0, rows, :]
            vj = v_ref[0, rows, :]
            s2 = lax.dot_general(q2, kj, NT, preferred_element_type=F32)
            new, ps = [], []
            for h in range(2):
                m, l, acc = carry[h]
                s = s2[h * tq:(h + 1) * tq] - fk_ref[0, 0, h, pl.ds(j, 1), :]
                if masked:
                    s = jnp.where(causal, s, NEG)
                m_new = jnp.maximum(m, jnp.max(s, axis=-1, keepdims=True))
                alpha = jnp.exp(m - m_new)
                p = jnp.exp(s - m_new)
                new.append((m_new, alpha * l + jnp.sum(p, axis=-1, keepdims=True), alpha, acc))
                ps.append(p.astype(BF16))
            pv = jnp.dot(jnp.concatenate(ps, axis=0), vj, preferred_element_type=F32)
            return tuple((m, l, alpha * acc + pv[h * tq:(h + 1) * tq]) for h, (m, l, alpha, acc) in enumerate(new))

        one = (jnp.full((tq, 1), NEG, F32), jnp.zeros((tq, 1), F32), jnp.zeros((tq, LANES), F32))
        carry = lax.fori_loop(0, i, lambda j, c: block(j, c, False), (one, one))
        (m0, l0, a0), (m1, l1, a1) = block(i, carry, True)
        o_ref[0] = jnp.where(masks[0], a0 / l0, a1 / l1).astype(BF16)
        two = lax.broadcasted_iota(jnp.int32, (1, 2), 1)
        lse_ref[0, 0] = jnp.where(two == 0, m0 + jnp.log(l0), m1 + jnp.log(l1))
        if hosted:
            finish()

    kv = lambda off: pl.BlockSpec((1, S, LANES), lambda b, hp, i: (b, 0, off + hp))
    outs = pl.pallas_call(
        body, name=name,
        out_shape=[jax.ShapeDtypeStruct((B, S, H * HEAD_DIM), BF16), jax.ShapeDtypeStruct((B, HP, S, 2), F32)]
        + (hosted.out_shape if hosted else []),
        grid=(B, HP, nq),
        in_specs=[pl.BlockSpec((1, tq, LANES), lambda b, hp, i: (b, i, hp)), kv(HP), kv(2 * HP),
                  pl.BlockSpec((1, 1, 2, nq, tq), lambda b, hp, i: (b, hp, 0, 0, 0))] + [HBM] * nh,
        out_specs=[pl.BlockSpec((1, tq, LANES), lambda b, hp, i: (b, i, hp)),
                   pl.BlockSpec((1, 1, tq, 2), lambda b, hp, i: (b, hp, i, 0))] + [HBM] * nh,
        scratch_shapes=hosted.scratch if hosted else [],
        compiler_params=_cparams("arbitrary", "arbitrary", "arbitrary"),
    )(proj, proj, proj, frow, *(hosted.arrays if hosted else []))
    return outs[0], outs[1], outs[2:]


def _attn_bwd(proj, ya, dya, lse, frow, *, name, H, tq, hosted=None):
    B, S, _ = proj.shape
    HP = H // 2
    nq = S // tq
    AW = H * HEAD_DIM
    scale = HEAD_DIM ** -0.5
    nh = hosted.n if hosted else 0

    def body(*refs):
        q_ref, k_ref, v_ref, o_ref, do_ref, lse_ref, fk_ref = refs[:7]
        dq_ref, dk_ref, dv_ref, dfk_ref, dfq_ref = refs[7 + nh:12 + nh]
        (q2_ref, do2_ref, lse2_ref, delta2_ref, dq2_acc, dfq2_acc, dk_acc, dv_acc,
         dfk_acc) = refs[12 + 2 * nh:21 + 2 * nh]
        if hosted:
            b, hp = pl.program_id(0), pl.program_id(1)
            start, finish = hosted.run((b == 0) & (hp == 0), (b == B - 1) & (hp == HP - 1),
                                       refs[7:7 + nh], refs[12 + nh:12 + 2 * nh], refs[21 + 2 * nh:])
            start()
        masks = _head_masks()
        rr = lax.broadcasted_iota(jnp.int32, (tq, tq), 0)
        cc = lax.broadcasted_iota(jnp.int32, (tq, tq), 1)
        causal = rr >= cc
        sc = jnp.asarray(scale, BF16)

        def stage(i, c):
            rows = pl.ds(pl.multiple_of(i * tq, tq), tq)
            dov = do_ref[0, rows, :]
            q2_ref[i] = _stack_heads(q_ref[0, rows, :], masks) * sc
            do2_ref[i] = _stack_heads(dov, masks)
            prod = dov.astype(F32) * o_ref[0, rows, :].astype(F32)
            delta2_ref[i] = jnp.concatenate(
                [jnp.sum(jnp.where(masks[h], prod, 0.0), axis=-1, keepdims=True) for h in range(2)], axis=0)
            lv = lse_ref[0, 0, rows, :]
            lse2_ref[i] = jnp.concatenate([lv[:, 0:1], lv[:, 1:2]], axis=0)
            return c

        lax.fori_loop(0, nq, stage, 0)
        dq2_acc[...] = jnp.zeros_like(dq2_acc)
        dfq2_acc[...] = jnp.zeros_like(dfq2_acc)

        def kv_block(j, carry):
            rows_j = pl.ds(pl.multiple_of(j * tq, tq), tq)
            kj = k_ref[0, rows_j, :]
            vj = v_ref[0, rows_j, :]
            ks = kj * sc
            dk_acc[...] = jnp.zeros_like(dk_acc)
            dv_acc[...] = jnp.zeros_like(dv_acc)
            dfk_acc[...] = jnp.zeros_like(dfk_acc)

            def logits(i):
                return (lax.dot_general(q2_ref[i], kj, NT, preferred_element_type=F32),
                        lax.dot_general(do2_ref[i], vj, NT, preferred_element_type=F32))

            def probs(i, s2, dp2, masked):
                lse2 = lse2_ref[i]
                delta2 = delta2_ref[i]
                ps, dss = [], []
                for h in range(2):
                    half = slice(h * tq, (h + 1) * tq)
                    p = jnp.exp(s2[half] - fk_ref[0, 0, h, pl.ds(j, 1), :] - lse2[half])
                    if masked:
                        p = jnp.where(causal, p, 0.0)
                    ds = p * (dp2[half] - delta2[half])
                    dfk_acc[h:h + 1, :] -= jnp.sum(ds, axis=0, keepdims=True)
                    dfq2_acc[i, half, :] += jnp.sum(ds, axis=1, keepdims=True)
                    ps.append(p.astype(BF16))
                    dss.append(ds.astype(BF16))
                return jnp.concatenate(ps, axis=0), jnp.concatenate(dss, axis=0)

            def grads(i, p2, ds2):
                dv_acc[...] += lax.dot_general(p2, do2_ref[i], TN, preferred_element_type=F32)
                dk_acc[...] += lax.dot_general(ds2, q2_ref[i], TN, preferred_element_type=F32)
                dq2_acc[i] += jnp.dot(ds2, ks, preferred_element_type=F32)

            grads(j, *probs(j, *logits(j), True))

            def rest(i, c):
                grads(i, *probs(i, *logits(i), False))
                return c

            lax.fori_loop(j + 1, nq, rest, 0)
            dk_ref[0, rows_j, :] = dk_acc[...].astype(BF16)
            dv_ref[0, rows_j, :] = dv_acc[...].astype(BF16)
            for h in range(2):
                dfk_ref[0, 0, h, pl.ds(j, 1), :] = dfk_acc[h:h + 1, :]
            return carry

        lax.fori_loop(0, nq, kv_block, 0)
        two = lax.broadcasted_iota(jnp.int32, (1, 2), 1)

        def finish_block(i, c):
            rows = pl.ds(pl.multiple_of(i * tq, tq), tq)
            dq2 = dq2_acc[i]
            dq_ref[0, rows, :] = jnp.where(masks[0], dq2[:tq], dq2[tq:]).astype(BF16)
            dfq2 = dfq2_acc[i]
            dfq_ref[0, 0, rows, :] = jnp.where(two == 0, dfq2[:tq], dfq2[tq:])
            return c

        lax.fori_loop(0, nq, finish_block, 0)
        if hosted:
            finish()

    col = lambda off: pl.BlockSpec((1, S, LANES), lambda b, hp: (b, 0, off + hp))
    stat = pl.BlockSpec((1, 1, S, 2), lambda b, hp: (b, hp, 0, 0))
    rowf = pl.BlockSpec((1, 1, 2, nq, tq), lambda b, hp: (b, hp, 0, 0, 0))
    grad = jax.ShapeDtypeStruct((B, S, AW), BF16)
    outs = pl.pallas_call(
        body, name=name,
        out_shape=[grad, grad, grad, jax.ShapeDtypeStruct((B, HP, 2, nq, tq), F32),
                   jax.ShapeDtypeStruct((B, HP, S, 2), F32)] + (hosted.out_shape if hosted else []),
        grid=(B, HP),
        in_specs=[col(0), col(HP), col(2 * HP), col(0), col(0), stat, rowf] + [HBM] * nh,
        out_specs=[col(0), col(0), col(0), rowf, stat] + [HBM] * nh,
        scratch_shapes=[pltpu.VMEM((nq, 2 * tq, LANES), BF16), pltpu.VMEM((nq, 2 * tq, LANES), BF16),
                        pltpu.VMEM((nq, 2 * tq, 1), F32), pltpu.VMEM((nq, 2 * tq, 1), F32),
                        pltpu.VMEM((nq, 2 * tq, LANES), F32), pltpu.VMEM((nq, 2 * tq, 1), F32),
                        pltpu.VMEM((tq, LANES), F32), pltpu.VMEM((tq, LANES), F32), pltpu.VMEM((2, tq), F32)]
        + (hosted.scratch if hosted else []),
        compiler_params=_cparams("arbitrary", "arbitrary"),
    )(proj, proj, proj, ya, dya, lse, frow, *(hosted.arrays if hosted else []))
    return outs[:5], outs[5:]


def _attn_fwd_old(proj, fcol, frow, *, name, H, tq):
    B, S, _ = proj.shape
    HP = H // 2
    nq = S // tq
    scale = HEAD_DIM ** -0.5

    def body(q_ref, k_ref, v_ref, fq_ref, fk_ref, o_ref, lse_ref):
        i = pl.program_id(2)
        q = q_ref[0]
        masks = _head_masks()
        rr = lax.broadcasted_iota(jnp.int32, (tq, tq), 0)
        cc = lax.broadcasted_iota(jnp.int32, (tq, tq), 1)
        causal = rr >= cc
        outs, lses = [], []
        for h in range(2):
            qm = jnp.where(masks[h], q, jnp.zeros_like(q)) * jnp.asarray(scale, BF16)
            fq = fq_ref[0, 0][:, h:h + 1]

            def block(j, carry, masked, h=h, qm=qm, fq=fq):
                m, l, acc = carry
                rows = pl.ds(pl.multiple_of(j * tq, tq), tq)
                kj = k_ref[0, rows, :]
                vj = v_ref[0, rows, :]
                s = lax.dot_general(qm, kj, NT, preferred_element_type=F32)
                s = s + (fq - fk_ref[0, 0, h, pl.ds(j, 1), :])
                if masked:
                    s = jnp.where(causal, s, NEG)
                m_new = jnp.maximum(m, jnp.max(s, axis=-1, keepdims=True))
                alpha = jnp.exp(m - m_new)
                p = jnp.exp(s - m_new)
                l = alpha * l + jnp.sum(p, axis=-1, keepdims=True)
                acc = alpha * acc + jnp.dot(p.astype(BF16), vj, preferred_element_type=F32)
                return m_new, l, acc

            init = (jnp.full((tq, 1), NEG, F32), jnp.zeros((tq, 1), F32), jnp.zeros((tq, LANES), F32))
            carry = lax.fori_loop(0, i, lambda j, c, block=block: block(j, c, False), init)
            m, l, acc = block(i, carry, True)
            outs.append(acc / l)
            lses.append(m + jnp.log(l))
        o_ref[0] = jnp.where(masks[0], outs[0], outs[1]).astype(BF16)
        two = lax.broadcasted_iota(jnp.int32, (1, 2), 1)
        lse_ref[0, 0] = jnp.where(two == 0, lses[0], lses[1])

    kv = lambda off: pl.BlockSpec((1, S, LANES), lambda b, hp, i: (b, 0, off + hp))
    return pl.pallas_call(
        body, name=name,
        out_shape=(jax.ShapeDtypeStruct((B, S, H * HEAD_DIM), BF16), jax.ShapeDtypeStruct((B, HP, S, 2), F32)),
        grid=(B, HP, nq),
        in_specs=[pl.BlockSpec((1, tq, LANES), lambda b, hp, i: (b, i, hp)), kv(HP), kv(2 * HP),
                  pl.BlockSpec((1, 1, tq, 2), lambda b, hp, i: (b, hp, i, 0)),
                  pl.BlockSpec((1, 1, 2, nq, tq), lambda b, hp, i: (b, hp, 0, 0, 0))],
        out_specs=(pl.BlockSpec((1, tq, LANES), lambda b, hp, i: (b, i, hp)),
                   pl.BlockSpec((1, 1, tq, 2), lambda b, hp, i: (b, hp, i, 0))),
        compiler_params=_cparams("parallel", "parallel", "arbitrary"),
    )(proj, proj, proj, fcol, frow)


def _attn_bwd_old(proj, ya, dya, lse, fcol, frow, *, name, H, tq):
    B, S, _ = proj.shape
    HP = H // 2
    nq = S // tq
    AW = H * HEAD_DIM
    scale = HEAD_DIM ** -0.5

    def body(q_ref, k_ref, v_ref, o_ref, do_ref, lse_ref, fq_ref, fk_ref,
             dq_ref, dk_ref, dv_ref, dfk_ref, dfq_ref, dq_acc, dk_acc, dv_acc, delta_ref, dfk_acc, dfq_acc):
        masks = _head_masks()
        rr = lax.broadcasted_iota(jnp.int32, (tq, tq), 0)
        cc = lax.broadcasted_iota(jnp.int32, (tq, tq), 1)
        causal = rr >= cc
        sc = jnp.asarray(scale, BF16)
        prod = do_ref[0].astype(F32) * o_ref[0].astype(F32)
        for h in range(2):
            delta_ref[h] = jnp.sum(jnp.where(masks[h], prod, 0.0), axis=-1, keepdims=True)
        dq_acc[...] = jnp.zeros_like(dq_acc)
        dfq_acc[...] = jnp.zeros_like(dfq_acc)

        def kv_block(j, carry):
            rows_j = pl.ds(pl.multiple_of(j * tq, tq), tq)
            kj = k_ref[0, rows_j, :]
            vj = v_ref[0, rows_j, :]
            dk_acc[...] = jnp.zeros_like(dk_acc)
            dv_acc[...] = jnp.zeros_like(dv_acc)
            dfk_acc[...] = jnp.zeros_like(dfk_acc)
            kms = [jnp.where(masks[h], kj, jnp.zeros_like(kj)) * sc for h in range(2)]

            def q_block(i, masked):
                rows_i = pl.ds(pl.multiple_of(i * tq, tq), tq)
                qi = q_ref[0, rows_i, :]
                doi = do_ref[0, rows_i, :]
                fqi = fq_ref[0, 0, rows_i, :]
                lsei = lse_ref[0, 0, rows_i, :]
                for h in range(2):
                    qm = jnp.where(masks[h], qi, jnp.zeros_like(qi)) * sc
                    dom = jnp.where(masks[h], doi, jnp.zeros_like(doi))
                    s = lax.dot_general(qm, kj, NT, preferred_element_type=F32)
                    s = s + (fqi[:, h:h + 1] - fk_ref[0, 0, h, pl.ds(j, 1), :])
                    p = jnp.exp(s - lsei[:, h:h + 1])
                    if masked:
                        p = jnp.where(causal, p, 0.0)
                    dp = lax.dot_general(dom, vj, NT, preferred_element_type=F32)
                    ds = p * (dp - delta_ref[h, rows_i, :])
                    pb, dsb = p.astype(BF16), ds.astype(BF16)
                    dv_acc[...] += lax.dot_general(pb, dom, TN, preferred_element_type=F32)
                    dk_acc[...] += lax.dot_general(dsb, qm, TN, preferred_element_type=F32)
                    dq_acc[rows_i, :] += jnp.dot(dsb, kms[h], preferred_element_type=F32)
                    dfk_acc[h:h + 1, :] -= jnp.sum(ds, axis=0, keepdims=True)
                    dfq_acc[h, rows_i, :] += jnp.sum(ds, axis=1, keepdims=True)

            q_block(j, True)

            def rest(i, c):
                q_block(i, False)
                return c

            lax.fori_loop(j + 1, nq, rest, 0)
            dk_ref[0, rows_j, :] = dk_acc[...].astype(BF16)
            dv_ref[0, rows_j, :] = dv_acc[...].astype(BF16)
            for h in range(2):
                dfk_ref[0, 0, h, pl.ds(j, 1), :] = dfk_acc[h:h + 1, :]
            return carry

        lax.fori_loop(0, nq, kv_block, 0)
        dq_ref[0] = dq_acc[...].astype(BF16)
        two = lax.broadcasted_iota(jnp.int32, (1, 2), 1)
        dfq_ref[0, 0] = jnp.where(two == 0, dfq_acc[0], dfq_acc[1])

    col = lambda off: pl.BlockSpec((1, S, LANES), lambda b, hp: (b, 0, off + hp))
    stat = pl.BlockSpec((1, 1, S, 2), lambda b, hp: (b, hp, 0, 0))
    rowf = pl.BlockSpec((1, 1, 2, nq, tq), lambda b, hp: (b, hp, 0, 0, 0))
    grad = jax.ShapeDtypeStruct((B, S, AW), BF16)
    return pl.pallas_call(
        body, name=name,
        out_shape=(grad, grad, grad, jax.ShapeDtypeStruct((B, HP, 2, nq, tq), F32),
                   jax.ShapeDtypeStruct((B, HP, S, 2), F32)),
        grid=(B, HP),
        in_specs=[col(0), col(HP), col(2 * HP), col(0), col(0), stat, stat, rowf],
        out_specs=(col(0), col(0), col(0), rowf, stat),
        scratch_shapes=[pltpu.VMEM((S, LANES), F32), pltpu.VMEM((tq, LANES), F32), pltpu.VMEM((tq, LANES), F32),
                        pltpu.VMEM((2, S, 1), F32), pltpu.VMEM((2, tq), F32), pltpu.VMEM((2, S, 1), F32)],
        compiler_params=_cparams("parallel", "parallel"),
    )(proj, proj, proj, ya, dya, lse, fcol, frow)


def _cmul(ar, ai, br, bi):
    return ar * br - ai * bi, ar * bi + ai * br


def _ssm_states(u_ref, bm, lam_ref, pw_ref, lamT_ref, hr_ref, hi_ref, inr_ref, ini_ref, T, NC, SP):
    lr, li = lam_ref[0, 0:1, :], lam_ref[0, 1:2, :]
    bu = jnp.dot(u_ref[0, 0], bm, preferred_element_type=F32)
    hr_ref[0] = bu[:, :SP]
    hi_ref[0] = bu[:, SP:]

    def step(t, c):
        bu = jnp.dot(u_ref[0, t], bm, preferred_element_type=F32)
        pr, pi = _cmul(hr_ref[t - 1], hi_ref[t - 1], lr, li)
        hr_ref[t] = pr + bu[:, :SP]
        hi_ref[t] = pi + bu[:, SP:]
        return c

    lax.fori_loop(1, T, step, 0, unroll=2)

    tr, ti = lamT_ref[0, 0:1, :], lamT_ref[0, 1:2, :]
    inr_ref[0:1, :] = jnp.zeros((1, SP), F32)
    ini_ref[0:1, :] = jnp.zeros((1, SP), F32)

    def chunk(n, c):
        prev = pl.ds(n - 1, 1)
        pr, pi = _cmul(inr_ref[prev, :], ini_ref[prev, :], tr, ti)
        inr_ref[pl.ds(n, 1), :] = pr + hr_ref[T - 1, prev, :]
        ini_ref[pl.ds(n, 1), :] = pi + hi_ref[T - 1, prev, :]
        return c

    lax.fori_loop(1, NC, chunk, 0)


def _ssm_entry_term(t, pw_ref, inr_ref, ini_ref):
    return _cmul(inr_ref[...], ini_ref[...], pw_ref[0, 0, pl.ds(t, 1), :], pw_ref[0, 1, pl.ds(t, 1), :])


def _ssm_fwd(u_tm, bmat, cmat, lam, pw, lamT, dskip, *, name):
    B, T, NC, W = u_tm.shape
    NS = W // LANES
    SP = bmat.shape[2] // 2

    def body(u_ref, b_ref, c_ref, lam_ref, pw_ref, lamT_ref, d_ref, y_ref, hr_ref, hi_ref, inr_ref, ini_ref):
        _ssm_states(u_ref, b_ref[0], lam_ref, pw_ref, lamT_ref, hr_ref, hi_ref, inr_ref, ini_ref, T, NC, SP)
        cm = c_ref[0]
        dv = d_ref[...]

        def out(t, c):
            cr, ci = _ssm_entry_term(t, pw_ref, inr_ref, ini_ref)
            hcat = jnp.concatenate([hr_ref[t] + cr, hi_ref[t] + ci], axis=1).astype(BF16)
            y_ref[0, t] = jnp.dot(hcat, cm, preferred_element_type=F32) + dv * u_ref[0, t].astype(F32)
            return c

        lax.fori_loop(0, T, out, 0, unroll=2)

    slab = lambda *shape: pl.BlockSpec((1,) + shape, lambda b, s: (s,) + (0,) * len(shape))
    tok = pl.BlockSpec((1, T, NC, LANES), lambda b, s: (b, 0, 0, s))
    return pl.pallas_call(
        body, name=name,
        out_shape=jax.ShapeDtypeStruct((B, T, NC, W), F32),
        grid=(B, NS),
        in_specs=[tok, slab(LANES, 2 * SP), slab(2 * SP, LANES), slab(2, SP), slab(2, T, SP), slab(2, SP),
                  pl.BlockSpec((1, LANES), lambda b, s: (0, s))],
        out_specs=tok,
        scratch_shapes=[pltpu.VMEM((T, NC, SP), F32), pltpu.VMEM((T, NC, SP), F32),
                        pltpu.VMEM((NC, SP), F32), pltpu.VMEM((NC, SP), F32)],
        compiler_params=_cparams("parallel", "parallel"),
    )(u_tm, bmat, cmat, lam, pw, lamT, dskip)


def _ssm_bwd(u_tm, dy_tm, bmat, bmat_t, cmat_t, lam, pw, lamT, dskip, *, name):
    B, T, NC, W = u_tm.shape
    NS = W // LANES
    SP = bmat.shape[2] // 2

    def body(u_ref, dy_ref, b_ref, bt_ref, ct_ref, lam_ref, pw_ref, lamT_ref, d_ref,
             du_ref, gb_ref, gc_ref, glam_ref, gd_ref,
             hr_ref, hi_ref, ar_ref, ai_ref, inr_ref, ini_ref, anr_ref, ani_ref):
        b = pl.program_id(1)
        _ssm_states(u_ref, b_ref[0], lam_ref, pw_ref, lamT_ref, hr_ref, hi_ref, inr_ref, ini_ref, T, NC, SP)

        def fix(t, c):
            cr, ci = _ssm_entry_term(t, pw_ref, inr_ref, ini_ref)
            hr_ref[t] += cr
            hi_ref[t] += ci
            return c

        lax.fori_loop(0, T, fix, 0)
        lr, li = lam_ref[0, 0:1, :], lam_ref[0, 1:2, :]
        ct = ct_ref[0]
        bt = bt_ref[0]
        dv = d_ref[...]

        gh = jnp.dot(dy_ref[0, T - 1].astype(BF16), ct, preferred_element_type=F32)
        ar_ref[T - 1] = gh[:, :SP]
        ai_ref[T - 1] = gh[:, SP:]

        def back(k, c):
            t = T - 2 - k
            gh = jnp.dot(dy_ref[0, t].astype(BF16), ct, preferred_element_type=F32)
            pr, pi = _cmul(ar_ref[t + 1], ai_ref[t + 1], lr, -li)
            ar_ref[t] = pr + gh[:, :SP]
            ai_ref[t] = pi + gh[:, SP:]
            return c

        lax.fori_loop(0, T - 1, back, 0, unroll=2)

        tr, ti = lamT_ref[0, 0:1, :], lamT_ref[0, 1:2, :]
        anr_ref[NC - 1:NC, :] = jnp.zeros((1, SP), F32)
        ani_ref[NC - 1:NC, :] = jnp.zeros((1, SP), F32)

        def chunk(k, c):
            n = NC - 2 - k
            nxt = pl.ds(n + 1, 1)
            pr, pi = _cmul(anr_ref[nxt, :], ani_ref[nxt, :], tr, -ti)
            anr_ref[pl.ds(n, 1), :] = pr + ar_ref[0, nxt, :]
            ani_ref[pl.ds(n, 1), :] = pi + ai_ref[0, nxt, :]
            return c

        lax.fori_loop(0, NC - 1, chunk, 0)

        @pl.when(b == 0)
        def _():
            gb_ref[...] = jnp.zeros_like(gb_ref)
            gc_ref[...] = jnp.zeros_like(gc_ref)
            glam_ref[...] = jnp.zeros_like(glam_ref)
            gd_ref[...] = jnp.zeros_like(gd_ref)

        def final(t, hpr, hpi, gl):
            back_pow = pl.ds(T - 1 - t, 1)
            cr, ci = _cmul(anr_ref[...], ani_ref[...], pw_ref[0, 0, back_pow, :], -pw_ref[0, 1, back_pow, :])
            a_r = ar_ref[t] + cr
            a_i = ai_ref[t] + ci
            gl = (gl[0] + jnp.sum(a_r * hpr + a_i * hpi, axis=0, keepdims=True),
                  gl[1] + jnp.sum(a_i * hpr - a_r * hpi, axis=0, keepdims=True))
            acat = jnp.concatenate([a_r, a_i], axis=1).astype(BF16)
            ut = u_ref[0, t]
            dyt = dy_ref[0, t]
            du_ref[0, t] = (jnp.dot(acat, bt, preferred_element_type=F32) + dv * dyt).astype(BF16)
            gb_ref[0] += lax.dot_general(acat, ut, TN, preferred_element_type=F32)
            hcat = jnp.concatenate([hr_ref[t], hi_ref[t]], axis=1).astype(BF16)
            gc_ref[0] += lax.dot_general(dyt.astype(BF16), hcat, TN, preferred_element_type=F32)
            gd_ref[0] += jnp.sum(dyt * ut.astype(F32), axis=0, keepdims=True)
            return gl

        zero = jnp.zeros((1, SP), F32)
        gl = final(0, inr_ref[...], ini_ref[...], (zero, zero))
        gl = lax.fori_loop(1, T, lambda t, gl: final(t, hr_ref[t - 1], hi_ref[t - 1], gl), gl)
        glam_ref[0, 0:1, :] += gl[0]
        glam_ref[0, 1:2, :] += gl[1]

    slab = lambda *shape: pl.BlockSpec((1,) + shape, lambda s, b: (s,) + (0,) * len(shape))
    tok = pl.BlockSpec((1, T, NC, LANES), lambda s, b: (b, 0, 0, s))
    big = pltpu.VMEM((T, NC, SP), F32)
    small = pltpu.VMEM((NC, SP), F32)
    return pl.pallas_call(
        body, name=name,
        out_shape=(jax.ShapeDtypeStruct((B, T, NC, W), BF16),
                   jax.ShapeDtypeStruct((NS, 2 * SP, LANES), F32), jax.ShapeDtypeStruct((NS, LANES, 2 * SP), F32),
                   jax.ShapeDtypeStruct((NS, 2, SP), F32), jax.ShapeDtypeStruct((NS, 1, LANES), F32)),
        grid=(NS, B),
        in_specs=[tok, tok, slab(LANES, 2 * SP), slab(2 * SP, LANES), slab(LANES, 2 * SP), slab(2, SP),
                  slab(2, T, SP), slab(2, SP), pl.BlockSpec((1, LANES), lambda s, b: (0, s))],
        out_specs=(tok, slab(2 * SP, LANES), slab(LANES, 2 * SP), slab(2, SP), slab(1, LANES)),
        scratch_shapes=[big, big, big, big, small, small, small, small],
        compiler_params=_cparams("parallel", "arbitrary"),
    )(u_tm, dy_tm, bmat, bmat_t, cmat_t, lam, pw, lamT, dskip)


def _glu_fwd(ys, w, b, *, name, tr=512):
    n, wd = ys.shape
    tr = _tile(n, tr, 8)

    def body(y_ref, w_ref, b_ref, o_ref):
        yb = _gelu(y_ref[...])
        z = jnp.dot(yb.astype(BF16), w_ref[...], preferred_element_type=F32) + b_ref[...]
        o_ref[...] = (yb * _sigmoid(z)).astype(BF16)

    row = pl.BlockSpec((tr, wd), lambda i: (i, 0))
    return pl.pallas_call(
        body, name=name, out_shape=jax.ShapeDtypeStruct((n, wd), BF16), grid=(n // tr,),
        in_specs=[row, pl.BlockSpec((wd, wd), lambda i: (0, 0)), pl.BlockSpec((1, wd), lambda i: (0, 0))],
        out_specs=row, compiler_params=_cparams("parallel"),
    )(ys, w, b.reshape(1, wd))


def _glu_bwd(ys, dyb2, w, w_t, b, *, name, tr=512):
    n, wd = ys.shape
    tr = _tile(n, tr, 8)

    def body(y_ref, d_ref, w_ref, wt_ref, b_ref, dys_ref, dz_ref, yb_ref, db_ref):
        i = pl.program_id(0)
        yv = y_ref[...]
        yb = _gelu(yv)
        ybb = yb.astype(BF16)
        sg = _sigmoid(jnp.dot(ybb, w_ref[...], preferred_element_type=F32) + b_ref[...])
        dv = d_ref[...].astype(F32)
        dz = dv * yb * sg * (1.0 - sg)
        dzb = dz.astype(BF16)
        dyb = dv * sg + jnp.dot(dzb, wt_ref[...], preferred_element_type=F32)
        dys_ref[...] = dyb * _gelu_grad(yv)
        dz_ref[...] = dzb
        yb_ref[...] = ybb
        part = jnp.sum(dz, axis=0, keepdims=True)

        @pl.when(i == 0)
        def _():
            db_ref[...] = part

        @pl.when(i > 0)
        def _():
            db_ref[...] += part

    row = pl.BlockSpec((tr, wd), lambda i: (i, 0))
    mat = pl.BlockSpec((wd, wd), lambda i: (0, 0))
    vec = pl.BlockSpec((1, wd), lambda i: (0, 0))
    return pl.pallas_call(
        body, name=name,
        out_shape=(jax.ShapeDtypeStruct((n, wd), F32), jax.ShapeDtypeStruct((n, wd), BF16),
                   jax.ShapeDtypeStruct((n, wd), BF16), jax.ShapeDtypeStruct((1, wd), F32)),
        grid=(n // tr,), in_specs=[row, row, mat, mat, vec], out_specs=(row, row, row, vec),
        compiler_params=_cparams("arbitrary"),
    )(ys, dyb2, w, w_t, b.reshape(1, wd))


def _merge_fwd(ya, yb2, wa, wb, proj, gate_blk, *, name, tr=512):
    n, aw = ya.shape
    d = wa.shape[1]
    tr = _tile(n, tr, 8)

    def body(ya_ref, yb_ref, wa_ref, wb_ref, ga_ref, gb_ref, mix_ref, pa_ref, pb_ref):
        pa = jnp.dot(ya_ref[...], wa_ref[...], preferred_element_type=F32)
        pb = jnp.dot(yb_ref[...], wb_ref[...], preferred_element_type=F32)
        mix = _sigmoid(ga_ref[...].astype(F32)) * pa + _sigmoid(gb_ref[...].astype(F32)) * pb
        mix_ref[...] = mix.astype(BF16)
        pa_ref[...] = pa.astype(BF16)
        pb_ref[...] = pb.astype(BF16)

    row = lambda wdt: pl.BlockSpec((tr, wdt), lambda i: (i, 0))
    full = lambda r, c: pl.BlockSpec((r, c), lambda i: (0, 0))
    out = jax.ShapeDtypeStruct((n, d), BF16)
    return pl.pallas_call(
        body, name=name, out_shape=(out, out, out), grid=(n // tr,),
        in_specs=[row(aw), row(yb2.shape[1]), full(*wa.shape), full(*wb.shape),
                  pl.BlockSpec((tr, d), lambda i: (i, gate_blk)), pl.BlockSpec((tr, d), lambda i: (i, gate_blk + 1))],
        out_specs=(row(d), row(d), row(d)), compiler_params=_cparams("parallel"),
    )(ya, yb2, wa, wb, proj, proj)


def _merge_bwd(dmix, proj, pa, pb, gate_blk, *, name, tr=512):
    n, d = dmix.shape
    tr = _tile(n, tr, 8)

    def body(dm_ref, ga_ref, gb_ref, pa_ref, pb_ref, dpa_ref, dpb_ref, dga_ref, dgb_ref):
        dm = dm_ref[...].astype(F32)
        sa = _sigmoid(ga_ref[...].astype(F32))
        sb = _sigmoid(gb_ref[...].astype(F32))
        dpa_ref[...] = (dm * sa).astype(BF16)
        dpb_ref[...] = (dm * sb).astype(BF16)
        dga_ref[...] = (dm * pa_ref[...].astype(F32) * sa * (1.0 - sa)).astype(BF16)
        dgb_ref[...] = (dm * pb_ref[...].astype(F32) * sb * (1.0 - sb)).astype(BF16)

    row = pl.BlockSpec((tr, d), lambda i: (i, 0))
    out = jax.ShapeDtypeStruct((n, d), BF16)
    return pl.pallas_call(
        body, name=name, out_shape=(out, out, out, out), grid=(n // tr,),
        in_specs=[row, pl.BlockSpec((tr, d), lambda i: (i, gate_blk)), pl.BlockSpec((tr, d), lambda i: (i, gate_blk + 1)),
                  row, row],
        out_specs=(row, row, row, row), compiler_params=_cparams("parallel"),
    )(dmix, proj, proj, pa, pb)


def _outproj_fwd(mixed, w, x0, g, *, name, tr=512):
    n, d = x0.shape
    tr = _tile(n, tr, 8)

    def body(m_ref, w_ref, x_ref, g_ref, x1_ref, h_ref, r_ref):
        x1 = x_ref[...] + jnp.dot(m_ref[...], w_ref[...], preferred_element_type=F32)
        r = lax.rsqrt(jnp.mean(x1 * x1, axis=-1, keepdims=True) + RMS_EPS)
        x1_ref[...] = x1
        h_ref[...] = (x1 * r * g_ref[...]).astype(BF16)
        r_ref[...] = r

    row = pl.BlockSpec((tr, d), lambda i: (i, 0))
    return pl.pallas_call(
        body, name=name,
        out_shape=(jax.ShapeDtypeStruct((n, d), F32), jax.ShapeDtypeStruct((n, d), BF16),
                   jax.ShapeDtypeStruct((n, 1), F32)),
        grid=(n // tr,),
        in_specs=[row, pl.BlockSpec((d, d), lambda i: (0, 0)), row, pl.BlockSpec((1, d), lambda i: (0, 0))],
        out_specs=(row, row, pl.BlockSpec((tr, 1), lambda i: (i, 0))),
        compiler_params=_cparams("parallel"),
    )(mixed, w, x0, g.reshape(1, d))


def _adamw(w, g, m, v, *, name):
    shape = w.shape
    total = w.size
    if total % PACK_COLS == 0 and ((total // PACK_COLS) % 8 == 0 or total // PACK_COLS <= 512):
        rows, cols = total // PACK_COLS, PACK_COLS
    elif w.ndim >= 2:
        rows, cols = total // shape[-1], shape[-1]
    else:
        rows, cols = 1, total
    tr = _tile(rows, 512, 8)

    def body(w_ref, g_ref, m_ref, v_ref, d_ref, nm_ref, nv_ref):
        gv = g_ref[...]
        mn = ADAM_B1 * m_ref[...] + (1.0 - ADAM_B1) * gv
        vn = ADAM_B2 * v_ref[...] + (1.0 - ADAM_B2) * (gv * gv)
        m_hat = mn / (1.0 - ADAM_B1 ** ADAM_STEP)
        v_hat = vn / (1.0 - ADAM_B2 ** ADAM_STEP)
        d_ref[...] = -ADAM_LR * (m_hat / (jnp.sqrt(v_hat) + ADAM_EPS) + ADAM_WD * w_ref[...])
        nm_ref[...] = mn
        nv_ref[...] = vn

    blk = pl.BlockSpec((tr, cols), lambda i: (i, 0))
    out = jax.ShapeDtypeStruct((rows, cols), F32)
    outs = pl.pallas_call(
        body, name=name, out_shape=(out, out, out), grid=(rows // tr,),
        in_specs=[blk] * 4, out_specs=(blk, blk, blk), compiler_params=_cparams("parallel"),
    )(*[t.reshape(rows, cols) for t in (w, g, m, v)])
    return tuple(o.reshape(shape) for o in outs)


HBM = pl.BlockSpec(memory_space=pltpu.HBM)
MESH = pl.DeviceIdType.MESH


def _all_gather(blocks, *, name):
    n = len(blocks)

    def body(*refs):
        x_refs, out_refs = refs[:n], refs[n:2 * n]
        send_sems, recv_sems, local_sems = refs[2 * n:]
        x, y, c = lax.axis_index("x"), lax.axis_index("y"), lax.axis_index("c")
        me, sibling = (x, y, c), (x, y, 1 - c)
        chips = [(1 - x, y), (x, 1 - y), (1 - x, 1 - y)]

        def slot(a, px, py, pc):
            return out_refs[a].at[4 * px + 2 * py + pc]

        def copy(a, k, block, to, src=None):
            return pltpu.make_async_remote_copy(
                src_ref=slot(a, *block) if src is None else src, dst_ref=slot(a, *block),
                send_sem=send_sems.at[7 * a + k], recv_sem=recv_sems.at[7 * a + k], device_id=to,
                device_id_type=MESH)

        started = []
        for a in range(n):
            mine = pltpu.make_async_copy(x_refs[a], slot(a, *me), local_sems.at[a])
            mine.start()
            started.append(mine)
        sends = []
        for a in range(n):
            first = [copy(a, 0, me, sibling, src=x_refs[a])]
            first += [copy(a, 1 + j, me, (*chip, c), src=x_refs[a]) for j, chip in enumerate(chips)]
            for cp in first:
                cp.start()
            sends += first
        for a in range(n):
            for j, chip in enumerate(chips):
                copy(a, 1 + j, (*chip, c), me).wait_recv()
                onward = copy(a, 4 + j, (*chip, c), sibling)
                onward.start()
                sends.append(onward)
        for a in range(n):
            copy(a, 0, sibling, me).wait_recv()
            for j, chip in enumerate(chips):
                copy(a, 4 + j, (*chip, 1 - c), me).wait_recv()
        for cp in sends:
            cp.wait_send()
        for mine in started:
            mine.wait()

    return pl.pallas_call(
        body, name=name, out_shape=[jax.ShapeDtypeStruct((N_DEV,) + b.shape, b.dtype) for b in blocks],
        in_specs=[HBM] * n, out_specs=[HBM] * n,
        scratch_shapes=[pltpu.SemaphoreType.DMA((7 * n,)), pltpu.SemaphoreType.DMA((7 * n,)),
                        pltpu.SemaphoreType.DMA((n,))],
    )(*blocks)


def _all_to_all(blocks, *, name):
    n = len(blocks)

    def body(*refs):
        x_refs, out_refs = refs[:n], refs[n:2 * n]
        send_sems, recv_sems, local_sems = refs[2 * n:]
        x, y, c = lax.axis_index("x"), lax.axis_index("y"), lax.axis_index("c")
        me = 4 * x + 2 * y + c
        copies = []
        for a in range(n):
            mine = pltpu.make_async_copy(x_refs[a].at[me], out_refs[a].at[me], local_sems.at[a])
            mine.start()
            copies.append(mine)
        for k in range(1, N_DEV):
            px = x if not (k >> 2) & 1 else 1 - x
            py = y if not (k >> 1) & 1 else 1 - y
            pc = c if not k & 1 else 1 - c
            for a in range(n):
                cp = pltpu.make_async_remote_copy(
                    src_ref=x_refs[a].at[4 * px + 2 * py + pc], dst_ref=out_refs[a].at[me],
                    send_sem=send_sems.at[7 * a + k - 1], recv_sem=recv_sems.at[7 * a + k - 1],
                    device_id=(px, py, pc), device_id_type=MESH)
                cp.start()
                copies.append(cp)
        for cp in copies:
            cp.wait()

    return pl.pallas_call(
        body, name=name, out_shape=[jax.ShapeDtypeStruct(b.shape, b.dtype) for b in blocks],
        in_specs=[HBM] * n, out_specs=[HBM] * n,
        scratch_shapes=[pltpu.SemaphoreType.DMA((7 * n,)), pltpu.SemaphoreType.DMA((7 * n,)),
                        pltpu.SemaphoreType.DMA((n,))],
    )(*blocks)


def _sum8(blocks, *, name, tr=PACK_ROWS):
    _, R, C = blocks.shape
    tr = _tile(R, tr, 16)

    def body(x_ref, o_ref):
        acc = x_ref[0].astype(F32)
        for i in range(1, N_DEV):
            acc = acc + x_ref[i].astype(F32)
        o_ref[...] = acc

    return pl.pallas_call(
        body, name=name, out_shape=jax.ShapeDtypeStruct((R, C), F32), grid=(R // tr,),
        in_specs=[pl.BlockSpec((N_DEV, tr, C), lambda i: (0, i, 0))],
        out_specs=pl.BlockSpec((tr, C), lambda i: (i, 0)), compiler_params=_cparams("parallel"),
    )(blocks)


def _pack_rows(flat_last):
    n = flat_last.shape[-1]
    unit = PACK_ROWS * PACK_COLS
    padded = -(-n // unit) * unit
    pad = [(0, 0)] * (flat_last.ndim - 1) + [(0, padded - n)]
    return jnp.pad(flat_last, pad).reshape(flat_last.shape[:-1] + (padded // PACK_COLS, PACK_COLS))


def _ssm_discretise(lre, lim, logdt, bre, bim):
    lam = lax.complex(lre, lim)
    dt = jnp.exp(logdt)[:, None]
    lam_bar = jnp.exp(lam * dt)
    b_bar = ((lam_bar - 1.0) / lam)[:, :, None] * lax.complex(bre, bim)
    return lam_bar.real, lam_bar.imag, b_bar.real, b_bar.imag


def _block_diag(a, rows_first):
    ns, g, r, c = a.shape
    eye = jnp.eye(g, dtype=a.dtype)
    return jnp.einsum("sgrc,gh->sgrhc", a, eye).reshape(ns, g * r, g * c)


def _diag_blocks(m, r, c):
    ns = m.shape[0]
    g = SLAB_GROUPS
    return jnp.einsum("sgrhc,gh->sgrc", m.reshape(ns, g, r, g, c), jnp.eye(g, dtype=m.dtype))


def _to_tm(a, T):
    b, s, w = a.shape
    return a.reshape(b, s // T, T, w).transpose(0, 2, 1, 3)


def _from_tm(a):
    b, t, nc, w = a.shape
    return a.transpose(0, 2, 1, 3).reshape(b, nc * t, w)


WEIGHTS = ["norm_mix", "w_in", "b_forget", "ssm_lambda_re", "ssm_lambda_im", "ssm_log_dt", "ssm_b_re", "ssm_b_im",
           "ssm_c_re", "ssm_c_im", "ssm_d", "w_glu", "b_glu", "w_branch_a", "w_branch_b", "w_out", "norm_mlp",
           "w_mlp_up", "w_mlp_down", "norm_final"]
SHARDED = {"w_in": 2, "w_glu": 1, "w_branch_a": 2, "w_branch_b": 2, "w_out": 1, "w_mlp_up": 2, "w_mlp_down": 1}


REST = [n for n in SHARDED if n != "w_in"]


def _whole(n, seg):
    ax = SHARDED[n] - 1
    shp = seg.shape[1:]
    return jnp.moveaxis(seg, 0, ax).reshape(shp[:ax] + (N_DEV * shp[ax],) + shp[ax + 1:])


def _blocks(n, g):
    ax = SHARDED[n] - 1
    shp = g.shape
    return jnp.moveaxis(g.reshape(shp[:ax] + (N_DEV, shp[ax] // N_DEV) + shp[ax + 1:]), ax, 0)


def _sum_blocks(n, got):
    return _sum8(got.reshape(N_DEV, -1, got.shape[-1]), name="sum_grads_" + n).reshape(got.shape[1:])


def _allreduce_small(grads):
    names = list(grads)
    flat = jnp.concatenate([grads[n].astype(F32).reshape(-1) for n in names])
    got = _all_gather([_pack_rows(flat)], name="gather_small_grads")[0]
    summed = _sum8(got, name="sum_small_grads").reshape(-1)
    out, off = {}, 0
    for n in names:
        out[n] = summed[off:off + grads[n].size].reshape(grads[n].shape)
        off += grads[n].size
    return out


def kernel(x, norm_mix, w_in, b_forget, ssm_lambda_re, ssm_lambda_im, ssm_log_dt, ssm_b_re, ssm_b_im, ssm_c_re, ssm_c_im, ssm_d, w_glu, b_glu, w_branch_a, w_branch_b, w_out, norm_mlp, w_mlp_up, w_mlp_down, norm_final, loss_target, m_norm_mix, m_w_in, m_b_forget, m_ssm_lambda_re, m_ssm_lambda_im, m_ssm_log_dt, m_ssm_b_re, m_ssm_b_im, m_ssm_c_re, m_ssm_c_im, m_ssm_d, m_w_glu, m_b_glu, m_w_branch_a, m_w_branch_b, m_w_out, m_norm_mlp, m_w_mlp_up, m_w_mlp_down, m_norm_final, v_norm_mix, v_w_in, v_b_forget, v_ssm_lambda_re, v_ssm_lambda_im, v_ssm_log_dt, v_ssm_b_re, v_ssm_b_im, v_ssm_c_re, v_ssm_c_im, v_ssm_d, v_w_glu, v_b_glu, v_w_branch_a, v_w_branch_b, v_w_out, v_norm_mlp, v_w_mlp_up, v_w_mlp_down, v_norm_final):
    args = dict(locals())
    w = {n: args[n] for n in WEIGHTS}
    Bl, S, D = x.shape
    L, H = b_forget.shape
    G, P, C = ssm_b_re.shape[1:]
    AW, W, HP = H * HEAD_DIM, G * C, H // 2
    N = Bl * S
    T = SSM_CHUNK
    NS = G // SLAB_GROUPS
    SP = SLAB_GROUPS * P
    tq = min(ATTN_BLOCK, S)
    nq = S // tq
    u_off = 3 * AW
    gate_blk = (u_off + W) // D
    assert (u_off + W) % D == 0 and W % LANES == 0 and AW % LANES == 0 and S % T == 0

    shard = {n: w[n].astype(BF16) for n in SHARDED}
    weights = [dict() for _ in range(L)]
    weights[0]["w_in"] = _whole("w_in", _all_gather([shard["w_in"][0]], name="gather_first")[0])
    tr_ = lambda a: jnp.swapaxes(a, 1, 2)

    ssm = []
    for l in range(L):
        disc, disc_vjp = jax.vjp(_ssm_discretise, ssm_lambda_re[l], ssm_lambda_im[l], ssm_log_dt[l],
                                 ssm_b_re[l], ssm_b_im[l])
        lbr, lbi, bbr, bbi = disc
        z = lax.complex(ssm_lambda_re[l], ssm_lambda_im[l]) * jnp.exp(ssm_log_dt[l])[:, None]
        powers = jnp.exp(z[None] * jnp.arange(1, T + 1, dtype=F32)[:, None, None])
        slabs = lambda a: a.reshape(NS, SP)
        lam = jnp.stack([slabs(lbr), slabs(lbi)], axis=1)
        lam_t = jnp.stack([slabs(powers[T - 1].real), slabs(powers[T - 1].imag)], axis=1)
        pw = jnp.stack([powers.real.reshape(T, NS, SP), powers.imag.reshape(T, NS, SP)], axis=0).transpose(2, 0, 1, 3)
        to_rows = lambda a: jnp.swapaxes(a.reshape(NS, SLAB_GROUPS, P, C), 2, 3)
        bmat = jnp.concatenate([_block_diag(to_rows(bbr), True), _block_diag(to_rows(bbi), True)], axis=2)
        cre = ssm_c_re[l].reshape(NS, SLAB_GROUPS, C, P)
        cim = ssm_c_im[l].reshape(NS, SLAB_GROUPS, C, P)
        cmat_t = jnp.concatenate([_block_diag(cre, True), -_block_diag(cim, True)], axis=2)
        ssm.append(dict(vjp=disc_vjp, lam=lam, lam_t=lam_t, pw=pw, bmat=bmat.astype(BF16),
                        bmat_t=tr_(bmat).astype(BF16), cmat=tr_(cmat_t).astype(BF16), cmat_t=cmat_t.astype(BF16),
                        d=ssm_d[l].reshape(1, W)))

    xcur = x.reshape(N, D)
    saved = []
    for l in range(L):
        s_, wl = ssm[l], weights[l]
        win = wl["w_in"]
        wl["wcat"] = jnp.concatenate([win[:, :3 * AW], win[:, 3 * AW + H:]], axis=1)
        wl["wf"] = jnp.pad(win[:, 3 * AW:3 * AW + H], ((0, 0), (0, LANES - H)))
        h, r0 = _rmsnorm_fwd(xcur, norm_mix[l], name="norm_mix_fwd")
        proj = _mm(h, wl["wcat"], name="in_proj")
        fl = _mm(h, wl["wf"], name="forget_proj", out_dtype=F32)
        ft = fl[:, :H].reshape(Bl, S, H).transpose(0, 2, 1)
        F = _fox_gate_fwd(ft, b_forget[l], name="forget_gate_fwd")
        frow = F.reshape(Bl, HP, 2, nq, tq)
        proj3 = proj.reshape(Bl, S, -1)
        coming = [shard[n][l] for n in REST] + ([shard["w_in"][l + 1]] if l + 1 < L else [])
        ya, lse, got = _attn_fwd(proj3, frow, name="attn_fwd" if l + 1 < L else "attn_fwd_last", H=H, tq=tq,
                                 hosted=_Hosted("gather", coming))
        for n, seg in zip(REST, got):
            wl[n] = _whole(n, seg)
        if l + 1 < L:
            weights[l + 1]["w_in"] = _whole("w_in", got[-1])
        u_tm = _to_tm(proj3[:, :, u_off:u_off + W], T)
        ys = _from_tm(_ssm_fwd(u_tm, s_["bmat"], s_["cmat"], s_["lam"], s_["pw"], s_["lam_t"], s_["d"],
                               name="ssm_fwd")).reshape(N, W)
        yb2 = _glu_fwd(ys, wl["w_glu"], b_glu[l], name="glu_fwd")
        ya2 = ya.reshape(N, AW)
        mixed, pa, pb = _merge_fwd(ya2, yb2, wl["w_branch_a"], wl["w_branch_b"], proj, gate_blk, name="merge_fwd")
        x1, h2, r1 = _outproj_fwd(mixed, wl["w_out"], xcur, norm_mlp[l], name="out_proj")
        a = _mm(h2, wl["w_mlp_up"], name="mlp_up")
        x2 = _mm(a, wl["w_mlp_down"], name="mlp_down", a_fn=_relu_sq, epi=lambda acc, res: acc + res,
                 extras=(x1,), out_dtype=F32)
        saved.append(dict(x0=xcur, h=h, r0=r0, proj=proj, ft=ft, frow=frow, ya=ya, lse=lse, u_tm=u_tm,
                          ys=ys, yb2=yb2, mixed=mixed, pa=pa, pb=pb, x1=x1, h2=h2, r1=r1, a=a))
        xcur = x2

    dx, g_final, loss_row = _loss_head(xcur, norm_final, loss_target.reshape(N, D), name="loss_head")
    loss = lax.psum(loss_row[0, 0], MESH_AXES)

    big = {n: [None] * L for n in SHARDED}
    small = {n: [None] * L for n in WEIGHTS if n not in SHARDED and n != "norm_final"}
    win_grad = None
    for l in reversed(range(L)):
        sv, s_, wl = saved[l], ssm[l], weights[l]
        a = sv["a"]
        gw = {}
        d_a = _mm(dx, wl["w_mlp_down"].T, name="mlp_down_dx",
                  epi=lambda acc, av: acc * (2.0 * jnp.maximum(av.astype(F32), 0.0)), extras=(a,))
        gw["w_mlp_down"] = _mm(a, dx, name="mlp_down_dw", ta=True, a_fn=_relu_sq)
        gw["w_mlp_up"] = _mm(sv["h2"], d_a, name="mlp_up_dw", ta=True)
        dh2 = _mm(d_a, wl["w_mlp_up"].T, name="mlp_up_dx", out_dtype=F32)
        dx1, g = _rmsnorm_bwd(dh2, sv["x1"], sv["r1"], norm_mlp[l], dx, name="norm_mlp_bwd")
        small["norm_mlp"][l] = g[0]
        dmix = _mm(dx1, wl["w_out"].T, name="out_proj_dx")
        gw["w_out"] = _mm(sv["mixed"], dx1, name="out_proj_dw", ta=True)
        dpa, dpb, dga, dgb = _merge_bwd(dmix, sv["proj"], sv["pa"], sv["pb"], gate_blk, name="merge_bwd")
        ya2 = sv["ya"].reshape(N, AW)
        gw["w_branch_a"] = _mm(ya2, dpa, name="branch_a_dw", ta=True)
        dya = _mm(dpa, wl["w_branch_a"].T, name="branch_a_dx")
        gw["w_branch_b"] = _mm(sv["yb2"], dpb, name="branch_b_dw", ta=True)
        dyb2 = _mm(dpb, wl["w_branch_b"].T, name="branch_b_dx")
        dys, dz, yb, g = _glu_bwd(sv["ys"], dyb2, wl["w_glu"], wl["w_glu"].T, b_glu[l], name="glu_bwd")
        small["b_glu"][l] = g[0]
        gw["w_glu"] = _mm(yb, dz, name="glu_dw", ta=True)

        du_tm, g_bt, g_ct, g_lam, g_d = _ssm_bwd(
            sv["u_tm"], _to_tm(dys.reshape(Bl, S, W), T), s_["bmat"], s_["bmat_t"], s_["cmat_t"], s_["lam"],
            s_["pw"], s_["lam_t"], s_["d"], name="ssm_bwd")
        du = _from_tm(du_tm).reshape(N, W)
        g_b = _diag_blocks(jnp.swapaxes(g_bt, 1, 2).reshape(NS, LANES, 2, SP).transpose(2, 0, 1, 3).reshape(
            2 * NS, LANES, SP), C, P).reshape(2, G, C, P)
        g_bbar = jnp.swapaxes(g_b, 2, 3)
        g_c = _diag_blocks(g_ct.reshape(NS, LANES, 2, SP).transpose(2, 0, 1, 3).reshape(2 * NS, LANES, SP),
                           C, P).reshape(2, G, C, P)
        g_lbar = g_lam.transpose(1, 0, 2).reshape(2, G, P)
        g_lre, g_lim, g_ldt, g_bre, g_bim = s_["vjp"]((g_lbar[0], g_lbar[1], g_bbar[0], g_bbar[1]))
        small["ssm_lambda_re"][l], small["ssm_lambda_im"][l], small["ssm_log_dt"][l] = g_lre, g_lim, g_ldt
        small["ssm_b_re"][l], small["ssm_b_im"][l] = g_bre, g_bim
        small["ssm_c_re"][l], small["ssm_c_im"][l] = g_c[0], -g_c[1]
        small["ssm_d"][l] = g_d.reshape(W)

        proj3 = sv["proj"].reshape(Bl, S, -1)
        leaving = [_blocks(n, gw[n]) for n in REST] + ([_blocks("w_in", win_grad)] if l + 1 < L else [])
        (dq, dk, dv, dfk, dfq), got = _attn_bwd(
            proj3, sv["ya"], dya.reshape(Bl, S, AW), sv["lse"], sv["frow"],
            name="attn_bwd" if l + 1 < L else "attn_bwd_top", H=H, tq=tq, hosted=_Hosted("exchange", leaving))
        for n, blocks in zip(REST, got):
            big[n][l] = _sum_blocks(n, blocks)
        if l + 1 < L:
            big["w_in"][l + 1] = _sum_blocks("w_in", got[-1])
        dF = dfk.reshape(Bl, H, S) + dfq.transpose(0, 1, 3, 2).reshape(Bl, H, S)
        dft, g = _fox_gate_bwd(dF, sv["ft"], b_forget[l], name="forget_gate_bwd")
        small["b_forget"][l] = g[:, 0]
        dfl = jnp.pad(dft.transpose(0, 2, 1).reshape(N, H), ((0, 0), (0, LANES - H))).astype(BF16)
        dproj = jnp.concatenate([dq.reshape(N, AW), dk.reshape(N, AW), dv.reshape(N, AW), du, dga, dgb, dfl], axis=1)
        gcat = _mm(sv["h"], dproj, name="in_proj_dw", ta=True, tn=1408)
        ncat = wl["wcat"].shape[1]
        win_grad = jnp.concatenate([gcat[:, :3 * AW], gcat[:, ncat:ncat + H], gcat[:, 3 * AW:ncat]], axis=1)
        dh = _mm(dproj, jnp.concatenate([wl["wcat"], wl["wf"]], axis=1).T, name="in_proj_dx", out_dtype=F32, tk=1408)
        dx, g = _rmsnorm_bwd(dh, sv["x0"], sv["r0"], norm_mix[l], dx1, name="norm_mix_bwd")
        small["norm_mix"][l] = g[0]

    big["w_in"][0] = _sum_blocks("w_in", _all_to_all([_blocks("w_in", win_grad)], name="exchange_last")[0])
    grads = {n: jnp.stack(big[n]) for n in SHARDED}
    small_stacked = {n: jnp.stack(small[n]) for n in small}
    small_stacked["norm_final"] = g_final[0]
    grads.update(_allreduce_small(small_stacked))

    deltas, new_m, new_v = {}, {}, {}
    for n in WEIGHTS:
        deltas[n], new_m[n], new_v[n] = _adamw(w[n], grads[n], args["m_" + n], args["v_" + n], name="adamw_" + n)
    return (loss, dx.reshape(Bl, S, D), *[grads[n] for n in WEIGHTS], *[deltas[n] for n in WEIGHTS],
            *[new_m[n] for n in WEIGHTS], *[new_v[n] for n in WEIGHTS])
```

```python
import functools

import jax
import jax.numpy as jnp
from jax import lax
from jax.experimental import pallas as pl
from jax.experimental.pallas import tpu as pltpu

F32 = jnp.float32
BF16 = jnp.bfloat16

N_DEV = 8
HEAD_DIM = 64
LANES = 128
SSM_CHUNK = 32
SLAB_GROUPS = 8
ATTN_BLOCK = 512
LONG_K = 2048
PACK_COLS = 1024
PACK_ROWS = 256
RMS_EPS = 1e-6
VMEM_LIMIT = 56 * 1024 * 1024
ADAM_LR, ADAM_B1, ADAM_B2, ADAM_EPS, ADAM_WD, ADAM_STEP = 0.001, 0.9, 0.999, 1e-08, 0.01, 10
MESH_AXES = ("x", "y", "c")
NEG = -1e30
NT = (((1,), (1,)), ((), ()))
TN = (((0,), (0,)), ((), ()))


def _cparams(*sem):
    return pltpu.CompilerParams(dimension_semantics=sem, vmem_limit_bytes=VMEM_LIMIT)


def _tile(dim, pref, unit=LANES):
    if dim <= pref:
        return dim
    best = None
    for t in range(unit, pref + 1, unit):
        if dim % t == 0:
            best = t
    assert best is not None, (dim, pref)
    return best


def _mm(a, b, *, name, ta=False, tb=False, a_fn=None, epi=None, extras=(), out_dtype=BF16, tm=1024, tn=1024,
        tk=1024):
    if ta:
        K, M = a.shape
    else:
        M, K = a.shape
    N, Kb = b.shape if tb else b.shape[::-1]
    assert K == Kb and not (ta and tb), (a.shape, b.shape)
    tm, tn, tk = _tile(M, tm), _tile(N, tn), _tile(K, tk)
    nk = K // tk
    ne = len(extras)

    def body(a_ref, b_ref, *rest):
        e_refs, o_ref = rest[:ne], rest[ne]
        acc_ref = rest[ne + 1] if nk > 1 else None
        k = pl.program_id(2)
        av = a_ref[...]
        if a_fn is not None:
            av = a_fn(av)
        av = av.astype(BF16)
        bv = b_ref[...].astype(BF16)
        dims = TN if ta else NT if tb else (((1,), (0,)), ((), ()))
        part = lax.dot_general(av, bv, dims, preferred_element_type=F32)
        if nk == 1:
            if epi is not None:
                part = epi(part, *[e[...] for e in e_refs])
            o_ref[...] = part.astype(o_ref.dtype)
            return

        @pl.when(k == 0)
        def _():
            acc_ref[...] = part

        @pl.when(k > 0)
        def _():
            acc_ref[...] += part

        @pl.when(k == nk - 1)
        def _():
            r = acc_ref[...]
            if epi is not None:
                r = epi(r, *[e[...] for e in e_refs])
            o_ref[...] = r.astype(o_ref.dtype)

    a_spec = pl.BlockSpec((tk, tm), lambda i, j, k: (k, i)) if ta else pl.BlockSpec((tm, tk), lambda i, j, k: (i, k))
    return pl.pallas_call(
        body, name=name,
        out_shape=jax.ShapeDtypeStruct((M, N), out_dtype),
        grid=(M // tm, N // tn, nk),
        in_specs=[a_spec, pl.BlockSpec((tn, tk), lambda i, j, k: (j, k)) if tb
                  else pl.BlockSpec((tk, tn), lambda i, j, k: (k, j))]
        + [pl.BlockSpec((tm, tn), lambda i, j, k: (i, j)) for _ in extras],
        out_specs=pl.BlockSpec((tm, tn), lambda i, j, k: (i, j)),
        scratch_shapes=[pltpu.VMEM((tm, tn), F32)] if nk > 1 else [],
        compiler_params=_cparams("parallel", "parallel", "arbitrary"),
    )(a, b, *extras)


def _relu_sq(v):
    r = jnp.maximum(v.astype(F32), 0.0)
    return r * r


def _sigmoid(v):
    return 1.0 / (1.0 + jnp.exp(-v))


GELU_C = 0.7978845608028654
GELU_A = 0.044715


def _gelu(v):
    return 0.5 * v * (1.0 + jnp.tanh(GELU_C * (v + GELU_A * v * v * v)))


def _gelu_grad(v):
    t = jnp.tanh(GELU_C * (v + GELU_A * v * v * v))
    return 0.5 * (1.0 + t) + 0.5 * v * (1.0 - t * t) * GELU_C * (1.0 + 3.0 * GELU_A * v * v)


def _rmsnorm_fwd(x, g, *, name, tr=512):
    n, d = x.shape
    tr = _tile(n, tr, 8)

    def body(x_ref, g_ref, h_ref, r_ref):
        xv = x_ref[...]
        r = lax.rsqrt(jnp.mean(xv * xv, axis=-1, keepdims=True) + RMS_EPS)
        h_ref[...] = (xv * r * g_ref[...]).astype(BF16)
        r_ref[...] = r

    return pl.pallas_call(
        body, name=name,
        out_shape=(jax.ShapeDtypeStruct((n, d), BF16), jax.ShapeDtypeStruct((n, 1), F32)),
        grid=(n // tr,),
        in_specs=[pl.BlockSpec((tr, d), lambda i: (i, 0)), pl.BlockSpec((1, d), lambda i: (0, 0))],
        out_specs=(pl.BlockSpec((tr, d), lambda i: (i, 0)), pl.BlockSpec((tr, 1), lambda i: (i, 0))),
        compiler_params=_cparams("parallel"),
    )(x, g.reshape(1, d))


def _rmsnorm_bwd(dh, x, r, g, dres, *, name, tr=512):
    n, d = x.shape
    tr = _tile(n, tr, 8)

    def body(dh_ref, x_ref, r_ref, g_ref, dres_ref, dx_ref, dg_ref):
        i = pl.program_id(0)
        rv = r_ref[...]
        xh = x_ref[...] * rv
        dhv = dh_ref[...].astype(F32)
        dxh = dhv * g_ref[...]
        m = jnp.mean(dxh * xh, axis=-1, keepdims=True)
        dx_ref[...] = rv * (dxh - xh * m) + dres_ref[...]
        part = jnp.sum(dhv * xh, axis=0, keepdims=True)

        @pl.when(i == 0)
        def _():
            dg_ref[...] = part

        @pl.when(i > 0)
        def _():
            dg_ref[...] += part

    row = pl.BlockSpec((tr, d), lambda i: (i, 0))
    vec = pl.BlockSpec((1, d), lambda i: (0, 0))
    return pl.pallas_call(
        body, name=name,
        out_shape=(jax.ShapeDtypeStruct((n, d), F32), jax.ShapeDtypeStruct((1, d), F32)),
        grid=(n // tr,),
        in_specs=[row, row, pl.BlockSpec((tr, 1), lambda i: (i, 0)), vec, row],
        out_specs=(row, vec),
        compiler_params=_cparams("arbitrary"),
    )(dh, x, r, g.reshape(1, d), dres)


def _loss_head(x, g, target, *, name, tr=512):
    n, d = x.shape
    tr = _tile(n, tr, 8)

    def body(x_ref, g_ref, t_ref, dx_ref, dg_ref, loss_ref):
        i = pl.program_id(0)
        xv = x_ref[...]
        gv = g_ref[...]
        r = lax.rsqrt(jnp.mean(xv * xv, axis=-1, keepdims=True) + RMS_EPS)
        xh = xv * r
        err = xh * gv - t_ref[...]
        lpart = 0.5 * jnp.sum(jnp.mean(err * err, axis=-1, keepdims=True), axis=0, keepdims=True)
        dy = err * (1.0 / d)
        dxh = dy * gv
        m = jnp.mean(dxh * xh, axis=-1, keepdims=True)
        dx_ref[...] = r * (dxh - xh * m)
        gpart = jnp.sum(dy * xh, axis=0, keepdims=True)
        lrow = jnp.broadcast_to(lpart, (1, LANES))

        @pl.when(i == 0)
        def _():
            dg_ref[...] = gpart
            loss_ref[...] = lrow

        @pl.when(i > 0)
        def _():
            dg_ref[...] += gpart
            loss_ref[...] += lrow

    row = pl.BlockSpec((tr, d), lambda i: (i, 0))
    vec = pl.BlockSpec((1, d), lambda i: (0, 0))
    return pl.pallas_call(
        body, name=name,
        out_shape=(jax.ShapeDtypeStruct((n, d), F32), jax.ShapeDtypeStruct((1, d), F32),
                   jax.ShapeDtypeStruct((1, LANES), F32)),
        grid=(n // tr,),
        in_specs=[row, vec, row],
        out_specs=(row, vec, pl.BlockSpec((1, LANES), lambda i: (0, 0))),
        compiler_params=_cparams("arbitrary"),
    )(x, g.reshape(1, d), target)


def _tri_dot(v, tri):
    hi = v.astype(BF16)
    r1 = v - hi.astype(F32)
    mid = r1.astype(BF16)
    lo = (r1 - mid.astype(F32)).astype(BF16)
    d = functools.partial(jnp.dot, preferred_element_type=F32)
    return d(hi, tri) + d(mid, tri) + d(lo, tri)


def _fox_gate_fwd(ft, bf, *, name, blk=256):
    B, H, S = ft.shape
    blk = _tile(S, blk)
    nb = S // blk

    def body(f_ref, b_ref, o_ref):
        x = f_ref[0] + b_ref[...]
        logf = jnp.minimum(x, 0.0) - jnp.log(1.0 + jnp.exp(-jnp.abs(x)))
        rr = lax.broadcasted_iota(jnp.int32, (blk, blk), 0)
        cc = lax.broadcasted_iota(jnp.int32, (blk, blk), 1)
        tri = (rr <= cc).astype(BF16)
        carry = jnp.zeros((H, 1), F32)
        for n in range(nb):
            c = _tri_dot(logf[:, n * blk:(n + 1) * blk], tri) + carry
            o_ref[0, :, n * blk:(n + 1) * blk] = c
            carry = c[:, blk - 1:blk]

    return pl.pallas_call(
        body, name=name,
        out_shape=jax.ShapeDtypeStruct((B, H, S), F32),
        grid=(B,),
        in_specs=[pl.BlockSpec((1, H, S), lambda b: (b, 0, 0)), pl.BlockSpec((H, 1), lambda b: (0, 0))],
        out_specs=pl.BlockSpec((1, H, S), lambda b: (b, 0, 0)),
        compiler_params=_cparams("parallel"),
    )(ft, bf.reshape(H, 1))


def _fox_gate_bwd(dF, ft, bf, *, name, blk=256):
    B, H, S = ft.shape
    blk = _tile(S, blk)
    nb = S // blk

    def body(d_ref, f_ref, b_ref, o_ref, db_ref):
        b = pl.program_id(0)
        x = f_ref[0] + b_ref[...]
        sneg = 1.0 / (1.0 + jnp.exp(x))
        dv = d_ref[0]
        rr = lax.broadcasted_iota(jnp.int32, (blk, blk), 0)
        cc = lax.broadcasted_iota(jnp.int32, (blk, blk), 1)
        tri = (rr >= cc).astype(BF16)
        carry = jnp.zeros((H, 1), F32)
        tot = jnp.zeros((H, 1), F32)
        for n in reversed(range(nb)):
            sl = slice(n * blk, (n + 1) * blk)
            c = _tri_dot(dv[:, sl], tri) + carry
            g = c * sneg[:, sl]
            o_ref[0, :, sl] = g
            tot = tot + jnp.sum(g, axis=1, keepdims=True)
            carry = c[:, 0:1]

        @pl.when(b == 0)
        def _():
            db_ref[...] = tot

        @pl.when(b > 0)
        def _():
            db_ref[...] += tot

    blkspec = pl.BlockSpec((1, H, S), lambda b: (b, 0, 0))
    return pl.pallas_call(
        body, name=name,
        out_shape=(jax.ShapeDtypeStruct((B, H, S), F32), jax.ShapeDtypeStruct((H, 1), F32)),
        grid=(B,),
        in_specs=[blkspec, blkspec, pl.BlockSpec((H, 1), lambda b: (0, 0))],
        out_specs=(blkspec, pl.BlockSpec((H, 1), lambda b: (0, 0))),
        compiler_params=_cparams("arbitrary"),
    )(dF, ft, bf.reshape(H, 1))


def _head_masks():
    lane = lax.broadcasted_iota(jnp.int32, (1, LANES), 1)
    return [lane < HEAD_DIM, lane >= HEAD_DIM]


HBM = pl.BlockSpec(memory_space=pltpu.HBM)
MESH = pl.DeviceIdType.MESH


def _direct_copies(kind, x_refs, out_refs, send_sems, recv_sems, local_sems):
    x, y, c = lax.axis_index("x"), lax.axis_index("y"), lax.axis_index("c")
    me = 4 * x + 2 * y + c
    gather = kind == "gather"
    copies = []
    for a, (xr, outr) in enumerate(zip(x_refs, out_refs)):
        copies.append(pltpu.make_async_copy(xr if gather else xr.at[me], outr.at[me], local_sems.at[a]))
    for k in range(1, N_DEV):
        px = 1 - x if (k >> 2) & 1 else x
        py = 1 - y if (k >> 1) & 1 else y
        pc = 1 - c if k & 1 else c
        for a, (xr, outr) in enumerate(zip(x_refs, out_refs)):
            copies.append(pltpu.make_async_remote_copy(
                src_ref=xr if gather else xr.at[4 * px + 2 * py + pc], dst_ref=outr.at[me],
                send_sem=send_sems.at[7 * a + k - 1], recv_sem=recv_sems.at[7 * a + k - 1],
                device_id=(px, py, pc), device_id_type=MESH))
    return copies


class _Hosted:
    def __init__(self, kind, arrays):
        self.kind, self.arrays, self.n = kind, list(arrays), len(arrays)
        lead = (N_DEV,) if kind == "gather" else ()
        self.out_shape = [jax.ShapeDtypeStruct(lead + a.shape, a.dtype) for a in self.arrays]
        self.scratch = [pltpu.SemaphoreType.DMA((7 * self.n,)), pltpu.SemaphoreType.DMA((7 * self.n,)),
                        pltpu.SemaphoreType.DMA((self.n,))]

    def run(self, first, last, x_refs, out_refs, sems):
        def go(when, act):
            @pl.when(when)
            def _():
                for cp in _direct_copies(self.kind, x_refs, out_refs, *sems):
                    act(cp)
        return (lambda: go(first, lambda cp: cp.start())), (lambda: go(last, lambda cp: cp.wait()))


def _stack_heads(x, masks):
    zero = jnp.zeros_like(x)
    return jnp.concatenate([jnp.where(masks[0], x, zero), jnp.where(masks[1], x, zero)], axis=0)


def _attn_fwd(proj, frow, *, name, H, tq, hosted=None):
    B, S, _ = proj.shape
    HP = H // 2
    nq = S // tq
    scale = HEAD_DIM ** -0.5
    nh = hosted.n if hosted else 0

    def body(*refs):
        q_ref, k_ref, v_ref, fk_ref = refs[:4]
        o_ref, lse_ref = refs[4 + nh:6 + nh]
        i = pl.program_id(2)
        if hosted:
            b, hp = pl.program_id(0), pl.program_id(1)
            start, finish = hosted.run((b == 0) & (hp == 0) & (i == 0), (b == B - 1) & (hp == HP - 1) & (i == nq - 1),
                                       refs[4:4 + nh], refs[6 + nh:6 + 2 * nh], refs[6 + 2 * nh:])
            start()
        masks = _head_masks()
        q2 = _stack_heads(q_ref[0], masks) * jnp.asarray(scale, BF16)
        rr = lax.broadcasted_iota(jnp.int32, (tq, tq), 0)
        cc = lax.broadcasted_iota(jnp.int32, (tq, tq), 1)
        causal = rr >= cc

        def block(j, carry, masked):
            rows = pl.ds(pl.multiple_of(j * tq, tq), tq)
            kj = k_ref[0, rows, :]
            vj = v_ref[0, rows, :]
            s2 = lax.dot_general(q2, kj, NT, preferred_element_type=F32)
            new, ps = [], []
            for h in range(2):
                m, l, acc = carry[h]
                s = s2[h * tq:(h + 1) * tq] - fk_ref[0, 0, h, pl.ds(j, 1), :]
                if masked:
                    s = jnp.where(causal, s, NEG)
                m_new = jnp.maximum(m, jnp.max(s, axis=-1, keepdims=True))
                alpha = jnp.exp(m - m_new)
                p = jnp.exp(s - m_new)
                new.append((m_new, alpha * l + jnp.sum(p, axis=-1, keepdims=True), alpha, acc))
                ps.append(p.astype(BF16))
            pv = jnp.dot(jnp.concatenate(ps, axis=0), vj, preferred_element_type=F32)
            return tuple((m, l, alpha * acc + pv[h * tq:(h + 1) * tq]) for h, (m, l, alpha, acc) in enumerate(new))

        one = (jnp.full((tq, 1), NEG, F32), jnp.zeros((tq, 1), F32), jnp.zeros((tq, LANES), F32))
        carry = lax.fori_loop(0, i, lambda j, c: block(j, c, False), (one, one))
        (m0, l0, a0), (m1, l1, a1) = block(i, carry, True)
        o_ref[0] = jnp.where(masks[0], a0 / l0, a1 / l1).astype(BF16)
        two = lax.broadcasted_iota(jnp.int32, (1, 2), 1)
        lse_ref[0, 0] = jnp.where(two == 0, m0 + jnp.log(l0), m1 + jnp.log(l1))
        if hosted:
            finish()

    kv = lambda off: pl.BlockSpec((1, S, LANES), lambda b, hp, i: (b, 0, off + hp))
    outs = pl.pallas_call(
        body, name=name,
        out_shape=[jax.ShapeDtypeStruct((B, S, H * HEAD_DIM), BF16), jax.ShapeDtypeStruct((B, HP, S, 2), F32)]
        + (hosted.out_shape if hosted else []),
        grid=(B, HP, nq),
        in_specs=[pl.BlockSpec((1, tq, LANES), lambda b, hp, i: (b, i, hp)), kv(HP), kv(2 * HP),
                  pl.BlockSpec((1, 1, 2, nq, tq), lambda b, hp, i: (b, hp, 0, 0, 0))] + [HBM] * nh,
        out_specs=[pl.BlockSpec((1, tq, LANES), lambda b, hp, i: (b, i, hp)),
                   pl.BlockSpec((1, 1, tq, 2), lambda b, hp, i: (b, hp, i, 0))] + [HBM] * nh,
        scratch_shapes=hosted.scratch if hosted else [],
        compiler_params=_cparams("arbitrary", "arbitrary", "arbitrary"),
    )(proj, proj, proj, frow, *(hosted.arrays if hosted else []))
    return outs[0], outs[1], outs[2:]


def _attn_bwd(proj, ya, dya, lse, frow, *, name, H, tq, hosted=None):
    B, S, _ = proj.shape
    HP = H // 2
    nq = S // tq
    AW = H * HEAD_DIM
    scale = HEAD_DIM ** -0.5
    nh = hosted.n if hosted else 0

    def body(*refs):
        q_ref, k_ref, v_ref, o_ref, do_ref, lse_ref, fk_ref = refs[:7]
        dq_ref, dk_ref, dv_ref, dfk_ref, dfq_ref = refs[7 + nh:12 + nh]
        (q2_ref, do2_ref, lse2_ref, delta2_ref, dq2_acc, dfq2_acc, dk_acc, dv_acc,
         dfk_acc) = refs[12 + 2 * nh:21 + 2 * nh]
        if hosted:
            b, hp = pl.program_id(0), pl.program_id(1)
            start, finish = hosted.run((b == 0) & (hp == 0), (b == B - 1) & (hp == HP - 1),
                                       refs[7:7 + nh], refs[12 + nh:12 + 2 * nh], refs[21 + 2 * nh:])
            start()
        masks = _head_masks()
        rr = lax.broadcasted_iota(jnp.int32, (tq, tq), 0)
        cc = lax.broadcasted_iota(jnp.int32, (tq, tq), 1)
        causal = rr >= cc
        sc = jnp.asarray(scale, BF16)

        def stage(i, c):
            rows = pl.ds(pl.multiple_of(i * tq, tq), tq)
            dov = do_ref[0, rows, :]
            q2_ref[i] = _stack_heads(q_ref[0, rows, :], masks) * sc
            do2_ref[i] = _stack_heads(dov, masks)
            prod = dov.astype(F32) * o_ref[0, rows, :].astype(F32)
            delta2_ref[i] = jnp.concatenate(
                [jnp.sum(jnp.where(masks[h], prod, 0.0), axis=-1, keepdims=True) for h in range(2)], axis=0)
            lv = lse_ref[0, 0, rows, :]
            lse2_ref[i] = jnp.concatenate([lv[:, 0:1], lv[:, 1:2]], axis=0)
            return c

        lax.fori_loop(0, nq, stage, 0)
        dq2_acc[...] = jnp.zeros_like(dq2_acc)
        dfq2_acc[...] = jnp.zeros_like(dfq2_acc)

        def kv_block(j, carry):
            rows_j = pl.ds(pl.multiple_of(j * tq, tq), tq)
            kj = k_ref[0, rows_j, :]
            vj = v_ref[0, rows_j, :]
            ks = kj * sc
            dk_acc[...] = jnp.zeros_like(dk_acc)
            dv_acc[...] = jnp.zeros_like(dv_acc)
            dfk_acc[...] = jnp.zeros_like(dfk_acc)

            def logits(i):
                return (lax.dot_general(q2_ref[i], kj, NT, preferred_element_type=F32),
                        lax.dot_general(do2_ref[i], vj, NT, preferred_element_type=F32))

            def probs(i, s2, dp2, masked):
                lse2 = lse2_ref[i]
                delta2 = delta2_ref[i]
                ps, dss = [], []
                for h in range(2):
                    half = slice(h * tq, (h + 1) * tq)
                    p = jnp.exp(s2[half] - fk_ref[0, 0, h, pl.ds(j, 1), :] - lse2[half])
                    if masked:
                        p = jnp.where(causal, p, 0.0)
                    ds = p * (dp2[half] - delta2[half])
                    dfk_acc[h:h + 1, :] -= jnp.sum(ds, axis=0, keepdims=True)
                    dfq2_acc[i, half, :] += jnp.sum(ds, axis=1, keepdims=True)
                    ps.append(p.astype(BF16))
                    dss.append(ds.astype(BF16))
                return jnp.concatenate(ps, axis=0), jnp.concatenate(dss, axis=0)

            def grads(i, p2, ds2):
                dv_acc[...] += lax.dot_general(p2, do2_ref[i], TN, preferred_element_type=F32)
                dk_acc[...] += lax.dot_general(ds2, q2_ref[i], TN, preferred_element_type=F32)
                dq2_acc[i] += jnp.dot(ds2, ks, preferred_element_type=F32)

            grads(j, *probs(j, *logits(j), True))

            def rest(i, c):
                grads(i, *probs(i, *logits(i), False))
                return c

            lax.fori_loop(j + 1, nq, rest, 0)
            dk_ref[0, rows_j, :] = dk_acc[...].astype(BF16)
            dv_ref[0, rows_j, :] = dv_acc[...].astype(BF16)
            for h in range(2):
                dfk_ref[0, 0, h, pl.ds(j, 1), :] = dfk_acc[h:h + 1, :]
            return carry

        lax.fori_loop(0, nq, kv_block, 0)
        two = lax.broadcasted_iota(jnp.int32, (1, 2), 1)

        def finish_block(i, c):
            rows = pl.ds(pl.multiple_of(i * tq, tq), tq)
            dq2 = dq2_acc[i]
            dq_ref[0, rows, :] = jnp.where(masks[0], dq2[:tq], dq2[tq:]).astype(BF16)
            dfq2 = dfq2_acc[i]
            dfq_ref[0, 0, rows, :] = jnp.where(two == 0, dfq2[:tq], dfq2[tq:])
            return c

        lax.fori_loop(0, nq, finish_block, 0)
        if hosted:
            finish()

    col = lambda off: pl.BlockSpec((1, S, LANES), lambda b, hp: (b, 0, off + hp))
    stat = pl.BlockSpec((1, 1, S, 2), lambda b, hp: (b, hp, 0, 0))
    rowf = pl.BlockSpec((1, 1, 2, nq, tq), lambda b, hp: (b, hp, 0, 0, 0))
    grad = jax.ShapeDtypeStruct((B, S, AW), BF16)
    outs = pl.pallas_call(
        body, name=name,
        out_shape=[grad, grad, grad, jax.ShapeDtypeStruct((B, HP, 2, nq, tq), F32),
                   jax.ShapeDtypeStruct((B, HP, S, 2), F32)] + (hosted.out_shape if hosted else []),
        grid=(B, HP),
        in_specs=[col(0), col(HP), col(2 * HP), col(0), col(0), stat, rowf] + [HBM] * nh,
        out_specs=[col(0), col(0), col(0), rowf, stat] + [HBM] * nh,
        scratch_shapes=[pltpu.VMEM((nq, 2 * tq, LANES), BF16), pltpu.VMEM((nq, 2 * tq, LANES), BF16),
                        pltpu.VMEM((nq, 2 * tq, 1), F32), pltpu.VMEM((nq, 2 * tq, 1), F32),
                        pltpu.VMEM((nq, 2 * tq, LANES), F32), pltpu.VMEM((nq, 2 * tq, 1), F32),
                        pltpu.VMEM((tq, LANES), F32), pltpu.VMEM((tq, LANES), F32), pltpu.VMEM((2, tq), F32)]
        + (hosted.scratch if hosted else []),
        compiler_params=_cparams("arbitrary", "arbitrary"),
    )(proj, proj, proj, ya, dya, lse, frow, *(hosted.arrays if hosted else []))
    return outs[:5], outs[5:]


def _attn_fwd_old(proj, fcol, frow, *, name, H, tq):
    B, S, _ = proj.shape
    HP = H // 2
    nq = S // tq
    scale = HEAD_DIM ** -0.5

    def body(q_ref, k_ref, v_ref, fq_ref, fk_ref, o_ref, lse_ref):
        i = pl.program_id(2)
        q = q_ref[0]
        masks = _head_masks()
        rr = lax.broadcasted_iota(jnp.int32, (tq, tq), 0)
        cc = lax.broadcasted_iota(jnp.int32, (tq, tq), 1)
        causal = rr >= cc
        outs, lses = [], []
        for h in range(2):
            qm = jnp.where(masks[h], q, jnp.zeros_like(q)) * jnp.asarray(scale, BF16)
            fq = fq_ref[0, 0][:, h:h + 1]

            def block(j, carry, masked, h=h, qm=qm, fq=fq):
                m, l, acc = carry
                rows = pl.ds(pl.multiple_of(j * tq, tq), tq)
                kj = k_ref[0, rows, :]
                vj = v_ref[0, rows, :]
                s = lax.dot_general(qm, kj, NT, preferred_element_type=F32)
                s = s + (fq - fk_ref[0, 0, h, pl.ds(j, 1), :])
                if masked:
                    s = jnp.where(causal, s, NEG)
                m_new = jnp.maximum(m, jnp.max(s, axis=-1, keepdims=True))
                alpha = jnp.exp(m - m_new)
                p = jnp.exp(s - m_new)
                l = alpha * l + jnp.sum(p, axis=-1, keepdims=True)
                acc = alpha * acc + jnp.dot(p.astype(BF16), vj, preferred_element_type=F32)
                return m_new, l, acc

            init = (jnp.full((tq, 1), NEG, F32), jnp.zeros((tq, 1), F32), jnp.zeros((tq, LANES), F32))
            carry = lax.fori_loop(0, i, lambda j, c, block=block: block(j, c, False), init)
            m, l, acc = block(i, carry, True)
            outs.append(acc / l)
            lses.append(m + jnp.log(l))
        o_ref[0] = jnp.where(masks[0], outs[0], outs[1]).astype(BF16)
        two = lax.broadcasted_iota(jnp.int32, (1, 2), 1)
        lse_ref[0, 0] = jnp.where(two == 0, lses[0], lses[1])

    kv = lambda off: pl.BlockSpec((1, S, LANES), lambda b, hp, i: (b, 0, off + hp))
    return pl.pallas_call(
        body, name=name,
        out_shape=(jax.ShapeDtypeStruct((B, S, H * HEAD_DIM), BF16), jax.ShapeDtypeStruct((B, HP, S, 2), F32)),
        grid=(B, HP, nq),
        in_specs=[pl.BlockSpec((1, tq, LANES), lambda b, hp, i: (b, i, hp)), kv(HP), kv(2 * HP),
                  pl.BlockSpec((1, 1, tq, 2), lambda b, hp, i: (b, hp, i, 0)),
                  pl.BlockSpec((1, 1, 2, nq, tq), lambda b, hp, i: (b, hp, 0, 0, 0))],
        out_specs=(pl.BlockSpec((1, tq, LANES), lambda b, hp, i: (b, i, hp)),
                   pl.BlockSpec((1, 1, tq, 2), lambda b, hp, i: (b, hp, i, 0))),
        compiler_params=_cparams("parallel", "parallel", "arbitrary"),
    )(proj, proj, proj, fcol, frow)


def _attn_bwd_old(proj, ya, dya, lse, fcol, frow, *, name, H, tq):
    B, S, _ = proj.shape
    HP = H // 2
    nq = S // tq
    AW = H * HEAD_DIM
    scale = HEAD_DIM ** -0.5

    def body(q_ref, k_ref, v_ref, o_ref, do_ref, lse_ref, fq_ref, fk_ref,
             dq_ref, dk_ref, dv_ref, dfk_ref, dfq_ref, dq_acc, dk_acc, dv_acc, delta_ref, dfk_acc, dfq_acc):
        masks = _head_masks()
        rr = lax.broadcasted_iota(jnp.int32, (tq, tq), 0)
        cc = lax.broadcasted_iota(jnp.int32, (tq, tq), 1)
        causal = rr >= cc
        sc = jnp.asarray(scale, BF16)
        prod = do_ref[0].astype(F32) * o_ref[0].astype(F32)
        for h in range(2):
            delta_ref[h] = jnp.sum(jnp.where(masks[h], prod, 0.0), axis=-1, keepdims=True)
        dq_acc[...] = jnp.zeros_like(dq_acc)
        dfq_acc[...] = jnp.zeros_like(dfq_acc)

        def kv_block(j, carry):
            rows_j = pl.ds(pl.multiple_of(j * tq, tq), tq)
            kj = k_ref[0, rows_j, :]
            vj = v_ref[0, rows_j, :]
            dk_acc[...] = jnp.zeros_like(dk_acc)
            dv_acc[...] = jnp.zeros_like(dv_acc)
            dfk_acc[...] = jnp.zeros_like(dfk_acc)
            kms = [jnp.where(masks[h], kj, jnp.zeros_like(kj)) * sc for h in range(2)]

            def q_block(i, masked):
                rows_i = pl.ds(pl.multiple_of(i * tq, tq), tq)
                qi = q_ref[0, rows_i, :]
                doi = do_ref[0, rows_i, :]
                fqi = fq_ref[0, 0, rows_i, :]
                lsei = lse_ref[0, 0, rows_i, :]
                for h in range(2):
                    qm = jnp.where(masks[h], qi, jnp.zeros_like(qi)) * sc
                    dom = jnp.where(masks[h], doi, jnp.zeros_like(doi))
                    s = lax.dot_general(qm, kj, NT, preferred_element_type=F32)
                    s = s + (fqi[:, h:h + 1] - fk_ref[0, 0, h, pl.ds(j, 1), :])
                    p = jnp.exp(s - lsei[:, h:h + 1])
                    if masked:
                        p = jnp.where(causal, p, 0.0)
                    dp = lax.dot_general(dom, vj, NT, preferred_element_type=F32)
                    ds = p * (dp - delta_ref[h, rows_i, :])
                    pb, dsb = p.astype(BF16), ds.astype(BF16)
                    dv_acc[...] += lax.dot_general(pb, dom, TN, preferred_element_type=F32)
                    dk_acc[...] += lax.dot_general(dsb, qm, TN, preferred_element_type=F32)
                    dq_acc[rows_i, :] += jnp.dot(dsb, kms[h], preferred_element_type=F32)
                    dfk_acc[h:h + 1, :] -= jnp.sum(ds, axis=0, keepdims=True)
                    dfq_acc[h, rows_i, :] += jnp.sum(ds, axis=1, keepdims=True)

            q_block(j, True)

            def rest(i, c):
                q_block(i, False)
                return c

            lax.fori_loop(j + 1, nq, rest, 0)
            dk_ref[0, rows_j, :] = dk_acc[...].astype(BF16)
            dv_ref[0, rows_j, :] = dv_acc[...].astype(BF16)
            for h in range(2):
                dfk_ref[0, 0, h, pl.ds(j, 1), :] = dfk_acc[h:h + 1, :]
            return carry

        lax.fori_loop(0, nq, kv_block, 0)
        dq_ref[0] = dq_acc[...].astype(BF16)
        two = lax.broadcasted_iota(jnp.int32, (1, 2), 1)
        dfq_ref[0, 0] = jnp.where(two == 0, dfq_acc[0], dfq_acc[1])

    col = lambda off: pl.BlockSpec((1, S, LANES), lambda b, hp: (b, 0, off + hp))
    stat = pl.BlockSpec((1, 1, S, 2), lambda b, hp: (b, hp, 0, 0))
    rowf = pl.BlockSpec((1, 1, 2, nq, tq), lambda b, hp: (b, hp, 0, 0, 0))
    grad = jax.ShapeDtypeStruct((B, S, AW), BF16)
    return pl.pallas_call(
        body, name=name,
        out_shape=(grad, grad, grad, jax.ShapeDtypeStruct((B, HP, 2, nq, tq), F32),
                   jax.ShapeDtypeStruct((B, HP, S, 2), F32)),
        grid=(B, HP),
        in_specs=[col(0), col(HP), col(2 * HP), col(0), col(0), stat, stat, rowf],
        out_specs=(col(0), col(0), col(0), rowf, stat),
        scratch_shapes=[pltpu.VMEM((S, LANES), F32), pltpu.VMEM((tq, LANES), F32), pltpu.VMEM((tq, LANES), F32),
                        pltpu.VMEM((2, S, 1), F32), pltpu.VMEM((2, tq), F32), pltpu.VMEM((2, S, 1), F32)],
        compiler_params=_cparams("parallel", "parallel"),
    )(proj, proj, proj, ya, dya, lse, fcol, frow)


def _cmul(ar, ai, br, bi):
    return ar * br - ai * bi, ar * bi + ai * br


def _ssm_states(u_ref, bm, lam_ref, pw_ref, lamT_ref, hr_ref, hi_ref, inr_ref, ini_ref, T, NC, SP):
    lr, li = lam_ref[0, 0:1, :], lam_ref[0, 1:2, :]
    bu = jnp.dot(u_ref[0, 0], bm, preferred_element_type=F32)
    hr_ref[0] = bu[:, :SP]
    hi_ref[0] = bu[:, SP:]

    def step(t, c):
        bu = jnp.dot(u_ref[0, t], bm, preferred_element_type=F32)
        pr, pi = _cmul(hr_ref[t - 1], hi_ref[t - 1], lr, li)
        hr_ref[t] = pr + bu[:, :SP]
        hi_ref[t] = pi + bu[:, SP:]
        return c

    lax.fori_loop(1, T, step, 0, unroll=2)

    tr, ti = lamT_ref[0, 0:1, :], lamT_ref[0, 1:2, :]
    inr_ref[0:1, :] = jnp.zeros((1, SP), F32)
    ini_ref[0:1, :] = jnp.zeros((1, SP), F32)

    def chunk(n, c):
        prev = pl.ds(n - 1, 1)
        pr, pi = _cmul(inr_ref[prev, :], ini_ref[prev, :], tr, ti)
        inr_ref[pl.ds(n, 1), :] = pr + hr_ref[T - 1, prev, :]
        ini_ref[pl.ds(n, 1), :] = pi + hi_ref[T - 1, prev, :]
        return c

    lax.fori_loop(1, NC, chunk, 0)


def _ssm_entry_term(t, pw_ref, inr_ref, ini_ref):
    return _cmul(inr_ref[...], ini_ref[...], pw_ref[0, 0, pl.ds(t, 1), :], pw_ref[0, 1, pl.ds(t, 1), :])


def _ssm_fwd(u_tm, bmat, cmat, lam, pw, lamT, dskip, *, name):
    B, T, NC, W = u_tm.shape
    NS = W // LANES
    SP = bmat.shape[2] // 2

    def body(u_ref, b_ref, c_ref, lam_ref, pw_ref, lamT_ref, d_ref, y_ref, hr_ref, hi_ref, inr_ref, ini_ref):
        _ssm_states(u_ref, b_ref[0], lam_ref, pw_ref, lamT_ref, hr_ref, hi_ref, inr_ref, ini_ref, T, NC, SP)
        cm = c_ref[0]
        dv = d_ref[...]

        def out(t, c):
            cr, ci = _ssm_entry_term(t, pw_ref, inr_ref, ini_ref)
            hcat = jnp.concatenate([hr_ref[t] + cr, hi_ref[t] + ci], axis=1).astype(BF16)
            y_ref[0, t] = jnp.dot(hcat, cm, preferred_element_type=F32) + dv * u_ref[0, t].astype(F32)
            return c

        lax.fori_loop(0, T, out, 0, unroll=2)

    slab = lambda *shape: pl.BlockSpec((1,) + shape, lambda b, s: (s,) + (0,) * len(shape))
    tok = pl.BlockSpec((1, T, NC, LANES), lambda b, s: (b, 0, 0, s))
    return pl.pallas_call(
        body, name=name,
        out_shape=jax.ShapeDtypeStruct((B, T, NC, W), F32),
        grid=(B, NS),
        in_specs=[tok, slab(LANES, 2 * SP), slab(2 * SP, LANES), slab(2, SP), slab(2, T, SP), slab(2, SP),
                  pl.BlockSpec((1, LANES), lambda b, s: (0, s))],
        out_specs=tok,
        scratch_shapes=[pltpu.VMEM((T, NC, SP), F32), pltpu.VMEM((T, NC, SP), F32),
                        pltpu.VMEM((NC, SP), F32), pltpu.VMEM((NC, SP), F32)],
        compiler_params=_cparams("parallel", "parallel"),
    )(u_tm, bmat, cmat, lam, pw, lamT, dskip)


def _ssm_bwd(u_tm, dy_tm, bmat, bmat_t, cmat_t, lam, pw, lamT, dskip, *, name):
    B, T, NC, W = u_tm.shape
    NS = W // LANES
    SP = bmat.shape[2] // 2

    def body(u_ref, dy_ref, b_ref, bt_ref, ct_ref, lam_ref, pw_ref, lamT_ref, d_ref,
             du_ref, gb_ref, gc_ref, glam_ref, gd_ref,
             hr_ref, hi_ref, ar_ref, ai_ref, inr_ref, ini_ref, anr_ref, ani_ref):
        b = pl.program_id(1)
        _ssm_states(u_ref, b_ref[0], lam_ref, pw_ref, lamT_ref, hr_ref, hi_ref, inr_ref, ini_ref, T, NC, SP)
        lr, li = lam_ref[0, 0:1, :], lam_ref[0, 1:2, :]
        ct = ct_ref[0]
        bt = bt_ref[0]
        dv = d_ref[...]

        gh = jnp.dot(dy_ref[0, T - 1].astype(BF16), ct, preferred_element_type=F32)
        ar_ref[T - 1] = gh[:, :SP]
        ai_ref[T - 1] = gh[:, SP:]

        def back(k, c):
            t = T - 2 - k
            gh = jnp.dot(dy_ref[0, t].astype(BF16), ct, preferred_element_type=F32)
            pr, pi = _cmul(ar_ref[t + 1], ai_ref[t + 1], lr, -li)
            ar_ref[t] = pr + gh[:, :SP]
            ai_ref[t] = pi + gh[:, SP:]
            return c

        lax.fori_loop(0, T - 1, back, 0, unroll=2)

        tr, ti = lamT_ref[0, 0:1, :], lamT_ref[0, 1:2, :]
        anr_ref[NC - 1:NC, :] = jnp.zeros((1, SP), F32)
        ani_ref[NC - 1:NC, :] = jnp.zeros((1, SP), F32)

        def chunk(k, c):
            n = NC - 2 - k
            nxt = pl.ds(n + 1, 1)
            pr, pi = _cmul(anr_ref[nxt, :], ani_ref[nxt, :], tr, -ti)
            anr_ref[pl.ds(n, 1), :] = pr + ar_ref[0, nxt, :]
            ani_ref[pl.ds(n, 1), :] = pi + ai_ref[0, nxt, :]
            return c

        lax.fori_loop(0, NC - 1, chunk, 0)

        @pl.when(b == 0)
        def _():
            gb_ref[...] = jnp.zeros_like(gb_ref)
            gc_ref[...] = jnp.zeros_like(gc_ref)
            glam_ref[...] = jnp.zeros_like(glam_ref)
            gd_ref[...] = jnp.zeros_like(gd_ref)

        def final(t, hpr, hpi, gl):
            back_pow = pl.ds(T - 1 - t, 1)
            cr, ci = _cmul(anr_ref[...], ani_ref[...], pw_ref[0, 0, back_pow, :], -pw_ref[0, 1, back_pow, :])
            a_r = ar_ref[t] + cr
            a_i = ai_ref[t] + ci
            gl = (gl[0] + jnp.sum(a_r * hpr + a_i * hpi, axis=0, keepdims=True),
                  gl[1] + jnp.sum(a_i * hpr - a_r * hpi, axis=0, keepdims=True))
            acat = jnp.concatenate([a_r, a_i], axis=1).astype(BF16)
            ut = u_ref[0, t]
            dyt = dy_ref[0, t]
            du_ref[0, t] = (jnp.dot(acat, bt, preferred_element_type=F32) + dv * dyt).astype(BF16)
            gb_ref[0] += lax.dot_general(acat, ut, TN, preferred_element_type=F32)
            er, ei = _ssm_entry_term(t, pw_ref, inr_ref, ini_ref)
            h_r = hr_ref[t] + er
            h_i = hi_ref[t] + ei
            hr_ref[t] = h_r
            hi_ref[t] = h_i
            hcat = jnp.concatenate([h_r, h_i], axis=1).astype(BF16)
            gc_ref[0] += lax.dot_general(dyt.astype(BF16), hcat, TN, preferred_element_type=F32)
            gd_ref[0] += jnp.sum(dyt * ut.astype(F32), axis=0, keepdims=True)
            return gl

        zero = jnp.zeros((1, SP), F32)
        gl = final(0, inr_ref[...], ini_ref[...], (zero, zero))
        gl = lax.fori_loop(1, T, lambda t, gl: final(t, hr_ref[t - 1], hi_ref[t - 1], gl), gl)
        glam_ref[0, 0:1, :] += gl[0]
        glam_ref[0, 1:2, :] += gl[1]

    slab = lambda *shape: pl.BlockSpec((1,) + shape, lambda s, b: (s,) + (0,) * len(shape))
    tok = pl.BlockSpec((1, T, NC, LANES), lambda s, b: (b, 0, 0, s))
    big = pltpu.VMEM((T, NC, SP), F32)
    small = pltpu.VMEM((NC, SP), F32)
    return pl.pallas_call(
        body, name=name,
        out_shape=(jax.ShapeDtypeStruct((B, T, NC, W), BF16),
                   jax.ShapeDtypeStruct((NS, 2 * SP, LANES), F32), jax.ShapeDtypeStruct((NS, LANES, 2 * SP), F32),
                   jax.ShapeDtypeStruct((NS, 2, SP), F32), jax.ShapeDtypeStruct((NS, 1, LANES), F32)),
        grid=(NS, B),
        in_specs=[tok, tok, slab(LANES, 2 * SP), slab(2 * SP, LANES), slab(LANES, 2 * SP), slab(2, SP),
                  slab(2, T, SP), slab(2, SP), pl.BlockSpec((1, LANES), lambda s, b: (0, s))],
        out_specs=(tok, slab(2 * SP, LANES), slab(LANES, 2 * SP), slab(2, SP), slab(1, LANES)),
        scratch_shapes=[big, big, big, big, small, small, small, small],
        compiler_params=_cparams("parallel", "arbitrary"),
    )(u_tm, dy_tm, bmat, bmat_t, cmat_t, lam, pw, lamT, dskip)


def _glu_fwd(ys, w, b, *, name, tr=512):
    n, wd = ys.shape
    tr = _tile(n, tr, 8)

    def body(y_ref, w_ref, b_ref, o_ref):
        yb = _gelu(y_ref[...])
        z = jnp.dot(yb.astype(BF16), w_ref[...], preferred_element_type=F32) + b_ref[...]
        o_ref[...] = (yb * _sigmoid(z)).astype(BF16)

    row = pl.BlockSpec((tr, wd), lambda i: (i, 0))
    return pl.pallas_call(
        body, name=name, out_shape=jax.ShapeDtypeStruct((n, wd), BF16), grid=(n // tr,),
        in_specs=[row, pl.BlockSpec((wd, wd), lambda i: (0, 0)), pl.BlockSpec((1, wd), lambda i: (0, 0))],
        out_specs=row, compiler_params=_cparams("parallel"),
    )(ys, w, b.reshape(1, wd))


def _glu_bwd(ys, dyb2, w, w_t, b, *, name, tr=512):
    n, wd = ys.shape
    tr = _tile(n, tr, 8)

    def body(y_ref, d_ref, w_ref, wt_ref, b_ref, dys_ref, dz_ref, yb_ref, db_ref):
        i = pl.program_id(0)
        yv = y_ref[...]
        yb = _gelu(yv)
        ybb = yb.astype(BF16)
        sg = _sigmoid(jnp.dot(ybb, w_ref[...], preferred_element_type=F32) + b_ref[...])
        dv = d_ref[...].astype(F32)
        dz = dv * yb * sg * (1.0 - sg)
        dzb = dz.astype(BF16)
        dyb = dv * sg + jnp.dot(dzb, wt_ref[...], preferred_element_type=F32)
        dys_ref[...] = dyb * _gelu_grad(yv)
        dz_ref[...] = dzb
        yb_ref[...] = ybb
        part = jnp.sum(dz, axis=0, keepdims=True)

        @pl.when(i == 0)
        def _():
            db_ref[...] = part

        @pl.when(i > 0)
        def _():
            db_ref[...] += part

    row = pl.BlockSpec((tr, wd), lambda i: (i, 0))
    mat = pl.BlockSpec((wd, wd), lambda i: (0, 0))
    vec = pl.BlockSpec((1, wd), lambda i: (0, 0))
    return pl.pallas_call(
        body, name=name,
        out_shape=(jax.ShapeDtypeStruct((n, wd), F32), jax.ShapeDtypeStruct((n, wd), BF16),
                   jax.ShapeDtypeStruct((n, wd), BF16), jax.ShapeDtypeStruct((1, wd), F32)),
        grid=(n // tr,), in_specs=[row, row, mat, mat, vec], out_specs=(row, row, row, vec),
        compiler_params=_cparams("arbitrary"),
    )(ys, dyb2, w, w_t, b.reshape(1, wd))


def _merge_fwd(ya, yb2, wa, wb, proj, gate_blk, *, name, tr=512):
    n, aw = ya.shape
    d = wa.shape[1]
    tr = _tile(n, tr, 8)

    def body(ya_ref, yb_ref, wa_ref, wb_ref, ga_ref, gb_ref, mix_ref, pa_ref, pb_ref):
        pa = jnp.dot(ya_ref[...], wa_ref[...], preferred_element_type=F32)
        pb = jnp.dot(yb_ref[...], wb_ref[...], preferred_element_type=F32)
        mix = _sigmoid(ga_ref[...].astype(F32)) * pa + _sigmoid(gb_ref[...].astype(F32)) * pb
        mix_ref[...] = mix.astype(BF16)
        pa_ref[...] = pa.astype(BF16)
        pb_ref[...] = pb.astype(BF16)

    row = lambda wdt: pl.BlockSpec((tr, wdt), lambda i: (i, 0))
    full = lambda r, c: pl.BlockSpec((r, c), lambda i: (0, 0))
    out = jax.ShapeDtypeStruct((n, d), BF16)
    return pl.pallas_call(
        body, name=name, out_shape=(out, out, out), grid=(n // tr,),
        in_specs=[row(aw), row(yb2.shape[1]), full(*wa.shape), full(*wb.shape),
                  pl.BlockSpec((tr, d), lambda i: (i, gate_blk)), pl.BlockSpec((tr, d), lambda i: (i, gate_blk + 1))],
        out_specs=(row(d), row(d), row(d)), compiler_params=_cparams("parallel"),
    )(ya, yb2, wa, wb, proj, proj)


def _merge_bwd(dmix, proj, pa, pb, gate_blk, *, name, tr=512):
    n, d = dmix.shape
    tr = _tile(n, tr, 8)

    def body(dm_ref, ga_ref, gb_ref, pa_ref, pb_ref, dpa_ref, dpb_ref, dga_ref, dgb_ref):
        dm = dm_ref[...].astype(F32)
        sa = _sigmoid(ga_ref[...].astype(F32))
        sb = _sigmoid(gb_ref[...].astype(F32))
        dpa_ref[...] = (dm * sa).astype(BF16)
        dpb_ref[...] = (dm * sb).astype(BF16)
        dga_ref[...] = (dm * pa_ref[...].astype(F32) * sa * (1.0 - sa)).astype(BF16)
        dgb_ref[...] = (dm * pb_ref[...].astype(F32) * sb * (1.0 - sb)).astype(BF16)

    row = pl.BlockSpec((tr, d), lambda i: (i, 0))
    out = jax.ShapeDtypeStruct((n, d), BF16)
    return pl.pallas_call(
        body, name=name, out_shape=(out, out, out, out), grid=(n // tr,),
        in_specs=[row, pl.BlockSpec((tr, d), lambda i: (i, gate_blk)), pl.BlockSpec((tr, d), lambda i: (i, gate_blk + 1)),
                  row, row],
        out_specs=(row, row, row, row), compiler_params=_cparams("parallel"),
    )(dmix, proj, proj, pa, pb)


def _outproj_fwd(mixed, w, x0, g, *, name, tr=512):
    n, d = x0.shape
    tr = _tile(n, tr, 8)

    def body(m_ref, w_ref, x_ref, g_ref, x1_ref, h_ref, r_ref):
        x1 = x_ref[...] + jnp.dot(m_ref[...], w_ref[...], preferred_element_type=F32)
        r = lax.rsqrt(jnp.mean(x1 * x1, axis=-1, keepdims=True) + RMS_EPS)
        x1_ref[...] = x1
        h_ref[...] = (x1 * r * g_ref[...]).astype(BF16)
        r_ref[...] = r

    row = pl.BlockSpec((tr, d), lambda i: (i, 0))
    return pl.pallas_call(
        body, name=name,
        out_shape=(jax.ShapeDtypeStruct((n, d), F32), jax.ShapeDtypeStruct((n, d), BF16),
                   jax.ShapeDtypeStruct((n, 1), F32)),
        grid=(n // tr,),
        in_specs=[row, pl.BlockSpec((d, d), lambda i: (0, 0)), row, pl.BlockSpec((1, d), lambda i: (0, 0))],
        out_specs=(row, row, pl.BlockSpec((tr, 1), lambda i: (i, 0))),
        compiler_params=_cparams("parallel"),
    )(mixed, w, x0, g.reshape(1, d))


def _adamw(w, g, m, v, *, name):
    shape = w.shape
    total = w.size
    if total % PACK_COLS == 0 and ((total // PACK_COLS) % 8 == 0 or total // PACK_COLS <= 512):
        rows, cols = total // PACK_COLS, PACK_COLS
    elif w.ndim >= 2:
        rows, cols = total // shape[-1], shape[-1]
    else:
        rows, cols = 1, total
    tr = _tile(rows, 512, 8)

    def body(w_ref, g_ref, m_ref, v_ref, d_ref, nm_ref, nv_ref):
        gv = g_ref[...]
        mn = ADAM_B1 * m_ref[...] + (1.0 - ADAM_B1) * gv
        vn = ADAM_B2 * v_ref[...] + (1.0 - ADAM_B2) * (gv * gv)
        m_hat = mn / (1.0 - ADAM_B1 ** ADAM_STEP)
        v_hat = vn / (1.0 - ADAM_B2 ** ADAM_STEP)
        d_ref[...] = -ADAM_LR * (m_hat / (jnp.sqrt(v_hat) + ADAM_EPS) + ADAM_WD * w_ref[...])
        nm_ref[...] = mn
        nv_ref[...] = vn

    blk = pl.BlockSpec((tr, cols), lambda i: (i, 0))
    out = jax.ShapeDtypeStruct((rows, cols), F32)
    outs = pl.pallas_call(
        body, name=name, out_shape=(out, out, out), grid=(rows // tr,),
        in_specs=[blk] * 4, out_specs=(blk, blk, blk), compiler_params=_cparams("parallel"),
    )(*[t.reshape(rows, cols) for t in (w, g, m, v)])
    return tuple(o.reshape(shape) for o in outs)


HBM = pl.BlockSpec(memory_space=pltpu.HBM)
MESH = pl.DeviceIdType.MESH


def _all_gather(blocks, *, name):
    n = len(blocks)

    def body(*refs):
        x_refs, out_refs = refs[:n], refs[n:2 * n]
        send_sems, recv_sems, local_sems = refs[2 * n:]
        x, y, c = lax.axis_index("x"), lax.axis_index("y"), lax.axis_index("c")
        me, sibling = (x, y, c), (x, y, 1 - c)
        chips = [(1 - x, y), (x, 1 - y), (1 - x, 1 - y)]

        def slot(a, px, py, pc):
            return out_refs[a].at[4 * px + 2 * py + pc]

        def copy(a, k, block, to, src=None):
            return pltpu.make_async_remote_copy(
                src_ref=slot(a, *block) if src is None else src, dst_ref=slot(a, *block),
                send_sem=send_sems.at[7 * a + k], recv_sem=recv_sems.at[7 * a + k], device_id=to,
                device_id_type=MESH)

        started = []
        for a in range(n):
            mine = pltpu.make_async_copy(x_refs[a], slot(a, *me), local_sems.at[a])
            mine.start()
            started.append(mine)
        sends = []
        for a in range(n):
            first = [copy(a, 0, me, sibling, src=x_refs[a])]
            first += [copy(a, 1 + j, me, (*chip, c), src=x_refs[a]) for j, chip in enumerate(chips)]
            for cp in first:
                cp.start()
            sends += first
        for a in range(n):
            for j, chip in enumerate(chips):
                copy(a, 1 + j, (*chip, c), me).wait_recv()
                onward = copy(a, 4 + j, (*chip, c), sibling)
                onward.start()
                sends.append(onward)
        for a in range(n):
            copy(a, 0, sibling, me).wait_recv()
            for j, chip in enumerate(chips):
                copy(a, 4 + j, (*chip, 1 - c), me).wait_recv()
        for cp in sends:
            cp.wait_send()
        for mine in started:
            mine.wait()

    return pl.pallas_call(
        body, name=name, out_shape=[jax.ShapeDtypeStruct((N_DEV,) + b.shape, b.dtype) for b in blocks],
        in_specs=[HBM] * n, out_specs=[HBM] * n,
        scratch_shapes=[pltpu.SemaphoreType.DMA((7 * n,)), pltpu.SemaphoreType.DMA((7 * n,)),
                        pltpu.SemaphoreType.DMA((n,))],
    )(*blocks)


def _all_to_all(blocks, *, name):
    n = len(blocks)

    def body(*refs):
        x_refs, out_refs = refs[:n], refs[n:2 * n]
        send_sems, recv_sems, local_sems = refs[2 * n:]
        x, y, c = lax.axis_index("x"), lax.axis_index("y"), lax.axis_index("c")
        me = 4 * x + 2 * y + c
        copies = []
        for a in range(n):
            mine = pltpu.make_async_copy(x_refs[a].at[me], out_refs[a].at[me], local_sems.at[a])
            mine.start()
            copies.append(mine)
        for k in range(1, N_DEV):
            px = x if not (k >> 2) & 1 else 1 - x
            py = y if not (k >> 1) & 1 else 1 - y
            pc = c if not k & 1 else 1 - c
            for a in range(n):
                cp = pltpu.make_async_remote_copy(
                    src_ref=x_refs[a].at[4 * px + 2 * py + pc], dst_ref=out_refs[a].at[me],
                    send_sem=send_sems.at[7 * a + k - 1], recv_sem=recv_sems.at[7 * a + k - 1],
                    device_id=(px, py, pc), device_id_type=MESH)
                cp.start()
                copies.append(cp)
        for cp in copies:
            cp.wait()

    return pl.pallas_call(
        body, name=name, out_shape=[jax.ShapeDtypeStruct(b.shape, b.dtype) for b in blocks],
        in_specs=[HBM] * n, out_specs=[HBM] * n,
        scratch_shapes=[pltpu.SemaphoreType.DMA((7 * n,)), pltpu.SemaphoreType.DMA((7 * n,)),
                        pltpu.SemaphoreType.DMA((n,))],
    )(*blocks)


def _sum8(blocks, *, name, tr=PACK_ROWS):
    _, R, C = blocks.shape
    tr = _tile(R, tr, 16)

    def body(x_ref, o_ref):
        acc = x_ref[0].astype(F32)
        for i in range(1, N_DEV):
            acc = acc + x_ref[i].astype(F32)
        o_ref[...] = acc

    return pl.pallas_call(
        body, name=name, out_shape=jax.ShapeDtypeStruct((R, C), F32), grid=(R // tr,),
        in_specs=[pl.BlockSpec((N_DEV, tr, C), lambda i: (0, i, 0))],
        out_specs=pl.BlockSpec((tr, C), lambda i: (i, 0)), compiler_params=_cparams("parallel"),
    )(blocks)


def _pack_rows(flat_last):
    n = flat_last.shape[-1]
    unit = PACK_ROWS * PACK_COLS
    padded = -(-n // unit) * unit
    pad = [(0, 0)] * (flat_last.ndim - 1) + [(0, padded - n)]
    return jnp.pad(flat_last, pad).reshape(flat_last.shape[:-1] + (padded // PACK_COLS, PACK_COLS))


def _ssm_discretise(lre, lim, logdt, bre, bim):
    lam = lax.complex(lre, lim)
    dt = jnp.exp(logdt)[:, None]
    lam_bar = jnp.exp(lam * dt)
    b_bar = ((lam_bar - 1.0) / lam)[:, :, None] * lax.complex(bre, bim)
    return lam_bar.real, lam_bar.imag, b_bar.real, b_bar.imag


def _block_diag(a, rows_first):
    ns, g, r, c = a.shape
    eye = jnp.eye(g, dtype=a.dtype)
    return jnp.einsum("sgrc,gh->sgrhc", a, eye).reshape(ns, g * r, g * c)


def _diag_blocks(m, r, c):
    ns = m.shape[0]
    g = SLAB_GROUPS
    return jnp.einsum("sgrhc,gh->sgrc", m.reshape(ns, g, r, g, c), jnp.eye(g, dtype=m.dtype))


def _to_tm(a, T):
    b, s, w = a.shape
    return a.reshape(b, s // T, T, w).transpose(0, 2, 1, 3)


def _from_tm(a):
    b, t, nc, w = a.shape
    return a.transpose(0, 2, 1, 3).reshape(b, nc * t, w)


WEIGHTS = ["norm_mix", "w_in", "b_forget", "ssm_lambda_re", "ssm_lambda_im", "ssm_log_dt", "ssm_b_re", "ssm_b_im",
           "ssm_c_re", "ssm_c_im", "ssm_d", "w_glu", "b_glu", "w_branch_a", "w_branch_b", "w_out", "norm_mlp",
           "w_mlp_up", "w_mlp_down", "norm_final"]
SHARDED = {"w_in": 2, "w_glu": 1, "w_branch_a": 2, "w_branch_b": 2, "w_out": 1, "w_mlp_up": 2, "w_mlp_down": 1}


REST = [n for n in SHARDED if n != "w_in"]


def _whole(n, seg):
    ax = SHARDED[n] - 1
    shp = seg.shape[1:]
    return jnp.moveaxis(seg, 0, ax).reshape(shp[:ax] + (N_DEV * shp[ax],) + shp[ax + 1:])


def _blocks(n, g):
    ax = SHARDED[n] - 1
    shp = g.shape
    return jnp.moveaxis(g.reshape(shp[:ax] + (N_DEV, shp[ax] // N_DEV) + shp[ax + 1:]), ax, 0)


def _sum_blocks(n, got):
    return _sum8(got.reshape(N_DEV, -1, got.shape[-1]), name="sum_grads_" + n).reshape(got.shape[1:])


def _allreduce_small(grads):
    names = list(grads)
    flat = jnp.concatenate([grads[n].astype(F32).reshape(-1) for n in names])
    got = _all_gather([_pack_rows(flat)], name="gather_small_grads")[0]
    summed = _sum8(got, name="sum_small_grads").reshape(-1)
    out, off = {}, 0
    for n in names:
        out[n] = summed[off:off + grads[n].size].reshape(grads[n].shape)
        off += grads[n].size
    return out


def kernel(x, norm_mix, w_in, b_forget, ssm_lambda_re, ssm_lambda_im, ssm_log_dt, ssm_b_re, ssm_b_im, ssm_c_re, ssm_c_im, ssm_d, w_glu, b_glu, w_branch_a, w_branch_b, w_out, norm_mlp, w_mlp_up, w_mlp_down, norm_final, loss_target, m_norm_mix, m_w_in, m_b_forget, m_ssm_lambda_re, m_ssm_lambda_im, m_ssm_log_dt, m_ssm_b_re, m_ssm_b_im, m_ssm_c_re, m_ssm_c_im, m_ssm_d, m_w_glu, m_b_glu, m_w_branch_a, m_w_branch_b, m_w_out, m_norm_mlp, m_w_mlp_up, m_w_mlp_down, m_norm_final, v_norm_mix, v_w_in, v_b_forget, v_ssm_lambda_re, v_ssm_lambda_im, v_ssm_log_dt, v_ssm_b_re, v_ssm_b_im, v_ssm_c_re, v_ssm_c_im, v_ssm_d, v_w_glu, v_b_glu, v_w_branch_a, v_w_branch_b, v_w_out, v_norm_mlp, v_w_mlp_up, v_w_mlp_down, v_norm_final):
    args = dict(locals())
    w = {n: args[n] for n in WEIGHTS}
    Bl, S, D = x.shape
    L, H = b_forget.shape
    G, P, C = ssm_b_re.shape[1:]
    AW, W, HP = H * HEAD_DIM, G * C, H // 2
    N = Bl * S
    T = SSM_CHUNK
    NS = G // SLAB_GROUPS
    SP = SLAB_GROUPS * P
    tq = min(ATTN_BLOCK, S)
    nq = S // tq
    u_off = 3 * AW
    gate_blk = (u_off + W) // D
    assert (u_off + W) % D == 0 and W % LANES == 0 and AW % LANES == 0 and S % T == 0

    shard = {n: w[n].astype(BF16) for n in SHARDED}
    weights = [dict() for _ in range(L)]
    weights[0]["w_in"] = _whole("w_in", _all_gather([shard["w_in"][0]], name="gather_first")[0])
    tr_ = lambda a: jnp.swapaxes(a, 1, 2)

    ssm = []
    for l in range(L):
        disc, disc_vjp = jax.vjp(_ssm_discretise, ssm_lambda_re[l], ssm_lambda_im[l], ssm_log_dt[l],
                                 ssm_b_re[l], ssm_b_im[l])
        lbr, lbi, bbr, bbi = disc
        z = lax.complex(ssm_lambda_re[l], ssm_lambda_im[l]) * jnp.exp(ssm_log_dt[l])[:, None]
        powers = jnp.exp(z[None] * jnp.arange(1, T + 1, dtype=F32)[:, None, None])
        slabs = lambda a: a.reshape(NS, SP)
        lam = jnp.stack([slabs(lbr), slabs(lbi)], axis=1)
        lam_t = jnp.stack([slabs(powers[T - 1].real), slabs(powers[T - 1].imag)], axis=1)
        pw = jnp.stack([powers.real.reshape(T, NS, SP), powers.imag.reshape(T, NS, SP)], axis=0).transpose(2, 0, 1, 3)
        to_rows = lambda a: jnp.swapaxes(a.reshape(NS, SLAB_GROUPS, P, C), 2, 3)
        bmat = jnp.concatenate([_block_diag(to_rows(bbr), True), _block_diag(to_rows(bbi), True)], axis=2)
        cre = ssm_c_re[l].reshape(NS, SLAB_GROUPS, C, P)
        cim = ssm_c_im[l].reshape(NS, SLAB_GROUPS, C, P)
        cmat_t = jnp.concatenate([_block_diag(cre, True), -_block_diag(cim, True)], axis=2)
        ssm.append(dict(vjp=disc_vjp, lam=lam, lam_t=lam_t, pw=pw, bmat=bmat.astype(BF16),
                        bmat_t=tr_(bmat).astype(BF16), cmat=tr_(cmat_t).astype(BF16), cmat_t=cmat_t.astype(BF16),
                        d=ssm_d[l].reshape(1, W)))

    xcur = x.reshape(N, D)
    saved = []
    for l in range(L):
        s_, wl = ssm[l], weights[l]
        win = wl["w_in"]
        wl["wcat"] = jnp.concatenate([win[:, :3 * AW], win[:, 3 * AW + H:]], axis=1)
        wl["wf"] = jnp.pad(win[:, 3 * AW:3 * AW + H], ((0, 0), (0, LANES - H)))
        h, r0 = _rmsnorm_fwd(xcur, norm_mix[l], name="norm_mix_fwd")
        proj = _mm(h, wl["wcat"], name="in_proj")
        fl = _mm(h, wl["wf"], name="forget_proj", out_dtype=F32)
        ft = fl[:, :H].reshape(Bl, S, H).transpose(0, 2, 1)
        F = _fox_gate_fwd(ft, b_forget[l], name="forget_gate_fwd")
        frow = F.reshape(Bl, HP, 2, nq, tq)
        proj3 = proj.reshape(Bl, S, -1)
        coming = [shard[n][l] for n in REST] + ([shard["w_in"][l + 1]] if l + 1 < L else [])
        ya, lse, got = _attn_fwd(proj3, frow, name="attn_fwd" if l + 1 < L else "attn_fwd_last", H=H, tq=tq,
                                 hosted=_Hosted("gather", coming))
        for n, seg in zip(REST, got):
            wl[n] = _whole(n, seg)
        if l + 1 < L:
            weights[l + 1]["w_in"] = _whole("w_in", got[-1])
        u_tm = _to_tm(proj3[:, :, u_off:u_off + W], T)
        ys = _from_tm(_ssm_fwd(u_tm, s_["bmat"], s_["cmat"], s_["lam"], s_["pw"], s_["lam_t"], s_["d"],
                               name="ssm_fwd")).reshape(N, W)
        yb2 = _glu_fwd(ys, wl["w_glu"], b_glu[l], name="glu_fwd")
        ya2 = ya.reshape(N, AW)
        mixed, pa, pb = _merge_fwd(ya2, yb2, wl["w_branch_a"], wl["w_branch_b"], proj, gate_blk, name="merge_fwd")
        x1, h2, r1 = _outproj_fwd(mixed, wl["w_out"], xcur, norm_mlp[l], name="out_proj")
        a = _mm(h2, wl["w_mlp_up"], name="mlp_up")
        x2 = _mm(a, wl["w_mlp_down"], name="mlp_down", a_fn=_relu_sq, epi=lambda acc, res: acc + res,
                 extras=(x1,), out_dtype=F32, tk=LONG_K)
        saved.append(dict(x0=xcur, h=h, r0=r0, proj=proj, ft=ft, frow=frow, ya=ya, lse=lse, u_tm=u_tm,
                          ys=ys, yb2=yb2, mixed=mixed, pa=pa, pb=pb, x1=x1, h2=h2, r1=r1, a=a))
        xcur = x2

    dx, g_final, loss_row = _loss_head(xcur, norm_final, loss_target.reshape(N, D), name="loss_head")
    loss = lax.psum(loss_row[0, 0], MESH_AXES)

    big = {n: [None] * L for n in SHARDED}
    small = {n: [None] * L for n in WEIGHTS if n not in SHARDED and n != "norm_final"}
    win_grad = None
    for l in reversed(range(L)):
        sv, s_, wl = saved[l], ssm[l], weights[l]
        a = sv["a"]
        gw = {}
        d_a = _mm(dx, wl["w_mlp_down"], name="mlp_down_dx", tb=True,
                  epi=lambda acc, av: acc * (2.0 * jnp.maximum(av.astype(F32), 0.0)), extras=(a,))
        gw["w_mlp_down"] = _mm(a, dx, name="mlp_down_dw", ta=True, a_fn=_relu_sq, tk=LONG_K)
        gw["w_mlp_up"] = _mm(sv["h2"], d_a, name="mlp_up_dw", ta=True, tk=LONG_K)
        dh2 = _mm(d_a, wl["w_mlp_up"], name="mlp_up_dx", tb=True, out_dtype=F32, tk=LONG_K)
        dx1, g = _rmsnorm_bwd(dh2, sv["x1"], sv["r1"], norm_mlp[l], dx, name="norm_mlp_bwd")
        small["norm_mlp"][l] = g[0]
        dmix = _mm(dx1, wl["w_out"], name="out_proj_dx", tb=True)
        gw["w_out"] = _mm(sv["mixed"], dx1, name="out_proj_dw", ta=True, tk=LONG_K)
        dpa, dpb, dga, dgb = _merge_bwd(dmix, sv["proj"], sv["pa"], sv["pb"], gate_blk, name="merge_bwd")
        ya2 = sv["ya"].reshape(N, AW)
        gw["w_branch_a"] = _mm(ya2, dpa, name="branch_a_dw", ta=True, tk=LONG_K)
        dya = _mm(dpa, wl["w_branch_a"], name="branch_a_dx", tb=True)
        gw["w_branch_b"] = _mm(sv["yb2"], dpb, name="branch_b_dw", ta=True, tk=LONG_K)
        dyb2 = _mm(dpb, wl["w_branch_b"], name="branch_b_dx", tb=True)
        dys, dz, yb, g = _glu_bwd(sv["ys"], dyb2, wl["w_glu"], wl["w_glu"].T, b_glu[l], name="glu_bwd")
        small["b_glu"][l] = g[0]
        gw["w_glu"] = _mm(yb, dz, name="glu_dw", ta=True, tk=LONG_K)

        du_tm, g_bt, g_ct, g_lam, g_d = _ssm_bwd(
            sv["u_tm"], _to_tm(dys.reshape(Bl, S, W), T), s_["bmat"], s_["bmat_t"], s_["cmat_t"], s_["lam"],
            s_["pw"], s_["lam_t"], s_["d"], name="ssm_bwd")
        du = _from_tm(du_tm).reshape(N, W)
        g_b = _diag_blocks(jnp.swapaxes(g_bt, 1, 2).reshape(NS, LANES, 2, SP).transpose(2, 0, 1, 3).reshape(
            2 * NS, LANES, SP), C, P).reshape(2, G, C, P)
        g_bbar = jnp.swapaxes(g_b, 2, 3)
        g_c = _diag_blocks(g_ct.reshape(NS, LANES, 2, SP).transpose(2, 0, 1, 3).reshape(2 * NS, LANES, SP),
                           C, P).reshape(2, G, C, P)
        g_lbar = g_lam.transpose(1, 0, 2).reshape(2, G, P)
        g_lre, g_lim, g_ldt, g_bre, g_bim = s_["vjp"]((g_lbar[0], g_lbar[1], g_bbar[0], g_bbar[1]))
        small["ssm_lambda_re"][l], small["ssm_lambda_im"][l], small["ssm_log_dt"][l] = g_lre, g_lim, g_ldt
        small["ssm_b_re"][l], small["ssm_b_im"][l] = g_bre, g_bim
        small["ssm_c_re"][l], small["ssm_c_im"][l] = g_c[0], -g_c[1]
        small["ssm_d"][l] = g_d.reshape(W)

        proj3 = sv["proj"].reshape(Bl, S, -1)
        leaving = [_blocks(n, gw[n]) for n in REST] + ([_blocks("w_in", win_grad)] if l + 1 < L else [])
        (dq, dk, dv, dfk, dfq), got = _attn_bwd(
            proj3, sv["ya"], dya.reshape(Bl, S, AW), sv["lse"], sv["frow"],
            name="attn_bwd" if l + 1 < L else "attn_bwd_top", H=H, tq=tq, hosted=_Hosted("exchange", leaving))
        for n, blocks in zip(REST, got):
            big[n][l] = _sum_blocks(n, blocks)
        if l + 1 < L:
            big["w_in"][l + 1] = _sum_blocks("w_in", got[-1])
        dF = dfk.reshape(Bl, H, S) + dfq.transpose(0, 1, 3, 2).reshape(Bl, H, S)
        dft, g = _fox_gate_bwd(dF, sv["ft"], b_forget[l], name="forget_gate_bwd")
        small["b_forget"][l] = g[:, 0]
        dfl = jnp.pad(dft.transpose(0, 2, 1).reshape(N, H), ((0, 0), (0, LANES - H))).astype(BF16)
        dproj = jnp.concatenate([dq.reshape(N, AW), dk.reshape(N, AW), dv.reshape(N, AW), du, dga, dgb, dfl], axis=1)
        gcat = _mm(sv["h"], dproj, name="in_proj_dw", ta=True, tn=1408, tk=LONG_K)
        ncat = wl["wcat"].shape[1]
        win_grad = jnp.concatenate([gcat[:, :3 * AW], gcat[:, ncat:ncat + H], gcat[:, 3 * AW:ncat]], axis=1)
        dh = _mm(dproj, jnp.concatenate([wl["wcat"], wl["wf"]], axis=1), name="in_proj_dx", tb=True, out_dtype=F32,
                 tk=1408)
        dx, g = _rmsnorm_bwd(dh, sv["x0"], sv["r0"], norm_mix[l], dx1, name="norm_mix_bwd")
        small["norm_mix"][l] = g[0]

    big["w_in"][0] = _sum_blocks("w_in", _all_to_all([_blocks("w_in", win_grad)], name="exchange_last")[0])
    grads = {n: jnp.stack(big[n]) for n in SHARDED}
    small_stacked = {n: jnp.stack(small[n]) for n in small}
    small_stacked["norm_final"] = g_final[0]
    grads.update(_allreduce_small(small_stacked))

    deltas, new_m, new_v = {}, {}, {}
    for n in WEIGHTS:
        deltas[n], new_m[n], new_v[n] = _adamw(w[n], grads[n], args["m_" + n], args["v_" + n], name="adamw_" + n)
    return (loss, dx.reshape(Bl, S, D), *[grads[n] for n in WEIGHTS], *[deltas[n] for n in WEIGHTS],
            *[new_m[n] for n in WEIGHTS], *[new_v[n] for n in WEIGHTS])
```

```python
import functools

import jax
import jax.numpy as jnp
from jax import lax
from jax.experimental import pallas as pl
from jax.experimental.pallas import tpu as pltpu

F32 = jnp.float32
BF16 = jnp.bfloat16

N_DEV = 8
HEAD_DIM = 64
LANES = 128
SSM_CHUNK = 32
SLAB_GROUPS = 8
ATTN_BLOCK = 512
LONG_K = 2048
WIDE_N = 2048
PACK_COLS = 1024
SUM_ROWS = 256
RMS_EPS = 1e-6
VMEM_LIMIT = 56 * 1024 * 1024
ADAM_LR, ADAM_B1, ADAM_B2, ADAM_EPS, ADAM_WD, ADAM_STEP = 0.001, 0.9, 0.999, 1e-08, 0.01, 10
MESH_AXES = ("x", "y", "c")
NEG = -1e30
NT = (((1,), (1,)), ((), ()))
TN = (((0,), (0,)), ((), ()))


def _cparams(*sem):
    return pltpu.CompilerParams(dimension_semantics=sem, vmem_limit_bytes=VMEM_LIMIT)


def _tile(dim, pref, unit=LANES):
    if dim <= pref:
        return dim
    best = None
    for t in range(unit, pref + 1, unit):
        if dim % t == 0:
            best = t
    assert best is not None, (dim, pref)
    return best


def _mm(a, b, *, name, ta=False, tb=False, a_fn=None, epi=None, extras=(), out_dtype=BF16, tm=1024, tn=1024,
        tk=1024, hosted=None):
    if ta:
        K, M = a.shape
    else:
        M, K = a.shape
    N, Kb = b.shape if tb else b.shape[::-1]
    assert K == Kb and not (ta and tb), (a.shape, b.shape)
    tm, tn, tk = _tile(M, tm), _tile(N, tn), _tile(K, tk)
    gm, gn, nk = M // tm, N // tn, K // tk
    ne = len(extras)
    nh = hosted.n if hosted else 0
    n_acc = 1 if nk > 1 else 0

    def body(a_ref, b_ref, *rest):
        e_refs, o_ref = rest[:ne], rest[ne + nh]
        acc_ref = rest[ne + 2 * nh + 1] if nk > 1 else None
        k = pl.program_id(2)
        if hosted:
            i, j = pl.program_id(0), pl.program_id(1)
            start, finish = hosted.run((i == 0) & (j == 0) & (k == 0), (i == gm - 1) & (j == gn - 1) & (k == nk - 1),
                                       rest[ne:ne + nh], rest[ne + nh + 1:ne + 2 * nh + 1],
                                       rest[ne + 2 * nh + 1 + n_acc:])
            start()
        av = a_ref[...]
        if a_fn is not None:
            av = a_fn(av)
        av = av.astype(BF16)
        bv = b_ref[...].astype(BF16)
        dims = TN if ta else NT if tb else (((1,), (0,)), ((), ()))
        part = lax.dot_general(av, bv, dims, preferred_element_type=F32)

        def finish_tile(r):
            if epi is not None:
                r = epi(r, *[e[...] for e in e_refs])
            o_ref[...] = r.astype(o_ref.dtype)

        if nk == 1:
            finish_tile(part)
        else:
            @pl.when(k == 0)
            def _():
                acc_ref[...] = part

            @pl.when(k > 0)
            def _():
                acc_ref[...] += part

            @pl.when(k == nk - 1)
            def _():
                finish_tile(acc_ref[...])

        if hosted:
            finish()

    a_spec = pl.BlockSpec((tk, tm), lambda i, j, k: (k, i)) if ta else pl.BlockSpec((tm, tk), lambda i, j, k: (i, k))
    outs = pl.pallas_call(
        body, name=name,
        out_shape=[jax.ShapeDtypeStruct((M, N), out_dtype)] + (hosted.out_shape if hosted else []),
        grid=(gm, gn, nk),
        in_specs=[a_spec, pl.BlockSpec((tn, tk), lambda i, j, k: (j, k)) if tb
                  else pl.BlockSpec((tk, tn), lambda i, j, k: (k, j))]
        + [pl.BlockSpec((tm, tn), lambda i, j, k: (i, j)) for _ in extras] + [HBM] * nh,
        out_specs=[pl.BlockSpec((tm, tn), lambda i, j, k: (i, j))] + [HBM] * nh,
        scratch_shapes=([pltpu.VMEM((tm, tn), F32)] if nk > 1 else []) + (hosted.scratch if hosted else []),
        compiler_params=_cparams(*(("arbitrary",) * 3 if hosted else ("parallel", "parallel", "arbitrary"))),
    )(a, b, *extras, *(hosted.arrays if hosted else []))
    return (outs[0], outs[1:]) if hosted else outs[0]


def _relu_sq(v):
    r = jnp.maximum(v.astype(F32), 0.0)
    return r * r


def _sigmoid(v):
    return 1.0 / (1.0 + jnp.exp(-v))


GELU_C = 0.7978845608028654
GELU_A = 0.044715


def _gelu(v):
    return 0.5 * v * (1.0 + jnp.tanh(GELU_C * (v + GELU_A * v * v * v)))


def _gelu_grad(v):
    t = jnp.tanh(GELU_C * (v + GELU_A * v * v * v))
    return 0.5 * (1.0 + t) + 0.5 * v * (1.0 - t * t) * GELU_C * (1.0 + 3.0 * GELU_A * v * v)


def _rmsnorm_fwd(x, g, *, name, tr=512):
    n, d = x.shape
    tr = _tile(n, tr, 8)

    def body(x_ref, g_ref, h_ref, r_ref):
        xv = x_ref[...]
        r = lax.rsqrt(jnp.mean(xv * xv, axis=-1, keepdims=True) + RMS_EPS)
        h_ref[...] = (xv * r * g_ref[...]).astype(BF16)
        r_ref[...] = r

    return pl.pallas_call(
        body, name=name,
        out_shape=(jax.ShapeDtypeStruct((n, d), BF16), jax.ShapeDtypeStruct((n, 1), F32)),
        grid=(n // tr,),
        in_specs=[pl.BlockSpec((tr, d), lambda i: (i, 0)), pl.BlockSpec((1, d), lambda i: (0, 0))],
        out_specs=(pl.BlockSpec((tr, d), lambda i: (i, 0)), pl.BlockSpec((tr, 1), lambda i: (i, 0))),
        compiler_params=_cparams("parallel"),
    )(x, g.reshape(1, d))


def _rmsnorm_bwd(dh, x, r, g, dres, *, name, tr=512):
    n, d = x.shape
    tr = _tile(n, tr, 8)

    def body(dh_ref, x_ref, r_ref, g_ref, dres_ref, dx_ref, dg_ref):
        i = pl.program_id(0)
        rv = r_ref[...]
        xh = x_ref[...] * rv
        dhv = dh_ref[...].astype(F32)
        dxh = dhv * g_ref[...]
        m = jnp.mean(dxh * xh, axis=-1, keepdims=True)
        dx_ref[...] = rv * (dxh - xh * m) + dres_ref[...]
        part = jnp.sum(dhv * xh, axis=0, keepdims=True)

        @pl.when(i == 0)
        def _():
            dg_ref[...] = part

        @pl.when(i > 0)
        def _():
            dg_ref[...] += part

    row = pl.BlockSpec((tr, d), lambda i: (i, 0))
    vec = pl.BlockSpec((1, d), lambda i: (0, 0))
    return pl.pallas_call(
        body, name=name,
        out_shape=(jax.ShapeDtypeStruct((n, d), F32), jax.ShapeDtypeStruct((1, d), F32)),
        grid=(n // tr,),
        in_specs=[row, row, pl.BlockSpec((tr, 1), lambda i: (i, 0)), vec, row],
        out_specs=(row, vec),
        compiler_params=_cparams("arbitrary"),
    )(dh, x, r, g.reshape(1, d), dres)


def _loss_head(x, g, target, *, name, tr=512):
    n, d = x.shape
    tr = _tile(n, tr, 8)

    def body(x_ref, g_ref, t_ref, dx_ref, dg_ref, loss_ref):
        i = pl.program_id(0)
        xv = x_ref[...]
        gv = g_ref[...]
        r = lax.rsqrt(jnp.mean(xv * xv, axis=-1, keepdims=True) + RMS_EPS)
        xh = xv * r
        err = xh * gv - t_ref[...]
        lpart = 0.5 * jnp.sum(jnp.mean(err * err, axis=-1, keepdims=True), axis=0, keepdims=True)
        dy = err * (1.0 / d)
        dxh = dy * gv
        m = jnp.mean(dxh * xh, axis=-1, keepdims=True)
        dx_ref[...] = r * (dxh - xh * m)
        gpart = jnp.sum(dy * xh, axis=0, keepdims=True)
        lrow = jnp.broadcast_to(lpart, (1, LANES))

        @pl.when(i == 0)
        def _():
            dg_ref[...] = gpart
            loss_ref[...] = lrow

        @pl.when(i > 0)
        def _():
            dg_ref[...] += gpart
            loss_ref[...] += lrow

    row = pl.BlockSpec((tr, d), lambda i: (i, 0))
    vec = pl.BlockSpec((1, d), lambda i: (0, 0))
    return pl.pallas_call(
        body, name=name,
        out_shape=(jax.ShapeDtypeStruct((n, d), F32), jax.ShapeDtypeStruct((1, d), F32),
                   jax.ShapeDtypeStruct((1, LANES), F32)),
        grid=(n // tr,),
        in_specs=[row, vec, row],
        out_specs=(row, vec, pl.BlockSpec((1, LANES), lambda i: (0, 0))),
        compiler_params=_cparams("arbitrary"),
    )(x, g.reshape(1, d), target)


def _tri_dot(v, tri):
    hi = v.astype(BF16)
    r1 = v - hi.astype(F32)
    mid = r1.astype(BF16)
    lo = (r1 - mid.astype(F32)).astype(BF16)
    d = functools.partial(jnp.dot, preferred_element_type=F32)
    return d(hi, tri) + d(mid, tri) + d(lo, tri)


def _fox_gate_fwd(ft, bf, *, name, blk=256):
    B, H, S = ft.shape
    blk = _tile(S, blk)
    nb = S // blk

    def body(f_ref, b_ref, o_ref):
        x = f_ref[0] + b_ref[...]
        logf = jnp.minimum(x, 0.0) - jnp.log(1.0 + jnp.exp(-jnp.abs(x)))
        rr = lax.broadcasted_iota(jnp.int32, (blk, blk), 0)
        cc = lax.broadcasted_iota(jnp.int32, (blk, blk), 1)
        tri = (rr <= cc).astype(BF16)
        carry = jnp.zeros((H, 1), F32)
        for n in range(nb):
            c = _tri_dot(logf[:, n * blk:(n + 1) * blk], tri) + carry
            o_ref[0, :, n * blk:(n + 1) * blk] = c
            carry = c[:, blk - 1:blk]

    return pl.pallas_call(
        body, name=name,
        out_shape=jax.ShapeDtypeStruct((B, H, S), F32),
        grid=(B,),
        in_specs=[pl.BlockSpec((1, H, S), lambda b: (b, 0, 0)), pl.BlockSpec((H, 1), lambda b: (0, 0))],
        out_specs=pl.BlockSpec((1, H, S), lambda b: (b, 0, 0)),
        compiler_params=_cparams("parallel"),
    )(ft, bf.reshape(H, 1))


def _fox_gate_bwd(dF, ft, bf, *, name, blk=256):
    B, H, S = ft.shape
    blk = _tile(S, blk)
    nb = S // blk

    def body(d_ref, f_ref, b_ref, o_ref, db_ref):
        b = pl.program_id(0)
        x = f_ref[0] + b_ref[...]
        sneg = 1.0 / (1.0 + jnp.exp(x))
        dv = d_ref[0]
        rr = lax.broadcasted_iota(jnp.int32, (blk, blk), 0)
        cc = lax.broadcasted_iota(jnp.int32, (blk, blk), 1)
        tri = (rr >= cc).astype(BF16)
        carry = jnp.zeros((H, 1), F32)
        tot = jnp.zeros((H, 1), F32)
        for n in reversed(range(nb)):
            sl = slice(n * blk, (n + 1) * blk)
            c = _tri_dot(dv[:, sl], tri) + carry
            g = c * sneg[:, sl]
            o_ref[0, :, sl] = g
            tot = tot + jnp.sum(g, axis=1, keepdims=True)
            carry = c[:, 0:1]

        @pl.when(b == 0)
        def _():
            db_ref[...] = tot

        @pl.when(b > 0)
        def _():
            db_ref[...] += tot

    blkspec = pl.BlockSpec((1, H, S), lambda b: (b, 0, 0))
    return pl.pallas_call(
        body, name=name,
        out_shape=(jax.ShapeDtypeStruct((B, H, S), F32), jax.ShapeDtypeStruct((H, 1), F32)),
        grid=(B,),
        in_specs=[blkspec, blkspec, pl.BlockSpec((H, 1), lambda b: (0, 0))],
        out_specs=(blkspec, pl.BlockSpec((H, 1), lambda b: (0, 0))),
        compiler_params=_cparams("arbitrary"),
    )(dF, ft, bf.reshape(H, 1))


def _head_masks():
    lane = lax.broadcasted_iota(jnp.int32, (1, LANES), 1)
    return [lane < HEAD_DIM, lane >= HEAD_DIM]


HBM = pl.BlockSpec(memory_space=pltpu.HBM)
MESH = pl.DeviceIdType.MESH


def _direct_copies(kinds, x_refs, out_refs, send_sems, recv_sems, local_sems):
    x, y, c = lax.axis_index("x"), lax.axis_index("y"), lax.axis_index("c")
    me = 4 * x + 2 * y + c
    copies = []
    for a, (kind, xr, outr) in enumerate(zip(kinds, x_refs, out_refs)):
        copies.append(pltpu.make_async_copy(xr if kind == "gather" else xr.at[me], outr.at[me], local_sems.at[a]))
    for k in range(1, N_DEV):
        px = 1 - x if (k >> 2) & 1 else x
        py = 1 - y if (k >> 1) & 1 else y
        pc = 1 - c if k & 1 else c
        for a, (kind, xr, outr) in enumerate(zip(kinds, x_refs, out_refs)):
            copies.append(pltpu.make_async_remote_copy(
                src_ref=xr if kind == "gather" else xr.at[4 * px + 2 * py + pc], dst_ref=outr.at[me],
                send_sem=send_sems.at[7 * a + k - 1], recv_sem=recv_sems.at[7 * a + k - 1],
                device_id=(px, py, pc), device_id_type=MESH))
    return copies


class _Hosted:
    def __init__(self, gather=(), exchange=()):
        self.arrays = list(gather) + list(exchange)
        self.kinds = ["gather"] * len(gather) + ["exchange"] * len(exchange)
        self.n = len(self.arrays)
        self.out_shape = [jax.ShapeDtypeStruct(((N_DEV,) if k == "gather" else ()) + a.shape, a.dtype)
                          for k, a in zip(self.kinds, self.arrays)]
        self.scratch = [pltpu.SemaphoreType.DMA((7 * self.n,)), pltpu.SemaphoreType.DMA((7 * self.n,)),
                        pltpu.SemaphoreType.DMA((self.n,))]

    def run(self, first, last, x_refs, out_refs, sems):
        def go(when, act):
            @pl.when(when)
            def _():
                for cp in _direct_copies(self.kinds, x_refs, out_refs, *sems):
                    act(cp)
        return (lambda: go(first, lambda cp: cp.start())), (lambda: go(last, lambda cp: cp.wait()))


def _stack_heads(x, masks):
    zero = jnp.zeros_like(x)
    return jnp.concatenate([jnp.where(masks[0], x, zero), jnp.where(masks[1], x, zero)], axis=0)


def _attn_fwd(proj, frow, *, name, H, tq, hosted=None):
    B, S, _ = proj.shape
    HP = H // 2
    nq = S // tq
    scale = HEAD_DIM ** -0.5
    nh = hosted.n if hosted else 0

    def body(*refs):
        q_ref, k_ref, v_ref, fk_ref = refs[:4]
        o_ref, lse_ref = refs[4 + nh:6 + nh]
        i = pl.program_id(2)
        if hosted:
            b, hp = pl.program_id(0), pl.program_id(1)
            start, finish = hosted.run((b == 0) & (hp == 0) & (i == 0), (b == B - 1) & (hp == HP - 1) & (i == nq - 1),
                                       refs[4:4 + nh], refs[6 + nh:6 + 2 * nh], refs[6 + 2 * nh:])
            start()
        masks = _head_masks()
        q2 = _stack_heads(q_ref[0], masks) * jnp.asarray(scale, BF16)
        rr = lax.broadcasted_iota(jnp.int32, (tq, tq), 0)
        cc = lax.broadcasted_iota(jnp.int32, (tq, tq), 1)
        causal = rr >= cc

        def block(j, carry, masked):
            rows = pl.ds(pl.multiple_of(j * tq, tq), tq)
            kj = k_ref[0, rows, :]
            vj = v_ref[0, rows, :]
            s2 = lax.dot_general(q2, kj, NT, preferred_element_type=F32)
            new, ps = [], []
            for h in range(2):
                m, l, acc = carry[h]
                s = s2[h * tq:(h + 1) * tq] - fk_ref[0, 0, h, pl.ds(j, 1), :]
                if masked:
                    s = jnp.where(causal, s, NEG)
                m_new = jnp.maximum(m, jnp.max(s, axis=-1, keepdims=True))
                alpha = jnp.exp(m - m_new)
                p = jnp.exp(s - m_new)
                new.append((m_new, alpha * l + jnp.sum(p, axis=-1, keepdims=True), alpha, acc))
                ps.append(p.astype(BF16))
            pv = jnp.dot(jnp.concatenate(ps, axis=0), vj, preferred_element_type=F32)
            return tuple((m, l, alpha * acc + pv[h * tq:(h + 1) * tq]) for h, (m, l, alpha, acc) in enumerate(new))

        one = (jnp.full((tq, 1), NEG, F32), jnp.zeros((tq, 1), F32), jnp.zeros((tq, LANES), F32))
        carry = lax.fori_loop(0, i, lambda j, c: block(j, c, False), (one, one))
        (m0, l0, a0), (m1, l1, a1) = block(i, carry, True)
        o_ref[0] = jnp.where(masks[0], a0 / l0, a1 / l1).astype(BF16)
        two = lax.broadcasted_iota(jnp.int32, (1, 2), 1)
        lse_ref[0, 0] = jnp.where(two == 0, m0 + jnp.log(l0), m1 + jnp.log(l1))
        if hosted:
            finish()

    kv = lambda off: pl.BlockSpec((1, S, LANES), lambda b, hp, i: (b, 0, off + hp))
    outs = pl.pallas_call(
        body, name=name,
        out_shape=[jax.ShapeDtypeStruct((B, S, H * HEAD_DIM), BF16), jax.ShapeDtypeStruct((B, HP, S, 2), F32)]
        + (hosted.out_shape if hosted else []),
        grid=(B, HP, nq),
        in_specs=[pl.BlockSpec((1, tq, LANES), lambda b, hp, i: (b, i, hp)), kv(HP), kv(2 * HP),
                  pl.BlockSpec((1, 1, 2, nq, tq), lambda b, hp, i: (b, hp, 0, 0, 0))] + [HBM] * nh,
        out_specs=[pl.BlockSpec((1, tq, LANES), lambda b, hp, i: (b, i, hp)),
                   pl.BlockSpec((1, 1, tq, 2), lambda b, hp, i: (b, hp, i, 0))] + [HBM] * nh,
        scratch_shapes=hosted.scratch if hosted else [],
        compiler_params=_cparams("arbitrary", "arbitrary", "arbitrary"),
    )(proj, proj, proj, frow, *(hosted.arrays if hosted else []))
    return outs[0], outs[1], outs[2:]


def _attn_bwd(proj, ya, dya, lse, frow, *, name, H, tq, hosted=None):
    B, S, _ = proj.shape
    HP = H // 2
    nq = S // tq
    AW = H * HEAD_DIM
    scale = HEAD_DIM ** -0.5
    nh = hosted.n if hosted else 0

    def body(*refs):
        q_ref, k_ref, v_ref, o_ref, do_ref, lse_ref, fk_ref = refs[:7]
        dq_ref, dk_ref, dv_ref, dfk_ref, dfq_ref = refs[7 + nh:12 + nh]
        (q2_ref, do2_ref, lse2_ref, delta2_ref, dq2_acc, dfq2_acc, dk_acc, dv_acc,
         dfk_acc) = refs[12 + 2 * nh:21 + 2 * nh]
        if hosted:
            b, hp = pl.program_id(0), pl.program_id(1)
            start, finish = hosted.run((b == 0) & (hp == 0), (b == B - 1) & (hp == HP - 1),
                                       refs[7:7 + nh], refs[12 + nh:12 + 2 * nh], refs[21 + 2 * nh:])
            start()
        masks = _head_masks()
        rr = lax.broadcasted_iota(jnp.int32, (tq, tq), 0)
        cc = lax.broadcasted_iota(jnp.int32, (tq, tq), 1)
        causal = rr >= cc
        sc = jnp.asarray(scale, BF16)

        def stage(i, c):
            rows = pl.ds(pl.multiple_of(i * tq, tq), tq)
            dov = do_ref[0, rows, :]
            q2_ref[i] = _stack_heads(q_ref[0, rows, :], masks) * sc
            do2_ref[i] = _stack_heads(dov, masks)
            prod = dov.astype(F32) * o_ref[0, rows, :].astype(F32)
            delta2_ref[i] = jnp.concatenate(
                [jnp.sum(jnp.where(masks[h], prod, 0.0), axis=-1, keepdims=True) for h in range(2)], axis=0)
            lv = lse_ref[0, 0, rows, :]
            lse2_ref[i] = jnp.concatenate([lv[:, 0:1], lv[:, 1:2]], axis=0)
            return c

        lax.fori_loop(0, nq, stage, 0)
        dq2_acc[...] = jnp.zeros_like(dq2_acc)
        dfq2_acc[...] = jnp.zeros_like(dfq2_acc)

        def kv_block(j, carry):
            rows_j = pl.ds(pl.multiple_of(j * tq, tq), tq)
            kj = k_ref[0, rows_j, :]
            vj = v_ref[0, rows_j, :]
            ks = kj * sc
            dk_acc[...] = jnp.zeros_like(dk_acc)
            dv_acc[...] = jnp.zeros_like(dv_acc)
            dfk_acc[...] = jnp.zeros_like(dfk_acc)

            def logits(i):
                return (lax.dot_general(q2_ref[i], kj, NT, preferred_element_type=F32),
                        lax.dot_general(do2_ref[i], vj, NT, preferred_element_type=F32))

            def probs(i, s2, dp2, masked):
                lse2 = lse2_ref[i]
                delta2 = delta2_ref[i]
                ps, dss = [], []
                for h in range(2):
                    half = slice(h * tq, (h + 1) * tq)
                    p = jnp.exp(s2[half] - fk_ref[0, 0, h, pl.ds(j, 1), :] - lse2[half])
                    if masked:
                        p = jnp.where(causal, p, 0.0)
                    ds = p * (dp2[half] - delta2[half])
                    dfk_acc[h:h + 1, :] -= jnp.sum(ds, axis=0, keepdims=True)
                    dfq2_acc[i, half, :] += jnp.sum(ds, axis=1, keepdims=True)
                    ps.append(p.astype(BF16))
                    dss.append(ds.astype(BF16))
                return jnp.concatenate(ps, axis=0), jnp.concatenate(dss, axis=0)

            def grads(i, p2, ds2):
                dv_acc[...] += lax.dot_general(p2, do2_ref[i], TN, preferred_element_type=F32)
                dk_acc[...] += lax.dot_general(ds2, q2_ref[i], TN, preferred_element_type=F32)
                dq2_acc[i] += jnp.dot(ds2, ks, preferred_element_type=F32)

            grads(j, *probs(j, *logits(j), True))

            def rest(i, c):
                grads(i, *probs(i, *logits(i), False))
                return c

            lax.fori_loop(j + 1, nq, rest, 0)
            dk_ref[0, rows_j, :] = dk_acc[...].astype(BF16)
            dv_ref[0, rows_j, :] = dv_acc[...].astype(BF16)
            for h in range(2):
                dfk_ref[0, 0, h, pl.ds(j, 1), :] = dfk_acc[h:h + 1, :]
            return carry

        lax.fori_loop(0, nq, kv_block, 0)
        two = lax.broadcasted_iota(jnp.int32, (1, 2), 1)

        def finish_block(i, c):
            rows = pl.ds(pl.multiple_of(i * tq, tq), tq)
            dq2 = dq2_acc[i]
            dq_ref[0, rows, :] = jnp.where(masks[0], dq2[:tq], dq2[tq:]).astype(BF16)
            dfq2 = dfq2_acc[i]
            dfq_ref[0, 0, rows, :] = jnp.where(two == 0, dfq2[:tq], dfq2[tq:])
            return c

        lax.fori_loop(0, nq, finish_block, 0)
        if hosted:
            finish()

    col = lambda off: pl.BlockSpec((1, S, LANES), lambda b, hp: (b, 0, off + hp))
    stat = pl.BlockSpec((1, 1, S, 2), lambda b, hp: (b, hp, 0, 0))
    rowf = pl.BlockSpec((1, 1, 2, nq, tq), lambda b, hp: (b, hp, 0, 0, 0))
    grad = jax.ShapeDtypeStruct((B, S, AW), BF16)
    outs = pl.pallas_call(
        body, name=name,
        out_shape=[grad, grad, grad, jax.ShapeDtypeStruct((B, HP, 2, nq, tq), F32),
                   jax.ShapeDtypeStruct((B, HP, S, 2), F32)] + (hosted.out_shape if hosted else []),
        grid=(B, HP),
        in_specs=[col(0), col(HP), col(2 * HP), col(0), col(0), stat, rowf] + [HBM] * nh,
        out_specs=[col(0), col(0), col(0), rowf, stat] + [HBM] * nh,
        scratch_shapes=[pltpu.VMEM((nq, 2 * tq, LANES), BF16), pltpu.VMEM((nq, 2 * tq, LANES), BF16),
                        pltpu.VMEM((nq, 2 * tq, 1), F32), pltpu.VMEM((nq, 2 * tq, 1), F32),
                        pltpu.VMEM((nq, 2 * tq, LANES), F32), pltpu.VMEM((nq, 2 * tq, 1), F32),
                        pltpu.VMEM((tq, LANES), F32), pltpu.VMEM((tq, LANES), F32), pltpu.VMEM((2, tq), F32)]
        + (hosted.scratch if hosted else []),
        compiler_params=_cparams("arbitrary", "arbitrary"),
    )(proj, proj, proj, ya, dya, lse, frow, *(hosted.arrays if hosted else []))
    return outs[:5], outs[5:]


def _cmul(ar, ai, br, bi):
    return ar * br - ai * bi, ar * bi + ai * br


def _ssm_states(u_ref, bm, lam_ref, pw_ref, lamT_ref, hr_ref, hi_ref, inr_ref, ini_ref, T, NC, SP):
    lr, li = lam_ref[0, 0:1, :], lam_ref[0, 1:2, :]
    bu = jnp.dot(u_ref[0, 0], bm, preferred_element_type=F32)
    hr_ref[0] = bu[:, :SP]
    hi_ref[0] = bu[:, SP:]

    def step(t, c):
        bu = jnp.dot(u_ref[0, t], bm, preferred_element_type=F32)
        pr, pi = _cmul(hr_ref[t - 1], hi_ref[t - 1], lr, li)
        hr_ref[t] = pr + bu[:, :SP]
        hi_ref[t] = pi + bu[:, SP:]
        return c

    lax.fori_loop(1, T, step, 0, unroll=2)

    tr, ti = lamT_ref[0, 0:1, :], lamT_ref[0, 1:2, :]
    inr_ref[0:1, :] = jnp.zeros((1, SP), F32)
    ini_ref[0:1, :] = jnp.zeros((1, SP), F32)

    def chunk(n, c):
        prev = pl.ds(n - 1, 1)
        pr, pi = _cmul(inr_ref[prev, :], ini_ref[prev, :], tr, ti)
        inr_ref[pl.ds(n, 1), :] = pr + hr_ref[T - 1, prev, :]
        ini_ref[pl.ds(n, 1), :] = pi + hi_ref[T - 1, prev, :]
        return c

    lax.fori_loop(1, NC, chunk, 0)


def _ssm_entry_term(t, pw_ref, inr_ref, ini_ref):
    return _cmul(inr_ref[...], ini_ref[...], pw_ref[0, 0, pl.ds(t, 1), :], pw_ref[0, 1, pl.ds(t, 1), :])


def _ssm_fwd(u_tm, bmat, cmat, lam, pw, lamT, dskip, *, name):
    B, T, NC, W = u_tm.shape
    NS = W // LANES
    SP = bmat.shape[2] // 2

    def body(u_ref, b_ref, c_ref, lam_ref, pw_ref, lamT_ref, d_ref, y_ref, hr_ref, hi_ref, inr_ref, ini_ref):
        _ssm_states(u_ref, b_ref[0], lam_ref, pw_ref, lamT_ref, hr_ref, hi_ref, inr_ref, ini_ref, T, NC, SP)
        cm = c_ref[0]
        dv = d_ref[...]

        def out(t, c):
            cr, ci = _ssm_entry_term(t, pw_ref, inr_ref, ini_ref)
            hcat = jnp.concatenate([hr_ref[t] + cr, hi_ref[t] + ci], axis=1).astype(BF16)
            y_ref[0, t] = jnp.dot(hcat, cm, preferred_element_type=F32) + dv * u_ref[0, t].astype(F32)
            return c

        lax.fori_loop(0, T, out, 0, unroll=2)

    slab = lambda *shape: pl.BlockSpec((1,) + shape, lambda b, s: (s,) + (0,) * len(shape))
    tok = pl.BlockSpec((1, T, NC, LANES), lambda b, s: (b, 0, 0, s))
    return pl.pallas_call(
        body, name=name,
        out_shape=jax.ShapeDtypeStruct((B, T, NC, W), F32),
        grid=(B, NS),
        in_specs=[tok, slab(LANES, 2 * SP), slab(2 * SP, LANES), slab(2, SP), slab(2, T, SP), slab(2, SP),
                  pl.BlockSpec((1, LANES), lambda b, s: (0, s))],
        out_specs=tok,
        scratch_shapes=[pltpu.VMEM((T, NC, SP), F32), pltpu.VMEM((T, NC, SP), F32),
                        pltpu.VMEM((NC, SP), F32), pltpu.VMEM((NC, SP), F32)],
        compiler_params=_cparams("parallel", "parallel"),
    )(u_tm, bmat, cmat, lam, pw, lamT, dskip)


def _ssm_bwd(u_tm, dy_tm, bmat, bmat_t, cmat_t, lam, pw, lamT, dskip, *, name):
    B, T, NC, W = u_tm.shape
    NS = W // LANES
    SP = bmat.shape[2] // 2

    def body(u_ref, dy_ref, b_ref, bt_ref, ct_ref, lam_ref, pw_ref, lamT_ref, d_ref,
             du_ref, gb_ref, gc_ref, glam_ref, gd_ref,
             hr_ref, hi_ref, ar_ref, ai_ref, inr_ref, ini_ref, anr_ref, ani_ref):
        b = pl.program_id(1)
        _ssm_states(u_ref, b_ref[0], lam_ref, pw_ref, lamT_ref, hr_ref, hi_ref, inr_ref, ini_ref, T, NC, SP)
        lr, li = lam_ref[0, 0:1, :], lam_ref[0, 1:2, :]
        ct = ct_ref[0]
        bt = bt_ref[0]
        dv = d_ref[...]

        gh = jnp.dot(dy_ref[0, T - 1].astype(BF16), ct, preferred_element_type=F32)
        ar_ref[T - 1] = gh[:, :SP]
        ai_ref[T - 1] = gh[:, SP:]

        def back(k, c):
            t = T - 2 - k
            gh = jnp.dot(dy_ref[0, t].astype(BF16), ct, preferred_element_type=F32)
            pr, pi = _cmul(ar_ref[t + 1], ai_ref[t + 1], lr, -li)
            ar_ref[t] = pr + gh[:, :SP]
            ai_ref[t] = pi + gh[:, SP:]
            return c

        lax.fori_loop(0, T - 1, back, 0, unroll=2)

        tr, ti = lamT_ref[0, 0:1, :], lamT_ref[0, 1:2, :]
        anr_ref[NC - 1:NC, :] = jnp.zeros((1, SP), F32)
        ani_ref[NC - 1:NC, :] = jnp.zeros((1, SP), F32)

        def chunk(k, c):
            n = NC - 2 - k
            nxt = pl.ds(n + 1, 1)
            pr, pi = _cmul(anr_ref[nxt, :], ani_ref[nxt, :], tr, -ti)
            anr_ref[pl.ds(n, 1), :] = pr + ar_ref[0, nxt, :]
            ani_ref[pl.ds(n, 1), :] = pi + ai_ref[0, nxt, :]
            return c

        lax.fori_loop(0, NC - 1, chunk, 0)

        @pl.when(b == 0)
        def _():
            gb_ref[...] = jnp.zeros_like(gb_ref)
            gc_ref[...] = jnp.zeros_like(gc_ref)
            glam_ref[...] = jnp.zeros_like(glam_ref)
            gd_ref[...] = jnp.zeros_like(gd_ref)

        def final(t, hpr, hpi, gl):
            back_pow = pl.ds(T - 1 - t, 1)
            cr, ci = _cmul(anr_ref[...], ani_ref[...], pw_ref[0, 0, back_pow, :], -pw_ref[0, 1, back_pow, :])
            a_r = ar_ref[t] + cr
            a_i = ai_ref[t] + ci
            gl = (gl[0] + jnp.sum(a_r * hpr + a_i * hpi, axis=0, keepdims=True),
                  gl[1] + jnp.sum(a_i * hpr - a_r * hpi, axis=0, keepdims=True))
            acat = jnp.concatenate([a_r, a_i], axis=1).astype(BF16)
            ut = u_ref[0, t]
            dyt = dy_ref[0, t]
            du_ref[0, t] = (jnp.dot(acat, bt, preferred_element_type=F32) + dv * dyt).astype(BF16)
            gb_ref[0] += lax.dot_general(acat, ut, TN, preferred_element_type=F32)
            er, ei = _ssm_entry_term(t, pw_ref, inr_ref, ini_ref)
            h_r = hr_ref[t] + er
            h_i = hi_ref[t] + ei
            hr_ref[t] = h_r
            hi_ref[t] = h_i
            hcat = jnp.concatenate([h_r, h_i], axis=1).astype(BF16)
            gc_ref[0] += lax.dot_general(dyt.astype(BF16), hcat, TN, preferred_element_type=F32)
            gd_ref[0] += jnp.sum(dyt * ut.astype(F32), axis=0, keepdims=True)
            return gl

        zero = jnp.zeros((1, SP), F32)
        gl = final(0, inr_ref[...], ini_ref[...], (zero, zero))
        gl = lax.fori_loop(1, T, lambda t, gl: final(t, hr_ref[t - 1], hi_ref[t - 1], gl), gl)
        glam_ref[0, 0:1, :] += gl[0]
        glam_ref[0, 1:2, :] += gl[1]

    slab = lambda *shape: pl.BlockSpec((1,) + shape, lambda s, b: (s,) + (0,) * len(shape))
    tok = pl.BlockSpec((1, T, NC, LANES), lambda s, b: (b, 0, 0, s))
    big = pltpu.VMEM((T, NC, SP), F32)
    small = pltpu.VMEM((NC, SP), F32)
    return pl.pallas_call(
        body, name=name,
        out_shape=(jax.ShapeDtypeStruct((B, T, NC, W), BF16),
                   jax.ShapeDtypeStruct((NS, 2 * SP, LANES), F32), jax.ShapeDtypeStruct((NS, LANES, 2 * SP), F32),
                   jax.ShapeDtypeStruct((NS, 2, SP), F32), jax.ShapeDtypeStruct((NS, 1, LANES), F32)),
        grid=(NS, B),
        in_specs=[tok, tok, slab(LANES, 2 * SP), slab(2 * SP, LANES), slab(LANES, 2 * SP), slab(2, SP),
                  slab(2, T, SP), slab(2, SP), pl.BlockSpec((1, LANES), lambda s, b: (0, s))],
        out_specs=(tok, slab(2 * SP, LANES), slab(LANES, 2 * SP), slab(2, SP), slab(1, LANES)),
        scratch_shapes=[big, big, big, big, small, small, small, small],
        compiler_params=_cparams("parallel", "arbitrary"),
    )(u_tm, dy_tm, bmat, bmat_t, cmat_t, lam, pw, lamT, dskip)


def _glu_fwd(ys, w, b, *, name, tr=512):
    n, wd = ys.shape
    tr = _tile(n, tr, 8)

    def body(y_ref, w_ref, b_ref, o_ref):
        yb = _gelu(y_ref[...])
        z = jnp.dot(yb.astype(BF16), w_ref[...], preferred_element_type=F32) + b_ref[...]
        o_ref[...] = (yb * _sigmoid(z)).astype(BF16)

    row = pl.BlockSpec((tr, wd), lambda i: (i, 0))
    return pl.pallas_call(
        body, name=name, out_shape=jax.ShapeDtypeStruct((n, wd), BF16), grid=(n // tr,),
        in_specs=[row, pl.BlockSpec((wd, wd), lambda i: (0, 0)), pl.BlockSpec((1, wd), lambda i: (0, 0))],
        out_specs=row, compiler_params=_cparams("parallel"),
    )(ys, w, b.reshape(1, wd))


def _glu_bwd(ys, dyb2, w, w_t, b, *, name, tr=512):
    n, wd = ys.shape
    tr = _tile(n, tr, 8)

    def body(y_ref, d_ref, w_ref, wt_ref, b_ref, dys_ref, dz_ref, yb_ref, db_ref):
        i = pl.program_id(0)
        yv = y_ref[...]
        yb = _gelu(yv)
        ybb = yb.astype(BF16)
        sg = _sigmoid(jnp.dot(ybb, w_ref[...], preferred_element_type=F32) + b_ref[...])
        dv = d_ref[...].astype(F32)
        dz = dv * yb * sg * (1.0 - sg)
        dzb = dz.astype(BF16)
        dyb = dv * sg + jnp.dot(dzb, wt_ref[...], preferred_element_type=F32)
        dys_ref[...] = dyb * _gelu_grad(yv)
        dz_ref[...] = dzb
        yb_ref[...] = ybb
        part = jnp.sum(dz, axis=0, keepdims=True)

        @pl.when(i == 0)
        def _():
            db_ref[...] = part

        @pl.when(i > 0)
        def _():
            db_ref[...] += part

    row = pl.BlockSpec((tr, wd), lambda i: (i, 0))
    mat = pl.BlockSpec((wd, wd), lambda i: (0, 0))
    vec = pl.BlockSpec((1, wd), lambda i: (0, 0))
    return pl.pallas_call(
        body, name=name,
        out_shape=(jax.ShapeDtypeStruct((n, wd), F32), jax.ShapeDtypeStruct((n, wd), BF16),
                   jax.ShapeDtypeStruct((n, wd), BF16), jax.ShapeDtypeStruct((1, wd), F32)),
        grid=(n // tr,), in_specs=[row, row, mat, mat, vec], out_specs=(row, row, row, vec),
        compiler_params=_cparams("arbitrary"),
    )(ys, dyb2, w, w_t, b.reshape(1, wd))


def _merge_fwd(ya, yb2, wa, wb, proj, gate_blk, *, name, tr=512):
    n, aw = ya.shape
    d = wa.shape[1]
    tr = _tile(n, tr, 8)

    def body(ya_ref, yb_ref, wa_ref, wb_ref, ga_ref, gb_ref, mix_ref, pa_ref, pb_ref):
        pa = jnp.dot(ya_ref[...], wa_ref[...], preferred_element_type=F32)
        pb = jnp.dot(yb_ref[...], wb_ref[...], preferred_element_type=F32)
        mix = _sigmoid(ga_ref[...].astype(F32)) * pa + _sigmoid(gb_ref[...].astype(F32)) * pb
        mix_ref[...] = mix.astype(BF16)
        pa_ref[...] = pa.astype(BF16)
        pb_ref[...] = pb.astype(BF16)

    row = lambda wdt: pl.BlockSpec((tr, wdt), lambda i: (i, 0))
    full = lambda r, c: pl.BlockSpec((r, c), lambda i: (0, 0))
    out = jax.ShapeDtypeStruct((n, d), BF16)
    return pl.pallas_call(
        body, name=name, out_shape=(out, out, out), grid=(n // tr,),
        in_specs=[row(aw), row(yb2.shape[1]), full(*wa.shape), full(*wb.shape),
                  pl.BlockSpec((tr, d), lambda i: (i, gate_blk)), pl.BlockSpec((tr, d), lambda i: (i, gate_blk + 1))],
        out_specs=(row(d), row(d), row(d)), compiler_params=_cparams("parallel"),
    )(ya, yb2, wa, wb, proj, proj)


def _merge_bwd(dmix, proj, pa, pb, gate_blk, *, name, tr=512):
    n, d = dmix.shape
    tr = _tile(n, tr, 8)

    def body(dm_ref, ga_ref, gb_ref, pa_ref, pb_ref, dpa_ref, dpb_ref, dga_ref, dgb_ref):
        dm = dm_ref[...].astype(F32)
        sa = _sigmoid(ga_ref[...].astype(F32))
        sb = _sigmoid(gb_ref[...].astype(F32))
        dpa_ref[...] = (dm * sa).astype(BF16)
        dpb_ref[...] = (dm * sb).astype(BF16)
        dga_ref[...] = (dm * pa_ref[...].astype(F32) * sa * (1.0 - sa)).astype(BF16)
        dgb_ref[...] = (dm * pb_ref[...].astype(F32) * sb * (1.0 - sb)).astype(BF16)

    row = pl.BlockSpec((tr, d), lambda i: (i, 0))
    out = jax.ShapeDtypeStruct((n, d), BF16)
    return pl.pallas_call(
        body, name=name, out_shape=(out, out, out, out), grid=(n // tr,),
        in_specs=[row, pl.BlockSpec((tr, d), lambda i: (i, gate_blk)), pl.BlockSpec((tr, d), lambda i: (i, gate_blk + 1)),
                  row, row],
        out_specs=(row, row, row, row), compiler_params=_cparams("parallel"),
    )(dmix, proj, proj, pa, pb)


def _outproj_fwd(mixed, w, x0, g, *, name, tr=512):
    n, d = x0.shape
    tr = _tile(n, tr, 8)

    def body(m_ref, w_ref, x_ref, g_ref, x1_ref, h_ref, r_ref):
        x1 = x_ref[...] + jnp.dot(m_ref[...], w_ref[...], preferred_element_type=F32)
        r = lax.rsqrt(jnp.mean(x1 * x1, axis=-1, keepdims=True) + RMS_EPS)
        x1_ref[...] = x1
        h_ref[...] = (x1 * r * g_ref[...]).astype(BF16)
        r_ref[...] = r

    row = pl.BlockSpec((tr, d), lambda i: (i, 0))
    return pl.pallas_call(
        body, name=name,
        out_shape=(jax.ShapeDtypeStruct((n, d), F32), jax.ShapeDtypeStruct((n, d), BF16),
                   jax.ShapeDtypeStruct((n, 1), F32)),
        grid=(n // tr,),
        in_specs=[row, pl.BlockSpec((d, d), lambda i: (0, 0)), row, pl.BlockSpec((1, d), lambda i: (0, 0))],
        out_specs=(row, row, pl.BlockSpec((tr, 1), lambda i: (i, 0))),
        compiler_params=_cparams("parallel"),
    )(mixed, w, x0, g.reshape(1, d))


def _adamw(w, g, m, v, *, name):
    shape = w.shape
    total = w.size
    if total % PACK_COLS == 0 and ((total // PACK_COLS) % 8 == 0 or total // PACK_COLS <= 512):
        rows, cols = total // PACK_COLS, PACK_COLS
    elif w.ndim >= 2:
        rows, cols = total // shape[-1], shape[-1]
    else:
        rows, cols = 1, total
    tr = _tile(rows, 512, 8)

    def body(w_ref, g_ref, m_ref, v_ref, d_ref, nm_ref, nv_ref):
        gv = g_ref[...]
        mn = ADAM_B1 * m_ref[...] + (1.0 - ADAM_B1) * gv
        vn = ADAM_B2 * v_ref[...] + (1.0 - ADAM_B2) * (gv * gv)
        m_hat = mn / (1.0 - ADAM_B1 ** ADAM_STEP)
        v_hat = vn / (1.0 - ADAM_B2 ** ADAM_STEP)
        d_ref[...] = -ADAM_LR * (m_hat / (jnp.sqrt(v_hat) + ADAM_EPS) + ADAM_WD * w_ref[...])
        nm_ref[...] = mn
        nv_ref[...] = vn

    blk = pl.BlockSpec((tr, cols), lambda i: (i, 0))
    out = jax.ShapeDtypeStruct((rows, cols), F32)
    outs = pl.pallas_call(
        body, name=name, out_shape=(out, out, out), grid=(rows // tr,),
        in_specs=[blk] * 4, out_specs=(blk, blk, blk), compiler_params=_cparams("parallel"),
    )(*[t.reshape(rows, cols) for t in (w, g, m, v)])
    return tuple(o.reshape(shape) for o in outs)


def _all_gather(blocks, *, name):
    n = len(blocks)

    def body(*refs):
        x_refs, out_refs = refs[:n], refs[n:2 * n]
        send_sems, recv_sems, local_sems = refs[2 * n:]
        x, y, c = lax.axis_index("x"), lax.axis_index("y"), lax.axis_index("c")
        me, sibling = (x, y, c), (x, y, 1 - c)
        chips = [(1 - x, y), (x, 1 - y), (1 - x, 1 - y)]

        def slot(a, px, py, pc):
            return out_refs[a].at[4 * px + 2 * py + pc]

        def copy(a, k, block, to, src=None):
            return pltpu.make_async_remote_copy(
                src_ref=slot(a, *block) if src is None else src, dst_ref=slot(a, *block),
                send_sem=send_sems.at[7 * a + k], recv_sem=recv_sems.at[7 * a + k], device_id=to,
                device_id_type=MESH)

        started = []
        for a in range(n):
            mine = pltpu.make_async_copy(x_refs[a], slot(a, *me), local_sems.at[a])
            mine.start()
            started.append(mine)
        sends = []
        for a in range(n):
            first = [copy(a, 0, me, sibling, src=x_refs[a])]
            first += [copy(a, 1 + j, me, (*chip, c), src=x_refs[a]) for j, chip in enumerate(chips)]
            for cp in first:
                cp.start()
            sends += first
        for a in range(n):
            for j, chip in enumerate(chips):
                copy(a, 1 + j, (*chip, c), me).wait_recv()
                onward = copy(a, 4 + j, (*chip, c), sibling)
                onward.start()
                sends.append(onward)
        for a in range(n):
            copy(a, 0, sibling, me).wait_recv()
            for j, chip in enumerate(chips):
                copy(a, 4 + j, (*chip, 1 - c), me).wait_recv()
        for cp in sends:
            cp.wait_send()
        for mine in started:
            mine.wait()

    return pl.pallas_call(
        body, name=name, out_shape=[jax.ShapeDtypeStruct((N_DEV,) + b.shape, b.dtype) for b in blocks],
        in_specs=[HBM] * n, out_specs=[HBM] * n,
        scratch_shapes=[pltpu.SemaphoreType.DMA((7 * n,)), pltpu.SemaphoreType.DMA((7 * n,)),
                        pltpu.SemaphoreType.DMA((n,))],
    )(*blocks)


def _sum8(blocks, *, name, tr=SUM_ROWS):
    _, R, C = blocks.shape
    tr = _tile(R, tr, 16)

    def body(x_ref, o_ref):
        acc = x_ref[0].astype(F32)
        for i in range(1, N_DEV):
            acc = acc + x_ref[i].astype(F32)
        o_ref[...] = acc

    return pl.pallas_call(
        body, name=name, out_shape=jax.ShapeDtypeStruct((R, C), F32), grid=(R // tr,),
        in_specs=[pl.BlockSpec((N_DEV, tr, C), lambda i: (0, i, 0))],
        out_specs=pl.BlockSpec((tr, C), lambda i: (i, 0)), compiler_params=_cparams("parallel"),
    )(blocks)


def _pack(parts):
    flat = jnp.concatenate([p.astype(F32).reshape(-1) for p in parts])
    unit = 8 * PACK_COLS
    padded = -(-flat.size // unit) * unit
    return jnp.pad(flat, (0, padded - flat.size)).reshape(padded // PACK_COLS, PACK_COLS)


def _unpack(buf, like):
    flat, out, off = buf.reshape(-1), [], 0
    for p in like:
        out.append(flat[off:off + p.size].reshape(p.shape))
        off += p.size
    return out


def _ssm_discretise(lre, lim, logdt, bre, bim):
    lam = lax.complex(lre, lim)
    dt = jnp.exp(logdt)[:, None]
    lam_bar = jnp.exp(lam * dt)
    b_bar = ((lam_bar - 1.0) / lam)[:, :, None] * lax.complex(bre, bim)
    return lam_bar.real, lam_bar.imag, b_bar.real, b_bar.imag


def _block_diag(a, rows_first):
    ns, g, r, c = a.shape
    eye = jnp.eye(g, dtype=a.dtype)
    return jnp.einsum("sgrc,gh->sgrhc", a, eye).reshape(ns, g * r, g * c)


def _diag_blocks(m, r, c):
    ns = m.shape[0]
    g = SLAB_GROUPS
    return jnp.einsum("sgrhc,gh->sgrc", m.reshape(ns, g, r, g, c), jnp.eye(g, dtype=m.dtype))


def _to_tm(a, T):
    b, s, w = a.shape
    return a.reshape(b, s // T, T, w).transpose(0, 2, 1, 3)


def _from_tm(a):
    b, t, nc, w = a.shape
    return a.transpose(0, 2, 1, 3).reshape(b, nc * t, w)


WEIGHTS = ["norm_mix", "w_in", "b_forget", "ssm_lambda_re", "ssm_lambda_im", "ssm_log_dt", "ssm_b_re", "ssm_b_im",
           "ssm_c_re", "ssm_c_im", "ssm_d", "w_glu", "b_glu", "w_branch_a", "w_branch_b", "w_out", "norm_mlp",
           "w_mlp_up", "w_mlp_down", "norm_final"]
SHARDED = {"w_in": 2, "w_glu": 1, "w_branch_a": 2, "w_branch_b": 2, "w_out": 1, "w_mlp_up": 2, "w_mlp_down": 1}


REST = [n for n in SHARDED if n != "w_in"]


def _whole(n, seg):
    ax = SHARDED[n] - 1
    shp = seg.shape[1:]
    return jnp.moveaxis(seg, 0, ax).reshape(shp[:ax] + (N_DEV * shp[ax],) + shp[ax + 1:])


def _blocks(n, g):
    ax = SHARDED[n] - 1
    shp = g.shape
    return jnp.moveaxis(g.reshape(shp[:ax] + (N_DEV, shp[ax] // N_DEV) + shp[ax + 1:]), ax, 0)


def _sum_blocks(n, got):
    return _sum8(got.reshape(N_DEV, -1, got.shape[-1]), name="sum_grads_" + n).reshape(got.shape[1:])


def kernel(x, norm_mix, w_in, b_forget, ssm_lambda_re, ssm_lambda_im, ssm_log_dt, ssm_b_re, ssm_b_im, ssm_c_re, ssm_c_im, ssm_d, w_glu, b_glu, w_branch_a, w_branch_b, w_out, norm_mlp, w_mlp_up, w_mlp_down, norm_final, loss_target, m_norm_mix, m_w_in, m_b_forget, m_ssm_lambda_re, m_ssm_lambda_im, m_ssm_log_dt, m_ssm_b_re, m_ssm_b_im, m_ssm_c_re, m_ssm_c_im, m_ssm_d, m_w_glu, m_b_glu, m_w_branch_a, m_w_branch_b, m_w_out, m_norm_mlp, m_w_mlp_up, m_w_mlp_down, m_norm_final, v_norm_mix, v_w_in, v_b_forget, v_ssm_lambda_re, v_ssm_lambda_im, v_ssm_log_dt, v_ssm_b_re, v_ssm_b_im, v_ssm_c_re, v_ssm_c_im, v_ssm_d, v_w_glu, v_b_glu, v_w_branch_a, v_w_branch_b, v_w_out, v_norm_mlp, v_w_mlp_up, v_w_mlp_down, v_norm_final):
    args = dict(locals())
    w = {n: args[n] for n in WEIGHTS}
    Bl, S, D = x.shape
    L, H = b_forget.shape
    G, P, C = ssm_b_re.shape[1:]
    AW, W, HP = H * HEAD_DIM, G * C, H // 2
    N = Bl * S
    T = SSM_CHUNK
    NS = G // SLAB_GROUPS
    SP = SLAB_GROUPS * P
    tq = min(ATTN_BLOCK, S)
    nq = S // tq
    u_off = 3 * AW
    gate_blk = (u_off + W) // D
    assert (u_off + W) % D == 0 and W % LANES == 0 and AW % LANES == 0 and S % T == 0

    shard = {n: w[n].astype(BF16) for n in SHARDED}
    weights = [dict() for _ in range(L)]
    weights[0]["w_in"] = _whole("w_in", _all_gather([shard["w_in"][0]], name="gather_first")[0])
    tr_ = lambda a: jnp.swapaxes(a, 1, 2)

    ssm = []
    for l in range(L):
        disc, disc_vjp = jax.vjp(_ssm_discretise, ssm_lambda_re[l], ssm_lambda_im[l], ssm_log_dt[l],
                                 ssm_b_re[l], ssm_b_im[l])
        lbr, lbi, bbr, bbi = disc
        z = lax.complex(ssm_lambda_re[l], ssm_lambda_im[l]) * jnp.exp(ssm_log_dt[l])[:, None]
        powers = jnp.exp(z[None] * jnp.arange(1, T + 1, dtype=F32)[:, None, None])
        slabs = lambda a: a.reshape(NS, SP)
        lam = jnp.stack([slabs(lbr), slabs(lbi)], axis=1)
        lam_t = jnp.stack([slabs(powers[T - 1].real), slabs(powers[T - 1].imag)], axis=1)
        pw = jnp.stack([powers.real.reshape(T, NS, SP), powers.imag.reshape(T, NS, SP)], axis=0).transpose(2, 0, 1, 3)
        to_rows = lambda a: jnp.swapaxes(a.reshape(NS, SLAB_GROUPS, P, C), 2, 3)
        bmat = jnp.concatenate([_block_diag(to_rows(bbr), True), _block_diag(to_rows(bbi), True)], axis=2)
        cre = ssm_c_re[l].reshape(NS, SLAB_GROUPS, C, P)
        cim = ssm_c_im[l].reshape(NS, SLAB_GROUPS, C, P)
        cmat_t = jnp.concatenate([_block_diag(cre, True), -_block_diag(cim, True)], axis=2)
        ssm.append(dict(vjp=disc_vjp, lam=lam, lam_t=lam_t, pw=pw, bmat=bmat.astype(BF16),
                        bmat_t=tr_(bmat).astype(BF16), cmat=tr_(cmat_t).astype(BF16), cmat_t=cmat_t.astype(BF16),
                        d=ssm_d[l].reshape(1, W)))

    xcur = x.reshape(N, D)
    saved = []
    for l in range(L):
        s_, wl = ssm[l], weights[l]
        win = wl["w_in"]
        wl["wcat"] = jnp.concatenate([win[:, :3 * AW], win[:, 3 * AW + H:]], axis=1)
        wl["wf"] = jnp.pad(win[:, 3 * AW:3 * AW + H], ((0, 0), (0, LANES - H)))
        h, r0 = _rmsnorm_fwd(xcur, norm_mix[l], name="norm_mix_fwd")
        proj = _mm(h, wl["wcat"], name="in_proj", tn=WIDE_N)
        fl = _mm(h, wl["wf"], name="forget_proj", out_dtype=F32)
        ft = fl[:, :H].reshape(Bl, S, H).transpose(0, 2, 1)
        F = _fox_gate_fwd(ft, b_forget[l], name="forget_gate_fwd")
        frow = F.reshape(Bl, HP, 2, nq, tq)
        proj3 = proj.reshape(Bl, S, -1)
        coming = [shard[n][l] for n in REST] + ([shard["w_in"][l + 1]] if l + 1 < L else [])
        ya, lse, got = _attn_fwd(proj3, frow, name="attn_fwd" if l + 1 < L else "attn_fwd_last", H=H, tq=tq,
                                 hosted=_Hosted(gather=coming))
        for n, seg in zip(REST, got):
            wl[n] = _whole(n, seg)
        if l + 1 < L:
            weights[l + 1]["w_in"] = _whole("w_in", got[-1])
        u_tm = _to_tm(proj3[:, :, u_off:u_off + W], T)
        ys = _from_tm(_ssm_fwd(u_tm, s_["bmat"], s_["cmat"], s_["lam"], s_["pw"], s_["lam_t"], s_["d"],
                               name="ssm_fwd")).reshape(N, W)
        yb2 = _glu_fwd(ys, wl["w_glu"], b_glu[l], name="glu_fwd")
        ya2 = ya.reshape(N, AW)
        mixed, pa, pb = _merge_fwd(ya2, yb2, wl["w_branch_a"], wl["w_branch_b"], proj, gate_blk, name="merge_fwd")
        x1, h2, r1 = _outproj_fwd(mixed, wl["w_out"], xcur, norm_mlp[l], name="out_proj")
        a = _mm(h2, wl["w_mlp_up"], name="mlp_up", tn=WIDE_N)
        x2 = _mm(a, wl["w_mlp_down"], name="mlp_down", a_fn=_relu_sq, epi=lambda acc, res: acc + res,
                 extras=(x1,), out_dtype=F32, tk=LONG_K)
        saved.append(dict(x0=xcur, h=h, r0=r0, proj=proj, ft=ft, frow=frow, ya=ya, lse=lse, u_tm=u_tm,
                          ys=ys, yb2=yb2, mixed=mixed, pa=pa, pb=pb, x1=x1, h2=h2, r1=r1, a=a))
        xcur = x2

    dx, g_final, loss_row = _loss_head(xcur, norm_final, loss_target.reshape(N, D), name="loss_head")
    loss = lax.psum(loss_row[0, 0], MESH_AXES)

    big = {n: [None] * L for n in SHARDED}
    small = {n: [None] * L for n in WEIGHTS if n not in SHARDED and n != "norm_final"}
    small_sums = [None] * L
    win_grad = small_above = None
    for l in reversed(range(L)):
        sv, s_, wl = saved[l], ssm[l], weights[l]
        a = sv["a"]
        gw = {}
        d_a = _mm(dx, wl["w_mlp_down"], name="mlp_down_dx", tb=True, tn=WIDE_N,
                  epi=lambda acc, av: acc * (2.0 * jnp.maximum(av.astype(F32), 0.0)), extras=(a,))
        gw["w_mlp_down"] = _mm(a, dx, name="mlp_down_dw", ta=True, a_fn=_relu_sq, tk=LONG_K)
        gw["w_mlp_up"] = _mm(sv["h2"], d_a, name="mlp_up_dw", ta=True, tk=LONG_K)
        dh2 = _mm(d_a, wl["w_mlp_up"], name="mlp_up_dx", tb=True, out_dtype=F32, tk=LONG_K)
        dx1, g = _rmsnorm_bwd(dh2, sv["x1"], sv["r1"], norm_mlp[l], dx, name="norm_mlp_bwd")
        small["norm_mlp"][l] = g[0]
        dmix = _mm(dx1, wl["w_out"], name="out_proj_dx", tb=True)
        gw["w_out"] = _mm(sv["mixed"], dx1, name="out_proj_dw", ta=True, tk=LONG_K)
        dpa, dpb, dga, dgb = _merge_bwd(dmix, sv["proj"], sv["pa"], sv["pb"], gate_blk, name="merge_bwd")
        ya2 = sv["ya"].reshape(N, AW)
        gw["w_branch_a"] = _mm(ya2, dpa, name="branch_a_dw", ta=True, tk=LONG_K)
        dya = _mm(dpa, wl["w_branch_a"], name="branch_a_dx", tb=True)
        gw["w_branch_b"] = _mm(sv["yb2"], dpb, name="branch_b_dw", ta=True, tk=LONG_K)
        dyb2 = _mm(dpb, wl["w_branch_b"], name="branch_b_dx", tb=True)
        dys, dz, yb, g = _glu_bwd(sv["ys"], dyb2, wl["w_glu"], wl["w_glu"].T, b_glu[l], name="glu_bwd")
        small["b_glu"][l] = g[0]
        gw["w_glu"] = _mm(yb, dz, name="glu_dw", ta=True, tk=LONG_K)

        du_tm, g_bt, g_ct, g_lam, g_d = _ssm_bwd(
            sv["u_tm"], _to_tm(dys.reshape(Bl, S, W), T), s_["bmat"], s_["bmat_t"], s_["cmat_t"], s_["lam"],
            s_["pw"], s_["lam_t"], s_["d"], name="ssm_bwd")
        du = _from_tm(du_tm).reshape(N, W)
        g_b = _diag_blocks(jnp.swapaxes(g_bt, 1, 2).reshape(NS, LANES, 2, SP).transpose(2, 0, 1, 3).reshape(
            2 * NS, LANES, SP), C, P).reshape(2, G, C, P)
        g_bbar = jnp.swapaxes(g_b, 2, 3)
        g_c = _diag_blocks(g_ct.reshape(NS, LANES, 2, SP).transpose(2, 0, 1, 3).reshape(2 * NS, LANES, SP),
                           C, P).reshape(2, G, C, P)
        g_lbar = g_lam.transpose(1, 0, 2).reshape(2, G, P)
        g_lre, g_lim, g_ldt, g_bre, g_bim = s_["vjp"]((g_lbar[0], g_lbar[1], g_bbar[0], g_bbar[1]))
        small["ssm_lambda_re"][l], small["ssm_lambda_im"][l], small["ssm_log_dt"][l] = g_lre, g_lim, g_ldt
        small["ssm_b_re"][l], small["ssm_b_im"][l] = g_bre, g_bim
        small["ssm_c_re"][l], small["ssm_c_im"][l] = g_c[0], -g_c[1]
        small["ssm_d"][l] = g_d.reshape(W)

        proj3 = sv["proj"].reshape(Bl, S, -1)
        leaving = [_blocks(n, gw[n]) for n in REST] + ([_blocks("w_in", win_grad)] if l + 1 < L else [])
        (dq, dk, dv, dfk, dfq), got = _attn_bwd(
            proj3, sv["ya"], dya.reshape(Bl, S, AW), sv["lse"], sv["frow"],
            name="attn_bwd" if l + 1 < L else "attn_bwd_top", H=H, tq=tq,
            hosted=_Hosted(gather=[small_above] if l + 1 < L else [], exchange=leaving))
        if l + 1 < L:
            small_sums[l + 1] = _sum8(got[0], name="sum_small_grads")
            big["w_in"][l + 1] = _sum_blocks("w_in", got[-1])
            got = got[1:]
        for n, blocks in zip(REST, got):
            big[n][l] = _sum_blocks(n, blocks)
        dF = dfk.reshape(Bl, H, S) + dfq.transpose(0, 1, 3, 2).reshape(Bl, H, S)
        dft, g = _fox_gate_bwd(dF, sv["ft"], b_forget[l], name="forget_gate_bwd")
        small["b_forget"][l] = g[:, 0]
        dfl = jnp.pad(dft.transpose(0, 2, 1).reshape(N, H), ((0, 0), (0, LANES - H))).astype(BF16)
        dproj = jnp.concatenate([dq.reshape(N, AW), dk.reshape(N, AW), dv.reshape(N, AW), du, dga, dgb, dfl], axis=1)
        gcat = _mm(sv["h"], dproj, name="in_proj_dw", ta=True, tn=1408, tk=LONG_K)
        ncat = wl["wcat"].shape[1]
        win_grad = jnp.concatenate([gcat[:, :3 * AW], gcat[:, ncat:ncat + H], gcat[:, 3 * AW:ncat]], axis=1)
        wfull = jnp.concatenate([wl["wcat"], wl["wf"]], axis=1)
        if l > 0:
            dh = _mm(dproj, wfull, name="in_proj_dx", tb=True, out_dtype=F32, tk=1408)
        else:
            dh, got = _mm(dproj, wfull, name="in_proj_dx_bottom", tb=True, out_dtype=F32, tk=1408,
                          hosted=_Hosted(exchange=[_blocks("w_in", win_grad)]))
            big["w_in"][0] = _sum_blocks("w_in", got[0])
        dx, g = _rmsnorm_bwd(dh, sv["x0"], sv["r0"], norm_mix[l], dx1, name="norm_mix_bwd")
        small["norm_mix"][l] = g[0]
        small_above = _pack([small[n][l] for n in small])

    last = [small[n][0] for n in small] + [g_final[0]]
    small_sums[0] = _sum8(_all_gather([_pack(last)], name="gather_small_grads")[0], name="sum_small_grads_last")
    grads = {n: jnp.stack(big[n]) for n in SHARDED}
    per_layer = [_unpack(small_sums[l], last if l == 0 else last[:-1]) for l in range(L)]
    for i, n in enumerate(small):
        grads[n] = jnp.stack([per_layer[l][i] for l in range(L)])
    grads["norm_final"] = per_layer[0][-1]

    deltas, new_m, new_v = {}, {}, {}
    for n in WEIGHTS:
        deltas[n], new_m[n], new_v[n] = _adamw(w[n], grads[n], args["m_" + n], args["v_" + n], name="adamw_" + n)
    return (loss, dx.reshape(Bl, S, D), *[grads[n] for n in WEIGHTS], *[deltas[n] for n in WEIGHTS],
            *[new_m[n] for n in WEIGHTS], *[new_v[n] for n in WEIGHTS])
```

```python
import functools

import jax
import jax.numpy as jnp
from jax import lax
from jax.experimental import pallas as pl
from jax.experimental.pallas import tpu as pltpu

F32 = jnp.float32
BF16 = jnp.bfloat16

N_DEV = 8
HEAD_DIM = 64
LANES = 128
SSM_CHUNK = 32
SLAB_GROUPS = 8
ATTN_BLOCK = 512
LONG_K = 2048
WIDE_N = 2048
PACK_COLS = 1024
SUM_ROWS = 256
RMS_EPS = 1e-6
VMEM_LIMIT = 56 * 1024 * 1024
ADAM_LR, ADAM_B1, ADAM_B2, ADAM_EPS, ADAM_WD, ADAM_STEP = 0.001, 0.9, 0.999, 1e-08, 0.01, 10
MESH_AXES = ("x", "y", "c")
NEG = -1e30
NT = (((1,), (1,)), ((), ()))
TN = (((0,), (0,)), ((), ()))


def _cparams(*sem):
    return pltpu.CompilerParams(dimension_semantics=sem, vmem_limit_bytes=VMEM_LIMIT)


def _tile(dim, pref, unit=LANES):
    if dim <= pref:
        return dim
    best = None
    for t in range(unit, pref + 1, unit):
        if dim % t == 0:
            best = t
    assert best is not None, (dim, pref)
    return best


def _mm(a, b, *, name, ta=False, tb=False, a_fn=None, epi=None, extras=(), out_dtype=BF16, tm=1024, tn=1024,
        tk=1024, hosted=None):
    if ta:
        K, M = a.shape
    else:
        M, K = a.shape
    N, Kb = b.shape if tb else b.shape[::-1]
    assert K == Kb and not (ta and tb), (a.shape, b.shape)
    tm, tn, tk = _tile(M, tm), _tile(N, tn), _tile(K, tk)
    gm, gn, nk = M // tm, N // tn, K // tk
    ne = len(extras)
    nh = hosted.n if hosted else 0
    n_acc = 1 if nk > 1 else 0

    def body(a_ref, b_ref, *rest):
        e_refs, o_ref = rest[:ne], rest[ne + nh]
        acc_ref = rest[ne + 2 * nh + 1] if nk > 1 else None
        k = pl.program_id(2)
        if hosted:
            i, j = pl.program_id(0), pl.program_id(1)
            start, finish = hosted.run((i == 0) & (j == 0) & (k == 0), (i == gm - 1) & (j == gn - 1) & (k == nk - 1),
                                       rest[ne:ne + nh], rest[ne + nh + 1:ne + 2 * nh + 1],
                                       rest[ne + 2 * nh + 1 + n_acc:])
            start()
        av = a_ref[...]
        if a_fn is not None:
            av = a_fn(av)
        av = av.astype(BF16)
        bv = b_ref[...].astype(BF16)
        dims = TN if ta else NT if tb else (((1,), (0,)), ((), ()))
        part = lax.dot_general(av, bv, dims, preferred_element_type=F32)

        def finish_tile(r):
            if epi is not None:
                r = epi(r, *[e[...] for e in e_refs])
            o_ref[...] = r.astype(o_ref.dtype)

        if nk == 1:
            finish_tile(part)
        else:
            @pl.when(k == 0)
            def _():
                acc_ref[...] = part

            @pl.when(k > 0)
            def _():
                acc_ref[...] += part

            @pl.when(k == nk - 1)
            def _():
                finish_tile(acc_ref[...])

        if hosted:
            finish()

    a_spec = pl.BlockSpec((tk, tm), lambda i, j, k: (k, i)) if ta else pl.BlockSpec((tm, tk), lambda i, j, k: (i, k))
    outs = pl.pallas_call(
        body, name=name,
        out_shape=[jax.ShapeDtypeStruct((M, N), out_dtype)] + (hosted.out_shape if hosted else []),
        grid=(gm, gn, nk),
        in_specs=[a_spec, pl.BlockSpec((tn, tk), lambda i, j, k: (j, k)) if tb
                  else pl.BlockSpec((tk, tn), lambda i, j, k: (k, j))]
        + [pl.BlockSpec((tm, tn), lambda i, j, k: (i, j)) for _ in extras] + [HBM] * nh,
        out_specs=[pl.BlockSpec((tm, tn), lambda i, j, k: (i, j))] + [HBM] * nh,
        scratch_shapes=([pltpu.VMEM((tm, tn), F32)] if nk > 1 else []) + (hosted.scratch if hosted else []),
        compiler_params=_cparams(*(("arbitrary",) * 3 if hosted else ("parallel", "parallel", "arbitrary"))),
    )(a, b, *extras, *(hosted.arrays if hosted else []))
    return (outs[0], outs[1:]) if hosted else outs[0]


def _relu_sq(v):
    r = jnp.maximum(v.astype(F32), 0.0)
    return r * r


def _sigmoid(v):
    return 1.0 / (1.0 + jnp.exp(-v))


GELU_C = 0.7978845608028654
GELU_A = 0.044715


def _gelu(v):
    return 0.5 * v * (1.0 + jnp.tanh(GELU_C * (v + GELU_A * v * v * v)))


def _gelu_grad(v):
    t = jnp.tanh(GELU_C * (v + GELU_A * v * v * v))
    return 0.5 * (1.0 + t) + 0.5 * v * (1.0 - t * t) * GELU_C * (1.0 + 3.0 * GELU_A * v * v)


def _rmsnorm_fwd(x, g, *, name, tr=512):
    n, d = x.shape
    tr = _tile(n, tr, 8)

    def body(x_ref, g_ref, h_ref, r_ref):
        xv = x_ref[...]
        r = lax.rsqrt(jnp.mean(xv * xv, axis=-1, keepdims=True) + RMS_EPS)
        h_ref[...] = (xv * r * g_ref[...]).astype(BF16)
        r_ref[...] = r

    return pl.pallas_call(
        body, name=name,
        out_shape=(jax.ShapeDtypeStruct((n, d), BF16), jax.ShapeDtypeStruct((n, 1), F32)),
        grid=(n // tr,),
        in_specs=[pl.BlockSpec((tr, d), lambda i: (i, 0)), pl.BlockSpec((1, d), lambda i: (0, 0))],
        out_specs=(pl.BlockSpec((tr, d), lambda i: (i, 0)), pl.BlockSpec((tr, 1), lambda i: (i, 0))),
        compiler_params=_cparams("parallel"),
    )(x, g.reshape(1, d))


def _rmsnorm_bwd(dh, x, r, g, dres, *, name, tr=512):
    n, d = x.shape
    tr = _tile(n, tr, 8)

    def body(dh_ref, x_ref, r_ref, g_ref, dres_ref, dx_ref, dg_ref):
        i = pl.program_id(0)
        rv = r_ref[...]
        xh = x_ref[...] * rv
        dhv = dh_ref[...].astype(F32)
        dxh = dhv * g_ref[...]
        m = jnp.mean(dxh * xh, axis=-1, keepdims=True)
        dx_ref[...] = rv * (dxh - xh * m) + dres_ref[...]
        part = jnp.sum(dhv * xh, axis=0, keepdims=True)

        @pl.when(i == 0)
        def _():
            dg_ref[...] = part

        @pl.when(i > 0)
        def _():
            dg_ref[...] += part

    row = pl.BlockSpec((tr, d), lambda i: (i, 0))
    vec = pl.BlockSpec((1, d), lambda i: (0, 0))
    return pl.pallas_call(
        body, name=name,
        out_shape=(jax.ShapeDtypeStruct((n, d), F32), jax.ShapeDtypeStruct((1, d), F32)),
        grid=(n // tr,),
        in_specs=[row, row, pl.BlockSpec((tr, 1), lambda i: (i, 0)), vec, row],
        out_specs=(row, vec),
        compiler_params=_cparams("arbitrary"),
    )(dh, x, r, g.reshape(1, d), dres)


def _loss_head(x, g, target, *, name, tr=512):
    n, d = x.shape
    tr = _tile(n, tr, 8)

    def body(x_ref, g_ref, t_ref, dx_ref, dg_ref, loss_ref):
        i = pl.program_id(0)
        xv = x_ref[...]
        gv = g_ref[...]
        r = lax.rsqrt(jnp.mean(xv * xv, axis=-1, keepdims=True) + RMS_EPS)
        xh = xv * r
        err = xh * gv - t_ref[...]
        lpart = 0.5 * jnp.sum(jnp.mean(err * err, axis=-1, keepdims=True), axis=0, keepdims=True)
        dy = err * (1.0 / d)
        dxh = dy * gv
        m = jnp.mean(dxh * xh, axis=-1, keepdims=True)
        dx_ref[...] = r * (dxh - xh * m)
        gpart = jnp.sum(dy * xh, axis=0, keepdims=True)
        lrow = jnp.broadcast_to(lpart, (1, LANES))

        @pl.when(i == 0)
        def _():
            dg_ref[...] = gpart
            loss_ref[...] = lrow

        @pl.when(i > 0)
        def _():
            dg_ref[...] += gpart
            loss_ref[...] += lrow

    row = pl.BlockSpec((tr, d), lambda i: (i, 0))
    vec = pl.BlockSpec((1, d), lambda i: (0, 0))
    return pl.pallas_call(
        body, name=name,
        out_shape=(jax.ShapeDtypeStruct((n, d), F32), jax.ShapeDtypeStruct((1, d), F32),
                   jax.ShapeDtypeStruct((1, LANES), F32)),
        grid=(n // tr,),
        in_specs=[row, vec, row],
        out_specs=(row, vec, pl.BlockSpec((1, LANES), lambda i: (0, 0))),
        compiler_params=_cparams("arbitrary"),
    )(x, g.reshape(1, d), target)


def _tri_dot(v, tri):
    hi = v.astype(BF16)
    r1 = v - hi.astype(F32)
    mid = r1.astype(BF16)
    lo = (r1 - mid.astype(F32)).astype(BF16)
    d = functools.partial(jnp.dot, preferred_element_type=F32)
    return d(hi, tri) + d(mid, tri) + d(lo, tri)


def _fox_gate_fwd(ft, bf, *, name, blk=256):
    B, H, S = ft.shape
    blk = _tile(S, blk)
    nb = S // blk

    def body(f_ref, b_ref, o_ref):
        x = f_ref[0] + b_ref[...]
        logf = jnp.minimum(x, 0.0) - jnp.log(1.0 + jnp.exp(-jnp.abs(x)))
        rr = lax.broadcasted_iota(jnp.int32, (blk, blk), 0)
        cc = lax.broadcasted_iota(jnp.int32, (blk, blk), 1)
        tri = (rr <= cc).astype(BF16)
        carry = jnp.zeros((H, 1), F32)
        for n in range(nb):
            c = _tri_dot(logf[:, n * blk:(n + 1) * blk], tri) + carry
            o_ref[0, :, n * blk:(n + 1) * blk] = c
            carry = c[:, blk - 1:blk]

    return pl.pallas_call(
        body, name=name,
        out_shape=jax.ShapeDtypeStruct((B, H, S), F32),
        grid=(B,),
        in_specs=[pl.BlockSpec((1, H, S), lambda b: (b, 0, 0)), pl.BlockSpec((H, 1), lambda b: (0, 0))],
        out_specs=pl.BlockSpec((1, H, S), lambda b: (b, 0, 0)),
        compiler_params=_cparams("parallel"),
    )(ft, bf.reshape(H, 1))


def _fox_gate_bwd(dF, ft, bf, *, name, blk=256):
    B, H, S = ft.shape
    blk = _tile(S, blk)
    nb = S // blk

    def body(d_ref, f_ref, b_ref, o_ref, db_ref):
        b = pl.program_id(0)
        x = f_ref[0] + b_ref[...]
        sneg = 1.0 / (1.0 + jnp.exp(x))
        dv = d_ref[0]
        rr = lax.broadcasted_iota(jnp.int32, (blk, blk), 0)
        cc = lax.broadcasted_iota(jnp.int32, (blk, blk), 1)
        tri = (rr >= cc).astype(BF16)
        carry = jnp.zeros((H, 1), F32)
        tot = jnp.zeros((H, 1), F32)
        for n in reversed(range(nb)):
            sl = slice(n * blk, (n + 1) * blk)
            c = _tri_dot(dv[:, sl], tri) + carry
            g = c * sneg[:, sl]
            o_ref[0, :, sl] = g
            tot = tot + jnp.sum(g, axis=1, keepdims=True)
            carry = c[:, 0:1]

        @pl.when(b == 0)
        def _():
            db_ref[...] = tot

        @pl.when(b > 0)
        def _():
            db_ref[...] += tot

    blkspec = pl.BlockSpec((1, H, S), lambda b: (b, 0, 0))
    return pl.pallas_call(
        body, name=name,
        out_shape=(jax.ShapeDtypeStruct((B, H, S), F32), jax.ShapeDtypeStruct((H, 1), F32)),
        grid=(B,),
        in_specs=[blkspec, blkspec, pl.BlockSpec((H, 1), lambda b: (0, 0))],
        out_specs=(blkspec, pl.BlockSpec((H, 1), lambda b: (0, 0))),
        compiler_params=_cparams("arbitrary"),
    )(dF, ft, bf.reshape(H, 1))


def _head_masks():
    lane = lax.broadcasted_iota(jnp.int32, (1, LANES), 1)
    return [lane < HEAD_DIM, lane >= HEAD_DIM]


HBM = pl.BlockSpec(memory_space=pltpu.HBM)
MESH = pl.DeviceIdType.MESH


def _direct_copies(kinds, x_refs, out_refs, send_sems, recv_sems, local_sems):
    x, y, c = lax.axis_index("x"), lax.axis_index("y"), lax.axis_index("c")
    me = 4 * x + 2 * y + c
    copies = []
    for a, (kind, xr, outr) in enumerate(zip(kinds, x_refs, out_refs)):
        copies.append(pltpu.make_async_copy(xr if kind == "gather" else xr.at[me], outr.at[me], local_sems.at[a]))
    for k in range(1, N_DEV):
        px = 1 - x if (k >> 2) & 1 else x
        py = 1 - y if (k >> 1) & 1 else y
        pc = 1 - c if k & 1 else c
        for a, (kind, xr, outr) in enumerate(zip(kinds, x_refs, out_refs)):
            copies.append(pltpu.make_async_remote_copy(
                src_ref=xr if kind == "gather" else xr.at[4 * px + 2 * py + pc], dst_ref=outr.at[me],
                send_sem=send_sems.at[7 * a + k - 1], recv_sem=recv_sems.at[7 * a + k - 1],
                device_id=(px, py, pc), device_id_type=MESH))
    return copies


class _Hosted:
    def __init__(self, gather=(), exchange=()):
        self.arrays = list(gather) + list(exchange)
        self.kinds = ["gather"] * len(gather) + ["exchange"] * len(exchange)
        self.n = len(self.arrays)
        self.out_shape = [jax.ShapeDtypeStruct(((N_DEV,) if k == "gather" else ()) + a.shape, a.dtype)
                          for k, a in zip(self.kinds, self.arrays)]
        self.scratch = [pltpu.SemaphoreType.DMA((7 * self.n,)), pltpu.SemaphoreType.DMA((7 * self.n,)),
                        pltpu.SemaphoreType.DMA((self.n,))]

    def run(self, first, last, x_refs, out_refs, sems):
        def go(when, act):
            @pl.when(when)
            def _():
                for cp in _direct_copies(self.kinds, x_refs, out_refs, *sems):
                    act(cp)
        return (lambda: go(first, lambda cp: cp.start())), (lambda: go(last, lambda cp: cp.wait()))


def _stack_heads(x, masks):
    zero = jnp.zeros_like(x)
    return jnp.concatenate([jnp.where(masks[0], x, zero), jnp.where(masks[1], x, zero)], axis=0)


def _attn_fwd(proj, frow, *, name, H, tq, hosted=None):
    B, S, _ = proj.shape
    HP = H // 2
    nq = S // tq
    scale = HEAD_DIM ** -0.5
    nh = hosted.n if hosted else 0

    def body(*refs):
        q_ref, k_ref, v_ref, fk_ref = refs[:4]
        o_ref, lse_ref = refs[4 + nh:6 + nh]
        i = pl.program_id(2)
        if hosted:
            b, hp = pl.program_id(0), pl.program_id(1)
            start, finish = hosted.run((b == 0) & (hp == 0) & (i == 0), (b == B - 1) & (hp == HP - 1) & (i == nq - 1),
                                       refs[4:4 + nh], refs[6 + nh:6 + 2 * nh], refs[6 + 2 * nh:])
            start()
        masks = _head_masks()
        q2 = _stack_heads(q_ref[0], masks) * jnp.asarray(scale, BF16)
        rr = lax.broadcasted_iota(jnp.int32, (tq, tq), 0)
        cc = lax.broadcasted_iota(jnp.int32, (tq, tq), 1)
        causal = rr >= cc

        def block(j, carry, masked):
            rows = pl.ds(pl.multiple_of(j * tq, tq), tq)
            kj = k_ref[0, rows, :]
            vj = v_ref[0, rows, :]
            s2 = lax.dot_general(q2, kj, NT, preferred_element_type=F32)
            new, ps = [], []
            for h in range(2):
                m, l, acc = carry[h]
                s = s2[h * tq:(h + 1) * tq] - fk_ref[0, 0, h, pl.ds(j, 1), :]
                if masked:
                    s = jnp.where(causal, s, NEG)
                m_new = jnp.maximum(m, jnp.max(s, axis=-1, keepdims=True))
                alpha = jnp.exp(m - m_new)
                p = jnp.exp(s - m_new)
                new.append((m_new, alpha * l + jnp.sum(p, axis=-1, keepdims=True), alpha, acc))
                ps.append(p.astype(BF16))
            pv = jnp.dot(jnp.concatenate(ps, axis=0), vj, preferred_element_type=F32)
            return tuple((m, l, alpha * acc + pv[h * tq:(h + 1) * tq]) for h, (m, l, alpha, acc) in enumerate(new))

        one = (jnp.full((tq, 1), NEG, F32), jnp.zeros((tq, 1), F32), jnp.zeros((tq, LANES), F32))
        carry = lax.fori_loop(0, i, lambda j, c: block(j, c, False), (one, one))
        (m0, l0, a0), (m1, l1, a1) = block(i, carry, True)
        o_ref[0] = jnp.where(masks[0], a0 / l0, a1 / l1).astype(BF16)
        two = lax.broadcasted_iota(jnp.int32, (1, 2), 1)
        lse_ref[0, 0] = jnp.where(two == 0, m0 + jnp.log(l0), m1 + jnp.log(l1))
        if hosted:
            finish()

    kv = lambda off: pl.BlockSpec((1, S, LANES), lambda b, hp, i: (b, 0, off + hp))
    outs = pl.pallas_call(
        body, name=name,
        out_shape=[jax.ShapeDtypeStruct((B, S, H * HEAD_DIM), BF16), jax.ShapeDtypeStruct((B, HP, S, 2), F32)]
        + (hosted.out_shape if hosted else []),
        grid=(B, HP, nq),
        in_specs=[pl.BlockSpec((1, tq, LANES), lambda b, hp, i: (b, i, hp)), kv(HP), kv(2 * HP),
                  pl.BlockSpec((1, 1, 2, nq, tq), lambda b, hp, i: (b, hp, 0, 0, 0))] + [HBM] * nh,
        out_specs=[pl.BlockSpec((1, tq, LANES), lambda b, hp, i: (b, i, hp)),
                   pl.BlockSpec((1, 1, tq, 2), lambda b, hp, i: (b, hp, i, 0))] + [HBM] * nh,
        scratch_shapes=hosted.scratch if hosted else [],
        compiler_params=_cparams("arbitrary", "arbitrary", "arbitrary"),
    )(proj, proj, proj, frow, *(hosted.arrays if hosted else []))
    return outs[0], outs[1], outs[2:]


def _attn_bwd(proj, ya, dya, lse, frow, *, name, H, tq, hosted=None):
    B, S, _ = proj.shape
    HP = H // 2
    nq = S // tq
    AW = H * HEAD_DIM
    scale = HEAD_DIM ** -0.5
    nh = hosted.n if hosted else 0

    def body(*refs):
        q_ref, k_ref, v_ref, o_ref, do_ref, lse_ref, fk_ref = refs[:7]
        dq_ref, dk_ref, dv_ref, dfk_ref, dfq_ref = refs[7 + nh:12 + nh]
        (q2_ref, do2_ref, lse2_ref, delta2_ref, dq2_acc, dfq2_acc, dk_acc, dv_acc,
         dfk_acc) = refs[12 + 2 * nh:21 + 2 * nh]
        if hosted:
            b, hp = pl.program_id(0), pl.program_id(1)
            start, finish = hosted.run((b == 0) & (hp == 0), (b == B - 1) & (hp == HP - 1),
                                       refs[7:7 + nh], refs[12 + nh:12 + 2 * nh], refs[21 + 2 * nh:])
            start()
        masks = _head_masks()
        rr = lax.broadcasted_iota(jnp.int32, (tq, tq), 0)
        cc = lax.broadcasted_iota(jnp.int32, (tq, tq), 1)
        causal = rr >= cc
        sc = jnp.asarray(scale, BF16)

        def stage(i, c):
            rows = pl.ds(pl.multiple_of(i * tq, tq), tq)
            dov = do_ref[0, rows, :]
            q2_ref[i] = _stack_heads(q_ref[0, rows, :], masks) * sc
            do2_ref[i] = _stack_heads(dov, masks)
            prod = dov.astype(F32) * o_ref[0, rows, :].astype(F32)
            delta2_ref[i] = jnp.concatenate(
                [jnp.sum(jnp.where(masks[h], prod, 0.0), axis=-1, keepdims=True) for h in range(2)], axis=0)
            lv = lse_ref[0, 0, rows, :]
            lse2_ref[i] = jnp.concatenate([lv[:, 0:1], lv[:, 1:2]], axis=0)
            return c

        lax.fori_loop(0, nq, stage, 0)
        dq2_acc[...] = jnp.zeros_like(dq2_acc)
        dfq2_acc[...] = jnp.zeros_like(dfq2_acc)

        def kv_block(j, carry):
            rows_j = pl.ds(pl.multiple_of(j * tq, tq), tq)
            kj = k_ref[0, rows_j, :]
            vj = v_ref[0, rows_j, :]
            ks = kj * sc
            dk_acc[...] = jnp.zeros_like(dk_acc)
            dv_acc[...] = jnp.zeros_like(dv_acc)
            dfk_acc[...] = jnp.zeros_like(dfk_acc)

            def logits(i):
                return (lax.dot_general(q2_ref[i], kj, NT, preferred_element_type=F32),
                        lax.dot_general(do2_ref[i], vj, NT, preferred_element_type=F32))

            def probs(i, s2, dp2, masked):
                lse2 = lse2_ref[i]
                delta2 = delta2_ref[i]
                ps, dss = [], []
                for h in range(2):
                    half = slice(h * tq, (h + 1) * tq)
                    p = jnp.exp(s2[half] - fk_ref[0, 0, h, pl.ds(j, 1), :] - lse2[half])
                    if masked:
                        p = jnp.where(causal, p, 0.0)
                    ds = p * (dp2[half] - delta2[half])
                    dfk_acc[h:h + 1, :] -= jnp.sum(ds, axis=0, keepdims=True)
                    dfq2_acc[i, half, :] += jnp.sum(ds, axis=1, keepdims=True)
                    ps.append(p.astype(BF16))
                    dss.append(ds.astype(BF16))
                return jnp.concatenate(ps, axis=0), jnp.concatenate(dss, axis=0)

            def grads(i, p2, ds2):
                dv_acc[...] += lax.dot_general(p2, do2_ref[i], TN, preferred_element_type=F32)
                dk_acc[...] += lax.dot_general(ds2, q2_ref[i], TN, preferred_element_type=F32)
                dq2_acc[i] += jnp.dot(ds2, ks, preferred_element_type=F32)

            grads(j, *probs(j, *logits(j), True))

            def rest(i, c):
                grads(i, *probs(i, *logits(i), False))
                return c

            lax.fori_loop(j + 1, nq, rest, 0)
            dk_ref[0, rows_j, :] = dk_acc[...].astype(BF16)
            dv_ref[0, rows_j, :] = dv_acc[...].astype(BF16)
            for h in range(2):
                dfk_ref[0, 0, h, pl.ds(j, 1), :] = dfk_acc[h:h + 1, :]
            return carry

        lax.fori_loop(0, nq, kv_block, 0)
        two = lax.broadcasted_iota(jnp.int32, (1, 2), 1)

        def finish_block(i, c):
            rows = pl.ds(pl.multiple_of(i * tq, tq), tq)
            dq2 = dq2_acc[i]
            dq_ref[0, rows, :] = jnp.where(masks[0], dq2[:tq], dq2[tq:]).astype(BF16)
            dfq2 = dfq2_acc[i]
            dfq_ref[0, 0, rows, :] = jnp.where(two == 0, dfq2[:tq], dfq2[tq:])
            return c

        lax.fori_loop(0, nq, finish_block, 0)
        if hosted:
            finish()

    col = lambda off: pl.BlockSpec((1, S, LANES), lambda b, hp: (b, 0, off + hp))
    stat = pl.BlockSpec((1, 1, S, 2), lambda b, hp: (b, hp, 0, 0))
    rowf = pl.BlockSpec((1, 1, 2, nq, tq), lambda b, hp: (b, hp, 0, 0, 0))
    grad = jax.ShapeDtypeStruct((B, S, AW), BF16)
    outs = pl.pallas_call(
        body, name=name,
        out_shape=[grad, grad, grad, jax.ShapeDtypeStruct((B, HP, 2, nq, tq), F32),
                   jax.ShapeDtypeStruct((B, HP, S, 2), F32)] + (hosted.out_shape if hosted else []),
        grid=(B, HP),
        in_specs=[col(0), col(HP), col(2 * HP), col(0), col(0), stat, rowf] + [HBM] * nh,
        out_specs=[col(0), col(0), col(0), rowf, stat] + [HBM] * nh,
        scratch_shapes=[pltpu.VMEM((nq, 2 * tq, LANES), BF16), pltpu.VMEM((nq, 2 * tq, LANES), BF16),
                        pltpu.VMEM((nq, 2 * tq, 1), F32), pltpu.VMEM((nq, 2 * tq, 1), F32),
                        pltpu.VMEM((nq, 2 * tq, LANES), F32), pltpu.VMEM((nq, 2 * tq, 1), F32),
                        pltpu.VMEM((tq, LANES), F32), pltpu.VMEM((tq, LANES), F32), pltpu.VMEM((2, tq), F32)]
        + (hosted.scratch if hosted else []),
        compiler_params=_cparams("arbitrary", "arbitrary"),
    )(proj, proj, proj, ya, dya, lse, frow, *(hosted.arrays if hosted else []))
    return outs[:5], outs[5:]


def _cmul(ar, ai, br, bi):
    return ar * br - ai * bi, ar * bi + ai * br


def _ssm_states(u_ref, bm, lam_ref, pw_ref, lamT_ref, hr_ref, hi_ref, inr_ref, ini_ref, T, NC, SP):
    lr, li = lam_ref[0, 0:1, :], lam_ref[0, 1:2, :]
    bu = jnp.dot(u_ref[0, 0], bm, preferred_element_type=F32)
    hr_ref[0] = bu[:, :SP]
    hi_ref[0] = bu[:, SP:]

    def step(t, c):
        bu = jnp.dot(u_ref[0, t], bm, preferred_element_type=F32)
        pr, pi = _cmul(hr_ref[t - 1], hi_ref[t - 1], lr, li)
        hr_ref[t] = pr + bu[:, :SP]
        hi_ref[t] = pi + bu[:, SP:]
        return c

    lax.fori_loop(1, T, step, 0, unroll=2)

    tr, ti = lamT_ref[0, 0:1, :], lamT_ref[0, 1:2, :]
    inr_ref[0:1, :] = jnp.zeros((1, SP), F32)
    ini_ref[0:1, :] = jnp.zeros((1, SP), F32)

    def chunk(n, c):
        prev = pl.ds(n - 1, 1)
        pr, pi = _cmul(inr_ref[prev, :], ini_ref[prev, :], tr, ti)
        inr_ref[pl.ds(n, 1), :] = pr + hr_ref[T - 1, prev, :]
        ini_ref[pl.ds(n, 1), :] = pi + hi_ref[T - 1, prev, :]
        return c

    lax.fori_loop(1, NC, chunk, 0)


def _ssm_entry_term(t, pw_ref, inr_ref, ini_ref):
    return _cmul(inr_ref[...], ini_ref[...], pw_ref[0, 0, pl.ds(t, 1), :], pw_ref[0, 1, pl.ds(t, 1), :])


def _ssm_fwd(u_tm, bmat, cmat, lam, pw, lamT, dskip, *, name):
    B, T, NC, W = u_tm.shape
    NS = W // LANES
    SP = bmat.shape[2] // 2

    def body(u_ref, b_ref, c_ref, lam_ref, pw_ref, lamT_ref, d_ref, y_ref, hr_ref, hi_ref, inr_ref, ini_ref):
        _ssm_states(u_ref, b_ref[0], lam_ref, pw_ref, lamT_ref, hr_ref, hi_ref, inr_ref, ini_ref, T, NC, SP)
        cm = c_ref[0]
        dv = d_ref[...]

        def out(t, c):
            cr, ci = _ssm_entry_term(t, pw_ref, inr_ref, ini_ref)
            hcat = jnp.concatenate([hr_ref[t] + cr, hi_ref[t] + ci], axis=1).astype(BF16)
            y_ref[0, t] = jnp.dot(hcat, cm, preferred_element_type=F32) + dv * u_ref[0, t].astype(F32)
            return c

        lax.fori_loop(0, T, out, 0, unroll=2)

    slab = lambda *shape: pl.BlockSpec((1,) + shape, lambda b, s: (s,) + (0,) * len(shape))
    tok = pl.BlockSpec((1, T, NC, LANES), lambda b, s: (b, 0, 0, s))
    return pl.pallas_call(
        body, name=name,
        out_shape=jax.ShapeDtypeStruct((B, T, NC, W), F32),
        grid=(B, NS),
        in_specs=[tok, slab(LANES, 2 * SP), slab(2 * SP, LANES), slab(2, SP), slab(2, T, SP), slab(2, SP),
                  pl.BlockSpec((1, LANES), lambda b, s: (0, s))],
        out_specs=tok,
        scratch_shapes=[pltpu.VMEM((T, NC, SP), F32), pltpu.VMEM((T, NC, SP), F32),
                        pltpu.VMEM((NC, SP), F32), pltpu.VMEM((NC, SP), F32)],
        compiler_params=_cparams("parallel", "parallel"),
    )(u_tm, bmat, cmat, lam, pw, lamT, dskip)


def _ssm_bwd(u_tm, dy_tm, bmat, bmat_t, cmat_t, lam, pw, lamT, dskip, *, name):
    B, T, NC, W = u_tm.shape
    NS = W // LANES
    SP = bmat.shape[2] // 2

    def body(u_ref, dy_ref, b_ref, bt_ref, ct_ref, lam_ref, pw_ref, lamT_ref, d_ref,
             du_ref, gb_ref, gc_ref, glam_ref, gd_ref,
             hr_ref, hi_ref, ar_ref, ai_ref, inr_ref, ini_ref, anr_ref, ani_ref):
        b = pl.program_id(1)
        _ssm_states(u_ref, b_ref[0], lam_ref, pw_ref, lamT_ref, hr_ref, hi_ref, inr_ref, ini_ref, T, NC, SP)
        lr, li = lam_ref[0, 0:1, :], lam_ref[0, 1:2, :]
        ct = ct_ref[0]
        bt = bt_ref[0]
        dv = d_ref[...]

        gh = jnp.dot(dy_ref[0, T - 1].astype(BF16), ct, preferred_element_type=F32)
        ar_ref[T - 1] = gh[:, :SP]
        ai_ref[T - 1] = gh[:, SP:]

        def back(k, c):
            t = T - 2 - k
            gh = jnp.dot(dy_ref[0, t].astype(BF16), ct, preferred_element_type=F32)
            pr, pi = _cmul(ar_ref[t + 1], ai_ref[t + 1], lr, -li)
            ar_ref[t] = pr + gh[:, :SP]
            ai_ref[t] = pi + gh[:, SP:]
            return c

        lax.fori_loop(0, T - 1, back, 0, unroll=2)

        tr, ti = lamT_ref[0, 0:1, :], lamT_ref[0, 1:2, :]
        anr_ref[NC - 1:NC, :] = jnp.zeros((1, SP), F32)
        ani_ref[NC - 1:NC, :] = jnp.zeros((1, SP), F32)

        def chunk(k, c):
            n = NC - 2 - k
            nxt = pl.ds(n + 1, 1)
            pr, pi = _cmul(anr_ref[nxt, :], ani_ref[nxt, :], tr, -ti)
            anr_ref[pl.ds(n, 1), :] = pr + ar_ref[0, nxt, :]
            ani_ref[pl.ds(n, 1), :] = pi + ai_ref[0, nxt, :]
            return c

        lax.fori_loop(0, NC - 1, chunk, 0)

        @pl.when(b == 0)
        def _():
            gb_ref[...] = jnp.zeros_like(gb_ref)
            gc_ref[...] = jnp.zeros_like(gc_ref)
            glam_ref[...] = jnp.zeros_like(glam_ref)
            gd_ref[...] = jnp.zeros_like(gd_ref)

        def final(t, hpr, hpi, gl):
            back_pow = pl.ds(T - 1 - t, 1)
            cr, ci = _cmul(anr_ref[...], ani_ref[...], pw_ref[0, 0, back_pow, :], -pw_ref[0, 1, back_pow, :])
            a_r = ar_ref[t] + cr
            a_i = ai_ref[t] + ci
            gl = (gl[0] + jnp.sum(a_r * hpr + a_i * hpi, axis=0, keepdims=True),
                  gl[1] + jnp.sum(a_i * hpr - a_r * hpi, axis=0, keepdims=True))
            acat = jnp.concatenate([a_r, a_i], axis=1).astype(BF16)
            ut = u_ref[0, t]
            dyt = dy_ref[0, t]
            du_ref[0, t] = (jnp.dot(acat, bt, preferred_element_type=F32) + dv * dyt).astype(BF16)
            gb_ref[0] += lax.dot_general(acat, ut, TN, preferred_element_type=F32)
            er, ei = _ssm_entry_term(t, pw_ref, inr_ref, ini_ref)
            h_r = hr_ref[t] + er
            h_i = hi_ref[t] + ei
            hr_ref[t] = h_r
            hi_ref[t] = h_i
            hcat = jnp.concatenate([h_r, h_i], axis=1).astype(BF16)
            gc_ref[0] += lax.dot_general(dyt.astype(BF16), hcat, TN, preferred_element_type=F32)
            gd_ref[0] += jnp.sum(dyt * ut.astype(F32), axis=0, keepdims=True)
            return gl

        zero = jnp.zeros((1, SP), F32)
        gl = final(0, inr_ref[...], ini_ref[...], (zero, zero))
        gl = lax.fori_loop(1, T, lambda t, gl: final(t, hr_ref[t - 1], hi_ref[t - 1], gl), gl)
        glam_ref[0, 0:1, :] += gl[0]
        glam_ref[0, 1:2, :] += gl[1]

    slab = lambda *shape: pl.BlockSpec((1,) + shape, lambda s, b: (s,) + (0,) * len(shape))
    tok = pl.BlockSpec((1, T, NC, LANES), lambda s, b: (b, 0, 0, s))
    big = pltpu.VMEM((T, NC, SP), F32)
    small = pltpu.VMEM((NC, SP), F32)
    return pl.pallas_call(
        body, name=name,
        out_shape=(jax.ShapeDtypeStruct((B, T, NC, W), BF16),
                   jax.ShapeDtypeStruct((NS, 2 * SP, LANES), F32), jax.ShapeDtypeStruct((NS, LANES, 2 * SP), F32),
                   jax.ShapeDtypeStruct((NS, 2, SP), F32), jax.ShapeDtypeStruct((NS, 1, LANES), F32)),
        grid=(NS, B),
        in_specs=[tok, tok, slab(LANES, 2 * SP), slab(2 * SP, LANES), slab(LANES, 2 * SP), slab(2, SP),
                  slab(2, T, SP), slab(2, SP), pl.BlockSpec((1, LANES), lambda s, b: (0, s))],
        out_specs=(tok, slab(2 * SP, LANES), slab(LANES, 2 * SP), slab(2, SP), slab(1, LANES)),
        scratch_shapes=[big, big, big, big, small, small, small, small],
        compiler_params=_cparams("parallel", "arbitrary"),
    )(u_tm, dy_tm, bmat, bmat_t, cmat_t, lam, pw, lamT, dskip)


def _glu_fwd(ys, w, b, *, name, tr=512):
    n, wd = ys.shape
    tr = _tile(n, tr, 8)

    def body(y_ref, w_ref, b_ref, o_ref):
        yb = _gelu(y_ref[...])
        z = jnp.dot(yb.astype(BF16), w_ref[...], preferred_element_type=F32) + b_ref[...]
        o_ref[...] = (yb * _sigmoid(z)).astype(BF16)

    row = pl.BlockSpec((tr, wd), lambda i: (i, 0))
    return pl.pallas_call(
        body, name=name, out_shape=jax.ShapeDtypeStruct((n, wd), BF16), grid=(n // tr,),
        in_specs=[row, pl.BlockSpec((wd, wd), lambda i: (0, 0)), pl.BlockSpec((1, wd), lambda i: (0, 0))],
        out_specs=row, compiler_params=_cparams("parallel"),
    )(ys, w, b.reshape(1, wd))


def _glu_bwd(ys, dyb2, w, w_t, b, *, name, tr=512):
    n, wd = ys.shape
    tr = _tile(n, tr, 8)

    def body(y_ref, d_ref, w_ref, wt_ref, b_ref, dys_ref, dz_ref, yb_ref, db_ref):
        i = pl.program_id(0)
        yv = y_ref[...]
        yb = _gelu(yv)
        ybb = yb.astype(BF16)
        sg = _sigmoid(jnp.dot(ybb, w_ref[...], preferred_element_type=F32) + b_ref[...])
        dv = d_ref[...].astype(F32)
        dz = dv * yb * sg * (1.0 - sg)
        dzb = dz.astype(BF16)
        dyb = dv * sg + jnp.dot(dzb, wt_ref[...], preferred_element_type=F32)
        dys_ref[...] = dyb * _gelu_grad(yv)
        dz_ref[...] = dzb
        yb_ref[...] = ybb
        part = jnp.sum(dz, axis=0, keepdims=True)

        @pl.when(i == 0)
        def _():
            db_ref[...] = part

        @pl.when(i > 0)
        def _():
            db_ref[...] += part

    row = pl.BlockSpec((tr, wd), lambda i: (i, 0))
    mat = pl.BlockSpec((wd, wd), lambda i: (0, 0))
    vec = pl.BlockSpec((1, wd), lambda i: (0, 0))
    return pl.pallas_call(
        body, name=name,
        out_shape=(jax.ShapeDtypeStruct((n, wd), F32), jax.ShapeDtypeStruct((n, wd), BF16),
                   jax.ShapeDtypeStruct((n, wd), BF16), jax.ShapeDtypeStruct((1, wd), F32)),
        grid=(n // tr,), in_specs=[row, row, mat, mat, vec], out_specs=(row, row, row, vec),
        compiler_params=_cparams("arbitrary"),
    )(ys, dyb2, w, w_t, b.reshape(1, wd))


def _merge_fwd(ya, yb2, wa, wb, proj, gate_blk, *, name, tr=512):
    n, aw = ya.shape
    d = wa.shape[1]
    tr = _tile(n, tr, 8)

    def body(ya_ref, yb_ref, wa_ref, wb_ref, ga_ref, gb_ref, mix_ref, pa_ref, pb_ref):
        pa = jnp.dot(ya_ref[...], wa_ref[...], preferred_element_type=F32)
        pb = jnp.dot(yb_ref[...], wb_ref[...], preferred_element_type=F32)
        mix = _sigmoid(ga_ref[...].astype(F32)) * pa + _sigmoid(gb_ref[...].astype(F32)) * pb
        mix_ref[...] = mix.astype(BF16)
        pa_ref[...] = pa.astype(BF16)
        pb_ref[...] = pb.astype(BF16)

    row = lambda wdt: pl.BlockSpec((tr, wdt), lambda i: (i, 0))
    full = lambda r, c: pl.BlockSpec((r, c), lambda i: (0, 0))
    out = jax.ShapeDtypeStruct((n, d), BF16)
    return pl.pallas_call(
        body, name=name, out_shape=(out, out, out), grid=(n // tr,),
        in_specs=[row(aw), row(yb2.shape[1]), full(*wa.shape), full(*wb.shape),
                  pl.BlockSpec((tr, d), lambda i: (i, gate_blk)), pl.BlockSpec((tr, d), lambda i: (i, gate_blk + 1))],
        out_specs=(row(d), row(d), row(d)), compiler_params=_cparams("parallel"),
    )(ya, yb2, wa, wb, proj, proj)


def _merge_bwd(dmix, proj, pa, pb, gate_blk, dproj_cols, *, name, tr=512):
    n, d = dmix.shape
    tr = _tile(n, tr, 8)
    assert gate_blk % 2 == 0

    def body(dm_ref, ga_ref, gb_ref, pa_ref, pb_ref, dpa_ref, dpb_ref, dg_ref):
        dm = dm_ref[...].astype(F32)
        sa = _sigmoid(ga_ref[...].astype(F32))
        sb = _sigmoid(gb_ref[...].astype(F32))
        dpa_ref[...] = (dm * sa).astype(BF16)
        dpb_ref[...] = (dm * sb).astype(BF16)
        dg_ref[:, :d] = (dm * pa_ref[...].astype(F32) * sa * (1.0 - sa)).astype(BF16)
        dg_ref[:, d:] = (dm * pb_ref[...].astype(F32) * sb * (1.0 - sb)).astype(BF16)

    row = pl.BlockSpec((tr, d), lambda i: (i, 0))
    out = jax.ShapeDtypeStruct((n, d), BF16)
    return pl.pallas_call(
        body, name=name, out_shape=(out, out, jax.ShapeDtypeStruct((n, dproj_cols), BF16)), grid=(n // tr,),
        in_specs=[row, pl.BlockSpec((tr, d), lambda i: (i, gate_blk)), pl.BlockSpec((tr, d), lambda i: (i, gate_blk + 1)),
                  row, row],
        out_specs=(row, row, pl.BlockSpec((tr, 2 * d), lambda i: (i, gate_blk // 2))),
        compiler_params=_cparams("parallel"),
    )(dmix, proj, proj, pa, pb)


def _outproj_fwd(mixed, w, x0, g, *, name, tr=512):
    n, d = x0.shape
    tr = _tile(n, tr, 8)

    def body(m_ref, w_ref, x_ref, g_ref, x1_ref, h_ref, r_ref):
        x1 = x_ref[...] + jnp.dot(m_ref[...], w_ref[...], preferred_element_type=F32)
        r = lax.rsqrt(jnp.mean(x1 * x1, axis=-1, keepdims=True) + RMS_EPS)
        x1_ref[...] = x1
        h_ref[...] = (x1 * r * g_ref[...]).astype(BF16)
        r_ref[...] = r

    row = pl.BlockSpec((tr, d), lambda i: (i, 0))
    return pl.pallas_call(
        body, name=name,
        out_shape=(jax.ShapeDtypeStruct((n, d), F32), jax.ShapeDtypeStruct((n, d), BF16),
                   jax.ShapeDtypeStruct((n, 1), F32)),
        grid=(n // tr,),
        in_specs=[row, pl.BlockSpec((d, d), lambda i: (0, 0)), row, pl.BlockSpec((1, d), lambda i: (0, 0))],
        out_specs=(row, row, pl.BlockSpec((tr, 1), lambda i: (i, 0))),
        compiler_params=_cparams("parallel"),
    )(mixed, w, x0, g.reshape(1, d))


def _adamw(w, g, m, v, *, name):
    shape = w.shape
    total = w.size
    if w.ndim == 3 and shape[1] % 8 == 0:
        lead, rows, cols = shape
    elif total % PACK_COLS == 0 and ((total // PACK_COLS) % 8 == 0 or total // PACK_COLS <= 512):
        lead, rows, cols = 1, total // PACK_COLS, PACK_COLS
    elif w.ndim >= 2:
        lead, rows, cols = 1, total // shape[-1], shape[-1]
    else:
        lead, rows, cols = 1, 1, total
    tr = _tile(rows, 512, 8)

    def body(w_ref, g_ref, m_ref, v_ref, d_ref, nm_ref, nv_ref):
        gv = g_ref[...]
        mn = ADAM_B1 * m_ref[...] + (1.0 - ADAM_B1) * gv
        vn = ADAM_B2 * v_ref[...] + (1.0 - ADAM_B2) * (gv * gv)
        m_hat = mn / (1.0 - ADAM_B1 ** ADAM_STEP)
        v_hat = vn / (1.0 - ADAM_B2 ** ADAM_STEP)
        d_ref[...] = -ADAM_LR * (m_hat / (jnp.sqrt(v_hat) + ADAM_EPS) + ADAM_WD * w_ref[...])
        nm_ref[...] = mn
        nv_ref[...] = vn

    blk = pl.BlockSpec((None, tr, cols), lambda l, i: (l, i, 0))
    out = jax.ShapeDtypeStruct((lead, rows, cols), F32)
    outs = pl.pallas_call(
        body, name=name, out_shape=(out, out, out), grid=(lead, rows // tr),
        in_specs=[blk] * 4, out_specs=(blk, blk, blk), compiler_params=_cparams("parallel", "parallel"),
    )(*[t.reshape(lead, rows, cols) for t in (w, g, m, v)])
    return tuple(o.reshape(shape) for o in outs)


def _all_gather(blocks, *, name):
    n = len(blocks)

    def body(*refs):
        x_refs, out_refs = refs[:n], refs[n:2 * n]
        send_sems, recv_sems, local_sems = refs[2 * n:]
        x, y, c = lax.axis_index("x"), lax.axis_index("y"), lax.axis_index("c")
        me, sibling = (x, y, c), (x, y, 1 - c)
        chips = [(1 - x, y), (x, 1 - y), (1 - x, 1 - y)]

        def slot(a, px, py, pc):
            return out_refs[a].at[4 * px + 2 * py + pc]

        def copy(a, k, block, to, src=None):
            return pltpu.make_async_remote_copy(
                src_ref=slot(a, *block) if src is None else src, dst_ref=slot(a, *block),
                send_sem=send_sems.at[7 * a + k], recv_sem=recv_sems.at[7 * a + k], device_id=to,
                device_id_type=MESH)

        started = []
        for a in range(n):
            mine = pltpu.make_async_copy(x_refs[a], slot(a, *me), local_sems.at[a])
            mine.start()
            started.append(mine)
        sends = []
        for a in range(n):
            first = [copy(a, 0, me, sibling, src=x_refs[a])]
            first += [copy(a, 1 + j, me, (*chip, c), src=x_refs[a]) for j, chip in enumerate(chips)]
            for cp in first:
                cp.start()
            sends += first
        for a in range(n):
            for j, chip in enumerate(chips):
                copy(a, 1 + j, (*chip, c), me).wait_recv()
                onward = copy(a, 4 + j, (*chip, c), sibling)
                onward.start()
                sends.append(onward)
        for a in range(n):
            copy(a, 0, sibling, me).wait_recv()
            for j, chip in enumerate(chips):
                copy(a, 4 + j, (*chip, 1 - c), me).wait_recv()
        for cp in sends:
            cp.wait_send()
        for mine in started:
            mine.wait()

    return pl.pallas_call(
        body, name=name, out_shape=[jax.ShapeDtypeStruct((N_DEV,) + b.shape, b.dtype) for b in blocks],
        in_specs=[HBM] * n, out_specs=[HBM] * n,
        scratch_shapes=[pltpu.SemaphoreType.DMA((7 * n,)), pltpu.SemaphoreType.DMA((7 * n,)),
                        pltpu.SemaphoreType.DMA((n,))],
    )(*blocks)


def _sum8(blocks, *, name, tr=SUM_ROWS):
    _, R, C = blocks.shape
    tr = _tile(R, tr, 16)

    def body(x_ref, o_ref):
        acc = x_ref[0].astype(F32)
        for i in range(1, N_DEV):
            acc = acc + x_ref[i].astype(F32)
        o_ref[...] = acc

    return pl.pallas_call(
        body, name=name, out_shape=jax.ShapeDtypeStruct((R, C), F32), grid=(R // tr,),
        in_specs=[pl.BlockSpec((N_DEV, tr, C), lambda i: (0, i, 0))],
        out_specs=pl.BlockSpec((tr, C), lambda i: (i, 0)), compiler_params=_cparams("parallel"),
    )(blocks)


def _pack(parts):
    flat = jnp.concatenate([p.astype(F32).reshape(-1) for p in parts])
    unit = 8 * PACK_COLS
    padded = -(-flat.size // unit) * unit
    return jnp.pad(flat, (0, padded - flat.size)).reshape(padded // PACK_COLS, PACK_COLS)


def _unpack(buf, like):
    flat, out, off = buf.reshape(-1), [], 0
    for p in like:
        out.append(flat[off:off + p.size].reshape(p.shape))
        off += p.size
    return out


def _ssm_discretise(lre, lim, logdt, bre, bim):
    lam = lax.complex(lre, lim)
    dt = jnp.exp(logdt)[:, None]
    lam_bar = jnp.exp(lam * dt)
    b_bar = ((lam_bar - 1.0) / lam)[:, :, None] * lax.complex(bre, bim)
    return lam_bar.real, lam_bar.imag, b_bar.real, b_bar.imag


def _block_diag(a, rows_first):
    ns, g, r, c = a.shape
    eye = jnp.eye(g, dtype=a.dtype)
    return jnp.einsum("sgrc,gh->sgrhc", a, eye).reshape(ns, g * r, g * c)


def _diag_blocks(m, r, c):
    ns = m.shape[0]
    g = SLAB_GROUPS
    return jnp.einsum("sgrhc,gh->sgrc", m.reshape(ns, g, r, g, c), jnp.eye(g, dtype=m.dtype))


def _to_tm(a, T):
    b, s, w = a.shape
    return a.reshape(b, s // T, T, w).transpose(0, 2, 1, 3)


def _from_tm(a):
    b, t, nc, w = a.shape
    return a.transpose(0, 2, 1, 3).reshape(b, nc * t, w)


WEIGHTS = ["norm_mix", "w_in", "b_forget", "ssm_lambda_re", "ssm_lambda_im", "ssm_log_dt", "ssm_b_re", "ssm_b_im",
           "ssm_c_re", "ssm_c_im", "ssm_d", "w_glu", "b_glu", "w_branch_a", "w_branch_b", "w_out", "norm_mlp",
           "w_mlp_up", "w_mlp_down", "norm_final"]
SHARDED = {"w_in": 2, "w_glu": 1, "w_branch_a": 2, "w_branch_b": 2, "w_out": 1, "w_mlp_up": 2, "w_mlp_down": 1}


REST = [n for n in SHARDED if n != "w_in"]


def _whole(n, seg):
    ax = SHARDED[n] - 1
    shp = seg.shape[1:]
    return jnp.moveaxis(seg, 0, ax).reshape(shp[:ax] + (N_DEV * shp[ax],) + shp[ax + 1:])


def _blocks(n, g):
    ax = SHARDED[n] - 1
    shp = g.shape
    return jnp.moveaxis(g.reshape(shp[:ax] + (N_DEV, shp[ax] // N_DEV) + shp[ax + 1:]), ax, 0)


def _sum_blocks(n, got):
    return _sum8(got.reshape(N_DEV, -1, got.shape[-1]), name="sum_grads_" + n).reshape(got.shape[1:])


def kernel(x, norm_mix, w_in, b_forget, ssm_lambda_re, ssm_lambda_im, ssm_log_dt, ssm_b_re, ssm_b_im, ssm_c_re, ssm_c_im, ssm_d, w_glu, b_glu, w_branch_a, w_branch_b, w_out, norm_mlp, w_mlp_up, w_mlp_down, norm_final, loss_target, m_norm_mix, m_w_in, m_b_forget, m_ssm_lambda_re, m_ssm_lambda_im, m_ssm_log_dt, m_ssm_b_re, m_ssm_b_im, m_ssm_c_re, m_ssm_c_im, m_ssm_d, m_w_glu, m_b_glu, m_w_branch_a, m_w_branch_b, m_w_out, m_norm_mlp, m_w_mlp_up, m_w_mlp_down, m_norm_final, v_norm_mix, v_w_in, v_b_forget, v_ssm_lambda_re, v_ssm_lambda_im, v_ssm_log_dt, v_ssm_b_re, v_ssm_b_im, v_ssm_c_re, v_ssm_c_im, v_ssm_d, v_w_glu, v_b_glu, v_w_branch_a, v_w_branch_b, v_w_out, v_norm_mlp, v_w_mlp_up, v_w_mlp_down, v_norm_final):
    args = dict(locals())
    w = {n: args[n] for n in WEIGHTS}
    Bl, S, D = x.shape
    L, H = b_forget.shape
    G, P, C = ssm_b_re.shape[1:]
    AW, W, HP = H * HEAD_DIM, G * C, H // 2
    N = Bl * S
    T = SSM_CHUNK
    NS = G // SLAB_GROUPS
    SP = SLAB_GROUPS * P
    tq = min(ATTN_BLOCK, S)
    nq = S // tq
    u_off = 3 * AW
    gate_blk = (u_off + W) // D
    assert (u_off + W) % D == 0 and W % LANES == 0 and AW % LANES == 0 and S % T == 0

    shard = {n: w[n].astype(BF16) for n in SHARDED}
    weights = [dict() for _ in range(L)]
    weights[0]["w_in"] = _whole("w_in", _all_gather([shard["w_in"][0]], name="gather_first")[0])
    tr_ = lambda a: jnp.swapaxes(a, 1, 2)

    ssm = []
    for l in range(L):
        disc, disc_vjp = jax.vjp(_ssm_discretise, ssm_lambda_re[l], ssm_lambda_im[l], ssm_log_dt[l],
                                 ssm_b_re[l], ssm_b_im[l])
        lbr, lbi, bbr, bbi = disc
        z = lax.complex(ssm_lambda_re[l], ssm_lambda_im[l]) * jnp.exp(ssm_log_dt[l])[:, None]
        powers = jnp.exp(z[None] * jnp.arange(1, T + 1, dtype=F32)[:, None, None])
        slabs = lambda a: a.reshape(NS, SP)
        lam = jnp.stack([slabs(lbr), slabs(lbi)], axis=1)
        lam_t = jnp.stack([slabs(powers[T - 1].real), slabs(powers[T - 1].imag)], axis=1)
        pw = jnp.stack([powers.real.reshape(T, NS, SP), powers.imag.reshape(T, NS, SP)], axis=0).transpose(2, 0, 1, 3)
        to_rows = lambda a: jnp.swapaxes(a.reshape(NS, SLAB_GROUPS, P, C), 2, 3)
        bmat = jnp.concatenate([_block_diag(to_rows(bbr), True), _block_diag(to_rows(bbi), True)], axis=2)
        cre = ssm_c_re[l].reshape(NS, SLAB_GROUPS, C, P)
        cim = ssm_c_im[l].reshape(NS, SLAB_GROUPS, C, P)
        cmat_t = jnp.concatenate([_block_diag(cre, True), -_block_diag(cim, True)], axis=2)
        ssm.append(dict(vjp=disc_vjp, lam=lam, lam_t=lam_t, pw=pw, bmat=bmat.astype(BF16),
                        bmat_t=tr_(bmat).astype(BF16), cmat=tr_(cmat_t).astype(BF16), cmat_t=cmat_t.astype(BF16),
                        d=ssm_d[l].reshape(1, W)))

    xcur = x.reshape(N, D)
    saved = []
    for l in range(L):
        s_, wl = ssm[l], weights[l]
        win = wl["w_in"]
        wl["wcat"] = jnp.concatenate([win[:, :3 * AW], win[:, 3 * AW + H:]], axis=1)
        wl["wf"] = jnp.pad(win[:, 3 * AW:3 * AW + H], ((0, 0), (0, LANES - H)))
        h, r0 = _rmsnorm_fwd(xcur, norm_mix[l], name="norm_mix_fwd")
        proj = _mm(h, wl["wcat"], name="in_proj", tm=WIDE_N, tn=WIDE_N)
        fl = _mm(h, wl["wf"], name="forget_proj", out_dtype=F32)
        ft = fl[:, :H].reshape(Bl, S, H).transpose(0, 2, 1)
        F = _fox_gate_fwd(ft, b_forget[l], name="forget_gate_fwd")
        frow = F.reshape(Bl, HP, 2, nq, tq)
        proj3 = proj.reshape(Bl, S, -1)
        coming = [shard[n][l] for n in REST] + ([shard["w_in"][l + 1]] if l + 1 < L else [])
        ya, lse, got = _attn_fwd(proj3, frow, name="attn_fwd" if l + 1 < L else "attn_fwd_last", H=H, tq=tq,
                                 hosted=_Hosted(gather=coming))
        for n, seg in zip(REST, got):
            wl[n] = _whole(n, seg)
        if l + 1 < L:
            weights[l + 1]["w_in"] = _whole("w_in", got[-1])
        u_tm = _to_tm(proj3[:, :, u_off:u_off + W], T)
        ys = _from_tm(_ssm_fwd(u_tm, s_["bmat"], s_["cmat"], s_["lam"], s_["pw"], s_["lam_t"], s_["d"],
                               name="ssm_fwd")).reshape(N, W)
        yb2 = _glu_fwd(ys, wl["w_glu"], b_glu[l], name="glu_fwd")
        ya2 = ya.reshape(N, AW)
        mixed, pa, pb = _merge_fwd(ya2, yb2, wl["w_branch_a"], wl["w_branch_b"], proj, gate_blk, name="merge_fwd")
        x1, h2, r1 = _outproj_fwd(mixed, wl["w_out"], xcur, norm_mlp[l], name="out_proj")
        a = _mm(h2, wl["w_mlp_up"], name="mlp_up", tm=WIDE_N, tn=WIDE_N)
        x2 = _mm(a, wl["w_mlp_down"], name="mlp_down", a_fn=_relu_sq, epi=lambda acc, res: acc + res,
                 extras=(x1,), out_dtype=F32, tk=LONG_K)
        saved.append(dict(x0=xcur, h=h, r0=r0, proj=proj, ft=ft, frow=frow, ya=ya, lse=lse, u_tm=u_tm,
                          ys=ys, yb2=yb2, mixed=mixed, pa=pa, pb=pb, x1=x1, h2=h2, r1=r1, a=a))
        xcur = x2

    dx, g_final, loss_row = _loss_head(xcur, norm_final, loss_target.reshape(N, D), name="loss_head")
    loss = lax.psum(loss_row[0, 0], MESH_AXES)

    big = {n: [None] * L for n in SHARDED}
    small = {n: [None] * L for n in WEIGHTS if n not in SHARDED and n != "norm_final"}
    small_sums = [None] * L
    win_grad = small_above = None
    for l in reversed(range(L)):
        sv, s_, wl = saved[l], ssm[l], weights[l]
        a = sv["a"]
        gw = {}
        d_a = _mm(dx, wl["w_mlp_down"], name="mlp_down_dx", tb=True, tn=WIDE_N,
                  epi=lambda acc, av: acc * (2.0 * jnp.maximum(av.astype(F32), 0.0)), extras=(a,))
        gw["w_mlp_down"] = _mm(a, dx, name="mlp_down_dw", ta=True, a_fn=_relu_sq, tk=LONG_K)
        gw["w_mlp_up"] = _mm(sv["h2"], d_a, name="mlp_up_dw", ta=True, tk=LONG_K)
        dh2 = _mm(d_a, wl["w_mlp_up"], name="mlp_up_dx", tb=True, out_dtype=F32, tk=LONG_K)
        dx1, g = _rmsnorm_bwd(dh2, sv["x1"], sv["r1"], norm_mlp[l], dx, name="norm_mlp_bwd")
        small["norm_mlp"][l] = g[0]
        dmix = _mm(dx1, wl["w_out"], name="out_proj_dx", tb=True)
        gw["w_out"] = _mm(sv["mixed"], dx1, name="out_proj_dw", ta=True, tk=LONG_K)
        ncat = wl["wcat"].shape[1]
        dpa, dpb, dproj = _merge_bwd(dmix, sv["proj"], sv["pa"], sv["pb"], gate_blk, ncat + LANES, name="merge_bwd")
        ya2 = sv["ya"].reshape(N, AW)
        gw["w_branch_a"] = _mm(ya2, dpa, name="branch_a_dw", ta=True, tk=LONG_K)
        dya = _mm(dpa, wl["w_branch_a"], name="branch_a_dx", tb=True)
        gw["w_branch_b"] = _mm(sv["yb2"], dpb, name="branch_b_dw", ta=True, tk=LONG_K)
        dyb2 = _mm(dpb, wl["w_branch_b"], name="branch_b_dx", tb=True)
        dys, dz, yb, g = _glu_bwd(sv["ys"], dyb2, wl["w_glu"], wl["w_glu"].T, b_glu[l], name="glu_bwd")
        small["b_glu"][l] = g[0]
        gw["w_glu"] = _mm(yb, dz, name="glu_dw", ta=True, tk=LONG_K)

        du_tm, g_bt, g_ct, g_lam, g_d = _ssm_bwd(
            sv["u_tm"], _to_tm(dys.reshape(Bl, S, W), T), s_["bmat"], s_["bmat_t"], s_["cmat_t"], s_["lam"],
            s_["pw"], s_["lam_t"], s_["d"], name="ssm_bwd")
        du = _from_tm(du_tm).reshape(N, W)
        g_b = _diag_blocks(jnp.swapaxes(g_bt, 1, 2).reshape(NS, LANES, 2, SP).transpose(2, 0, 1, 3).reshape(
            2 * NS, LANES, SP), C, P).reshape(2, G, C, P)
        g_bbar = jnp.swapaxes(g_b, 2, 3)
        g_c = _diag_blocks(g_ct.reshape(NS, LANES, 2, SP).transpose(2, 0, 1, 3).reshape(2 * NS, LANES, SP),
                           C, P).reshape(2, G, C, P)
        g_lbar = g_lam.transpose(1, 0, 2).reshape(2, G, P)
        g_lre, g_lim, g_ldt, g_bre, g_bim = s_["vjp"]((g_lbar[0], g_lbar[1], g_bbar[0], g_bbar[1]))
        small["ssm_lambda_re"][l], small["ssm_lambda_im"][l], small["ssm_log_dt"][l] = g_lre, g_lim, g_ldt
        small["ssm_b_re"][l], small["ssm_b_im"][l] = g_bre, g_bim
        small["ssm_c_re"][l], small["ssm_c_im"][l] = g_c[0], -g_c[1]
        small["ssm_d"][l] = g_d.reshape(W)

        proj3 = sv["proj"].reshape(Bl, S, -1)
        leaving = [_blocks(n, gw[n]) for n in REST] + ([_blocks("w_in", win_grad)] if l + 1 < L else [])
        (dq, dk, dv, dfk, dfq), got = _attn_bwd(
            proj3, sv["ya"], dya.reshape(Bl, S, AW), sv["lse"], sv["frow"],
            name="attn_bwd" if l + 1 < L else "attn_bwd_top", H=H, tq=tq,
            hosted=_Hosted(gather=[small_above] if l + 1 < L else [], exchange=leaving))
        if l + 1 < L:
            small_sums[l + 1] = _sum8(got[0], name="sum_small_grads")
            big["w_in"][l + 1] = _sum_blocks("w_in", got[-1])
            got = got[1:]
        for n, blocks in zip(REST, got):
            big[n][l] = _sum_blocks(n, blocks)
        dF = dfk.reshape(Bl, H, S) + dfq.transpose(0, 1, 3, 2).reshape(Bl, H, S)
        dft, g = _fox_gate_bwd(dF, sv["ft"], b_forget[l], name="forget_gate_bwd")
        small["b_forget"][l] = g[:, 0]
        dfl = jnp.pad(dft.transpose(0, 2, 1).reshape(N, H), ((0, 0), (0, LANES - H))).astype(BF16)
        for off, piece in ((0, dq.reshape(N, AW)), (AW, dk.reshape(N, AW)), (2 * AW, dv.reshape(N, AW)), (u_off, du),
                           (ncat, dfl)):
            dproj = lax.dynamic_update_slice(dproj, piece, (0, off))
        gcat = _mm(sv["h"], dproj, name="in_proj_dw", ta=True, tn=1408, tk=LONG_K)
        ncat = wl["wcat"].shape[1]
        win_grad = jnp.concatenate([gcat[:, :3 * AW], gcat[:, ncat:ncat + H], gcat[:, 3 * AW:ncat]], axis=1)
        wfull = jnp.concatenate([wl["wcat"], wl["wf"]], axis=1)
        if l > 0:
            dh = _mm(dproj, wfull, name="in_proj_dx", tb=True, out_dtype=F32, tk=1408)
        else:
            dh, got = _mm(dproj, wfull, name="in_proj_dx_bottom", tb=True, out_dtype=F32, tk=1408,
                          hosted=_Hosted(exchange=[_blocks("w_in", win_grad)]))
            big["w_in"][0] = _sum_blocks("w_in", got[0])
        dx, g = _rmsnorm_bwd(dh, sv["x0"], sv["r0"], norm_mix[l], dx1, name="norm_mix_bwd")
        small["norm_mix"][l] = g[0]
        small_above = _pack([small[n][l] for n in small])

    last = [small[n][0] for n in small] + [g_final[0]]
    small_sums[0] = _sum8(_all_gather([_pack(last)], name="gather_small_grads")[0], name="sum_small_grads_last")
    grads = {n: jnp.stack(big[n]) for n in SHARDED}
    per_layer = [_unpack(small_sums[l], last if l == 0 else last[:-1]) for l in range(L)]
    for i, n in enumerate(small):
        grads[n] = jnp.stack([per_layer[l][i] for l in range(L)])
    grads["norm_final"] = per_layer[0][-1]

    deltas, new_m, new_v = {}, {}, {}
    for n in WEIGHTS:
        deltas[n], new_m[n], new_v[n] = _adamw(w[n], grads[n], args["m_" + n], args["v_" + n], name="adamw_" + n)
    return (loss, dx.reshape(Bl, S, D), *[grads[n] for n in WEIGHTS], *[deltas[n] for n in WEIGHTS],
            *[new_m[n] for n in WEIGHTS], *[new_v[n] for n in WEIGHTS])
```

```python
import functools

import jax
import jax.numpy as jnp
from jax import lax
from jax.experimental import pallas as pl
from jax.experimental.pallas import tpu as pltpu

F32 = jnp.float32
BF16 = jnp.bfloat16

N_DEV = 8
HEAD_DIM = 64
LANES = 128
SSM_CHUNK = 32
SLAB_GROUPS = 8
ATTN_BLOCK = 512
LONG_K = 2048
WIDE_N = 2048
PACK_COLS = 1024
SUM_ROWS = 256
RMS_EPS = 1e-6
VMEM_LIMIT = 56 * 1024 * 1024
ADAM_LR, ADAM_B1, ADAM_B2, ADAM_EPS, ADAM_WD, ADAM_STEP = 0.001, 0.9, 0.999, 1e-08, 0.01, 10
MESH_AXES = ("x", "y", "c")
NEG = -1e30
NT = (((1,), (1,)), ((), ()))
TN = (((0,), (0,)), ((), ()))


def _cparams(*sem):
    return pltpu.CompilerParams(dimension_semantics=sem, vmem_limit_bytes=VMEM_LIMIT)


def _tile(dim, pref, unit=LANES):
    if dim <= pref:
        return dim
    best = None
    for t in range(unit, pref + 1, unit):
        if dim % t == 0:
            best = t
    assert best is not None, (dim, pref)
    return best


def _mm(a, b, *, name, ta=False, tb=False, a_fn=None, epi=None, extras=(), out_dtype=BF16, tm=1024, tn=1024,
        tk=1024, hosted=None):
    if ta:
        K, M = a.shape
    else:
        M, K = a.shape
    N, Kb = b.shape if tb else b.shape[::-1]
    assert K == Kb and not (ta and tb), (a.shape, b.shape)
    tm, tn, tk = _tile(M, tm), _tile(N, tn), _tile(K, tk)
    gm, gn, nk = M // tm, N // tn, K // tk
    ne = len(extras)
    nh = hosted.n if hosted else 0
    n_acc = 1 if nk > 1 else 0

    def body(a_ref, b_ref, *rest):
        e_refs, o_ref = rest[:ne], rest[ne + nh]
        acc_ref = rest[ne + 2 * nh + 1] if nk > 1 else None
        k = pl.program_id(2)
        if hosted:
            i, j = pl.program_id(0), pl.program_id(1)
            start, finish = hosted.run((i == 0) & (j == 0) & (k == 0), (i == gm - 1) & (j == gn - 1) & (k == nk - 1),
                                       rest[ne:ne + nh], rest[ne + nh + 1:ne + 2 * nh + 1],
                                       rest[ne + 2 * nh + 1 + n_acc:])
            start()
        av = a_ref[...]
        if a_fn is not None:
            av = a_fn(av)
        av = av.astype(BF16)
        bv = b_ref[...].astype(BF16)
        dims = TN if ta else NT if tb else (((1,), (0,)), ((), ()))
        part = lax.dot_general(av, bv, dims, preferred_element_type=F32)

        def finish_tile(r):
            if epi is not None:
                r = epi(r, *[e[...] for e in e_refs])
            o_ref[...] = r.astype(o_ref.dtype)

        if nk == 1:
            finish_tile(part)
        else:
            @pl.when(k == 0)
            def _():
                acc_ref[...] = part

            @pl.when(k > 0)
            def _():
                acc_ref[...] += part

            @pl.when(k == nk - 1)
            def _():
                finish_tile(acc_ref[...])

        if hosted:
            finish()

    a_spec = pl.BlockSpec((tk, tm), lambda i, j, k: (k, i)) if ta else pl.BlockSpec((tm, tk), lambda i, j, k: (i, k))
    outs = pl.pallas_call(
        body, name=name,
        out_shape=[jax.ShapeDtypeStruct((M, N), out_dtype)] + (hosted.out_shape if hosted else []),
        grid=(gm, gn, nk),
        in_specs=[a_spec, pl.BlockSpec((tn, tk), lambda i, j, k: (j, k)) if tb
                  else pl.BlockSpec((tk, tn), lambda i, j, k: (k, j))]
        + [pl.BlockSpec((tm, tn), lambda i, j, k: (i, j)) for _ in extras] + [HBM] * nh,
        out_specs=[pl.BlockSpec((tm, tn), lambda i, j, k: (i, j))] + [HBM] * nh,
        scratch_shapes=([pltpu.VMEM((tm, tn), F32)] if nk > 1 else []) + (hosted.scratch if hosted else []),
        compiler_params=_cparams(*(("arbitrary",) * 3 if hosted else ("parallel", "parallel", "arbitrary"))),
    )(a, b, *extras, *(hosted.arrays if hosted else []))
    return (outs[0], outs[1:]) if hosted else outs[0]


def _relu_sq(v):
    r = jnp.maximum(v.astype(F32), 0.0)
    return r * r


def _sigmoid(v):
    return 1.0 / (1.0 + jnp.exp(-v))


GELU_C = 0.7978845608028654
GELU_A = 0.044715


def _gelu(v):
    return 0.5 * v * (1.0 + jnp.tanh(GELU_C * (v + GELU_A * v * v * v)))


def _gelu_grad(v):
    t = jnp.tanh(GELU_C * (v + GELU_A * v * v * v))
    return 0.5 * (1.0 + t) + 0.5 * v * (1.0 - t * t) * GELU_C * (1.0 + 3.0 * GELU_A * v * v)


def _rmsnorm_fwd(x, g, *, name, tr=512):
    n, d = x.shape
    tr = _tile(n, tr, 8)

    def body(x_ref, g_ref, h_ref, r_ref):
        xv = x_ref[...]
        r = lax.rsqrt(jnp.mean(xv * xv, axis=-1, keepdims=True) + RMS_EPS)
        h_ref[...] = (xv * r * g_ref[...]).astype(BF16)
        r_ref[...] = r

    return pl.pallas_call(
        body, name=name,
        out_shape=(jax.ShapeDtypeStruct((n, d), BF16), jax.ShapeDtypeStruct((n, 1), F32)),
        grid=(n // tr,),
        in_specs=[pl.BlockSpec((tr, d), lambda i: (i, 0)), pl.BlockSpec((1, d), lambda i: (0, 0))],
        out_specs=(pl.BlockSpec((tr, d), lambda i: (i, 0)), pl.BlockSpec((tr, 1), lambda i: (i, 0))),
        compiler_params=_cparams("parallel"),
    )(x, g.reshape(1, d))


def _norm_bwd_tile(dh, x, r, g, dres):
    xh = x * r
    dxh = dh * g
    m = jnp.mean(dxh * xh, axis=-1, keepdims=True)
    return r * (dxh - xh * m) + dres, jnp.sum(dh * xh, axis=0, keepdims=True)


def _mm_norm_bwd(a, w, x, r, g, dres, *, name, tm=1024, tk=1024):
    M, K = a.shape
    D = w.shape[0]
    tm, tk = _tile(M, tm, 8), _tile(K, tk)
    nk = K // tk

    def body(a_ref, w_ref, x_ref, r_ref, g_ref, dres_ref, dx_ref, dg_ref, acc_ref):
        i, k = pl.program_id(0), pl.program_id(1)
        part = lax.dot_general(a_ref[...].astype(BF16), w_ref[...], NT, preferred_element_type=F32)

        @pl.when(k == 0)
        def _():
            acc_ref[...] = part

        @pl.when(k > 0)
        def _():
            acc_ref[...] += part

        @pl.when(k == nk - 1)
        def _():
            dx, dg = _norm_bwd_tile(acc_ref[...], x_ref[...], r_ref[...], g_ref[...], dres_ref[...])
            dx_ref[...] = dx

            @pl.when(i == 0)
            def _():
                dg_ref[...] = dg

            @pl.when(i > 0)
            def _():
                dg_ref[...] += dg

    row = pl.BlockSpec((tm, D), lambda i, k: (i, 0))
    vec = pl.BlockSpec((1, D), lambda i, k: (0, 0))
    return pl.pallas_call(
        body, name=name,
        out_shape=(jax.ShapeDtypeStruct((M, D), F32), jax.ShapeDtypeStruct((1, D), F32)),
        grid=(M // tm, nk),
        in_specs=[pl.BlockSpec((tm, tk), lambda i, k: (i, k)), pl.BlockSpec((D, tk), lambda i, k: (0, k)),
                  row, pl.BlockSpec((tm, 1), lambda i, k: (i, 0)), vec, row],
        out_specs=(row, vec),
        scratch_shapes=[pltpu.VMEM((tm, D), F32)],
        compiler_params=_cparams("arbitrary", "arbitrary"),
    )(a, w, x, r, g.reshape(1, D), dres)


def _rmsnorm_bwd(dh, x, r, g, dres, *, name, tr=512):
    n, d = x.shape
    tr = _tile(n, tr, 8)

    def body(dh_ref, x_ref, r_ref, g_ref, dres_ref, dx_ref, dg_ref):
        i = pl.program_id(0)
        dx_ref[...], part = _norm_bwd_tile(dh_ref[...].astype(F32), x_ref[...], r_ref[...], g_ref[...],
                                           dres_ref[...])

        @pl.when(i == 0)
        def _():
            dg_ref[...] = part

        @pl.when(i > 0)
        def _():
            dg_ref[...] += part

    row = pl.BlockSpec((tr, d), lambda i: (i, 0))
    vec = pl.BlockSpec((1, d), lambda i: (0, 0))
    return pl.pallas_call(
        body, name=name,
        out_shape=(jax.ShapeDtypeStruct((n, d), F32), jax.ShapeDtypeStruct((1, d), F32)),
        grid=(n // tr,),
        in_specs=[row, row, pl.BlockSpec((tr, 1), lambda i: (i, 0)), vec, row],
        out_specs=(row, vec),
        compiler_params=_cparams("arbitrary"),
    )(dh, x, r, g.reshape(1, d), dres)


def _loss_head(x, g, target, *, name, tr=512):
    n, d = x.shape
    tr = _tile(n, tr, 8)

    def body(x_ref, g_ref, t_ref, dx_ref, dg_ref, loss_ref):
        i = pl.program_id(0)
        xv = x_ref[...]
        gv = g_ref[...]
        r = lax.rsqrt(jnp.mean(xv * xv, axis=-1, keepdims=True) + RMS_EPS)
        xh = xv * r
        err = xh * gv - t_ref[...]
        lpart = 0.5 * jnp.sum(jnp.mean(err * err, axis=-1, keepdims=True), axis=0, keepdims=True)
        dy = err * (1.0 / d)
        dxh = dy * gv
        m = jnp.mean(dxh * xh, axis=-1, keepdims=True)
        dx_ref[...] = r * (dxh - xh * m)
        gpart = jnp.sum(dy * xh, axis=0, keepdims=True)
        lrow = jnp.broadcast_to(lpart, (1, LANES))

        @pl.when(i == 0)
        def _():
            dg_ref[...] = gpart
            loss_ref[...] = lrow

        @pl.when(i > 0)
        def _():
            dg_ref[...] += gpart
            loss_ref[...] += lrow

    row = pl.BlockSpec((tr, d), lambda i: (i, 0))
    vec = pl.BlockSpec((1, d), lambda i: (0, 0))
    return pl.pallas_call(
        body, name=name,
        out_shape=(jax.ShapeDtypeStruct((n, d), F32), jax.ShapeDtypeStruct((1, d), F32),
                   jax.ShapeDtypeStruct((1, LANES), F32)),
        grid=(n // tr,),
        in_specs=[row, vec, row],
        out_specs=(row, vec, pl.BlockSpec((1, LANES), lambda i: (0, 0))),
        compiler_params=_cparams("arbitrary"),
    )(x, g.reshape(1, d), target)


def _tri_dot(v, tri):
    hi = v.astype(BF16)
    r1 = v - hi.astype(F32)
    mid = r1.astype(BF16)
    lo = (r1 - mid.astype(F32)).astype(BF16)
    d = functools.partial(jnp.dot, preferred_element_type=F32)
    return d(hi, tri) + d(mid, tri) + d(lo, tri)


def _fox_gate_fwd(ft, bf, *, name, blk=256):
    B, H, S = ft.shape
    blk = _tile(S, blk)
    nb = S // blk

    def body(f_ref, b_ref, o_ref):
        x = f_ref[0] + b_ref[...]
        logf = jnp.minimum(x, 0.0) - jnp.log(1.0 + jnp.exp(-jnp.abs(x)))
        rr = lax.broadcasted_iota(jnp.int32, (blk, blk), 0)
        cc = lax.broadcasted_iota(jnp.int32, (blk, blk), 1)
        tri = (rr <= cc).astype(BF16)
        carry = jnp.zeros((H, 1), F32)
        for n in range(nb):
            c = _tri_dot(logf[:, n * blk:(n + 1) * blk], tri) + carry
            o_ref[0, :, n * blk:(n + 1) * blk] = c
            carry = c[:, blk - 1:blk]

    return pl.pallas_call(
        body, name=name,
        out_shape=jax.ShapeDtypeStruct((B, H, S), F32),
        grid=(B,),
        in_specs=[pl.BlockSpec((1, H, S), lambda b: (b, 0, 0)), pl.BlockSpec((H, 1), lambda b: (0, 0))],
        out_specs=pl.BlockSpec((1, H, S), lambda b: (b, 0, 0)),
        compiler_params=_cparams("parallel"),
    )(ft, bf.reshape(H, 1))


def _fox_gate_bwd(dF, ft, bf, *, name, blk=256):
    B, H, S = ft.shape
    blk = _tile(S, blk)
    nb = S // blk

    def body(d_ref, f_ref, b_ref, o_ref, db_ref):
        b = pl.program_id(0)
        x = f_ref[0] + b_ref[...]
        sneg = 1.0 / (1.0 + jnp.exp(x))
        dv = d_ref[0]
        rr = lax.broadcasted_iota(jnp.int32, (blk, blk), 0)
        cc = lax.broadcasted_iota(jnp.int32, (blk, blk), 1)
        tri = (rr >= cc).astype(BF16)
        carry = jnp.zeros((H, 1), F32)
        tot = jnp.zeros((H, 1), F32)
        for n in reversed(range(nb)):
            sl = slice(n * blk, (n + 1) * blk)
            c = _tri_dot(dv[:, sl], tri) + carry
            g = c * sneg[:, sl]
            o_ref[0, :, sl] = g
            tot = tot + jnp.sum(g, axis=1, keepdims=True)
            carry = c[:, 0:1]

        @pl.when(b == 0)
        def _():
            db_ref[...] = tot

        @pl.when(b > 0)
        def _():
            db_ref[...] += tot

    blkspec = pl.BlockSpec((1, H, S), lambda b: (b, 0, 0))
    return pl.pallas_call(
        body, name=name,
        out_shape=(jax.ShapeDtypeStruct((B, H, S), F32), jax.ShapeDtypeStruct((H, 1), F32)),
        grid=(B,),
        in_specs=[blkspec, blkspec, pl.BlockSpec((H, 1), lambda b: (0, 0))],
        out_specs=(blkspec, pl.BlockSpec((H, 1), lambda b: (0, 0))),
        compiler_params=_cparams("arbitrary"),
    )(dF, ft, bf.reshape(H, 1))


def _head_masks():
    lane = lax.broadcasted_iota(jnp.int32, (1, LANES), 1)
    return [lane < HEAD_DIM, lane >= HEAD_DIM]


HBM = pl.BlockSpec(memory_space=pltpu.HBM)
MESH = pl.DeviceIdType.MESH


def _direct_copies(kinds, x_refs, out_refs, send_sems, recv_sems, local_sems):
    x, y, c = lax.axis_index("x"), lax.axis_index("y"), lax.axis_index("c")
    me = 4 * x + 2 * y + c
    copies = []
    for a, (kind, xr, outr) in enumerate(zip(kinds, x_refs, out_refs)):
        copies.append(pltpu.make_async_copy(xr if kind == "gather" else xr.at[me], outr.at[me], local_sems.at[a]))
    for k in range(1, N_DEV):
        px = 1 - x if (k >> 2) & 1 else x
        py = 1 - y if (k >> 1) & 1 else y
        pc = 1 - c if k & 1 else c
        for a, (kind, xr, outr) in enumerate(zip(kinds, x_refs, out_refs)):
            copies.append(pltpu.make_async_remote_copy(
                src_ref=xr if kind == "gather" else xr.at[4 * px + 2 * py + pc], dst_ref=outr.at[me],
                send_sem=send_sems.at[7 * a + k - 1], recv_sem=recv_sems.at[7 * a + k - 1],
                device_id=(px, py, pc), device_id_type=MESH))
    return copies


class _Hosted:
    def __init__(self, gather=(), exchange=()):
        self.arrays = list(gather) + list(exchange)
        self.kinds = ["gather"] * len(gather) + ["exchange"] * len(exchange)
        self.n = len(self.arrays)
        self.out_shape = [jax.ShapeDtypeStruct(((N_DEV,) if k == "gather" else ()) + a.shape, a.dtype)
                          for k, a in zip(self.kinds, self.arrays)]
        self.scratch = [pltpu.SemaphoreType.DMA((7 * self.n,)), pltpu.SemaphoreType.DMA((7 * self.n,)),
                        pltpu.SemaphoreType.DMA((self.n,))]

    def run(self, first, last, x_refs, out_refs, sems):
        def go(when, act):
            @pl.when(when)
            def _():
                for cp in _direct_copies(self.kinds, x_refs, out_refs, *sems):
                    act(cp)
        return (lambda: go(first, lambda cp: cp.start())), (lambda: go(last, lambda cp: cp.wait()))


def _stack_heads(x, masks):
    zero = jnp.zeros_like(x)
    return jnp.concatenate([jnp.where(masks[0], x, zero), jnp.where(masks[1], x, zero)], axis=0)


def _attn_fwd(proj, frow, *, name, H, tq, hosted=None):
    B, S, _ = proj.shape
    HP = H // 2
    nq = S // tq
    scale = HEAD_DIM ** -0.5
    nh = hosted.n if hosted else 0

    def body(*refs):
        q_ref, k_ref, v_ref, fk_ref = refs[:4]
        o_ref, lse_ref = refs[4 + nh:6 + nh]
        i = pl.program_id(2)
        if hosted:
            b, hp = pl.program_id(0), pl.program_id(1)
            start, finish = hosted.run((b == 0) & (hp == 0) & (i == 0), (b == B - 1) & (hp == HP - 1) & (i == nq - 1),
                                       refs[4:4 + nh], refs[6 + nh:6 + 2 * nh], refs[6 + 2 * nh:])
            start()
        masks = _head_masks()
        q2 = _stack_heads(q_ref[0], masks) * jnp.asarray(scale, BF16)
        rr = lax.broadcasted_iota(jnp.int32, (tq, tq), 0)
        cc = lax.broadcasted_iota(jnp.int32, (tq, tq), 1)
        causal = rr >= cc

        def block(j, carry, masked):
            rows = pl.ds(pl.multiple_of(j * tq, tq), tq)
            kj = k_ref[0, rows, :]
            vj = v_ref[0, rows, :]
            s2 = lax.dot_general(q2, kj, NT, preferred_element_type=F32)
            new, ps = [], []
            for h in range(2):
                m, l, acc = carry[h]
                s = s2[h * tq:(h + 1) * tq] - fk_ref[0, 0, h, pl.ds(j, 1), :]
                if masked:
                    s = jnp.where(causal, s, NEG)
                m_new = jnp.maximum(m, jnp.max(s, axis=-1, keepdims=True))
                alpha = jnp.exp(m - m_new)
                p = jnp.exp(s - m_new)
                new.append((m_new, alpha * l + jnp.sum(p, axis=-1, keepdims=True), alpha, acc))
                ps.append(p.astype(BF16))
            pv = jnp.dot(jnp.concatenate(ps, axis=0), vj, preferred_element_type=F32)
            return tuple((m, l, alpha * acc + pv[h * tq:(h + 1) * tq]) for h, (m, l, alpha, acc) in enumerate(new))

        one = (jnp.full((tq, 1), NEG, F32), jnp.zeros((tq, 1), F32), jnp.zeros((tq, LANES), F32))
        carry = lax.fori_loop(0, i, lambda j, c: block(j, c, False), (one, one))
        (m0, l0, a0), (m1, l1, a1) = block(i, carry, True)
        o_ref[0] = jnp.where(masks[0], a0 / l0, a1 / l1).astype(BF16)
        two = lax.broadcasted_iota(jnp.int32, (1, 2), 1)
        lse_ref[0, 0] = jnp.where(two == 0, m0 + jnp.log(l0), m1 + jnp.log(l1))
        if hosted:
            finish()

    kv = lambda off: pl.BlockSpec((1, S, LANES), lambda b, hp, i: (b, 0, off + hp))
    outs = pl.pallas_call(
        body, name=name,
        out_shape=[jax.ShapeDtypeStruct((B, S, H * HEAD_DIM), BF16), jax.ShapeDtypeStruct((B, HP, S, 2), F32)]
        + (hosted.out_shape if hosted else []),
        grid=(B, HP, nq),
        in_specs=[pl.BlockSpec((1, tq, LANES), lambda b, hp, i: (b, i, hp)), kv(HP), kv(2 * HP),
                  pl.BlockSpec((1, 1, 2, nq, tq), lambda b, hp, i: (b, hp, 0, 0, 0))] + [HBM] * nh,
        out_specs=[pl.BlockSpec((1, tq, LANES), lambda b, hp, i: (b, i, hp)),
                   pl.BlockSpec((1, 1, tq, 2), lambda b, hp, i: (b, hp, i, 0))] + [HBM] * nh,
        scratch_shapes=hosted.scratch if hosted else [],
        compiler_params=_cparams("arbitrary", "arbitrary", "arbitrary"),
    )(proj, proj, proj, frow, *(hosted.arrays if hosted else []))
    return outs[0], outs[1], outs[2:]


def _attn_bwd(proj, ya, dya, lse, frow, *, name, H, tq, hosted=None):
    B, S, _ = proj.shape
    HP = H // 2
    nq = S // tq
    AW = H * HEAD_DIM
    scale = HEAD_DIM ** -0.5
    nh = hosted.n if hosted else 0

    def body(*refs):
        q_ref, k_ref, v_ref, o_ref, do_ref, lse_ref, fk_ref = refs[:7]
        dq_ref, dk_ref, dv_ref, dfk_ref, dfq_ref = refs[7 + nh:12 + nh]
        (q2_ref, do2_ref, lse2_ref, delta2_ref, dq2_acc, dfq2_acc, dk_acc, dv_acc,
         dfk_acc) = refs[12 + 2 * nh:21 + 2 * nh]
        if hosted:
            b, hp = pl.program_id(0), pl.program_id(1)
            start, finish = hosted.run((b == 0) & (hp == 0), (b == B - 1) & (hp == HP - 1),
                                       refs[7:7 + nh], refs[12 + nh:12 + 2 * nh], refs[21 + 2 * nh:])
            start()
        masks = _head_masks()
        rr = lax.broadcasted_iota(jnp.int32, (tq, tq), 0)
        cc = lax.broadcasted_iota(jnp.int32, (tq, tq), 1)
        causal = rr >= cc
        sc = jnp.asarray(scale, BF16)

        def stage(i, c):
            rows = pl.ds(pl.multiple_of(i * tq, tq), tq)
            dov = do_ref[0, rows, :]
            q2_ref[i] = _stack_heads(q_ref[0, rows, :], masks) * sc
            do2_ref[i] = _stack_heads(dov, masks)
            prod = dov.astype(F32) * o_ref[0, rows, :].astype(F32)
            delta2_ref[i] = jnp.concatenate(
                [jnp.sum(jnp.where(masks[h], prod, 0.0), axis=-1, keepdims=True) for h in range(2)], axis=0)
            lv = lse_ref[0, 0, rows, :]
            lse2_ref[i] = jnp.concatenate([lv[:, 0:1], lv[:, 1:2]], axis=0)
            return c

        lax.fori_loop(0, nq, stage, 0)
        dq2_acc[...] = jnp.zeros_like(dq2_acc)
        dfq2_acc[...] = jnp.zeros_like(dfq2_acc)

        def kv_block(j, carry):
            rows_j = pl.ds(pl.multiple_of(j * tq, tq), tq)
            kj = k_ref[0, rows_j, :]
            vj = v_ref[0, rows_j, :]
            ks = kj * sc
            dk_acc[...] = jnp.zeros_like(dk_acc)
            dv_acc[...] = jnp.zeros_like(dv_acc)
            dfk_acc[...] = jnp.zeros_like(dfk_acc)

            def logits(i):
                return (lax.dot_general(q2_ref[i], kj, NT, preferred_element_type=F32),
                        lax.dot_general(do2_ref[i], vj, NT, preferred_element_type=F32))

            def probs(i, s2, dp2, masked):
                lse2 = lse2_ref[i]
                delta2 = delta2_ref[i]
                ps, dss = [], []
                for h in range(2):
                    half = slice(h * tq, (h + 1) * tq)
                    p = jnp.exp(s2[half] - fk_ref[0, 0, h, pl.ds(j, 1), :] - lse2[half])
                    if masked:
                        p = jnp.where(causal, p, 0.0)
                    ds = p * (dp2[half] - delta2[half])
                    dfk_acc[h:h + 1, :] -= jnp.sum(ds, axis=0, keepdims=True)
                    dfq2_acc[i, half, :] += jnp.sum(ds, axis=1, keepdims=True)
                    ps.append(p.astype(BF16))
                    dss.append(ds.astype(BF16))
                return jnp.concatenate(ps, axis=0), jnp.concatenate(dss, axis=0)

            def grads(i, p2, ds2):
                dv_acc[...] += lax.dot_general(p2, do2_ref[i], TN, preferred_element_type=F32)
                dk_acc[...] += lax.dot_general(ds2, q2_ref[i], TN, preferred_element_type=F32)
                dq2_acc[i] += jnp.dot(ds2, ks, preferred_element_type=F32)

            grads(j, *probs(j, *logits(j), True))

            def rest(i, c):
                grads(i, *probs(i, *logits(i), False))
                return c

            lax.fori_loop(j + 1, nq, rest, 0)
            dk_ref[0, rows_j, :] = dk_acc[...].astype(BF16)
            dv_ref[0, rows_j, :] = dv_acc[...].astype(BF16)
            for h in range(2):
                dfk_ref[0, 0, h, pl.ds(j, 1), :] = dfk_acc[h:h + 1, :]
            return carry

        lax.fori_loop(0, nq, kv_block, 0)
        two = lax.broadcasted_iota(jnp.int32, (1, 2), 1)

        def finish_block(i, c):
            rows = pl.ds(pl.multiple_of(i * tq, tq), tq)
            dq2 = dq2_acc[i]
            dq_ref[0, rows, :] = jnp.where(masks[0], dq2[:tq], dq2[tq:]).astype(BF16)
            dfq2 = dfq2_acc[i]
            dfq_ref[0, 0, rows, :] = jnp.where(two == 0, dfq2[:tq], dfq2[tq:])
            return c

        lax.fori_loop(0, nq, finish_block, 0)
        if hosted:
            finish()

    col = lambda off: pl.BlockSpec((1, S, LANES), lambda b, hp: (b, 0, off + hp))
    stat = pl.BlockSpec((1, 1, S, 2), lambda b, hp: (b, hp, 0, 0))
    rowf = pl.BlockSpec((1, 1, 2, nq, tq), lambda b, hp: (b, hp, 0, 0, 0))
    grad = jax.ShapeDtypeStruct((B, S, AW), BF16)
    outs = pl.pallas_call(
        body, name=name,
        out_shape=[grad, grad, grad, jax.ShapeDtypeStruct((B, HP, 2, nq, tq), F32),
                   jax.ShapeDtypeStruct((B, HP, S, 2), F32)] + (hosted.out_shape if hosted else []),
        grid=(B, HP),
        in_specs=[col(0), col(HP), col(2 * HP), col(0), col(0), stat, rowf] + [HBM] * nh,
        out_specs=[col(0), col(0), col(0), rowf, stat] + [HBM] * nh,
        scratch_shapes=[pltpu.VMEM((nq, 2 * tq, LANES), BF16), pltpu.VMEM((nq, 2 * tq, LANES), BF16),
                        pltpu.VMEM((nq, 2 * tq, 1), F32), pltpu.VMEM((nq, 2 * tq, 1), F32),
                        pltpu.VMEM((nq, 2 * tq, LANES), F32), pltpu.VMEM((nq, 2 * tq, 1), F32),
                        pltpu.VMEM((tq, LANES), F32), pltpu.VMEM((tq, LANES), F32), pltpu.VMEM((2, tq), F32)]
        + (hosted.scratch if hosted else []),
        compiler_params=_cparams("arbitrary", "arbitrary"),
    )(proj, proj, proj, ya, dya, lse, frow, *(hosted.arrays if hosted else []))
    return outs[:5], outs[5:]


def _cmul(ar, ai, br, bi):
    return ar * br - ai * bi, ar * bi + ai * br


def _ssm_states(u_ref, bm, lam_ref, pw_ref, lamT_ref, hr_ref, hi_ref, inr_ref, ini_ref, T, NC, SP):
    lr, li = lam_ref[0, 0:1, :], lam_ref[0, 1:2, :]
    bu = jnp.dot(u_ref[0, 0], bm, preferred_element_type=F32)
    hr_ref[0] = bu[:, :SP]
    hi_ref[0] = bu[:, SP:]

    def step(t, c):
        bu = jnp.dot(u_ref[0, t], bm, preferred_element_type=F32)
        pr, pi = _cmul(hr_ref[t - 1], hi_ref[t - 1], lr, li)
        hr_ref[t] = pr + bu[:, :SP]
        hi_ref[t] = pi + bu[:, SP:]
        return c

    lax.fori_loop(1, T, step, 0, unroll=2)

    tr, ti = lamT_ref[0, 0:1, :], lamT_ref[0, 1:2, :]
    inr_ref[0:1, :] = jnp.zeros((1, SP), F32)
    ini_ref[0:1, :] = jnp.zeros((1, SP), F32)

    def chunk(n, c):
        prev = pl.ds(n - 1, 1)
        pr, pi = _cmul(inr_ref[prev, :], ini_ref[prev, :], tr, ti)
        inr_ref[pl.ds(n, 1), :] = pr + hr_ref[T - 1, prev, :]
        ini_ref[pl.ds(n, 1), :] = pi + hi_ref[T - 1, prev, :]
        return c

    lax.fori_loop(1, NC, chunk, 0)


def _ssm_entry_term(t, pw_ref, inr_ref, ini_ref):
    return _cmul(inr_ref[...], ini_ref[...], pw_ref[0, 0, pl.ds(t, 1), :], pw_ref[0, 1, pl.ds(t, 1), :])


def _ssm_fwd(u_tm, bmat, cmat, lam, pw, lamT, dskip, *, name):
    B, T, NC, W = u_tm.shape
    NS = W // LANES
    SP = bmat.shape[2] // 2

    def body(u_ref, b_ref, c_ref, lam_ref, pw_ref, lamT_ref, d_ref, y_ref, hr_ref, hi_ref, inr_ref, ini_ref):
        _ssm_states(u_ref, b_ref[0], lam_ref, pw_ref, lamT_ref, hr_ref, hi_ref, inr_ref, ini_ref, T, NC, SP)
        cm = c_ref[0]
        dv = d_ref[...]

        def out(t, c):
            cr, ci = _ssm_entry_term(t, pw_ref, inr_ref, ini_ref)
            hcat = jnp.concatenate([hr_ref[t] + cr, hi_ref[t] + ci], axis=1).astype(BF16)
            y_ref[0, t] = jnp.dot(hcat, cm, preferred_element_type=F32) + dv * u_ref[0, t].astype(F32)
            return c

        lax.fori_loop(0, T, out, 0, unroll=2)

    slab = lambda *shape: pl.BlockSpec((1,) + shape, lambda b, s: (s,) + (0,) * len(shape))
    tok = pl.BlockSpec((1, T, NC, LANES), lambda b, s: (b, 0, 0, s))
    return pl.pallas_call(
        body, name=name,
        out_shape=jax.ShapeDtypeStruct((B, T, NC, W), F32),
        grid=(B, NS),
        in_specs=[tok, slab(LANES, 2 * SP), slab(2 * SP, LANES), slab(2, SP), slab(2, T, SP), slab(2, SP),
                  pl.BlockSpec((1, LANES), lambda b, s: (0, s))],
        out_specs=tok,
        scratch_shapes=[pltpu.VMEM((T, NC, SP), F32), pltpu.VMEM((T, NC, SP), F32),
                        pltpu.VMEM((NC, SP), F32), pltpu.VMEM((NC, SP), F32)],
        compiler_params=_cparams("parallel", "parallel"),
    )(u_tm, bmat, cmat, lam, pw, lamT, dskip)


def _ssm_bwd(u_tm, dy_tm, bmat, bmat_t, cmat_t, lam, pw, lamT, dskip, *, name):
    B, T, NC, W = u_tm.shape
    NS = W // LANES
    SP = bmat.shape[2] // 2

    def body(u_ref, dy_ref, b_ref, bt_ref, ct_ref, lam_ref, pw_ref, lamT_ref, d_ref,
             du_ref, gb_ref, gc_ref, glam_ref, gd_ref,
             hr_ref, hi_ref, ar_ref, ai_ref, inr_ref, ini_ref, anr_ref, ani_ref):
        b = pl.program_id(1)
        _ssm_states(u_ref, b_ref[0], lam_ref, pw_ref, lamT_ref, hr_ref, hi_ref, inr_ref, ini_ref, T, NC, SP)
        lr, li = lam_ref[0, 0:1, :], lam_ref[0, 1:2, :]
        ct = ct_ref[0]
        bt = bt_ref[0]
        dv = d_ref[...]

        gh = jnp.dot(dy_ref[0, T - 1].astype(BF16), ct, preferred_element_type=F32)
        ar_ref[T - 1] = gh[:, :SP]
        ai_ref[T - 1] = gh[:, SP:]

        def back(k, c):
            t = T - 2 - k
            gh = jnp.dot(dy_ref[0, t].astype(BF16), ct, preferred_element_type=F32)
            pr, pi = _cmul(ar_ref[t + 1], ai_ref[t + 1], lr, -li)
            ar_ref[t] = pr + gh[:, :SP]
            ai_ref[t] = pi + gh[:, SP:]
            return c

        lax.fori_loop(0, T - 1, back, 0, unroll=2)

        tr, ti = lamT_ref[0, 0:1, :], lamT_ref[0, 1:2, :]
        anr_ref[NC - 1:NC, :] = jnp.zeros((1, SP), F32)
        ani_ref[NC - 1:NC, :] = jnp.zeros((1, SP), F32)

        def chunk(k, c):
            n = NC - 2 - k
            nxt = pl.ds(n + 1, 1)
            pr, pi = _cmul(anr_ref[nxt, :], ani_ref[nxt, :], tr, -ti)
            anr_ref[pl.ds(n, 1), :] = pr + ar_ref[0, nxt, :]
            ani_ref[pl.ds(n, 1), :] = pi + ai_ref[0, nxt, :]
            return c

        lax.fori_loop(0, NC - 1, chunk, 0)

        @pl.when(b == 0)
        def _():
            gb_ref[...] = jnp.zeros_like(gb_ref)
            gc_ref[...] = jnp.zeros_like(gc_ref)
            glam_ref[...] = jnp.zeros_like(glam_ref)
            gd_ref[...] = jnp.zeros_like(gd_ref)

        def final(t, hpr, hpi, gl):
            back_pow = pl.ds(T - 1 - t, 1)
            cr, ci = _cmul(anr_ref[...], ani_ref[...], pw_ref[0, 0, back_pow, :], -pw_ref[0, 1, back_pow, :])
            a_r = ar_ref[t] + cr
            a_i = ai_ref[t] + ci
            gl = (gl[0] + jnp.sum(a_r * hpr + a_i * hpi, axis=0, keepdims=True),
                  gl[1] + jnp.sum(a_i * hpr - a_r * hpi, axis=0, keepdims=True))
            acat = jnp.concatenate([a_r, a_i], axis=1).astype(BF16)
            ut = u_ref[0, t]
            dyt = dy_ref[0, t]
            du_ref[0, t] = (jnp.dot(acat, bt, preferred_element_type=F32) + dv * dyt).astype(BF16)
            gb_ref[0] += lax.dot_general(acat, ut, TN, preferred_element_type=F32)
            er, ei = _ssm_entry_term(t, pw_ref, inr_ref, ini_ref)
            h_r = hr_ref[t] + er
            h_i = hi_ref[t] + ei
            hr_ref[t] = h_r
            hi_ref[t] = h_i
            hcat = jnp.concatenate([h_r, h_i], axis=1).astype(BF16)
            gc_ref[0] += lax.dot_general(dyt.astype(BF16), hcat, TN, preferred_element_type=F32)
            gd_ref[0] += jnp.sum(dyt * ut.astype(F32), axis=0, keepdims=True)
            return gl

        zero = jnp.zeros((1, SP), F32)
        gl = final(0, inr_ref[...], ini_ref[...], (zero, zero))
        gl = lax.fori_loop(1, T, lambda t, gl: final(t, hr_ref[t - 1], hi_ref[t - 1], gl), gl)
        glam_ref[0, 0:1, :] += gl[0]
        glam_ref[0, 1:2, :] += gl[1]

    slab = lambda *shape: pl.BlockSpec((1,) + shape, lambda s, b: (s,) + (0,) * len(shape))
    tok = pl.BlockSpec((1, T, NC, LANES), lambda s, b: (b, 0, 0, s))
    big = pltpu.VMEM((T, NC, SP), F32)
    small = pltpu.VMEM((NC, SP), F32)
    return pl.pallas_call(
        body, name=name,
        out_shape=(jax.ShapeDtypeStruct((B, T, NC, W), BF16),
                   jax.ShapeDtypeStruct((NS, 2 * SP, LANES), F32), jax.ShapeDtypeStruct((NS, LANES, 2 * SP), F32),
                   jax.ShapeDtypeStruct((NS, 2, SP), F32), jax.ShapeDtypeStruct((NS, 1, LANES), F32)),
        grid=(NS, B),
        in_specs=[tok, tok, slab(LANES, 2 * SP), slab(2 * SP, LANES), slab(LANES, 2 * SP), slab(2, SP),
                  slab(2, T, SP), slab(2, SP), pl.BlockSpec((1, LANES), lambda s, b: (0, s))],
        out_specs=(tok, slab(2 * SP, LANES), slab(LANES, 2 * SP), slab(2, SP), slab(1, LANES)),
        scratch_shapes=[big, big, big, big, small, small, small, small],
        compiler_params=_cparams("parallel", "arbitrary"),
    )(u_tm, dy_tm, bmat, bmat_t, cmat_t, lam, pw, lamT, dskip)


def _glu_fwd(ys, w, b, *, name, tr=512):
    n, wd = ys.shape
    tr = _tile(n, tr, 8)

    def body(y_ref, w_ref, b_ref, o_ref):
        yb = _gelu(y_ref[...])
        z = jnp.dot(yb.astype(BF16), w_ref[...], preferred_element_type=F32) + b_ref[...]
        o_ref[...] = (yb * _sigmoid(z)).astype(BF16)

    row = pl.BlockSpec((tr, wd), lambda i: (i, 0))
    return pl.pallas_call(
        body, name=name, out_shape=jax.ShapeDtypeStruct((n, wd), BF16), grid=(n // tr,),
        in_specs=[row, pl.BlockSpec((wd, wd), lambda i: (0, 0)), pl.BlockSpec((1, wd), lambda i: (0, 0))],
        out_specs=row, compiler_params=_cparams("parallel"),
    )(ys, w, b.reshape(1, wd))


def _glu_bwd(ys, dyb2, w, w_t, b, *, name, tr=512):
    n, wd = ys.shape
    tr = _tile(n, tr, 8)

    def body(y_ref, d_ref, w_ref, wt_ref, b_ref, dys_ref, dz_ref, yb_ref, db_ref):
        i = pl.program_id(0)
        yv = y_ref[...]
        yb = _gelu(yv)
        ybb = yb.astype(BF16)
        sg = _sigmoid(jnp.dot(ybb, w_ref[...], preferred_element_type=F32) + b_ref[...])
        dv = d_ref[...].astype(F32)
        dz = dv * yb * sg * (1.0 - sg)
        dzb = dz.astype(BF16)
        dyb = dv * sg + jnp.dot(dzb, wt_ref[...], preferred_element_type=F32)
        dys_ref[...] = dyb * _gelu_grad(yv)
        dz_ref[...] = dzb
        yb_ref[...] = ybb
        part = jnp.sum(dz, axis=0, keepdims=True)

        @pl.when(i == 0)
        def _():
            db_ref[...] = part

        @pl.when(i > 0)
        def _():
            db_ref[...] += part

    row = pl.BlockSpec((tr, wd), lambda i: (i, 0))
    mat = pl.BlockSpec((wd, wd), lambda i: (0, 0))
    vec = pl.BlockSpec((1, wd), lambda i: (0, 0))
    return pl.pallas_call(
        body, name=name,
        out_shape=(jax.ShapeDtypeStruct((n, wd), F32), jax.ShapeDtypeStruct((n, wd), BF16),
                   jax.ShapeDtypeStruct((n, wd), BF16), jax.ShapeDtypeStruct((1, wd), F32)),
        grid=(n // tr,), in_specs=[row, row, mat, mat, vec], out_specs=(row, row, row, vec),
        compiler_params=_cparams("arbitrary"),
    )(ys, dyb2, w, w_t, b.reshape(1, wd))


def _merge_fwd(ya, yb2, wa, wb, proj, gate_blk, *, name, tr=512):
    n, aw = ya.shape
    d = wa.shape[1]
    tr = _tile(n, tr, 8)

    def body(ya_ref, yb_ref, wa_ref, wb_ref, ga_ref, gb_ref, mix_ref, pa_ref, pb_ref):
        pa = jnp.dot(ya_ref[...], wa_ref[...], preferred_element_type=F32)
        pb = jnp.dot(yb_ref[...], wb_ref[...], preferred_element_type=F32)
        mix = _sigmoid(ga_ref[...].astype(F32)) * pa + _sigmoid(gb_ref[...].astype(F32)) * pb
        mix_ref[...] = mix.astype(BF16)
        pa_ref[...] = pa.astype(BF16)
        pb_ref[...] = pb.astype(BF16)

    row = lambda wdt: pl.BlockSpec((tr, wdt), lambda i: (i, 0))
    full = lambda r, c: pl.BlockSpec((r, c), lambda i: (0, 0))
    out = jax.ShapeDtypeStruct((n, d), BF16)
    return pl.pallas_call(
        body, name=name, out_shape=(out, out, out), grid=(n // tr,),
        in_specs=[row(aw), row(yb2.shape[1]), full(*wa.shape), full(*wb.shape),
                  pl.BlockSpec((tr, d), lambda i: (i, gate_blk)), pl.BlockSpec((tr, d), lambda i: (i, gate_blk + 1))],
        out_specs=(row(d), row(d), row(d)), compiler_params=_cparams("parallel"),
    )(ya, yb2, wa, wb, proj, proj)


def _merge_bwd(dx1, w_out, proj, pa, pb, gate_blk, dproj_cols, *, name, tr=512):
    n, d = dx1.shape
    tr = _tile(n, tr, 8)
    assert gate_blk % 2 == 0

    def body(dx_ref, w_ref, ga_ref, gb_ref, pa_ref, pb_ref, dpa_ref, dpb_ref, dg_ref):
        dm = lax.dot_general(dx_ref[...].astype(BF16), w_ref[...], NT, preferred_element_type=F32)
        sa = _sigmoid(ga_ref[...].astype(F32))
        sb = _sigmoid(gb_ref[...].astype(F32))
        dpa_ref[...] = (dm * sa).astype(BF16)
        dpb_ref[...] = (dm * sb).astype(BF16)
        dg_ref[:, :d] = (dm * pa_ref[...].astype(F32) * sa * (1.0 - sa)).astype(BF16)
        dg_ref[:, d:] = (dm * pb_ref[...].astype(F32) * sb * (1.0 - sb)).astype(BF16)

    row = pl.BlockSpec((tr, d), lambda i: (i, 0))
    out = jax.ShapeDtypeStruct((n, d), BF16)
    return pl.pallas_call(
        body, name=name, out_shape=(out, out, jax.ShapeDtypeStruct((n, dproj_cols), BF16)), grid=(n // tr,),
        in_specs=[row, pl.BlockSpec((d, d), lambda i: (0, 0)), pl.BlockSpec((tr, d), lambda i: (i, gate_blk)),
                  pl.BlockSpec((tr, d), lambda i: (i, gate_blk + 1)), row, row],
        out_specs=(row, row, pl.BlockSpec((tr, 2 * d), lambda i: (i, gate_blk // 2))),
        compiler_params=_cparams("parallel"),
    )(dx1, w_out, proj, proj, pa, pb)


def _outproj_fwd(mixed, w, x0, g, *, name, tr=512):
    n, d = x0.shape
    tr = _tile(n, tr, 8)

    def body(m_ref, w_ref, x_ref, g_ref, x1_ref, h_ref, r_ref):
        x1 = x_ref[...] + jnp.dot(m_ref[...], w_ref[...], preferred_element_type=F32)
        r = lax.rsqrt(jnp.mean(x1 * x1, axis=-1, keepdims=True) + RMS_EPS)
        x1_ref[...] = x1
        h_ref[...] = (x1 * r * g_ref[...]).astype(BF16)
        r_ref[...] = r

    row = pl.BlockSpec((tr, d), lambda i: (i, 0))
    return pl.pallas_call(
        body, name=name,
        out_shape=(jax.ShapeDtypeStruct((n, d), F32), jax.ShapeDtypeStruct((n, d), BF16),
                   jax.ShapeDtypeStruct((n, 1), F32)),
        grid=(n // tr,),
        in_specs=[row, pl.BlockSpec((d, d), lambda i: (0, 0)), row, pl.BlockSpec((1, d), lambda i: (0, 0))],
        out_specs=(row, row, pl.BlockSpec((tr, 1), lambda i: (i, 0))),
        compiler_params=_cparams("parallel"),
    )(mixed, w, x0, g.reshape(1, d))


def _adamw(w, g, m, v, *, name):
    shape = w.shape
    total = w.size
    if w.ndim == 3 and shape[1] % 8 == 0:
        lead, rows, cols = shape
    elif total % PACK_COLS == 0 and ((total // PACK_COLS) % 8 == 0 or total // PACK_COLS <= 512):
        lead, rows, cols = 1, total // PACK_COLS, PACK_COLS
    elif w.ndim >= 2:
        lead, rows, cols = 1, total // shape[-1], shape[-1]
    else:
        lead, rows, cols = 1, 1, total
    tr = _tile(rows, 512, 8)

    def body(w_ref, g_ref, m_ref, v_ref, d_ref, nm_ref, nv_ref):
        gv = g_ref[...]
        mn = ADAM_B1 * m_ref[...] + (1.0 - ADAM_B1) * gv
        vn = ADAM_B2 * v_ref[...] + (1.0 - ADAM_B2) * (gv * gv)
        m_hat = mn / (1.0 - ADAM_B1 ** ADAM_STEP)
        v_hat = vn / (1.0 - ADAM_B2 ** ADAM_STEP)
        d_ref[...] = -ADAM_LR * (m_hat / (jnp.sqrt(v_hat) + ADAM_EPS) + ADAM_WD * w_ref[...])
        nm_ref[...] = mn
        nv_ref[...] = vn

    blk = pl.BlockSpec((None, tr, cols), lambda l, i: (l, i, 0))
    out = jax.ShapeDtypeStruct((lead, rows, cols), F32)
    outs = pl.pallas_call(
        body, name=name, out_shape=(out, out, out), grid=(lead, rows // tr),
        in_specs=[blk] * 4, out_specs=(blk, blk, blk), compiler_params=_cparams("parallel", "parallel"),
    )(*[t.reshape(lead, rows, cols) for t in (w, g, m, v)])
    return tuple(o.reshape(shape) for o in outs)


def _all_gather(blocks, *, name):
    n = len(blocks)

    def body(*refs):
        x_refs, out_refs = refs[:n], refs[n:2 * n]
        send_sems, recv_sems, local_sems = refs[2 * n:]
        x, y, c = lax.axis_index("x"), lax.axis_index("y"), lax.axis_index("c")
        me, sibling = (x, y, c), (x, y, 1 - c)
        chips = [(1 - x, y), (x, 1 - y), (1 - x, 1 - y)]

        def slot(a, px, py, pc):
            return out_refs[a].at[4 * px + 2 * py + pc]

        def copy(a, k, block, to, src=None):
            return pltpu.make_async_remote_copy(
                src_ref=slot(a, *block) if src is None else src, dst_ref=slot(a, *block),
                send_sem=send_sems.at[7 * a + k], recv_sem=recv_sems.at[7 * a + k], device_id=to,
                device_id_type=MESH)

        started = []
        for a in range(n):
            mine = pltpu.make_async_copy(x_refs[a], slot(a, *me), local_sems.at[a])
            mine.start()
            started.append(mine)
        sends = []
        for a in range(n):
            first = [copy(a, 0, me, sibling, src=x_refs[a])]
            first += [copy(a, 1 + j, me, (*chip, c), src=x_refs[a]) for j, chip in enumerate(chips)]
            for cp in first:
                cp.start()
            sends += first
        for a in range(n):
            for j, chip in enumerate(chips):
                copy(a, 1 + j, (*chip, c), me).wait_recv()
                onward = copy(a, 4 + j, (*chip, c), sibling)
                onward.start()
                sends.append(onward)
        for a in range(n):
            copy(a, 0, sibling, me).wait_recv()
            for j, chip in enumerate(chips):
                copy(a, 4 + j, (*chip, 1 - c), me).wait_recv()
        for cp in sends:
            cp.wait_send()
        for mine in started:
            mine.wait()

    return pl.pallas_call(
        body, name=name, out_shape=[jax.ShapeDtypeStruct((N_DEV,) + b.shape, b.dtype) for b in blocks],
        in_specs=[HBM] * n, out_specs=[HBM] * n,
        scratch_shapes=[pltpu.SemaphoreType.DMA((7 * n,)), pltpu.SemaphoreType.DMA((7 * n,)),
                        pltpu.SemaphoreType.DMA((n,))],
    )(*blocks)


def _sum8(blocks, *, name, tr=SUM_ROWS):
    _, R, C = blocks.shape
    tr = _tile(R, tr, 16)

    def body(x_ref, o_ref):
        acc = x_ref[0].astype(F32)
        for i in range(1, N_DEV):
            acc = acc + x_ref[i].astype(F32)
        o_ref[...] = acc

    return pl.pallas_call(
        body, name=name, out_shape=jax.ShapeDtypeStruct((R, C), F32), grid=(R // tr,),
        in_specs=[pl.BlockSpec((N_DEV, tr, C), lambda i: (0, i, 0))],
        out_specs=pl.BlockSpec((tr, C), lambda i: (i, 0)), compiler_params=_cparams("parallel"),
    )(blocks)


def _pack(parts):
    flat = jnp.concatenate([p.astype(F32).reshape(-1) for p in parts])
    unit = 8 * PACK_COLS
    padded = -(-flat.size // unit) * unit
    return jnp.pad(flat, (0, padded - flat.size)).reshape(padded // PACK_COLS, PACK_COLS)


def _unpack(buf, like):
    flat, out, off = buf.reshape(-1), [], 0
    for p in like:
        out.append(flat[off:off + p.size].reshape(p.shape))
        off += p.size
    return out


def _ssm_discretise(lre, lim, logdt, bre, bim):
    lam = lax.complex(lre, lim)
    dt = jnp.exp(logdt)[:, None]
    lam_bar = jnp.exp(lam * dt)
    b_bar = ((lam_bar - 1.0) / lam)[:, :, None] * lax.complex(bre, bim)
    return lam_bar.real, lam_bar.imag, b_bar.real, b_bar.imag


def _block_diag(a, rows_first):
    ns, g, r, c = a.shape
    eye = jnp.eye(g, dtype=a.dtype)
    return jnp.einsum("sgrc,gh->sgrhc", a, eye).reshape(ns, g * r, g * c)


def _diag_blocks(m, r, c):
    ns = m.shape[0]
    g = SLAB_GROUPS
    return jnp.einsum("sgrhc,gh->sgrc", m.reshape(ns, g, r, g, c), jnp.eye(g, dtype=m.dtype))


def _to_tm(a, T):
    b, s, w = a.shape
    return a.reshape(b, s // T, T, w).transpose(0, 2, 1, 3)


def _from_tm(a):
    b, t, nc, w = a.shape
    return a.transpose(0, 2, 1, 3).reshape(b, nc * t, w)


WEIGHTS = ["norm_mix", "w_in", "b_forget", "ssm_lambda_re", "ssm_lambda_im", "ssm_log_dt", "ssm_b_re", "ssm_b_im",
           "ssm_c_re", "ssm_c_im", "ssm_d", "w_glu", "b_glu", "w_branch_a", "w_branch_b", "w_out", "norm_mlp",
           "w_mlp_up", "w_mlp_down", "norm_final"]
SHARDED = {"w_in": 2, "w_glu": 1, "w_branch_a": 2, "w_branch_b": 2, "w_out": 1, "w_mlp_up": 2, "w_mlp_down": 1}


REST = [n for n in SHARDED if n != "w_in"]


def _whole(n, seg):
    ax = SHARDED[n] - 1
    shp = seg.shape[1:]
    return jnp.moveaxis(seg, 0, ax).reshape(shp[:ax] + (N_DEV * shp[ax],) + shp[ax + 1:])


def _blocks(n, g):
    ax = SHARDED[n] - 1
    shp = g.shape
    return jnp.moveaxis(g.reshape(shp[:ax] + (N_DEV, shp[ax] // N_DEV) + shp[ax + 1:]), ax, 0)


def _sum_blocks(n, got):
    return _sum8(got.reshape(N_DEV, -1, got.shape[-1]), name="sum_grads_" + n).reshape(got.shape[1:])


def kernel(x, norm_mix, w_in, b_forget, ssm_lambda_re, ssm_lambda_im, ssm_log_dt, ssm_b_re, ssm_b_im, ssm_c_re, ssm_c_im, ssm_d, w_glu, b_glu, w_branch_a, w_branch_b, w_out, norm_mlp, w_mlp_up, w_mlp_down, norm_final, loss_target, m_norm_mix, m_w_in, m_b_forget, m_ssm_lambda_re, m_ssm_lambda_im, m_ssm_log_dt, m_ssm_b_re, m_ssm_b_im, m_ssm_c_re, m_ssm_c_im, m_ssm_d, m_w_glu, m_b_glu, m_w_branch_a, m_w_branch_b, m_w_out, m_norm_mlp, m_w_mlp_up, m_w_mlp_down, m_norm_final, v_norm_mix, v_w_in, v_b_forget, v_ssm_lambda_re, v_ssm_lambda_im, v_ssm_log_dt, v_ssm_b_re, v_ssm_b_im, v_ssm_c_re, v_ssm_c_im, v_ssm_d, v_w_glu, v_b_glu, v_w_branch_a, v_w_branch_b, v_w_out, v_norm_mlp, v_w_mlp_up, v_w_mlp_down, v_norm_final):
    args = dict(locals())
    w = {n: args[n] for n in WEIGHTS}
    Bl, S, D = x.shape
    L, H = b_forget.shape
    G, P, C = ssm_b_re.shape[1:]
    AW, W, HP = H * HEAD_DIM, G * C, H // 2
    N = Bl * S
    T = SSM_CHUNK
    NS = G // SLAB_GROUPS
    SP = SLAB_GROUPS * P
    tq = min(ATTN_BLOCK, S)
    nq = S // tq
    u_off = 3 * AW
    gate_blk = (u_off + W) // D
    assert (u_off + W) % D == 0 and W % LANES == 0 and AW % LANES == 0 and S % T == 0

    shard = {n: w[n].astype(BF16) for n in SHARDED}
    weights = [dict() for _ in range(L)]
    weights[0]["w_in"] = _whole("w_in", _all_gather([shard["w_in"][0]], name="gather_first")[0])
    tr_ = lambda a: jnp.swapaxes(a, 1, 2)

    ssm = []
    for l in range(L):
        disc, disc_vjp = jax.vjp(_ssm_discretise, ssm_lambda_re[l], ssm_lambda_im[l], ssm_log_dt[l],
                                 ssm_b_re[l], ssm_b_im[l])
        lbr, lbi, bbr, bbi = disc
        z = lax.complex(ssm_lambda_re[l], ssm_lambda_im[l]) * jnp.exp(ssm_log_dt[l])[:, None]
        powers = jnp.exp(z[None] * jnp.arange(1, T + 1, dtype=F32)[:, None, None])
        slabs = lambda a: a.reshape(NS, SP)
        lam = jnp.stack([slabs(lbr), slabs(lbi)], axis=1)
        lam_t = jnp.stack([slabs(powers[T - 1].real), slabs(powers[T - 1].imag)], axis=1)
        pw = jnp.stack([powers.real.reshape(T, NS, SP), powers.imag.reshape(T, NS, SP)], axis=0).transpose(2, 0, 1, 3)
        to_rows = lambda a: jnp.swapaxes(a.reshape(NS, SLAB_GROUPS, P, C), 2, 3)
        bmat = jnp.concatenate([_block_diag(to_rows(bbr), True), _block_diag(to_rows(bbi), True)], axis=2)
        cre = ssm_c_re[l].reshape(NS, SLAB_GROUPS, C, P)
        cim = ssm_c_im[l].reshape(NS, SLAB_GROUPS, C, P)
        cmat_t = jnp.concatenate([_block_diag(cre, True), -_block_diag(cim, True)], axis=2)
        ssm.append(dict(vjp=disc_vjp, lam=lam, lam_t=lam_t, pw=pw, bmat=bmat.astype(BF16),
                        bmat_t=tr_(bmat).astype(BF16), cmat=tr_(cmat_t).astype(BF16), cmat_t=cmat_t.astype(BF16),
                        d=ssm_d[l].reshape(1, W)))

    xcur = x.reshape(N, D)
    saved = []
    for l in range(L):
        s_, wl = ssm[l], weights[l]
        win = wl["w_in"]
        wl["wcat"] = jnp.concatenate([win[:, :3 * AW], win[:, 3 * AW + H:]], axis=1)
        wl["wf"] = jnp.pad(win[:, 3 * AW:3 * AW + H], ((0, 0), (0, LANES - H)))
        h, r0 = _rmsnorm_fwd(xcur, norm_mix[l], name="norm_mix_fwd")
        proj = _mm(h, wl["wcat"], name="in_proj", tm=WIDE_N, tn=WIDE_N)
        fl = _mm(h, wl["wf"], name="forget_proj", out_dtype=F32)
        ft = fl[:, :H].reshape(Bl, S, H).transpose(0, 2, 1)
        F = _fox_gate_fwd(ft, b_forget[l], name="forget_gate_fwd")
        frow = F.reshape(Bl, HP, 2, nq, tq)
        proj3 = proj.reshape(Bl, S, -1)
        coming = [shard[n][l] for n in REST] + ([shard["w_in"][l + 1]] if l + 1 < L else [])
        ya, lse, got = _attn_fwd(proj3, frow, name="attn_fwd" if l + 1 < L else "attn_fwd_last", H=H, tq=tq,
                                 hosted=_Hosted(gather=coming))
        for n, seg in zip(REST, got):
            wl[n] = _whole(n, seg)
        if l + 1 < L:
            weights[l + 1]["w_in"] = _whole("w_in", got[-1])
        u_tm = _to_tm(proj3[:, :, u_off:u_off + W], T)
        ys = _from_tm(_ssm_fwd(u_tm, s_["bmat"], s_["cmat"], s_["lam"], s_["pw"], s_["lam_t"], s_["d"],
                               name="ssm_fwd")).reshape(N, W)
        yb2 = _glu_fwd(ys, wl["w_glu"], b_glu[l], name="glu_fwd")
        ya2 = ya.reshape(N, AW)
        mixed, pa, pb = _merge_fwd(ya2, yb2, wl["w_branch_a"], wl["w_branch_b"], proj, gate_blk, name="merge_fwd")
        x1, h2, r1 = _outproj_fwd(mixed, wl["w_out"], xcur, norm_mlp[l], name="out_proj")
        a = _mm(h2, wl["w_mlp_up"], name="mlp_up", tm=WIDE_N, tn=WIDE_N)
        x2 = _mm(a, wl["w_mlp_down"], name="mlp_down", a_fn=_relu_sq, epi=lambda acc, res: acc + res,
                 extras=(x1,), out_dtype=F32, tk=LONG_K)
        saved.append(dict(x0=xcur, h=h, r0=r0, proj=proj, ft=ft, frow=frow, ya=ya, lse=lse, u_tm=u_tm,
                          ys=ys, yb2=yb2, mixed=mixed, pa=pa, pb=pb, x1=x1, h2=h2, r1=r1, a=a))
        xcur = x2

    dx, g_final, loss_row = _loss_head(xcur, norm_final, loss_target.reshape(N, D), name="loss_head")
    loss = lax.psum(loss_row[0, 0], MESH_AXES)

    big = {n: [None] * L for n in SHARDED}
    small = {n: [None] * L for n in WEIGHTS if n not in SHARDED and n != "norm_final"}
    small_sums = [None] * L
    win_grad = small_above = None
    for l in reversed(range(L)):
        sv, s_, wl = saved[l], ssm[l], weights[l]
        a = sv["a"]
        gw = {}
        d_a = _mm(dx, wl["w_mlp_down"], name="mlp_down_dx", tb=True, tn=WIDE_N,
                  epi=lambda acc, av: acc * (2.0 * jnp.maximum(av.astype(F32), 0.0)), extras=(a,))
        gw["w_mlp_down"] = _mm(a, dx, name="mlp_down_dw", ta=True, a_fn=_relu_sq, tk=LONG_K)
        gw["w_mlp_up"] = _mm(sv["h2"], d_a, name="mlp_up_dw", ta=True, tk=LONG_K)
        dx1, g = _mm_norm_bwd(d_a, wl["w_mlp_up"], sv["x1"], sv["r1"], norm_mlp[l], dx, name="mlp_up_dx_norm",
                              tk=LONG_K)
        small["norm_mlp"][l] = g[0]
        gw["w_out"] = _mm(sv["mixed"], dx1, name="out_proj_dw", ta=True, tk=LONG_K)
        ncat = wl["wcat"].shape[1]
        dpa, dpb, dproj = _merge_bwd(dx1, wl["w_out"], sv["proj"], sv["pa"], sv["pb"], gate_blk, ncat + LANES,
                                     name="merge_bwd")
        ya2 = sv["ya"].reshape(N, AW)
        gw["w_branch_a"] = _mm(ya2, dpa, name="branch_a_dw", ta=True, tk=LONG_K)
        dya = _mm(dpa, wl["w_branch_a"], name="branch_a_dx", tb=True)
        gw["w_branch_b"] = _mm(sv["yb2"], dpb, name="branch_b_dw", ta=True, tk=LONG_K)
        dyb2 = _mm(dpb, wl["w_branch_b"], name="branch_b_dx", tb=True)
        dys, dz, yb, g = _glu_bwd(sv["ys"], dyb2, wl["w_glu"], wl["w_glu"].T, b_glu[l], name="glu_bwd")
        small["b_glu"][l] = g[0]
        gw["w_glu"] = _mm(yb, dz, name="glu_dw", ta=True, tk=LONG_K)

        du_tm, g_bt, g_ct, g_lam, g_d = _ssm_bwd(
            sv["u_tm"], _to_tm(dys.reshape(Bl, S, W), T), s_["bmat"], s_["bmat_t"], s_["cmat_t"], s_["lam"],
            s_["pw"], s_["lam_t"], s_["d"], name="ssm_bwd")
        du = _from_tm(du_tm).reshape(N, W)
        g_b = _diag_blocks(jnp.swapaxes(g_bt, 1, 2).reshape(NS, LANES, 2, SP).transpose(2, 0, 1, 3).reshape(
            2 * NS, LANES, SP), C, P).reshape(2, G, C, P)
        g_bbar = jnp.swapaxes(g_b, 2, 3)
        g_c = _diag_blocks(g_ct.reshape(NS, LANES, 2, SP).transpose(2, 0, 1, 3).reshape(2 * NS, LANES, SP),
                           C, P).reshape(2, G, C, P)
        g_lbar = g_lam.transpose(1, 0, 2).reshape(2, G, P)
        g_lre, g_lim, g_ldt, g_bre, g_bim = s_["vjp"]((g_lbar[0], g_lbar[1], g_bbar[0], g_bbar[1]))
        small["ssm_lambda_re"][l], small["ssm_lambda_im"][l], small["ssm_log_dt"][l] = g_lre, g_lim, g_ldt
        small["ssm_b_re"][l], small["ssm_b_im"][l] = g_bre, g_bim
        small["ssm_c_re"][l], small["ssm_c_im"][l] = g_c[0], -g_c[1]
        small["ssm_d"][l] = g_d.reshape(W)

        proj3 = sv["proj"].reshape(Bl, S, -1)
        leaving = [_blocks(n, gw[n]) for n in REST] + ([_blocks("w_in", win_grad)] if l + 1 < L else [])
        (dq, dk, dv, dfk, dfq), got = _attn_bwd(
            proj3, sv["ya"], dya.reshape(Bl, S, AW), sv["lse"], sv["frow"],
            name="attn_bwd" if l + 1 < L else "attn_bwd_top", H=H, tq=tq,
            hosted=_Hosted(gather=[small_above] if l + 1 < L else [], exchange=leaving))
        if l + 1 < L:
            small_sums[l + 1] = _sum8(got[0], name="sum_small_grads")
            big["w_in"][l + 1] = _sum_blocks("w_in", got[-1])
            got = got[1:]
        for n, blocks in zip(REST, got):
            big[n][l] = _sum_blocks(n, blocks)
        dF = dfk.reshape(Bl, H, S) + dfq.transpose(0, 1, 3, 2).reshape(Bl, H, S)
        dft, g = _fox_gate_bwd(dF, sv["ft"], b_forget[l], name="forget_gate_bwd")
        small["b_forget"][l] = g[:, 0]
        dfl = jnp.pad(dft.transpose(0, 2, 1).reshape(N, H), ((0, 0), (0, LANES - H))).astype(BF16)
        for off, piece in ((0, dq.reshape(N, AW)), (AW, dk.reshape(N, AW)), (2 * AW, dv.reshape(N, AW)), (u_off, du),
                           (ncat, dfl)):
            dproj = lax.dynamic_update_slice(dproj, piece, (0, off))
        gcat = _mm(sv["h"], dproj, name="in_proj_dw", ta=True, tn=1408, tk=LONG_K)
        ncat = wl["wcat"].shape[1]
        win_grad = jnp.concatenate([gcat[:, :3 * AW], gcat[:, ncat:ncat + H], gcat[:, 3 * AW:ncat]], axis=1)
        wfull = jnp.concatenate([wl["wcat"], wl["wf"]], axis=1)
        if l > 0:
            dx, g = _mm_norm_bwd(dproj, wfull, sv["x0"], sv["r0"], norm_mix[l], dx1, name="in_proj_dx_norm", tk=1408)
        else:
            dh, got = _mm(dproj, wfull, name="in_proj_dx_bottom", tb=True, out_dtype=F32, tk=1408,
                          hosted=_Hosted(exchange=[_blocks("w_in", win_grad)]))
            big["w_in"][0] = _sum_blocks("w_in", got[0])
            dx, g = _rmsnorm_bwd(dh, sv["x0"], sv["r0"], norm_mix[l], dx1, name="norm_mix_bwd")
        small["norm_mix"][l] = g[0]
        small_above = _pack([small[n][l] for n in small])

    last = [small[n][0] for n in small] + [g_final[0]]
    small_sums[0] = _sum8(_all_gather([_pack(last)], name="gather_small_grads")[0], name="sum_small_grads_last")
    grads = {n: jnp.stack(big[n]) for n in SHARDED}
    per_layer = [_unpack(small_sums[l], last if l == 0 else last[:-1]) for l in range(L)]
    for i, n in enumerate(small):
        grads[n] = jnp.stack([per_layer[l][i] for l in range(L)])
    grads["norm_final"] = per_layer[0][-1]

    deltas, new_m, new_v = {}, {}, {}
    for n in WEIGHTS:
        deltas[n], new_m[n], new_v[n] = _adamw(w[n], grads[n], args["m_" + n], args["v_" + n], name="adamw_" + n)
    return (loss, dx.reshape(Bl, S, D), *[grads[n] for n in WEIGHTS], *[deltas[n] for n in WEIGHTS],
            *[new_m[n] for n in WEIGHTS], *[new_v[n] for n in WEIGHTS])
```

```python
import functools

import jax
import jax.numpy as jnp
from jax import lax
from jax.experimental import pallas as pl
from jax.experimental.pallas import tpu as pltpu

F32 = jnp.float32
BF16 = jnp.bfloat16

N_DEV = 8
HEAD_DIM = 64
LANES = 128
SSM_CHUNK = 32
SLAB_GROUPS = 8
ATTN_BLOCK = 512
LONG_K = 2048
WIDE_N = 2048
PACK_COLS = 1024
SUM_ROWS = 256
RMS_EPS = 1e-6
VMEM_LIMIT = 56 * 1024 * 1024
ADAM_LR, ADAM_B1, ADAM_B2, ADAM_EPS, ADAM_WD, ADAM_STEP = 0.001, 0.9, 0.999, 1e-08, 0.01, 10
MESH_AXES = ("x", "y", "c")
NEG = -1e30
NT = (((1,), (1,)), ((), ()))
TN = (((0,), (0,)), ((), ()))


def _cparams(*sem):
    return pltpu.CompilerParams(dimension_semantics=sem, vmem_limit_bytes=VMEM_LIMIT)


def _tile(dim, pref, unit=LANES):
    if dim <= pref:
        return dim
    best = None
    for t in range(unit, pref + 1, unit):
        if dim % t == 0:
            best = t
    assert best is not None, (dim, pref)
    return best


def _mm(a, b, *, name, ta=False, tb=False, a_fn=None, epi=None, extras=(), out_dtype=BF16, tm=1024, tn=1024,
        tk=1024, hosted=None):
    if ta:
        K, M = a.shape
    else:
        M, K = a.shape
    N, Kb = b.shape if tb else b.shape[::-1]
    assert K == Kb and not (ta and tb), (a.shape, b.shape)
    tm, tn, tk = _tile(M, tm), _tile(N, tn), _tile(K, tk)
    gm, gn, nk = M // tm, N // tn, K // tk
    ne = len(extras)
    nh = hosted.n if hosted else 0
    n_acc = 1 if nk > 1 else 0

    def body(a_ref, b_ref, *rest):
        e_refs, o_ref = rest[:ne], rest[ne + nh]
        acc_ref = rest[ne + 2 * nh + 1] if nk > 1 else None
        k = pl.program_id(2)
        if hosted:
            i, j = pl.program_id(0), pl.program_id(1)
            start, finish = hosted.run((i == 0) & (j == 0) & (k == 0), (i == gm - 1) & (j == gn - 1) & (k == nk - 1),
                                       rest[ne:ne + nh], rest[ne + nh + 1:ne + 2 * nh + 1],
                                       rest[ne + 2 * nh + 1 + n_acc:])
            start()
        av = a_ref[...]
        if a_fn is not None:
            av = a_fn(av)
        av = av.astype(BF16)
        bv = b_ref[...].astype(BF16)
        dims = TN if ta else NT if tb else (((1,), (0,)), ((), ()))
        part = lax.dot_general(av, bv, dims, preferred_element_type=F32)

        def finish_tile(r):
            if epi is not None:
                r = epi(r, *[e[...] for e in e_refs])
            o_ref[...] = r.astype(o_ref.dtype)

        if nk == 1:
            finish_tile(part)
        else:
            @pl.when(k == 0)
            def _():
                acc_ref[...] = part

            @pl.when(k > 0)
            def _():
                acc_ref[...] += part

            @pl.when(k == nk - 1)
            def _():
                finish_tile(acc_ref[...])

        if hosted:
            finish()

    a_spec = pl.BlockSpec((tk, tm), lambda i, j, k: (k, i)) if ta else pl.BlockSpec((tm, tk), lambda i, j, k: (i, k))
    outs = pl.pallas_call(
        body, name=name,
        out_shape=[jax.ShapeDtypeStruct((M, N), out_dtype)] + (hosted.out_shape if hosted else []),
        grid=(gm, gn, nk),
        in_specs=[a_spec, pl.BlockSpec((tn, tk), lambda i, j, k: (j, k)) if tb
                  else pl.BlockSpec((tk, tn), lambda i, j, k: (k, j))]
        + [pl.BlockSpec((tm, tn), lambda i, j, k: (i, j)) for _ in extras] + [HBM] * nh,
        out_specs=[pl.BlockSpec((tm, tn), lambda i, j, k: (i, j))] + [HBM] * nh,
        scratch_shapes=([pltpu.VMEM((tm, tn), F32)] if nk > 1 else []) + (hosted.scratch if hosted else []),
        compiler_params=_cparams(*(("arbitrary",) * 3 if hosted else ("parallel", "parallel", "arbitrary"))),
    )(a, b, *extras, *(hosted.arrays if hosted else []))
    return (outs[0], outs[1:]) if hosted else outs[0]


def _relu_sq(v):
    r = jnp.maximum(v.astype(F32), 0.0)
    return r * r


def _sigmoid(v):
    return 1.0 / (1.0 + jnp.exp(-v))


GELU_C = 0.7978845608028654
GELU_A = 0.044715


def _gelu(v):
    return 0.5 * v * (1.0 + jnp.tanh(GELU_C * (v + GELU_A * v * v * v)))


def _gelu_grad(v):
    t = jnp.tanh(GELU_C * (v + GELU_A * v * v * v))
    return 0.5 * (1.0 + t) + 0.5 * v * (1.0 - t * t) * GELU_C * (1.0 + 3.0 * GELU_A * v * v)


def _rmsnorm_fwd(x, g, *, name, tr=512):
    n, d = x.shape
    tr = _tile(n, tr, 8)

    def body(x_ref, g_ref, h_ref, r_ref):
        xv = x_ref[...]
        r = lax.rsqrt(jnp.mean(xv * xv, axis=-1, keepdims=True) + RMS_EPS)
        h_ref[...] = (xv * r * g_ref[...]).astype(BF16)
        r_ref[...] = r

    return pl.pallas_call(
        body, name=name,
        out_shape=(jax.ShapeDtypeStruct((n, d), BF16), jax.ShapeDtypeStruct((n, 1), F32)),
        grid=(n // tr,),
        in_specs=[pl.BlockSpec((tr, d), lambda i: (i, 0)), pl.BlockSpec((1, d), lambda i: (0, 0))],
        out_specs=(pl.BlockSpec((tr, d), lambda i: (i, 0)), pl.BlockSpec((tr, 1), lambda i: (i, 0))),
        compiler_params=_cparams("parallel"),
    )(x, g.reshape(1, d))


def _norm_bwd_tile(dh, x, r, g, dres):
    xh = x * r
    dxh = dh * g
    m = jnp.mean(dxh * xh, axis=-1, keepdims=True)
    return r * (dxh - xh * m) + dres, jnp.sum(dh * xh, axis=0, keepdims=True)


def _mm_norm_bwd(a, w, x, r, g, dres, *, name, tm=1024, tk=1024):
    M, K = a.shape
    D = w.shape[1]
    tm, tk = _tile(M, tm, 8), _tile(K, tk)
    nk = K // tk

    def body(a_ref, w_ref, x_ref, r_ref, g_ref, dres_ref, dx_ref, dg_ref, acc_ref):
        i, k = pl.program_id(0), pl.program_id(1)
        part = jnp.dot(a_ref[...].astype(BF16), w_ref[...], preferred_element_type=F32)

        @pl.when(k == 0)
        def _():
            acc_ref[...] = part

        @pl.when(k > 0)
        def _():
            acc_ref[...] += part

        @pl.when(k == nk - 1)
        def _():
            dx, dg = _norm_bwd_tile(acc_ref[...], x_ref[...], r_ref[...], g_ref[...], dres_ref[...])
            dx_ref[...] = dx

            @pl.when(i == 0)
            def _():
                dg_ref[...] = dg

            @pl.when(i > 0)
            def _():
                dg_ref[...] += dg

    row = pl.BlockSpec((tm, D), lambda i, k: (i, 0))
    vec = pl.BlockSpec((1, D), lambda i, k: (0, 0))
    return pl.pallas_call(
        body, name=name,
        out_shape=(jax.ShapeDtypeStruct((M, D), F32), jax.ShapeDtypeStruct((1, D), F32)),
        grid=(M // tm, nk),
        in_specs=[pl.BlockSpec((tm, tk), lambda i, k: (i, k)), pl.BlockSpec((tk, D), lambda i, k: (k, 0)),
                  row, pl.BlockSpec((tm, 1), lambda i, k: (i, 0)), vec, row],
        out_specs=(row, vec),
        scratch_shapes=[pltpu.VMEM((tm, D), F32)],
        compiler_params=_cparams("arbitrary", "arbitrary"),
    )(a, w, x, r, g.reshape(1, D), dres)


def _rmsnorm_bwd(dh, x, r, g, dres, *, name, tr=512):
    n, d = x.shape
    tr = _tile(n, tr, 8)

    def body(dh_ref, x_ref, r_ref, g_ref, dres_ref, dx_ref, dg_ref):
        i = pl.program_id(0)
        dx_ref[...], part = _norm_bwd_tile(dh_ref[...].astype(F32), x_ref[...], r_ref[...], g_ref[...],
                                           dres_ref[...])

        @pl.when(i == 0)
        def _():
            dg_ref[...] = part

        @pl.when(i > 0)
        def _():
            dg_ref[...] += part

    row = pl.BlockSpec((tr, d), lambda i: (i, 0))
    vec = pl.BlockSpec((1, d), lambda i: (0, 0))
    return pl.pallas_call(
        body, name=name,
        out_shape=(jax.ShapeDtypeStruct((n, d), F32), jax.ShapeDtypeStruct((1, d), F32)),
        grid=(n // tr,),
        in_specs=[row, row, pl.BlockSpec((tr, 1), lambda i: (i, 0)), vec, row],
        out_specs=(row, vec),
        compiler_params=_cparams("arbitrary"),
    )(dh, x, r, g.reshape(1, d), dres)


def _loss_head(x, g, target, *, name, tr=512):
    n, d = x.shape
    tr = _tile(n, tr, 8)

    def body(x_ref, g_ref, t_ref, dx_ref, dg_ref, loss_ref):
        i = pl.program_id(0)
        xv = x_ref[...]
        gv = g_ref[...]
        r = lax.rsqrt(jnp.mean(xv * xv, axis=-1, keepdims=True) + RMS_EPS)
        xh = xv * r
        err = xh * gv - t_ref[...]
        lpart = 0.5 * jnp.sum(jnp.mean(err * err, axis=-1, keepdims=True), axis=0, keepdims=True)
        dy = err * (1.0 / d)
        dxh = dy * gv
        m = jnp.mean(dxh * xh, axis=-1, keepdims=True)
        dx_ref[...] = r * (dxh - xh * m)
        gpart = jnp.sum(dy * xh, axis=0, keepdims=True)
        lrow = jnp.broadcast_to(lpart, (1, LANES))

        @pl.when(i == 0)
        def _():
            dg_ref[...] = gpart
            loss_ref[...] = lrow

        @pl.when(i > 0)
        def _():
            dg_ref[...] += gpart
            loss_ref[...] += lrow

    row = pl.BlockSpec((tr, d), lambda i: (i, 0))
    vec = pl.BlockSpec((1, d), lambda i: (0, 0))
    return pl.pallas_call(
        body, name=name,
        out_shape=(jax.ShapeDtypeStruct((n, d), F32), jax.ShapeDtypeStruct((1, d), F32),
                   jax.ShapeDtypeStruct((1, LANES), F32)),
        grid=(n // tr,),
        in_specs=[row, vec, row],
        out_specs=(row, vec, pl.BlockSpec((1, LANES), lambda i: (0, 0))),
        compiler_params=_cparams("arbitrary"),
    )(x, g.reshape(1, d), target)


def _tri_dot(v, tri):
    hi = v.astype(BF16)
    r1 = v - hi.astype(F32)
    mid = r1.astype(BF16)
    lo = (r1 - mid.astype(F32)).astype(BF16)
    d = functools.partial(jnp.dot, preferred_element_type=F32)
    return d(hi, tri) + d(mid, tri) + d(lo, tri)


def _fox_gate_fwd(ft, bf, *, name, blk=256):
    B, H, S = ft.shape
    blk = _tile(S, blk)
    nb = S // blk

    def body(f_ref, b_ref, o_ref):
        x = f_ref[0] + b_ref[...]
        logf = jnp.minimum(x, 0.0) - jnp.log(1.0 + jnp.exp(-jnp.abs(x)))
        rr = lax.broadcasted_iota(jnp.int32, (blk, blk), 0)
        cc = lax.broadcasted_iota(jnp.int32, (blk, blk), 1)
        tri = (rr <= cc).astype(BF16)
        carry = jnp.zeros((H, 1), F32)
        for n in range(nb):
            c = _tri_dot(logf[:, n * blk:(n + 1) * blk], tri) + carry
            o_ref[0, :, n * blk:(n + 1) * blk] = c
            carry = c[:, blk - 1:blk]

    return pl.pallas_call(
        body, name=name,
        out_shape=jax.ShapeDtypeStruct((B, H, S), F32),
        grid=(B,),
        in_specs=[pl.BlockSpec((1, H, S), lambda b: (b, 0, 0)), pl.BlockSpec((H, 1), lambda b: (0, 0))],
        out_specs=pl.BlockSpec((1, H, S), lambda b: (b, 0, 0)),
        compiler_params=_cparams("parallel"),
    )(ft, bf.reshape(H, 1))


def _fox_gate_bwd(dF, ft, bf, *, name, blk=256):
    B, H, S = ft.shape
    blk = _tile(S, blk)
    nb = S // blk

    def body(d_ref, f_ref, b_ref, o_ref, db_ref):
        b = pl.program_id(0)
        x = f_ref[0] + b_ref[...]
        sneg = 1.0 / (1.0 + jnp.exp(x))
        dv = d_ref[0]
        rr = lax.broadcasted_iota(jnp.int32, (blk, blk), 0)
        cc = lax.broadcasted_iota(jnp.int32, (blk, blk), 1)
        tri = (rr >= cc).astype(BF16)
        carry = jnp.zeros((H, 1), F32)
        tot = jnp.zeros((H, 1), F32)
        for n in reversed(range(nb)):
            sl = slice(n * blk, (n + 1) * blk)
            c = _tri_dot(dv[:, sl], tri) + carry
            g = c * sneg[:, sl]
            o_ref[0, :, sl] = g
            tot = tot + jnp.sum(g, axis=1, keepdims=True)
            carry = c[:, 0:1]

        @pl.when(b == 0)
        def _():
            db_ref[...] = tot

        @pl.when(b > 0)
        def _():
            db_ref[...] += tot

    blkspec = pl.BlockSpec((1, H, S), lambda b: (b, 0, 0))
    return pl.pallas_call(
        body, name=name,
        out_shape=(jax.ShapeDtypeStruct((B, H, S), F32), jax.ShapeDtypeStruct((H, 1), F32)),
        grid=(B,),
        in_specs=[blkspec, blkspec, pl.BlockSpec((H, 1), lambda b: (0, 0))],
        out_specs=(blkspec, pl.BlockSpec((H, 1), lambda b: (0, 0))),
        compiler_params=_cparams("arbitrary"),
    )(dF, ft, bf.reshape(H, 1))


def _head_masks():
    lane = lax.broadcasted_iota(jnp.int32, (1, LANES), 1)
    return [lane < HEAD_DIM, lane >= HEAD_DIM]


HBM = pl.BlockSpec(memory_space=pltpu.HBM)
MESH = pl.DeviceIdType.MESH


def _direct_copies(kinds, x_refs, out_refs, send_sems, recv_sems, local_sems):
    x, y, c = lax.axis_index("x"), lax.axis_index("y"), lax.axis_index("c")
    me = 4 * x + 2 * y + c
    copies = []
    for a, (kind, xr, outr) in enumerate(zip(kinds, x_refs, out_refs)):
        copies.append(pltpu.make_async_copy(xr if kind == "gather" else xr.at[me], outr.at[me], local_sems.at[a]))
    for k in range(1, N_DEV):
        px = 1 - x if (k >> 2) & 1 else x
        py = 1 - y if (k >> 1) & 1 else y
        pc = 1 - c if k & 1 else c
        for a, (kind, xr, outr) in enumerate(zip(kinds, x_refs, out_refs)):
            copies.append(pltpu.make_async_remote_copy(
                src_ref=xr if kind == "gather" else xr.at[4 * px + 2 * py + pc], dst_ref=outr.at[me],
                send_sem=send_sems.at[7 * a + k - 1], recv_sem=recv_sems.at[7 * a + k - 1],
                device_id=(px, py, pc), device_id_type=MESH))
    return copies


class _Hosted:
    def __init__(self, gather=(), exchange=()):
        self.arrays = list(gather) + list(exchange)
        self.kinds = ["gather"] * len(gather) + ["exchange"] * len(exchange)
        self.n = len(self.arrays)
        self.out_shape = [jax.ShapeDtypeStruct(((N_DEV,) if k == "gather" else ()) + a.shape, a.dtype)
                          for k, a in zip(self.kinds, self.arrays)]
        self.scratch = [pltpu.SemaphoreType.DMA((7 * self.n,)), pltpu.SemaphoreType.DMA((7 * self.n,)),
                        pltpu.SemaphoreType.DMA((self.n,))]

    def run(self, first, last, x_refs, out_refs, sems):
        def go(when, act):
            @pl.when(when)
            def _():
                for cp in _direct_copies(self.kinds, x_refs, out_refs, *sems):
                    act(cp)
        return (lambda: go(first, lambda cp: cp.start())), (lambda: go(last, lambda cp: cp.wait()))


def _stack_heads(x, masks):
    zero = jnp.zeros_like(x)
    return jnp.concatenate([jnp.where(masks[0], x, zero), jnp.where(masks[1], x, zero)], axis=0)


def _attn_fwd(proj, frow, *, name, H, tq, hosted=None):
    B, S, _ = proj.shape
    HP = H // 2
    nq = S // tq
    scale = HEAD_DIM ** -0.5
    nh = hosted.n if hosted else 0

    def body(*refs):
        q_ref, k_ref, v_ref, fk_ref = refs[:4]
        o_ref, lse_ref = refs[4 + nh:6 + nh]
        i = pl.program_id(2)
        if hosted:
            b, hp = pl.program_id(0), pl.program_id(1)
            start, finish = hosted.run((b == 0) & (hp == 0) & (i == 0), (b == B - 1) & (hp == HP - 1) & (i == nq - 1),
                                       refs[4:4 + nh], refs[6 + nh:6 + 2 * nh], refs[6 + 2 * nh:])
            start()
        masks = _head_masks()
        q2 = _stack_heads(q_ref[0], masks) * jnp.asarray(scale, BF16)
        rr = lax.broadcasted_iota(jnp.int32, (tq, tq), 0)
        cc = lax.broadcasted_iota(jnp.int32, (tq, tq), 1)
        causal = rr >= cc

        def block(j, carry, masked):
            rows = pl.ds(pl.multiple_of(j * tq, tq), tq)
            kj = k_ref[0, rows, :]
            vj = v_ref[0, rows, :]
            s2 = lax.dot_general(q2, kj, NT, preferred_element_type=F32)
            new, ps = [], []
            for h in range(2):
                m, l, acc = carry[h]
                s = s2[h * tq:(h + 1) * tq] - fk_ref[0, 0, h, pl.ds(j, 1), :]
                if masked:
                    s = jnp.where(causal, s, NEG)
                m_new = jnp.maximum(m, jnp.max(s, axis=-1, keepdims=True))
                alpha = jnp.exp(m - m_new)
                p = jnp.exp(s - m_new)
                new.append((m_new, alpha * l + jnp.sum(p, axis=-1, keepdims=True), alpha, acc))
                ps.append(p.astype(BF16))
            pv = jnp.dot(jnp.concatenate(ps, axis=0), vj, preferred_element_type=F32)
            return tuple((m, l, alpha * acc + pv[h * tq:(h + 1) * tq]) for h, (m, l, alpha, acc) in enumerate(new))

        one = (jnp.full((tq, 1), NEG, F32), jnp.zeros((tq, 1), F32), jnp.zeros((tq, LANES), F32))
        carry = lax.fori_loop(0, i, lambda j, c: block(j, c, False), (one, one))
        (m0, l0, a0), (m1, l1, a1) = block(i, carry, True)
        o_ref[0] = jnp.where(masks[0], a0 / l0, a1 / l1).astype(BF16)
        two = lax.broadcasted_iota(jnp.int32, (1, 2), 1)
        lse_ref[0, 0] = jnp.where(two == 0, m0 + jnp.log(l0), m1 + jnp.log(l1))
        if hosted:
            finish()

    kv = lambda off: pl.BlockSpec((1, S, LANES), lambda b, hp, i: (b, 0, off + hp))
    outs = pl.pallas_call(
        body, name=name,
        out_shape=[jax.ShapeDtypeStruct((B, S, H * HEAD_DIM), BF16), jax.ShapeDtypeStruct((B, HP, S, 2), F32)]
        + (hosted.out_shape if hosted else []),
        grid=(B, HP, nq),
        in_specs=[pl.BlockSpec((1, tq, LANES), lambda b, hp, i: (b, i, hp)), kv(HP), kv(2 * HP),
                  pl.BlockSpec((1, 1, 2, nq, tq), lambda b, hp, i: (b, hp, 0, 0, 0))] + [HBM] * nh,
        out_specs=[pl.BlockSpec((1, tq, LANES), lambda b, hp, i: (b, i, hp)),
                   pl.BlockSpec((1, 1, tq, 2), lambda b, hp, i: (b, hp, i, 0))] + [HBM] * nh,
        scratch_shapes=hosted.scratch if hosted else [],
        compiler_params=_cparams("arbitrary", "arbitrary", "arbitrary"),
    )(proj, proj, proj, frow, *(hosted.arrays if hosted else []))
    return outs[0], outs[1], outs[2:]


def _attn_bwd(proj, ya, dya, lse, frow, *, name, H, tq, hosted=None):
    B, S, _ = proj.shape
    HP = H // 2
    nq = S // tq
    AW = H * HEAD_DIM
    scale = HEAD_DIM ** -0.5
    nh = hosted.n if hosted else 0

    def body(*refs):
        q_ref, k_ref, v_ref, o_ref, do_ref, lse_ref, fk_ref = refs[:7]
        dq_ref, dk_ref, dv_ref, dfk_ref, dfq_ref = refs[7 + nh:12 + nh]
        (q2_ref, do2_ref, lse2_ref, delta2_ref, dq2_acc, dfq2_acc, dk_acc, dv_acc,
         dfk_acc) = refs[12 + 2 * nh:21 + 2 * nh]
        if hosted:
            b, hp = pl.program_id(0), pl.program_id(1)
            start, finish = hosted.run((b == 0) & (hp == 0), (b == B - 1) & (hp == HP - 1),
                                       refs[7:7 + nh], refs[12 + nh:12 + 2 * nh], refs[21 + 2 * nh:])
            start()
        masks = _head_masks()
        rr = lax.broadcasted_iota(jnp.int32, (tq, tq), 0)
        cc = lax.broadcasted_iota(jnp.int32, (tq, tq), 1)
        causal = rr >= cc
        sc = jnp.asarray(scale, BF16)

        def stage(i, c):
            rows = pl.ds(pl.multiple_of(i * tq, tq), tq)
            dov = do_ref[0, rows, :]
            q2_ref[i] = _stack_heads(q_ref[0, rows, :], masks) * sc
            do2_ref[i] = _stack_heads(dov, masks)
            prod = dov.astype(F32) * o_ref[0, rows, :].astype(F32)
            delta2_ref[i] = jnp.concatenate(
                [jnp.sum(jnp.where(masks[h], prod, 0.0), axis=-1, keepdims=True) for h in range(2)], axis=0)
            lv = lse_ref[0, 0, rows, :]
            lse2_ref[i] = jnp.concatenate([lv[:, 0:1], lv[:, 1:2]], axis=0)
            return c

        lax.fori_loop(0, nq, stage, 0)
        dq2_acc[...] = jnp.zeros_like(dq2_acc)
        dfq2_acc[...] = jnp.zeros_like(dfq2_acc)

        def kv_block(j, carry):
            rows_j = pl.ds(pl.multiple_of(j * tq, tq), tq)
            kj = k_ref[0, rows_j, :]
            vj = v_ref[0, rows_j, :]
            ks = kj * sc
            dk_acc[...] = jnp.zeros_like(dk_acc)
            dv_acc[...] = jnp.zeros_like(dv_acc)
            dfk_acc[...] = jnp.zeros_like(dfk_acc)

            def logits(i):
                return (lax.dot_general(q2_ref[i], kj, NT, preferred_element_type=F32),
                        lax.dot_general(do2_ref[i], vj, NT, preferred_element_type=F32))

            def probs(i, s2, dp2, masked):
                lse2 = lse2_ref[i]
                delta2 = delta2_ref[i]
                ps, dss = [], []
                for h in range(2):
                    half = slice(h * tq, (h + 1) * tq)
                    p = jnp.exp(s2[half] - fk_ref[0, 0, h, pl.ds(j, 1), :] - lse2[half])
                    if masked:
                        p = jnp.where(causal, p, 0.0)
                    ds = p * (dp2[half] - delta2[half])
                    dfk_acc[h:h + 1, :] -= jnp.sum(ds, axis=0, keepdims=True)
                    dfq2_acc[i, half, :] += jnp.sum(ds, axis=1, keepdims=True)
                    ps.append(p.astype(BF16))
                    dss.append(ds.astype(BF16))
                return jnp.concatenate(ps, axis=0), jnp.concatenate(dss, axis=0)

            def grads(i, p2, ds2):
                dv_acc[...] += lax.dot_general(p2, do2_ref[i], TN, preferred_element_type=F32)
                dk_acc[...] += lax.dot_general(ds2, q2_ref[i], TN, preferred_element_type=F32)
                dq2_acc[i] += jnp.dot(ds2, ks, preferred_element_type=F32)

            grads(j, *probs(j, *logits(j), True))

            def rest(i, c):
                grads(i, *probs(i, *logits(i), False))
                return c

            lax.fori_loop(j + 1, nq, rest, 0)
            dk_ref[0, rows_j, :] = dk_acc[...].astype(BF16)
            dv_ref[0, rows_j, :] = dv_acc[...].astype(BF16)
            for h in range(2):
                dfk_ref[0, 0, h, pl.ds(j, 1), :] = dfk_acc[h:h + 1, :]
            return carry

        lax.fori_loop(0, nq, kv_block, 0)
        two = lax.broadcasted_iota(jnp.int32, (1, 2), 1)

        def finish_block(i, c):
            rows = pl.ds(pl.multiple_of(i * tq, tq), tq)
            dq2 = dq2_acc[i]
            dq_ref[0, rows, :] = jnp.where(masks[0], dq2[:tq], dq2[tq:]).astype(BF16)
            dfq2 = dfq2_acc[i]
            dfq_ref[0, 0, rows, :] = jnp.where(two == 0, dfq2[:tq], dfq2[tq:])
            return c

        lax.fori_loop(0, nq, finish_block, 0)
        if hosted:
            finish()

    col = lambda off: pl.BlockSpec((1, S, LANES), lambda b, hp: (b, 0, off + hp))
    stat = pl.BlockSpec((1, 1, S, 2), lambda b, hp: (b, hp, 0, 0))
    rowf = pl.BlockSpec((1, 1, 2, nq, tq), lambda b, hp: (b, hp, 0, 0, 0))
    grad = jax.ShapeDtypeStruct((B, S, AW), BF16)
    outs = pl.pallas_call(
        body, name=name,
        out_shape=[grad, grad, grad, jax.ShapeDtypeStruct((B, HP, 2, nq, tq), F32),
                   jax.ShapeDtypeStruct((B, HP, S, 2), F32)] + (hosted.out_shape if hosted else []),
        grid=(B, HP),
        in_specs=[col(0), col(HP), col(2 * HP), col(0), col(0), stat, rowf] + [HBM] * nh,
        out_specs=[col(0), col(0), col(0), rowf, stat] + [HBM] * nh,
        scratch_shapes=[pltpu.VMEM((nq, 2 * tq, LANES), BF16), pltpu.VMEM((nq, 2 * tq, LANES), BF16),
                        pltpu.VMEM((nq, 2 * tq, 1), F32), pltpu.VMEM((nq, 2 * tq, 1), F32),
                        pltpu.VMEM((nq, 2 * tq, LANES), F32), pltpu.VMEM((nq, 2 * tq, 1), F32),
                        pltpu.VMEM((tq, LANES), F32), pltpu.VMEM((tq, LANES), F32), pltpu.VMEM((2, tq), F32)]
        + (hosted.scratch if hosted else []),
        compiler_params=_cparams("arbitrary", "arbitrary"),
    )(proj, proj, proj, ya, dya, lse, frow, *(hosted.arrays if hosted else []))
    return outs[:5], outs[5:]


def _cmul(ar, ai, br, bi):
    return ar * br - ai * bi, ar * bi + ai * br


def _ssm_states(u_ref, bm, lam_ref, pw_ref, lamT_ref, hr_ref, hi_ref, inr_ref, ini_ref, T, NC, SP):
    lr, li = lam_ref[0, 0:1, :], lam_ref[0, 1:2, :]
    bu = jnp.dot(u_ref[0, 0], bm, preferred_element_type=F32)
    hr_ref[0] = bu[:, :SP]
    hi_ref[0] = bu[:, SP:]

    def step(t, c):
        bu = jnp.dot(u_ref[0, t], bm, preferred_element_type=F32)
        pr, pi = _cmul(hr_ref[t - 1], hi_ref[t - 1], lr, li)
        hr_ref[t] = pr + bu[:, :SP]
        hi_ref[t] = pi + bu[:, SP:]
        return c

    lax.fori_loop(1, T, step, 0, unroll=2)

    tr, ti = lamT_ref[0, 0:1, :], lamT_ref[0, 1:2, :]
    inr_ref[0:1, :] = jnp.zeros((1, SP), F32)
    ini_ref[0:1, :] = jnp.zeros((1, SP), F32)

    def chunk(n, c):
        prev = pl.ds(n - 1, 1)
        pr, pi = _cmul(inr_ref[prev, :], ini_ref[prev, :], tr, ti)
        inr_ref[pl.ds(n, 1), :] = pr + hr_ref[T - 1, prev, :]
        ini_ref[pl.ds(n, 1), :] = pi + hi_ref[T - 1, prev, :]
        return c

    lax.fori_loop(1, NC, chunk, 0)


def _ssm_entry_term(t, pw_ref, inr_ref, ini_ref):
    return _cmul(inr_ref[...], ini_ref[...], pw_ref[0, 0, pl.ds(t, 1), :], pw_ref[0, 1, pl.ds(t, 1), :])


def _ssm_fwd(u_tm, bmat, cmat, lam, pw, lamT, dskip, *, name):
    B, T, NC, W = u_tm.shape
    NS = W // LANES
    SP = bmat.shape[2] // 2

    def body(u_ref, b_ref, c_ref, lam_ref, pw_ref, lamT_ref, d_ref, y_ref, hr_ref, hi_ref, inr_ref, ini_ref):
        _ssm_states(u_ref, b_ref[0], lam_ref, pw_ref, lamT_ref, hr_ref, hi_ref, inr_ref, ini_ref, T, NC, SP)
        cm = c_ref[0]
        dv = d_ref[...]

        def out(t, c):
            cr, ci = _ssm_entry_term(t, pw_ref, inr_ref, ini_ref)
            hcat = jnp.concatenate([hr_ref[t] + cr, hi_ref[t] + ci], axis=1).astype(BF16)
            y_ref[0, t] = jnp.dot(hcat, cm, preferred_element_type=F32) + dv * u_ref[0, t].astype(F32)
            return c

        lax.fori_loop(0, T, out, 0, unroll=2)

    slab = lambda *shape: pl.BlockSpec((1,) + shape, lambda b, s: (s,) + (0,) * len(shape))
    tok = pl.BlockSpec((1, T, NC, LANES), lambda b, s: (b, 0, 0, s))
    return pl.pallas_call(
        body, name=name,
        out_shape=jax.ShapeDtypeStruct((B, T, NC, W), F32),
        grid=(B, NS),
        in_specs=[tok, slab(LANES, 2 * SP), slab(2 * SP, LANES), slab(2, SP), slab(2, T, SP), slab(2, SP),
                  pl.BlockSpec((1, LANES), lambda b, s: (0, s))],
        out_specs=tok,
        scratch_shapes=[pltpu.VMEM((T, NC, SP), F32), pltpu.VMEM((T, NC, SP), F32),
                        pltpu.VMEM((NC, SP), F32), pltpu.VMEM((NC, SP), F32)],
        compiler_params=_cparams("parallel", "parallel"),
    )(u_tm, bmat, cmat, lam, pw, lamT, dskip)


def _ssm_bwd(u_tm, dy_tm, bmat, bmat_t, cmat_t, lam, pw, lamT, dskip, *, name):
    B, T, NC, W = u_tm.shape
    NS = W // LANES
    SP = bmat.shape[2] // 2

    def body(u_ref, dy_ref, b_ref, bt_ref, ct_ref, lam_ref, pw_ref, lamT_ref, d_ref,
             du_ref, gb_ref, gc_ref, glam_ref, gd_ref,
             hr_ref, hi_ref, ar_ref, ai_ref, inr_ref, ini_ref, anr_ref, ani_ref):
        b = pl.program_id(1)
        _ssm_states(u_ref, b_ref[0], lam_ref, pw_ref, lamT_ref, hr_ref, hi_ref, inr_ref, ini_ref, T, NC, SP)
        lr, li = lam_ref[0, 0:1, :], lam_ref[0, 1:2, :]
        ct = ct_ref[0]
        bt = bt_ref[0]
        dv = d_ref[...]

        gh = jnp.dot(dy_ref[0, T - 1].astype(BF16), ct, preferred_element_type=F32)
        ar_ref[T - 1] = gh[:, :SP]
        ai_ref[T - 1] = gh[:, SP:]

        def back(k, c):
            t = T - 2 - k
            gh = jnp.dot(dy_ref[0, t].astype(BF16), ct, preferred_element_type=F32)
            pr, pi = _cmul(ar_ref[t + 1], ai_ref[t + 1], lr, -li)
            ar_ref[t] = pr + gh[:, :SP]
            ai_ref[t] = pi + gh[:, SP:]
            return c

        lax.fori_loop(0, T - 1, back, 0, unroll=2)

        tr, ti = lamT_ref[0, 0:1, :], lamT_ref[0, 1:2, :]
        anr_ref[NC - 1:NC, :] = jnp.zeros((1, SP), F32)
        ani_ref[NC - 1:NC, :] = jnp.zeros((1, SP), F32)

        def chunk(k, c):
            n = NC - 2 - k
            nxt = pl.ds(n + 1, 1)
            pr, pi = _cmul(anr_ref[nxt, :], ani_ref[nxt, :], tr, -ti)
            anr_ref[pl.ds(n, 1), :] = pr + ar_ref[0, nxt, :]
            ani_ref[pl.ds(n, 1), :] = pi + ai_ref[0, nxt, :]
            return c

        lax.fori_loop(0, NC - 1, chunk, 0)

        @pl.when(b == 0)
        def _():
            gb_ref[...] = jnp.zeros_like(gb_ref)
            gc_ref[...] = jnp.zeros_like(gc_ref)
            glam_ref[...] = jnp.zeros_like(glam_ref)
            gd_ref[...] = jnp.zeros_like(gd_ref)

        def final(t, hpr, hpi, gl):
            back_pow = pl.ds(T - 1 - t, 1)
            cr, ci = _cmul(anr_ref[...], ani_ref[...], pw_ref[0, 0, back_pow, :], -pw_ref[0, 1, back_pow, :])
            a_r = ar_ref[t] + cr
            a_i = ai_ref[t] + ci
            gl = (gl[0] + jnp.sum(a_r * hpr + a_i * hpi, axis=0, keepdims=True),
                  gl[1] + jnp.sum(a_i * hpr - a_r * hpi, axis=0, keepdims=True))
            acat = jnp.concatenate([a_r, a_i], axis=1).astype(BF16)
            ut = u_ref[0, t]
            dyt = dy_ref[0, t]
            du_ref[0, t] = (jnp.dot(acat, bt, preferred_element_type=F32) + dv * dyt).astype(BF16)
            gb_ref[0] += lax.dot_general(acat, ut, TN, preferred_element_type=F32)
            er, ei = _ssm_entry_term(t, pw_ref, inr_ref, ini_ref)
            h_r = hr_ref[t] + er
            h_i = hi_ref[t] + ei
            hr_ref[t] = h_r
            hi_ref[t] = h_i
            hcat = jnp.concatenate([h_r, h_i], axis=1).astype(BF16)
            gc_ref[0] += lax.dot_general(dyt.astype(BF16), hcat, TN, preferred_element_type=F32)
            gd_ref[0] += jnp.sum(dyt * ut.astype(F32), axis=0, keepdims=True)
            return gl

        zero = jnp.zeros((1, SP), F32)
        gl = final(0, inr_ref[...], ini_ref[...], (zero, zero))
        gl = lax.fori_loop(1, T, lambda t, gl: final(t, hr_ref[t - 1], hi_ref[t - 1], gl), gl)
        glam_ref[0, 0:1, :] += gl[0]
        glam_ref[0, 1:2, :] += gl[1]

    slab = lambda *shape: pl.BlockSpec((1,) + shape, lambda s, b: (s,) + (0,) * len(shape))
    tok = pl.BlockSpec((1, T, NC, LANES), lambda s, b: (b, 0, 0, s))
    big = pltpu.VMEM((T, NC, SP), F32)
    small = pltpu.VMEM((NC, SP), F32)
    return pl.pallas_call(
        body, name=name,
        out_shape=(jax.ShapeDtypeStruct((B, T, NC, W), BF16),
                   jax.ShapeDtypeStruct((NS, 2 * SP, LANES), F32), jax.ShapeDtypeStruct((NS, LANES, 2 * SP), F32),
                   jax.ShapeDtypeStruct((NS, 2, SP), F32), jax.ShapeDtypeStruct((NS, 1, LANES), F32)),
        grid=(NS, B),
        in_specs=[tok, tok, slab(LANES, 2 * SP), slab(2 * SP, LANES), slab(LANES, 2 * SP), slab(2, SP),
                  slab(2, T, SP), slab(2, SP), pl.BlockSpec((1, LANES), lambda s, b: (0, s))],
        out_specs=(tok, slab(2 * SP, LANES), slab(LANES, 2 * SP), slab(2, SP), slab(1, LANES)),
        scratch_shapes=[big, big, big, big, small, small, small, small],
        compiler_params=_cparams("parallel", "arbitrary"),
    )(u_tm, dy_tm, bmat, bmat_t, cmat_t, lam, pw, lamT, dskip)


def _glu_fwd(ys, w, b, *, name, tr=512):
    n, wd = ys.shape
    tr = _tile(n, tr, 8)

    def body(y_ref, w_ref, b_ref, o_ref):
        yb = _gelu(y_ref[...])
        z = jnp.dot(yb.astype(BF16), w_ref[...], preferred_element_type=F32) + b_ref[...]
        o_ref[...] = (yb * _sigmoid(z)).astype(BF16)

    row = pl.BlockSpec((tr, wd), lambda i: (i, 0))
    return pl.pallas_call(
        body, name=name, out_shape=jax.ShapeDtypeStruct((n, wd), BF16), grid=(n // tr,),
        in_specs=[row, pl.BlockSpec((wd, wd), lambda i: (0, 0)), pl.BlockSpec((1, wd), lambda i: (0, 0))],
        out_specs=row, compiler_params=_cparams("parallel"),
    )(ys, w, b.reshape(1, wd))


def _glu_bwd(ys, dyb2, w, w_t, b, *, name, tr=512):
    n, wd = ys.shape
    tr = _tile(n, tr, 8)

    def body(y_ref, d_ref, w_ref, wt_ref, b_ref, dys_ref, dz_ref, yb_ref, db_ref):
        i = pl.program_id(0)
        yv = y_ref[...]
        yb = _gelu(yv)
        ybb = yb.astype(BF16)
        sg = _sigmoid(jnp.dot(ybb, w_ref[...], preferred_element_type=F32) + b_ref[...])
        dv = d_ref[...].astype(F32)
        dz = dv * yb * sg * (1.0 - sg)
        dzb = dz.astype(BF16)
        dyb = dv * sg + jnp.dot(dzb, wt_ref[...], preferred_element_type=F32)
        dys_ref[...] = dyb * _gelu_grad(yv)
        dz_ref[...] = dzb
        yb_ref[...] = ybb
        part = jnp.sum(dz, axis=0, keepdims=True)

        @pl.when(i == 0)
        def _():
            db_ref[...] = part

        @pl.when(i > 0)
        def _():
            db_ref[...] += part

    row = pl.BlockSpec((tr, wd), lambda i: (i, 0))
    mat = pl.BlockSpec((wd, wd), lambda i: (0, 0))
    vec = pl.BlockSpec((1, wd), lambda i: (0, 0))
    return pl.pallas_call(
        body, name=name,
        out_shape=(jax.ShapeDtypeStruct((n, wd), F32), jax.ShapeDtypeStruct((n, wd), BF16),
                   jax.ShapeDtypeStruct((n, wd), BF16), jax.ShapeDtypeStruct((1, wd), F32)),
        grid=(n // tr,), in_specs=[row, row, mat, mat, vec], out_specs=(row, row, row, vec),
        compiler_params=_cparams("arbitrary"),
    )(ys, dyb2, w, w_t, b.reshape(1, wd))


def _merge_fwd(ya, yb2, wa_t, wb_t, proj, gate_blk, *, name, tr=512):
    n, aw = ya.shape
    d = wa_t.shape[0]
    tr = _tile(n, tr, 8)
    wa, wb = wa_t, wb_t

    def body(ya_ref, yb_ref, wa_ref, wb_ref, ga_ref, gb_ref, mix_ref, pa_ref, pb_ref):
        pa = lax.dot_general(ya_ref[...], wa_ref[...], NT, preferred_element_type=F32)
        pb = lax.dot_general(yb_ref[...], wb_ref[...], NT, preferred_element_type=F32)
        mix = _sigmoid(ga_ref[...].astype(F32)) * pa + _sigmoid(gb_ref[...].astype(F32)) * pb
        mix_ref[...] = mix.astype(BF16)
        pa_ref[...] = pa.astype(BF16)
        pb_ref[...] = pb.astype(BF16)

    row = lambda wdt: pl.BlockSpec((tr, wdt), lambda i: (i, 0))
    full = lambda r, c: pl.BlockSpec((r, c), lambda i: (0, 0))
    out = jax.ShapeDtypeStruct((n, d), BF16)
    return pl.pallas_call(
        body, name=name, out_shape=(out, out, out), grid=(n // tr,),
        in_specs=[row(aw), row(yb2.shape[1]), full(*wa.shape), full(*wb.shape),
                  pl.BlockSpec((tr, d), lambda i: (i, gate_blk)), pl.BlockSpec((tr, d), lambda i: (i, gate_blk + 1))],
        out_specs=(row(d), row(d), row(d)), compiler_params=_cparams("parallel"),
    )(ya, yb2, wa, wb, proj, proj)


def _merge_bwd(dx1, w_out, proj, pa, pb, gate_blk, dproj_cols, *, name, tr=512):
    n, d = dx1.shape
    tr = _tile(n, tr, 8)
    assert gate_blk % 2 == 0

    def body(dx_ref, w_ref, ga_ref, gb_ref, pa_ref, pb_ref, dpa_ref, dpb_ref, dg_ref):
        dm = lax.dot_general(dx_ref[...].astype(BF16), w_ref[...], NT, preferred_element_type=F32)
        sa = _sigmoid(ga_ref[...].astype(F32))
        sb = _sigmoid(gb_ref[...].astype(F32))
        dpa_ref[...] = (dm * sa).astype(BF16)
        dpb_ref[...] = (dm * sb).astype(BF16)
        dg_ref[:, :d] = (dm * pa_ref[...].astype(F32) * sa * (1.0 - sa)).astype(BF16)
        dg_ref[:, d:] = (dm * pb_ref[...].astype(F32) * sb * (1.0 - sb)).astype(BF16)

    row = pl.BlockSpec((tr, d), lambda i: (i, 0))
    out = jax.ShapeDtypeStruct((n, d), BF16)
    return pl.pallas_call(
        body, name=name, out_shape=(out, out, jax.ShapeDtypeStruct((n, dproj_cols), BF16)), grid=(n // tr,),
        in_specs=[row, pl.BlockSpec((d, d), lambda i: (0, 0)), pl.BlockSpec((tr, d), lambda i: (i, gate_blk)),
                  pl.BlockSpec((tr, d), lambda i: (i, gate_blk + 1)), row, row],
        out_specs=(row, row, pl.BlockSpec((tr, 2 * d), lambda i: (i, gate_blk // 2))),
        compiler_params=_cparams("parallel"),
    )(dx1, w_out, proj, proj, pa, pb)


def _outproj_fwd(mixed, w, x0, g, *, name, tr=512):
    n, d = x0.shape
    tr = _tile(n, tr, 8)

    def body(m_ref, w_ref, x_ref, g_ref, x1_ref, h_ref, r_ref):
        x1 = x_ref[...] + jnp.dot(m_ref[...], w_ref[...], preferred_element_type=F32)
        r = lax.rsqrt(jnp.mean(x1 * x1, axis=-1, keepdims=True) + RMS_EPS)
        x1_ref[...] = x1
        h_ref[...] = (x1 * r * g_ref[...]).astype(BF16)
        r_ref[...] = r

    row = pl.BlockSpec((tr, d), lambda i: (i, 0))
    return pl.pallas_call(
        body, name=name,
        out_shape=(jax.ShapeDtypeStruct((n, d), F32), jax.ShapeDtypeStruct((n, d), BF16),
                   jax.ShapeDtypeStruct((n, 1), F32)),
        grid=(n // tr,),
        in_specs=[row, pl.BlockSpec((d, d), lambda i: (0, 0)), row, pl.BlockSpec((1, d), lambda i: (0, 0))],
        out_specs=(row, row, pl.BlockSpec((tr, 1), lambda i: (i, 0))),
        compiler_params=_cparams("parallel"),
    )(mixed, w, x0, g.reshape(1, d))


def _adamw(w, g, m, v, *, name):
    shape = w.shape
    total = w.size
    if w.ndim == 3 and shape[1] % 8 == 0:
        lead, rows, cols = shape
    elif total % PACK_COLS == 0 and ((total // PACK_COLS) % 8 == 0 or total // PACK_COLS <= 512):
        lead, rows, cols = 1, total // PACK_COLS, PACK_COLS
    elif w.ndim >= 2:
        lead, rows, cols = 1, total // shape[-1], shape[-1]
    else:
        lead, rows, cols = 1, 1, total
    tr = _tile(rows, 512, 8)

    def body(w_ref, g_ref, m_ref, v_ref, d_ref, nm_ref, nv_ref):
        gv = g_ref[...]
        mn = ADAM_B1 * m_ref[...] + (1.0 - ADAM_B1) * gv
        vn = ADAM_B2 * v_ref[...] + (1.0 - ADAM_B2) * (gv * gv)
        m_hat = mn / (1.0 - ADAM_B1 ** ADAM_STEP)
        v_hat = vn / (1.0 - ADAM_B2 ** ADAM_STEP)
        d_ref[...] = -ADAM_LR * (m_hat / (jnp.sqrt(v_hat) + ADAM_EPS) + ADAM_WD * w_ref[...])
        nm_ref[...] = mn
        nv_ref[...] = vn

    blk = pl.BlockSpec((None, tr, cols), lambda l, i: (l, i, 0))
    out = jax.ShapeDtypeStruct((lead, rows, cols), F32)
    outs = pl.pallas_call(
        body, name=name, out_shape=(out, out, out), grid=(lead, rows // tr),
        in_specs=[blk] * 4, out_specs=(blk, blk, blk), compiler_params=_cparams("parallel", "parallel"),
    )(*[t.reshape(lead, rows, cols) for t in (w, g, m, v)])
    return tuple(o.reshape(shape) for o in outs)


def _all_gather(blocks, *, name):
    n = len(blocks)

    def body(*refs):
        x_refs, out_refs = refs[:n], refs[n:2 * n]
        send_sems, recv_sems, local_sems = refs[2 * n:]
        x, y, c = lax.axis_index("x"), lax.axis_index("y"), lax.axis_index("c")
        me, sibling = (x, y, c), (x, y, 1 - c)
        chips = [(1 - x, y), (x, 1 - y), (1 - x, 1 - y)]

        def slot(a, px, py, pc):
            return out_refs[a].at[4 * px + 2 * py + pc]

        def copy(a, k, block, to, src=None):
            return pltpu.make_async_remote_copy(
                src_ref=slot(a, *block) if src is None else src, dst_ref=slot(a, *block),
                send_sem=send_sems.at[7 * a + k], recv_sem=recv_sems.at[7 * a + k], device_id=to,
                device_id_type=MESH)

        started = []
        for a in range(n):
            mine = pltpu.make_async_copy(x_refs[a], slot(a, *me), local_sems.at[a])
            mine.start()
            started.append(mine)
        sends = []
        for a in range(n):
            first = [copy(a, 0, me, sibling, src=x_refs[a])]
            first += [copy(a, 1 + j, me, (*chip, c), src=x_refs[a]) for j, chip in enumerate(chips)]
            for cp in first:
                cp.start()
            sends += first
        for a in range(n):
            for j, chip in enumerate(chips):
                copy(a, 1 + j, (*chip, c), me).wait_recv()
                onward = copy(a, 4 + j, (*chip, c), sibling)
                onward.start()
                sends.append(onward)
        for a in range(n):
            copy(a, 0, sibling, me).wait_recv()
            for j, chip in enumerate(chips):
                copy(a, 4 + j, (*chip, 1 - c), me).wait_recv()
        for cp in sends:
            cp.wait_send()
        for mine in started:
            mine.wait()

    return pl.pallas_call(
        body, name=name, out_shape=[jax.ShapeDtypeStruct((N_DEV,) + b.shape, b.dtype) for b in blocks],
        in_specs=[HBM] * n, out_specs=[HBM] * n,
        scratch_shapes=[pltpu.SemaphoreType.DMA((7 * n,)), pltpu.SemaphoreType.DMA((7 * n,)),
                        pltpu.SemaphoreType.DMA((n,))],
    )(*blocks)


def _sum8(blocks, *, name, tr=SUM_ROWS):
    _, R, C = blocks.shape
    tr = _tile(R, tr, 16) if R % 16 == 0 else R

    def body(x_ref, o_ref):
        acc = x_ref[0].astype(F32)
        for i in range(1, N_DEV):
            acc = acc + x_ref[i].astype(F32)
        o_ref[...] = acc

    return pl.pallas_call(
        body, name=name, out_shape=jax.ShapeDtypeStruct((R, C), F32), grid=(R // tr,),
        in_specs=[pl.BlockSpec((N_DEV, tr, C), lambda i: (0, i, 0))],
        out_specs=pl.BlockSpec((tr, C), lambda i: (i, 0)), compiler_params=_cparams("parallel"),
    )(blocks)


def _pack(parts):
    flat = jnp.concatenate([p.astype(F32).reshape(-1) for p in parts])
    unit = 8 * PACK_COLS
    padded = -(-flat.size // unit) * unit
    return jnp.pad(flat, (0, padded - flat.size)).reshape(padded // PACK_COLS, PACK_COLS)


def _unpack(buf, like):
    flat, out, off = buf.reshape(-1), [], 0
    for p in like:
        out.append(flat[off:off + p.size].reshape(p.shape))
        off += p.size
    return out


def _ssm_discretise(lre, lim, logdt, bre, bim):
    lam = lax.complex(lre, lim)
    dt = jnp.exp(logdt)[:, None]
    lam_bar = jnp.exp(lam * dt)
    b_bar = ((lam_bar - 1.0) / lam)[:, :, None] * lax.complex(bre, bim)
    return lam_bar.real, lam_bar.imag, b_bar.real, b_bar.imag


def _block_diag(a, rows_first):
    ns, g, r, c = a.shape
    eye = jnp.eye(g, dtype=a.dtype)
    return jnp.einsum("sgrc,gh->sgrhc", a, eye).reshape(ns, g * r, g * c)


def _diag_blocks(m, r, c):
    ns = m.shape[0]
    g = SLAB_GROUPS
    return jnp.einsum("sgrhc,gh->sgrc", m.reshape(ns, g, r, g, c), jnp.eye(g, dtype=m.dtype))


def _to_tm(a, T):
    b, s, w = a.shape
    return a.reshape(b, s // T, T, w).transpose(0, 2, 1, 3)


def _from_tm(a):
    b, t, nc, w = a.shape
    return a.transpose(0, 2, 1, 3).reshape(b, nc * t, w)


WEIGHTS = ["norm_mix", "w_in", "b_forget", "ssm_lambda_re", "ssm_lambda_im", "ssm_log_dt", "ssm_b_re", "ssm_b_im",
           "ssm_c_re", "ssm_c_im", "ssm_d", "w_glu", "b_glu", "w_branch_a", "w_branch_b", "w_out", "norm_mlp",
           "w_mlp_up", "w_mlp_down", "norm_final"]
SHARDED = {"w_in": 2, "w_glu": 1, "w_branch_a": 2, "w_branch_b": 2, "w_out": 1, "w_mlp_up": 2, "w_mlp_down": 1}


REST = [n for n in SHARDED if n != "w_in"]
BY_COLUMNS = [n for n in SHARDED if SHARDED[n] == 2]


def _key(n):
    return n + "_t" if n in BY_COLUMNS else n


def _whole(seg):
    return seg.reshape(-1, seg.shape[-1])


def _blocks(g):
    return g.reshape(N_DEV, -1, g.shape[-1])


def _sum_blocks(n, got):
    return _sum8(got.reshape(N_DEV, -1, got.shape[-1]), name="sum_grads_" + n).reshape(got.shape[1:])


def kernel(x, norm_mix, w_in, b_forget, ssm_lambda_re, ssm_lambda_im, ssm_log_dt, ssm_b_re, ssm_b_im, ssm_c_re, ssm_c_im, ssm_d, w_glu, b_glu, w_branch_a, w_branch_b, w_out, norm_mlp, w_mlp_up, w_mlp_down, norm_final, loss_target, m_norm_mix, m_w_in, m_b_forget, m_ssm_lambda_re, m_ssm_lambda_im, m_ssm_log_dt, m_ssm_b_re, m_ssm_b_im, m_ssm_c_re, m_ssm_c_im, m_ssm_d, m_w_glu, m_b_glu, m_w_branch_a, m_w_branch_b, m_w_out, m_norm_mlp, m_w_mlp_up, m_w_mlp_down, m_norm_final, v_norm_mix, v_w_in, v_b_forget, v_ssm_lambda_re, v_ssm_lambda_im, v_ssm_log_dt, v_ssm_b_re, v_ssm_b_im, v_ssm_c_re, v_ssm_c_im, v_ssm_d, v_w_glu, v_b_glu, v_w_branch_a, v_w_branch_b, v_w_out, v_norm_mlp, v_w_mlp_up, v_w_mlp_down, v_norm_final):
    args = dict(locals())
    w = {n: args[n] for n in WEIGHTS}
    Bl, S, D = x.shape
    L, H = b_forget.shape
    G, P, C = ssm_b_re.shape[1:]
    AW, W, HP = H * HEAD_DIM, G * C, H // 2
    N = Bl * S
    T = SSM_CHUNK
    NS = G // SLAB_GROUPS
    SP = SLAB_GROUPS * P
    tq = min(ATTN_BLOCK, S)
    nq = S // tq
    u_off = 3 * AW
    gate_blk = (u_off + W) // D
    assert (u_off + W) % D == 0 and W % LANES == 0 and AW % LANES == 0 and S % T == 0

    tr_ = lambda a: jnp.swapaxes(a, 1, 2)
    shard = {n: tr_(w[n].astype(BF16)) if n in BY_COLUMNS else w[n].astype(BF16) for n in SHARDED}
    weights = [dict() for _ in range(L)]
    weights[0]["w_in_t"] = _whole(_all_gather([shard["w_in"][0]], name="gather_first")[0])

    ssm = []
    for l in range(L):
        disc, disc_vjp = jax.vjp(_ssm_discretise, ssm_lambda_re[l], ssm_lambda_im[l], ssm_log_dt[l],
                                 ssm_b_re[l], ssm_b_im[l])
        lbr, lbi, bbr, bbi = disc
        z = lax.complex(ssm_lambda_re[l], ssm_lambda_im[l]) * jnp.exp(ssm_log_dt[l])[:, None]
        powers = jnp.exp(z[None] * jnp.arange(1, T + 1, dtype=F32)[:, None, None])
        slabs = lambda a: a.reshape(NS, SP)
        lam = jnp.stack([slabs(lbr), slabs(lbi)], axis=1)
        lam_t = jnp.stack([slabs(powers[T - 1].real), slabs(powers[T - 1].imag)], axis=1)
        pw = jnp.stack([powers.real.reshape(T, NS, SP), powers.imag.reshape(T, NS, SP)], axis=0).transpose(2, 0, 1, 3)
        to_rows = lambda a: jnp.swapaxes(a.reshape(NS, SLAB_GROUPS, P, C), 2, 3)
        bmat = jnp.concatenate([_block_diag(to_rows(bbr), True), _block_diag(to_rows(bbi), True)], axis=2)
        cre = ssm_c_re[l].reshape(NS, SLAB_GROUPS, C, P)
        cim = ssm_c_im[l].reshape(NS, SLAB_GROUPS, C, P)
        cmat_t = jnp.concatenate([_block_diag(cre, True), -_block_diag(cim, True)], axis=2)
        ssm.append(dict(vjp=disc_vjp, lam=lam, lam_t=lam_t, pw=pw, bmat=bmat.astype(BF16),
                        bmat_t=tr_(bmat).astype(BF16), cmat=tr_(cmat_t).astype(BF16), cmat_t=cmat_t.astype(BF16),
                        d=ssm_d[l].reshape(1, W)))

    xcur = x.reshape(N, D)
    saved = []
    for l in range(L):
        s_, wl = ssm[l], weights[l]
        win_t = wl["w_in_t"]
        wl["wcat_t"] = jnp.concatenate([win_t[:3 * AW], win_t[3 * AW + H:]], axis=0)
        wl["wf_t"] = jnp.pad(win_t[3 * AW:3 * AW + H], ((0, LANES - H), (0, 0)))
        h, r0 = _rmsnorm_fwd(xcur, norm_mix[l], name="norm_mix_fwd")
        proj = _mm(h, wl["wcat_t"], name="in_proj", tb=True, tm=WIDE_N, tn=WIDE_N)
        fl = _mm(h, wl["wf_t"], name="forget_proj", tb=True, out_dtype=F32)
        ft = fl[:, :H].reshape(Bl, S, H).transpose(0, 2, 1)
        F = _fox_gate_fwd(ft, b_forget[l], name="forget_gate_fwd")
        frow = F.reshape(Bl, HP, 2, nq, tq)
        proj3 = proj.reshape(Bl, S, -1)
        coming = [shard[n][l] for n in REST] + ([shard["w_in"][l + 1]] if l + 1 < L else [])
        ya, lse, got = _attn_fwd(proj3, frow, name="attn_fwd" if l + 1 < L else "attn_fwd_last", H=H, tq=tq,
                                 hosted=_Hosted(gather=coming))
        for n, seg in zip(REST, got):
            wl[_key(n)] = _whole(seg)
        if l + 1 < L:
            weights[l + 1]["w_in_t"] = _whole(got[-1])
        u_tm = _to_tm(proj3[:, :, u_off:u_off + W], T)
        ys = _from_tm(_ssm_fwd(u_tm, s_["bmat"], s_["cmat"], s_["lam"], s_["pw"], s_["lam_t"], s_["d"],
                               name="ssm_fwd")).reshape(N, W)
        yb2 = _glu_fwd(ys, wl["w_glu"], b_glu[l], name="glu_fwd")
        ya2 = ya.reshape(N, AW)
        mixed, pa, pb = _merge_fwd(ya2, yb2, wl["w_branch_a_t"], wl["w_branch_b_t"], proj, gate_blk,
                                   name="merge_fwd")
        x1, h2, r1 = _outproj_fwd(mixed, wl["w_out"], xcur, norm_mlp[l], name="out_proj")
        a = _mm(h2, wl["w_mlp_up_t"], name="mlp_up", tb=True, tm=WIDE_N, tn=WIDE_N)
        x2 = _mm(a, wl["w_mlp_down"], name="mlp_down", a_fn=_relu_sq, epi=lambda acc, res: acc + res,
                 extras=(x1,), out_dtype=F32, tk=LONG_K)
        saved.append(dict(x0=xcur, h=h, r0=r0, proj=proj, ft=ft, frow=frow, ya=ya, lse=lse, u_tm=u_tm,
                          ys=ys, yb2=yb2, mixed=mixed, pa=pa, pb=pb, x1=x1, h2=h2, r1=r1, a=a))
        xcur = x2

    dx, g_final, loss_row = _loss_head(xcur, norm_final, loss_target.reshape(N, D), name="loss_head")
    loss = lax.psum(loss_row[0, 0], MESH_AXES)

    big = {n: [None] * L for n in SHARDED}
    small = {n: [None] * L for n in WEIGHTS if n not in SHARDED and n != "norm_final"}
    small_sums = [None] * L
    win_grad_t = small_above = None
    for l in reversed(range(L)):
        sv, s_, wl = saved[l], ssm[l], weights[l]
        a = sv["a"]
        gw = {}
        d_a = _mm(dx, wl["w_mlp_down"], name="mlp_down_dx", tb=True, tn=WIDE_N,
                  epi=lambda acc, av: acc * (2.0 * jnp.maximum(av.astype(F32), 0.0)), extras=(a,))
        gw["w_mlp_down"] = _mm(a, dx, name="mlp_down_dw", ta=True, a_fn=_relu_sq, tk=LONG_K)
        gw["w_mlp_up"] = _mm(d_a, sv["h2"], name="mlp_up_dw", ta=True, tk=LONG_K)
        dx1, g = _mm_norm_bwd(d_a, wl["w_mlp_up_t"], sv["x1"], sv["r1"], norm_mlp[l], dx, name="mlp_up_dx_norm",
                              tk=LONG_K)
        small["norm_mlp"][l] = g[0]
        gw["w_out"] = _mm(sv["mixed"], dx1, name="out_proj_dw", ta=True, tk=LONG_K)
        ncat = wl["wcat_t"].shape[0]
        dpa, dpb, dproj = _merge_bwd(dx1, wl["w_out"], sv["proj"], sv["pa"], sv["pb"], gate_blk, ncat + LANES,
                                     name="merge_bwd")
        ya2 = sv["ya"].reshape(N, AW)
        gw["w_branch_a"] = _mm(dpa, ya2, name="branch_a_dw", ta=True, tk=LONG_K)
        dya = _mm(dpa, wl["w_branch_a_t"], name="branch_a_dx")
        gw["w_branch_b"] = _mm(dpb, sv["yb2"], name="branch_b_dw", ta=True, tk=LONG_K)
        dyb2 = _mm(dpb, wl["w_branch_b_t"], name="branch_b_dx")
        dys, dz, yb, g = _glu_bwd(sv["ys"], dyb2, wl["w_glu"], wl["w_glu"].T, b_glu[l], name="glu_bwd")
        small["b_glu"][l] = g[0]
        gw["w_glu"] = _mm(yb, dz, name="glu_dw", ta=True, tk=LONG_K)

        du_tm, g_bt, g_ct, g_lam, g_d = _ssm_bwd(
            sv["u_tm"], _to_tm(dys.reshape(Bl, S, W), T), s_["bmat"], s_["bmat_t"], s_["cmat_t"], s_["lam"],
            s_["pw"], s_["lam_t"], s_["d"], name="ssm_bwd")
        du = _from_tm(du_tm).reshape(N, W)
        g_b = _diag_blocks(jnp.swapaxes(g_bt, 1, 2).reshape(NS, LANES, 2, SP).transpose(2, 0, 1, 3).reshape(
            2 * NS, LANES, SP), C, P).reshape(2, G, C, P)
        g_bbar = jnp.swapaxes(g_b, 2, 3)
        g_c = _diag_blocks(g_ct.reshape(NS, LANES, 2, SP).transpose(2, 0, 1, 3).reshape(2 * NS, LANES, SP),
                           C, P).reshape(2, G, C, P)
        g_lbar = g_lam.transpose(1, 0, 2).reshape(2, G, P)
        g_lre, g_lim, g_ldt, g_bre, g_bim = s_["vjp"]((g_lbar[0], g_lbar[1], g_bbar[0], g_bbar[1]))
        small["ssm_lambda_re"][l], small["ssm_lambda_im"][l], small["ssm_log_dt"][l] = g_lre, g_lim, g_ldt
        small["ssm_b_re"][l], small["ssm_b_im"][l] = g_bre, g_bim
        small["ssm_c_re"][l], small["ssm_c_im"][l] = g_c[0], -g_c[1]
        small["ssm_d"][l] = g_d.reshape(W)

        proj3 = sv["proj"].reshape(Bl, S, -1)
        leaving = [_blocks(gw[n]) for n in REST] + ([_blocks(win_grad_t)] if l + 1 < L else [])
        (dq, dk, dv, dfk, dfq), got = _attn_bwd(
            proj3, sv["ya"], dya.reshape(Bl, S, AW), sv["lse"], sv["frow"],
            name="attn_bwd" if l + 1 < L else "attn_bwd_top", H=H, tq=tq,
            hosted=_Hosted(gather=[small_above] if l + 1 < L else [], exchange=leaving))
        if l + 1 < L:
            small_sums[l + 1] = _sum8(got[0], name="sum_small_grads")
            big["w_in"][l + 1] = _sum_blocks("w_in", got[-1]).T
            got = got[1:]
        for n, blocks in zip(REST, got):
            big[n][l] = _sum_blocks(n, blocks).T if n in BY_COLUMNS else _sum_blocks(n, blocks)
        dF = dfk.reshape(Bl, H, S) + dfq.transpose(0, 1, 3, 2).reshape(Bl, H, S)
        dft, g = _fox_gate_bwd(dF, sv["ft"], b_forget[l], name="forget_gate_bwd")
        small["b_forget"][l] = g[:, 0]
        dfl = jnp.pad(dft.transpose(0, 2, 1).reshape(N, H), ((0, 0), (0, LANES - H))).astype(BF16)
        for off, piece in ((0, dq.reshape(N, AW)), (AW, dk.reshape(N, AW)), (2 * AW, dv.reshape(N, AW)), (u_off, du),
                           (ncat, dfl)):
            dproj = lax.dynamic_update_slice(dproj, piece, (0, off))
        gcat_t = _mm(dproj, sv["h"], name="in_proj_dw", ta=True, tm=1408, tk=LONG_K)
        win_grad_t = jnp.concatenate([gcat_t[:3 * AW], gcat_t[ncat:ncat + H], gcat_t[3 * AW:ncat]], axis=0)
        wfull_t = jnp.concatenate([wl["wcat_t"], wl["wf_t"]], axis=0)
        if l > 0:
            dx, g = _mm_norm_bwd(dproj, wfull_t, sv["x0"], sv["r0"], norm_mix[l], dx1, name="in_proj_dx_norm",
                                 tk=1408)
        else:
            dh, got = _mm(dproj, wfull_t, name="in_proj_dx_bottom", out_dtype=F32, tk=1408,
                          hosted=_Hosted(exchange=[_blocks(win_grad_t)]))
            big["w_in"][0] = _sum_blocks("w_in", got[0]).T
            dx, g = _rmsnorm_bwd(dh, sv["x0"], sv["r0"], norm_mix[l], dx1, name="norm_mix_bwd")
        small["norm_mix"][l] = g[0]
        small_above = _pack([small[n][l] for n in small])

    last = [small[n][0] for n in small] + [g_final[0]]
    small_sums[0] = _sum8(_all_gather([_pack(last)], name="gather_small_grads")[0], name="sum_small_grads_last")
    grads = {n: jnp.stack(big[n]) for n in SHARDED}
    per_layer = [_unpack(small_sums[l], last if l == 0 else last[:-1]) for l in range(L)]
    for i, n in enumerate(small):
        grads[n] = jnp.stack([per_layer[l][i] for l in range(L)])
    grads["norm_final"] = per_layer[0][-1]

    deltas, new_m, new_v = {}, {}, {}
    for n in WEIGHTS:
        deltas[n], new_m[n], new_v[n] = _adamw(w[n], grads[n], args["m_" + n], args["v_" + n], name="adamw_" + n)
    return (loss, dx.reshape(Bl, S, D), *[grads[n] for n in WEIGHTS], *[deltas[n] for n in WEIGHTS],
            *[new_m[n] for n in WEIGHTS], *[new_v[n] for n in WEIGHTS])
```

```python
import functools

import jax
import jax.numpy as jnp
from jax import lax
from jax.experimental import pallas as pl
from jax.experimental.pallas import tpu as pltpu

F32 = jnp.float32
BF16 = jnp.bfloat16

N_DEV = 8
HEAD_DIM = 64
LANES = 128
SSM_CHUNK = 32
SLAB_GROUPS = 8
ATTN_BLOCK = 512
LONG_K = 2048
WIDE_N = 2048
PACK_COLS = 1024
SUM_ROWS = 256
RMS_EPS = 1e-6
VMEM_LIMIT = 56 * 1024 * 1024
ADAM_LR, ADAM_B1, ADAM_B2, ADAM_EPS, ADAM_WD, ADAM_STEP = 0.001, 0.9, 0.999, 1e-08, 0.01, 10
MESH_AXES = ("x", "y", "c")
NEG = -1e30
NT = (((1,), (1,)), ((), ()))
TN = (((0,), (0,)), ((), ()))


def _cparams(*sem):
    return pltpu.CompilerParams(dimension_semantics=sem, vmem_limit_bytes=VMEM_LIMIT)


def _tile(dim, pref, unit=LANES):
    if dim <= pref:
        return dim
    best = None
    for t in range(unit, pref + 1, unit):
        if dim % t == 0:
            best = t
    assert best is not None, (dim, pref)
    return best


def _mm(a, b, *, name, ta=False, tb=False, a_fn=None, epi=None, extras=(), out_dtype=BF16, tm=1024, tn=1024,
        tk=1024, hosted=None):
    if ta:
        K, M = a.shape
    else:
        M, K = a.shape
    N, Kb = b.shape if tb else b.shape[::-1]
    assert K == Kb and not (ta and tb), (a.shape, b.shape)
    tm, tn, tk = _tile(M, tm), _tile(N, tn), _tile(K, tk)
    gm, gn, nk = M // tm, N // tn, K // tk
    ne = len(extras)
    nh = hosted.n if hosted else 0
    n_acc = 1 if nk > 1 else 0

    def body(a_ref, b_ref, *rest):
        e_refs, o_ref = rest[:ne], rest[ne + nh]
        acc_ref = rest[ne + 2 * nh + 1] if nk > 1 else None
        k = pl.program_id(2)
        if hosted:
            i, j = pl.program_id(0), pl.program_id(1)
            start, finish = hosted.run((i == 0) & (j == 0) & (k == 0), (i == gm - 1) & (j == gn - 1) & (k == nk - 1),
                                       rest[ne:ne + nh], rest[ne + nh + 1:ne + 2 * nh + 1],
                                       rest[ne + 2 * nh + 1 + n_acc:])
            start()
        av = a_ref[...]
        if a_fn is not None:
            av = a_fn(av)
        av = av.astype(BF16)
        bv = b_ref[...].astype(BF16)
        dims = TN if ta else NT if tb else (((1,), (0,)), ((), ()))
        part = lax.dot_general(av, bv, dims, preferred_element_type=F32)

        def finish_tile(r):
            if epi is not None:
                r = epi(r, *[e[...] for e in e_refs])
            o_ref[...] = r.astype(o_ref.dtype)

        if nk == 1:
            finish_tile(part)
        else:
            @pl.when(k == 0)
            def _():
                acc_ref[...] = part

            @pl.when(k > 0)
            def _():
                acc_ref[...] += part

            @pl.when(k == nk - 1)
            def _():
                finish_tile(acc_ref[...])

        if hosted:
            finish()

    a_spec = pl.BlockSpec((tk, tm), lambda i, j, k: (k, i)) if ta else pl.BlockSpec((tm, tk), lambda i, j, k: (i, k))
    outs = pl.pallas_call(
        body, name=name,
        out_shape=[jax.ShapeDtypeStruct((M, N), out_dtype)] + (hosted.out_shape if hosted else []),
        grid=(gm, gn, nk),
        in_specs=[a_spec, pl.BlockSpec((tn, tk), lambda i, j, k: (j, k)) if tb
                  else pl.BlockSpec((tk, tn), lambda i, j, k: (k, j))]
        + [pl.BlockSpec((tm, tn), lambda i, j, k: (i, j)) for _ in extras] + [HBM] * nh,
        out_specs=[pl.BlockSpec((tm, tn), lambda i, j, k: (i, j))] + [HBM] * nh,
        scratch_shapes=([pltpu.VMEM((tm, tn), F32)] if nk > 1 else []) + (hosted.scratch if hosted else []),
        compiler_params=_cparams(*(("arbitrary",) * 3 if hosted else ("parallel", "parallel", "arbitrary"))),
    )(a, b, *extras, *(hosted.arrays if hosted else []))
    return (outs[0], outs[1:]) if hosted else outs[0]


def _relu_sq(v):
    r = jnp.maximum(v.astype(F32), 0.0)
    return r * r


def _sigmoid(v):
    return 1.0 / (1.0 + jnp.exp(-v))


GELU_C = 0.7978845608028654
GELU_A = 0.044715


def _gelu(v):
    return 0.5 * v * (1.0 + jnp.tanh(GELU_C * (v + GELU_A * v * v * v)))


def _gelu_grad(v):
    t = jnp.tanh(GELU_C * (v + GELU_A * v * v * v))
    return 0.5 * (1.0 + t) + 0.5 * v * (1.0 - t * t) * GELU_C * (1.0 + 3.0 * GELU_A * v * v)


def _rmsnorm_fwd(x, g, wf_t, *, name, tr=512):
    n, d = x.shape
    nf = wf_t.shape[0]
    tr = _tile(n, tr, 8)

    def body(x_ref, g_ref, wf_ref, h_ref, r_ref, f_ref):
        xv = x_ref[...]
        r = lax.rsqrt(jnp.mean(xv * xv, axis=-1, keepdims=True) + RMS_EPS)
        h = (xv * r * g_ref[...]).astype(BF16)
        h_ref[...] = h
        r_ref[...] = r
        f_ref[...] = lax.dot_general(h, wf_ref[...], NT, preferred_element_type=F32)

    return pl.pallas_call(
        body, name=name,
        out_shape=(jax.ShapeDtypeStruct((n, d), BF16), jax.ShapeDtypeStruct((n, 1), F32),
                   jax.ShapeDtypeStruct((n, nf), F32)),
        grid=(n // tr,),
        in_specs=[pl.BlockSpec((tr, d), lambda i: (i, 0)), pl.BlockSpec((1, d), lambda i: (0, 0)),
                  pl.BlockSpec((nf, d), lambda i: (0, 0))],
        out_specs=(pl.BlockSpec((tr, d), lambda i: (i, 0)), pl.BlockSpec((tr, 1), lambda i: (i, 0)),
                   pl.BlockSpec((tr, nf), lambda i: (i, 0))),
        compiler_params=_cparams("parallel"),
    )(x, g.reshape(1, d), wf_t)


def _norm_bwd_tile(dh, x, r, g, dres):
    xh = x * r
    dxh = dh * g
    m = jnp.mean(dxh * xh, axis=-1, keepdims=True)
    return r * (dxh - xh * m) + dres, jnp.sum(dh * xh, axis=0, keepdims=True)


def _mm_norm_bwd(a, w, x, r, g, dres, *, name, tm=1024, tk=1024):
    M, K = a.shape
    D = w.shape[1]
    tm, tk = _tile(M, tm, 8), _tile(K, tk)
    nk = K // tk

    def body(a_ref, w_ref, x_ref, r_ref, g_ref, dres_ref, dx_ref, dg_ref, acc_ref):
        i, k = pl.program_id(0), pl.program_id(1)
        part = jnp.dot(a_ref[...].astype(BF16), w_ref[...], preferred_element_type=F32)

        @pl.when(k == 0)
        def _():
            acc_ref[...] = part

        @pl.when(k > 0)
        def _():
            acc_ref[...] += part

        @pl.when(k == nk - 1)
        def _():
            dx, dg = _norm_bwd_tile(acc_ref[...], x_ref[...], r_ref[...], g_ref[...], dres_ref[...])
            dx_ref[...] = dx

            @pl.when(i == 0)
            def _():
                dg_ref[...] = dg

            @pl.when(i > 0)
            def _():
                dg_ref[...] += dg

    row = pl.BlockSpec((tm, D), lambda i, k: (i, 0))
    vec = pl.BlockSpec((1, D), lambda i, k: (0, 0))
    return pl.pallas_call(
        body, name=name,
        out_shape=(jax.ShapeDtypeStruct((M, D), F32), jax.ShapeDtypeStruct((1, D), F32)),
        grid=(M // tm, nk),
        in_specs=[pl.BlockSpec((tm, tk), lambda i, k: (i, k)), pl.BlockSpec((tk, D), lambda i, k: (k, 0)),
                  row, pl.BlockSpec((tm, 1), lambda i, k: (i, 0)), vec, row],
        out_specs=(row, vec),
        scratch_shapes=[pltpu.VMEM((tm, D), F32)],
        compiler_params=_cparams("arbitrary", "arbitrary"),
    )(a, w, x, r, g.reshape(1, D), dres)


def _rmsnorm_bwd(dh, x, r, g, dres, *, name, tr=512):
    n, d = x.shape
    tr = _tile(n, tr, 8)

    def body(dh_ref, x_ref, r_ref, g_ref, dres_ref, dx_ref, dg_ref):
        i = pl.program_id(0)
        dx_ref[...], part = _norm_bwd_tile(dh_ref[...].astype(F32), x_ref[...], r_ref[...], g_ref[...],
                                           dres_ref[...])

        @pl.when(i == 0)
        def _():
            dg_ref[...] = part

        @pl.when(i > 0)
        def _():
            dg_ref[...] += part

    row = pl.BlockSpec((tr, d), lambda i: (i, 0))
    vec = pl.BlockSpec((1, d), lambda i: (0, 0))
    return pl.pallas_call(
        body, name=name,
        out_shape=(jax.ShapeDtypeStruct((n, d), F32), jax.ShapeDtypeStruct((1, d), F32)),
        grid=(n // tr,),
        in_specs=[row, row, pl.BlockSpec((tr, 1), lambda i: (i, 0)), vec, row],
        out_specs=(row, vec),
        compiler_params=_cparams("arbitrary"),
    )(dh, x, r, g.reshape(1, d), dres)


def _loss_head(x, g, target, *, name, tr=512):
    n, d = x.shape
    tr = _tile(n, tr, 8)

    def body(x_ref, g_ref, t_ref, dx_ref, dg_ref, loss_ref):
        i = pl.program_id(0)
        xv = x_ref[...]
        gv = g_ref[...]
        r = lax.rsqrt(jnp.mean(xv * xv, axis=-1, keepdims=True) + RMS_EPS)
        xh = xv * r
        err = xh * gv - t_ref[...]
        lpart = 0.5 * jnp.sum(jnp.mean(err * err, axis=-1, keepdims=True), axis=0, keepdims=True)
        dy = err * (1.0 / d)
        dxh = dy * gv
        m = jnp.mean(dxh * xh, axis=-1, keepdims=True)
        dx_ref[...] = r * (dxh - xh * m)
        gpart = jnp.sum(dy * xh, axis=0, keepdims=True)
        lrow = jnp.broadcast_to(lpart, (1, LANES))

        @pl.when(i == 0)
        def _():
            dg_ref[...] = gpart
            loss_ref[...] = lrow

        @pl.when(i > 0)
        def _():
            dg_ref[...] += gpart
            loss_ref[...] += lrow

    row = pl.BlockSpec((tr, d), lambda i: (i, 0))
    vec = pl.BlockSpec((1, d), lambda i: (0, 0))
    return pl.pallas_call(
        body, name=name,
        out_shape=(jax.ShapeDtypeStruct((n, d), F32), jax.ShapeDtypeStruct((1, d), F32),
                   jax.ShapeDtypeStruct((1, LANES), F32)),
        grid=(n // tr,),
        in_specs=[row, vec, row],
        out_specs=(row, vec, pl.BlockSpec((1, LANES), lambda i: (0, 0))),
        compiler_params=_cparams("arbitrary"),
    )(x, g.reshape(1, d), target)


def _tri_dot(v, tri):
    hi = v.astype(BF16)
    r1 = v - hi.astype(F32)
    mid = r1.astype(BF16)
    lo = (r1 - mid.astype(F32)).astype(BF16)
    d = functools.partial(jnp.dot, preferred_element_type=F32)
    return d(hi, tri) + d(mid, tri) + d(lo, tri)


def _fox_gate_fwd(ft, bf, *, name, blk=256):
    B, H, S = ft.shape
    blk = _tile(S, blk)
    nb = S // blk

    def body(f_ref, b_ref, o_ref):
        x = f_ref[0] + b_ref[...]
        logf = jnp.minimum(x, 0.0) - jnp.log(1.0 + jnp.exp(-jnp.abs(x)))
        rr = lax.broadcasted_iota(jnp.int32, (blk, blk), 0)
        cc = lax.broadcasted_iota(jnp.int32, (blk, blk), 1)
        tri = (rr <= cc).astype(BF16)
        carry = jnp.zeros((H, 1), F32)
        for n in range(nb):
            c = _tri_dot(logf[:, n * blk:(n + 1) * blk], tri) + carry
            o_ref[0, :, n * blk:(n + 1) * blk] = c
            carry = c[:, blk - 1:blk]

    return pl.pallas_call(
        body, name=name,
        out_shape=jax.ShapeDtypeStruct((B, H, S), F32),
        grid=(B,),
        in_specs=[pl.BlockSpec((1, H, S), lambda b: (b, 0, 0)), pl.BlockSpec((H, 1), lambda b: (0, 0))],
        out_specs=pl.BlockSpec((1, H, S), lambda b: (b, 0, 0)),
        compiler_params=_cparams("parallel"),
    )(ft, bf.reshape(H, 1))


def _fox_gate_bwd(dF, ft, bf, *, name, blk=256):
    B, H, S = ft.shape
    blk = _tile(S, blk)
    nb = S // blk

    def body(d_ref, f_ref, b_ref, o_ref, db_ref):
        b = pl.program_id(0)
        x = f_ref[0] + b_ref[...]
        sneg = 1.0 / (1.0 + jnp.exp(x))
        dv = d_ref[0]
        rr = lax.broadcasted_iota(jnp.int32, (blk, blk), 0)
        cc = lax.broadcasted_iota(jnp.int32, (blk, blk), 1)
        tri = (rr >= cc).astype(BF16)
        carry = jnp.zeros((H, 1), F32)
        tot = jnp.zeros((H, 1), F32)
        for n in reversed(range(nb)):
            sl = slice(n * blk, (n + 1) * blk)
            c = _tri_dot(dv[:, sl], tri) + carry
            g = c * sneg[:, sl]
            o_ref[0, :, sl] = g
            tot = tot + jnp.sum(g, axis=1, keepdims=True)
            carry = c[:, 0:1]

        @pl.when(b == 0)
        def _():
            db_ref[...] = tot

        @pl.when(b > 0)
        def _():
            db_ref[...] += tot

    blkspec = pl.BlockSpec((1, H, S), lambda b: (b, 0, 0))
    return pl.pallas_call(
        body, name=name,
        out_shape=(jax.ShapeDtypeStruct((B, H, S), F32), jax.ShapeDtypeStruct((H, 1), F32)),
        grid=(B,),
        in_specs=[blkspec, blkspec, pl.BlockSpec((H, 1), lambda b: (0, 0))],
        out_specs=(blkspec, pl.BlockSpec((H, 1), lambda b: (0, 0))),
        compiler_params=_cparams("arbitrary"),
    )(dF, ft, bf.reshape(H, 1))


def _head_masks():
    lane = lax.broadcasted_iota(jnp.int32, (1, LANES), 1)
    return [lane < HEAD_DIM, lane >= HEAD_DIM]


HBM = pl.BlockSpec(memory_space=pltpu.HBM)
MESH = pl.DeviceIdType.MESH


def _direct_copies(kinds, x_refs, out_refs, send_sems, recv_sems, local_sems):
    x, y, c = lax.axis_index("x"), lax.axis_index("y"), lax.axis_index("c")
    me = 4 * x + 2 * y + c
    copies = []
    for a, (kind, xr, outr) in enumerate(zip(kinds, x_refs, out_refs)):
        copies.append(pltpu.make_async_copy(xr if kind == "gather" else xr.at[me], outr.at[me], local_sems.at[a]))
    for k in range(1, N_DEV):
        px = 1 - x if (k >> 2) & 1 else x
        py = 1 - y if (k >> 1) & 1 else y
        pc = 1 - c if k & 1 else c
        for a, (kind, xr, outr) in enumerate(zip(kinds, x_refs, out_refs)):
            copies.append(pltpu.make_async_remote_copy(
                src_ref=xr if kind == "gather" else xr.at[4 * px + 2 * py + pc], dst_ref=outr.at[me],
                send_sem=send_sems.at[7 * a + k - 1], recv_sem=recv_sems.at[7 * a + k - 1],
                device_id=(px, py, pc), device_id_type=MESH))
    return copies


class _Hosted:
    def __init__(self, gather=(), exchange=()):
        self.arrays = list(gather) + list(exchange)
        self.kinds = ["gather"] * len(gather) + ["exchange"] * len(exchange)
        self.n = len(self.arrays)
        self.out_shape = [jax.ShapeDtypeStruct(((N_DEV,) if k == "gather" else ()) + a.shape, a.dtype)
                          for k, a in zip(self.kinds, self.arrays)]
        self.scratch = [pltpu.SemaphoreType.DMA((7 * self.n,)), pltpu.SemaphoreType.DMA((7 * self.n,)),
                        pltpu.SemaphoreType.DMA((self.n,))]

    def run(self, first, last, x_refs, out_refs, sems):
        def go(when, act):
            @pl.when(when)
            def _():
                for cp in _direct_copies(self.kinds, x_refs, out_refs, *sems):
                    act(cp)
        return (lambda: go(first, lambda cp: cp.start())), (lambda: go(last, lambda cp: cp.wait()))


def _stack_heads(x, masks):
    zero = jnp.zeros_like(x)
    return jnp.concatenate([jnp.where(masks[0], x, zero), jnp.where(masks[1], x, zero)], axis=0)


def _attn_fwd(proj, frow, *, name, H, tq, hosted=None):
    B, S, _ = proj.shape
    HP = H // 2
    nq = S // tq
    scale = HEAD_DIM ** -0.5
    nh = hosted.n if hosted else 0

    def body(*refs):
        q_ref, k_ref, v_ref, fk_ref = refs[:4]
        o_ref, lse_ref = refs[4 + nh:6 + nh]
        i = pl.program_id(2)
        if hosted:
            b, hp = pl.program_id(0), pl.program_id(1)
            start, finish = hosted.run((b == 0) & (hp == 0) & (i == 0), (b == B - 1) & (hp == HP - 1) & (i == nq - 1),
                                       refs[4:4 + nh], refs[6 + nh:6 + 2 * nh], refs[6 + 2 * nh:])
            start()
        masks = _head_masks()
        q2 = _stack_heads(q_ref[0], masks) * jnp.asarray(scale, BF16)
        rr = lax.broadcasted_iota(jnp.int32, (tq, tq), 0)
        cc = lax.broadcasted_iota(jnp.int32, (tq, tq), 1)
        causal = rr >= cc

        def block(j, carry, masked):
            rows = pl.ds(pl.multiple_of(j * tq, tq), tq)
            kj = k_ref[0, rows, :]
            vj = v_ref[0, rows, :]
            s2 = lax.dot_general(q2, kj, NT, preferred_element_type=F32)
            new, ps = [], []
            for h in range(2):
                m, l, acc = carry[h]
                s = s2[h * tq:(h + 1) * tq] - fk_ref[0, 0, h, pl.ds(j, 1), :]
                if masked:
                    s = jnp.where(causal, s, NEG)
                m_new = jnp.maximum(m, jnp.max(s, axis=-1, keepdims=True))
                alpha = jnp.exp(m - m_new)
                p = jnp.exp(s - m_new)
                new.append((m_new, alpha * l + jnp.sum(p, axis=-1, keepdims=True), alpha, acc))
                ps.append(p.astype(BF16))
            pv = jnp.dot(jnp.concatenate(ps, axis=0), vj, preferred_element_type=F32)
            return tuple((m, l, alpha * acc + pv[h * tq:(h + 1) * tq]) for h, (m, l, alpha, acc) in enumerate(new))

        one = (jnp.full((tq, 1), NEG, F32), jnp.zeros((tq, 1), F32), jnp.zeros((tq, LANES), F32))
        carry = lax.fori_loop(0, i, lambda j, c: block(j, c, False), (one, one))
        (m0, l0, a0), (m1, l1, a1) = block(i, carry, True)
        o_ref[0] = jnp.where(masks[0], a0 / l0, a1 / l1).astype(BF16)
        two = lax.broadcasted_iota(jnp.int32, (1, 2), 1)
        lse_ref[0, 0] = jnp.where(two == 0, m0 + jnp.log(l0), m1 + jnp.log(l1))
        if hosted:
            finish()

    kv = lambda off: pl.BlockSpec((1, S, LANES), lambda b, hp, i: (b, 0, off + hp))
    outs = pl.pallas_call(
        body, name=name,
        out_shape=[jax.ShapeDtypeStruct((B, S, H * HEAD_DIM), BF16), jax.ShapeDtypeStruct((B, HP, S, 2), F32)]
        + (hosted.out_shape if hosted else []),
        grid=(B, HP, nq),
        in_specs=[pl.BlockSpec((1, tq, LANES), lambda b, hp, i: (b, i, hp)), kv(HP), kv(2 * HP),
                  pl.BlockSpec((1, 1, 2, nq, tq), lambda b, hp, i: (b, hp, 0, 0, 0))] + [HBM] * nh,
        out_specs=[pl.BlockSpec((1, tq, LANES), lambda b, hp, i: (b, i, hp)),
                   pl.BlockSpec((1, 1, tq, 2), lambda b, hp, i: (b, hp, i, 0))] + [HBM] * nh,
        scratch_shapes=hosted.scratch if hosted else [],
        compiler_params=_cparams("arbitrary", "arbitrary", "arbitrary"),
    )(proj, proj, proj, frow, *(hosted.arrays if hosted else []))
    return outs[0], outs[1], outs[2:]


def _attn_bwd(proj, ya, dya, lse, frow, *, name, H, tq, hosted=None):
    B, S, _ = proj.shape
    HP = H // 2
    nq = S // tq
    AW = H * HEAD_DIM
    scale = HEAD_DIM ** -0.5
    nh = hosted.n if hosted else 0

    def body(*refs):
        q_ref, k_ref, v_ref, o_ref, do_ref, lse_ref, fk_ref = refs[:7]
        dq_ref, dk_ref, dv_ref, dfk_ref, dfq_ref = refs[7 + nh:12 + nh]
        (q2_ref, do2_ref, lse2_ref, delta2_ref, dq2_acc, dfq2_acc, dk_acc, dv_acc,
         dfk_acc) = refs[12 + 2 * nh:21 + 2 * nh]
        if hosted:
            b, hp = pl.program_id(0), pl.program_id(1)
            start, finish = hosted.run((b == 0) & (hp == 0), (b == B - 1) & (hp == HP - 1),
                                       refs[7:7 + nh], refs[12 + nh:12 + 2 * nh], refs[21 + 2 * nh:])
            start()
        masks = _head_masks()
        rr = lax.broadcasted_iota(jnp.int32, (tq, tq), 0)
        cc = lax.broadcasted_iota(jnp.int32, (tq, tq), 1)
        causal = rr >= cc
        sc = jnp.asarray(scale, BF16)

        def stage(i, c):
            rows = pl.ds(pl.multiple_of(i * tq, tq), tq)
            dov = do_ref[0, rows, :]
            q2_ref[i] = _stack_heads(q_ref[0, rows, :], masks) * sc
            do2_ref[i] = _stack_heads(dov, masks)
            prod = dov.astype(F32) * o_ref[0, rows, :].astype(F32)
            delta2_ref[i] = jnp.concatenate(
                [jnp.sum(jnp.where(masks[h], prod, 0.0), axis=-1, keepdims=True) for h in range(2)], axis=0)
            lv = lse_ref[0, 0, rows, :]
            lse2_ref[i] = jnp.concatenate([lv[:, 0:1], lv[:, 1:2]], axis=0)
            return c

        lax.fori_loop(0, nq, stage, 0)
        dq2_acc[...] = jnp.zeros_like(dq2_acc)
        dfq2_acc[...] = jnp.zeros_like(dfq2_acc)

        def kv_block(j, carry):
            rows_j = pl.ds(pl.multiple_of(j * tq, tq), tq)
            kj = k_ref[0, rows_j, :]
            vj = v_ref[0, rows_j, :]
            ks = kj * sc
            dk_acc[...] = jnp.zeros_like(dk_acc)
            dv_acc[...] = jnp.zeros_like(dv_acc)
            dfk_acc[...] = jnp.zeros_like(dfk_acc)

            def logits(i):
                return (lax.dot_general(q2_ref[i], kj, NT, preferred_element_type=F32),
                        lax.dot_general(do2_ref[i], vj, NT, preferred_element_type=F32))

            def probs(i, s2, dp2, masked):
                lse2 = lse2_ref[i]
                delta2 = delta2_ref[i]
                ps, dss = [], []
                for h in range(2):
                    half = slice(h * tq, (h + 1) * tq)
                    p = jnp.exp(s2[half] - fk_ref[0, 0, h, pl.ds(j, 1), :] - lse2[half])
                    if masked:
                        p = jnp.where(causal, p, 0.0)
                    ds = p * (dp2[half] - delta2[half])
                    dfk_acc[h:h + 1, :] -= jnp.sum(ds, axis=0, keepdims=True)
                    dfq2_acc[i, half, :] += jnp.sum(ds, axis=1, keepdims=True)
                    ps.append(p.astype(BF16))
                    dss.append(ds.astype(BF16))
                return jnp.concatenate(ps, axis=0), jnp.concatenate(dss, axis=0)

            def grads(i, p2, ds2):
                dv_acc[...] += lax.dot_general(p2, do2_ref[i], TN, preferred_element_type=F32)
                dk_acc[...] += lax.dot_general(ds2, q2_ref[i], TN, preferred_element_type=F32)
                dq2_acc[i] += jnp.dot(ds2, ks, preferred_element_type=F32)

            grads(j, *probs(j, *logits(j), True))

            def rest(i, c):
                grads(i, *probs(i, *logits(i), False))
                return c

            lax.fori_loop(j + 1, nq, rest, 0)
            dk_ref[0, rows_j, :] = dk_acc[...].astype(BF16)
            dv_ref[0, rows_j, :] = dv_acc[...].astype(BF16)
            for h in range(2):
                dfk_ref[0, 0, h, pl.ds(j, 1), :] = dfk_acc[h:h + 1, :]
            return carry

        lax.fori_loop(0, nq, kv_block, 0)
        two = lax.broadcasted_iota(jnp.int32, (1, 2), 1)

        def finish_block(i, c):
            rows = pl.ds(pl.multiple_of(i * tq, tq), tq)
            dq2 = dq2_acc[i]
            dq_ref[0, rows, :] = jnp.where(masks[0], dq2[:tq], dq2[tq:]).astype(BF16)
            dfq2 = dfq2_acc[i]
            dfq_ref[0, 0, rows, :] = jnp.where(two == 0, dfq2[:tq], dfq2[tq:])
            return c

        lax.fori_loop(0, nq, finish_block, 0)
        if hosted:
            finish()

    col = lambda off: pl.BlockSpec((1, S, LANES), lambda b, hp: (b, 0, off + hp))
    stat = pl.BlockSpec((1, 1, S, 2), lambda b, hp: (b, hp, 0, 0))
    rowf = pl.BlockSpec((1, 1, 2, nq, tq), lambda b, hp: (b, hp, 0, 0, 0))
    grad = jax.ShapeDtypeStruct((B, S, AW), BF16)
    outs = pl.pallas_call(
        body, name=name,
        out_shape=[grad, grad, grad, jax.ShapeDtypeStruct((B, HP, 2, nq, tq), F32),
                   jax.ShapeDtypeStruct((B, HP, S, 2), F32)] + (hosted.out_shape if hosted else []),
        grid=(B, HP),
        in_specs=[col(0), col(HP), col(2 * HP), col(0), col(0), stat, rowf] + [HBM] * nh,
        out_specs=[col(0), col(0), col(0), rowf, stat] + [HBM] * nh,
        scratch_shapes=[pltpu.VMEM((nq, 2 * tq, LANES), BF16), pltpu.VMEM((nq, 2 * tq, LANES), BF16),
                        pltpu.VMEM((nq, 2 * tq, 1), F32), pltpu.VMEM((nq, 2 * tq, 1), F32),
                        pltpu.VMEM((nq, 2 * tq, LANES), F32), pltpu.VMEM((nq, 2 * tq, 1), F32),
                        pltpu.VMEM((tq, LANES), F32), pltpu.VMEM((tq, LANES), F32), pltpu.VMEM((2, tq), F32)]
        + (hosted.scratch if hosted else []),
        compiler_params=_cparams("arbitrary", "arbitrary"),
    )(proj, proj, proj, ya, dya, lse, frow, *(hosted.arrays if hosted else []))
    return outs[:5], outs[5:]


def _cmul(ar, ai, br, bi):
    return ar * br - ai * bi, ar * bi + ai * br


def _ssm_states(u_ref, bm, lam_ref, pw_ref, lamT_ref, hr_ref, hi_ref, inr_ref, ini_ref, T, NC, SP):
    lr, li = lam_ref[0, 0:1, :], lam_ref[0, 1:2, :]
    bu = jnp.dot(u_ref[0, 0], bm, preferred_element_type=F32)
    hr_ref[0] = bu[:, :SP]
    hi_ref[0] = bu[:, SP:]

    def step(t, c):
        bu = jnp.dot(u_ref[0, t], bm, preferred_element_type=F32)
        pr, pi = _cmul(hr_ref[t - 1], hi_ref[t - 1], lr, li)
        hr_ref[t] = pr + bu[:, :SP]
        hi_ref[t] = pi + bu[:, SP:]
        return c

    lax.fori_loop(1, T, step, 0, unroll=2)

    tr, ti = lamT_ref[0, 0:1, :], lamT_ref[0, 1:2, :]
    inr_ref[0:1, :] = jnp.zeros((1, SP), F32)
    ini_ref[0:1, :] = jnp.zeros((1, SP), F32)

    def chunk(n, c):
        prev = pl.ds(n - 1, 1)
        pr, pi = _cmul(inr_ref[prev, :], ini_ref[prev, :], tr, ti)
        inr_ref[pl.ds(n, 1), :] = pr + hr_ref[T - 1, prev, :]
        ini_ref[pl.ds(n, 1), :] = pi + hi_ref[T - 1, prev, :]
        return c

    lax.fori_loop(1, NC, chunk, 0)


def _ssm_entry_term(t, pw_ref, inr_ref, ini_ref):
    return _cmul(inr_ref[...], ini_ref[...], pw_ref[0, 0, pl.ds(t, 1), :], pw_ref[0, 1, pl.ds(t, 1), :])


def _ssm_fwd(u_tm, bmat, cmat, lam, pw, lamT, dskip, *, name):
    B, T, NC, W = u_tm.shape
    NS = W // LANES
    SP = bmat.shape[2] // 2

    def body(u_ref, b_ref, c_ref, lam_ref, pw_ref, lamT_ref, d_ref, y_ref, hr_ref, hi_ref, inr_ref, ini_ref):
        _ssm_states(u_ref, b_ref[0], lam_ref, pw_ref, lamT_ref, hr_ref, hi_ref, inr_ref, ini_ref, T, NC, SP)
        cm = c_ref[0]
        dv = d_ref[...]

        def out(t, c):
            cr, ci = _ssm_entry_term(t, pw_ref, inr_ref, ini_ref)
            hcat = jnp.concatenate([hr_ref[t] + cr, hi_ref[t] + ci], axis=1).astype(BF16)
            y_ref[0, t] = jnp.dot(hcat, cm, preferred_element_type=F32) + dv * u_ref[0, t].astype(F32)
            return c

        lax.fori_loop(0, T, out, 0, unroll=2)

    slab = lambda *shape: pl.BlockSpec((1,) + shape, lambda b, s: (s,) + (0,) * len(shape))
    tok = pl.BlockSpec((1, T, NC, LANES), lambda b, s: (b, 0, 0, s))
    return pl.pallas_call(
        body, name=name,
        out_shape=jax.ShapeDtypeStruct((B, T, NC, W), F32),
        grid=(B, NS),
        in_specs=[tok, slab(LANES, 2 * SP), slab(2 * SP, LANES), slab(2, SP), slab(2, T, SP), slab(2, SP),
                  pl.BlockSpec((1, LANES), lambda b, s: (0, s))],
        out_specs=tok,
        scratch_shapes=[pltpu.VMEM((T, NC, SP), F32), pltpu.VMEM((T, NC, SP), F32),
                        pltpu.VMEM((NC, SP), F32), pltpu.VMEM((NC, SP), F32)],
        compiler_params=_cparams("parallel", "parallel"),
    )(u_tm, bmat, cmat, lam, pw, lamT, dskip)


def _ssm_bwd(u_tm, dy_tm, bmat, bmat_t, cmat_t, lam, pw, lamT, dskip, *, name):
    B, T, NC, W = u_tm.shape
    NS = W // LANES
    SP = bmat.shape[2] // 2

    def body(u_ref, dy_ref, b_ref, bt_ref, ct_ref, lam_ref, pw_ref, lamT_ref, d_ref,
             du_ref, gb_ref, gc_ref, glam_ref, gd_ref,
             hr_ref, hi_ref, ar_ref, ai_ref, inr_ref, ini_ref, anr_ref, ani_ref):
        b = pl.program_id(1)
        _ssm_states(u_ref, b_ref[0], lam_ref, pw_ref, lamT_ref, hr_ref, hi_ref, inr_ref, ini_ref, T, NC, SP)
        lr, li = lam_ref[0, 0:1, :], lam_ref[0, 1:2, :]
        ct = ct_ref[0]
        bt = bt_ref[0]
        dv = d_ref[...]

        gh = jnp.dot(dy_ref[0, T - 1].astype(BF16), ct, preferred_element_type=F32)
        ar_ref[T - 1] = gh[:, :SP]
        ai_ref[T - 1] = gh[:, SP:]

        def back(k, c):
            t = T - 2 - k
            gh = jnp.dot(dy_ref[0, t].astype(BF16), ct, preferred_element_type=F32)
            pr, pi = _cmul(ar_ref[t + 1], ai_ref[t + 1], lr, -li)
            ar_ref[t] = pr + gh[:, :SP]
            ai_ref[t] = pi + gh[:, SP:]
            return c

        lax.fori_loop(0, T - 1, back, 0, unroll=2)

        tr, ti = lamT_ref[0, 0:1, :], lamT_ref[0, 1:2, :]
        anr_ref[NC - 1:NC, :] = jnp.zeros((1, SP), F32)
        ani_ref[NC - 1:NC, :] = jnp.zeros((1, SP), F32)

        def chunk(k, c):
            n = NC - 2 - k
            nxt = pl.ds(n + 1, 1)
            pr, pi = _cmul(anr_ref[nxt, :], ani_ref[nxt, :], tr, -ti)
            anr_ref[pl.ds(n, 1), :] = pr + ar_ref[0, nxt, :]
            ani_ref[pl.ds(n, 1), :] = pi + ai_ref[0, nxt, :]
            return c

        lax.fori_loop(0, NC - 1, chunk, 0)

        @pl.when(b == 0)
        def _():
            gb_ref[...] = jnp.zeros_like(gb_ref)
            gc_ref[...] = jnp.zeros_like(gc_ref)
            glam_ref[...] = jnp.zeros_like(glam_ref)
            gd_ref[...] = jnp.zeros_like(gd_ref)

        def final(t, hpr, hpi, gl):
            back_pow = pl.ds(T - 1 - t, 1)
            cr, ci = _cmul(anr_ref[...], ani_ref[...], pw_ref[0, 0, back_pow, :], -pw_ref[0, 1, back_pow, :])
            a_r = ar_ref[t] + cr
            a_i = ai_ref[t] + ci
            gl = (gl[0] + jnp.sum(a_r * hpr + a_i * hpi, axis=0, keepdims=True),
                  gl[1] + jnp.sum(a_i * hpr - a_r * hpi, axis=0, keepdims=True))
            acat = jnp.concatenate([a_r, a_i], axis=1).astype(BF16)
            ut = u_ref[0, t]
            dyt = dy_ref[0, t]
            du_ref[0, t] = (jnp.dot(acat, bt, preferred_element_type=F32) + dv * dyt).astype(BF16)
            gb_ref[0] += lax.dot_general(acat, ut, TN, preferred_element_type=F32)
            er, ei = _ssm_entry_term(t, pw_ref, inr_ref, ini_ref)
            h_r = hr_ref[t] + er
            h_i = hi_ref[t] + ei
            hr_ref[t] = h_r
            hi_ref[t] = h_i
            hcat = jnp.concatenate([h_r, h_i], axis=1).astype(BF16)
            gc_ref[0] += lax.dot_general(dyt.astype(BF16), hcat, TN, preferred_element_type=F32)
            gd_ref[0] += jnp.sum(dyt * ut.astype(F32), axis=0, keepdims=True)
            return gl

        zero = jnp.zeros((1, SP), F32)
        gl = final(0, inr_ref[...], ini_ref[...], (zero, zero))
        gl = lax.fori_loop(1, T, lambda t, gl: final(t, hr_ref[t - 1], hi_ref[t - 1], gl), gl)
        glam_ref[0, 0:1, :] += gl[0]
        glam_ref[0, 1:2, :] += gl[1]

    slab = lambda *shape: pl.BlockSpec((1,) + shape, lambda s, b: (s,) + (0,) * len(shape))
    tok = pl.BlockSpec((1, T, NC, LANES), lambda s, b: (b, 0, 0, s))
    big = pltpu.VMEM((T, NC, SP), F32)
    small = pltpu.VMEM((NC, SP), F32)
    return pl.pallas_call(
        body, name=name,
        out_shape=(jax.ShapeDtypeStruct((B, T, NC, W), BF16),
                   jax.ShapeDtypeStruct((NS, 2 * SP, LANES), F32), jax.ShapeDtypeStruct((NS, LANES, 2 * SP), F32),
                   jax.ShapeDtypeStruct((NS, 2, SP), F32), jax.ShapeDtypeStruct((NS, 1, LANES), F32)),
        grid=(NS, B),
        in_specs=[tok, tok, slab(LANES, 2 * SP), slab(2 * SP, LANES), slab(LANES, 2 * SP), slab(2, SP),
                  slab(2, T, SP), slab(2, SP), pl.BlockSpec((1, LANES), lambda s, b: (0, s))],
        out_specs=(tok, slab(2 * SP, LANES), slab(LANES, 2 * SP), slab(2, SP), slab(1, LANES)),
        scratch_shapes=[big, big, big, big, small, small, small, small],
        compiler_params=_cparams("parallel", "arbitrary"),
    )(u_tm, dy_tm, bmat, bmat_t, cmat_t, lam, pw, lamT, dskip)


def _glu_fwd(ys, w, b, *, name, tr=512):
    n, wd = ys.shape
    tr = _tile(n, tr, 8)

    def body(y_ref, w_ref, b_ref, o_ref):
        yb = _gelu(y_ref[...])
        z = jnp.dot(yb.astype(BF16), w_ref[...], preferred_element_type=F32) + b_ref[...]
        o_ref[...] = (yb * _sigmoid(z)).astype(BF16)

    row = pl.BlockSpec((tr, wd), lambda i: (i, 0))
    return pl.pallas_call(
        body, name=name, out_shape=jax.ShapeDtypeStruct((n, wd), BF16), grid=(n // tr,),
        in_specs=[row, pl.BlockSpec((wd, wd), lambda i: (0, 0)), pl.BlockSpec((1, wd), lambda i: (0, 0))],
        out_specs=row, compiler_params=_cparams("parallel"),
    )(ys, w, b.reshape(1, wd))


def _glu_bwd(ys, dyb2, w, w_t, b, *, name, tr=512):
    n, wd = ys.shape
    tr = _tile(n, tr, 8)

    def body(y_ref, d_ref, w_ref, wt_ref, b_ref, dys_ref, dz_ref, yb_ref, db_ref):
        i = pl.program_id(0)
        yv = y_ref[...]
        yb = _gelu(yv)
        ybb = yb.astype(BF16)
        sg = _sigmoid(jnp.dot(ybb, w_ref[...], preferred_element_type=F32) + b_ref[...])
        dv = d_ref[...].astype(F32)
        dz = dv * yb * sg * (1.0 - sg)
        dzb = dz.astype(BF16)
        dyb = dv * sg + jnp.dot(dzb, wt_ref[...], preferred_element_type=F32)
        dys_ref[...] = dyb * _gelu_grad(yv)
        dz_ref[...] = dzb
        yb_ref[...] = ybb
        part = jnp.sum(dz, axis=0, keepdims=True)

        @pl.when(i == 0)
        def _():
            db_ref[...] = part

        @pl.when(i > 0)
        def _():
            db_ref[...] += part

    row = pl.BlockSpec((tr, wd), lambda i: (i, 0))
    mat = pl.BlockSpec((wd, wd), lambda i: (0, 0))
    vec = pl.BlockSpec((1, wd), lambda i: (0, 0))
    return pl.pallas_call(
        body, name=name,
        out_shape=(jax.ShapeDtypeStruct((n, wd), F32), jax.ShapeDtypeStruct((n, wd), BF16),
                   jax.ShapeDtypeStruct((n, wd), BF16), jax.ShapeDtypeStruct((1, wd), F32)),
        grid=(n // tr,), in_specs=[row, row, mat, mat, vec], out_specs=(row, row, row, vec),
        compiler_params=_cparams("arbitrary"),
    )(ys, dyb2, w, w_t, b.reshape(1, wd))


def _merge_fwd(ya, yb2, wa_t, wb_t, proj, gate_blk, *, name, tr=512):
    n, aw = ya.shape
    d = wa_t.shape[0]
    tr = _tile(n, tr, 8)
    wa, wb = wa_t, wb_t

    def body(ya_ref, yb_ref, wa_ref, wb_ref, ga_ref, gb_ref, mix_ref, pa_ref, pb_ref):
        pa = lax.dot_general(ya_ref[...], wa_ref[...], NT, preferred_element_type=F32)
        pb = lax.dot_general(yb_ref[...], wb_ref[...], NT, preferred_element_type=F32)
        mix = _sigmoid(ga_ref[...].astype(F32)) * pa + _sigmoid(gb_ref[...].astype(F32)) * pb
        mix_ref[...] = mix.astype(BF16)
        pa_ref[...] = pa.astype(BF16)
        pb_ref[...] = pb.astype(BF16)

    row = lambda wdt: pl.BlockSpec((tr, wdt), lambda i: (i, 0))
    full = lambda r, c: pl.BlockSpec((r, c), lambda i: (0, 0))
    out = jax.ShapeDtypeStruct((n, d), BF16)
    return pl.pallas_call(
        body, name=name, out_shape=(out, out, out), grid=(n // tr,),
        in_specs=[row(aw), row(yb2.shape[1]), full(*wa.shape), full(*wb.shape),
                  pl.BlockSpec((tr, d), lambda i: (i, gate_blk)), pl.BlockSpec((tr, d), lambda i: (i, gate_blk + 1))],
        out_specs=(row(d), row(d), row(d)), compiler_params=_cparams("parallel"),
    )(ya, yb2, wa, wb, proj, proj)


def _merge_bwd(dx1, w_out, proj, pa, pb, gate_blk, dproj_cols, *, name, tr=512):
    n, d = dx1.shape
    tr = _tile(n, tr, 8)
    assert gate_blk % 2 == 0

    def body(dx_ref, w_ref, ga_ref, gb_ref, pa_ref, pb_ref, dpa_ref, dpb_ref, dg_ref):
        dm = lax.dot_general(dx_ref[...].astype(BF16), w_ref[...], NT, preferred_element_type=F32)
        sa = _sigmoid(ga_ref[...].astype(F32))
        sb = _sigmoid(gb_ref[...].astype(F32))
        dpa_ref[...] = (dm * sa).astype(BF16)
        dpb_ref[...] = (dm * sb).astype(BF16)
        dg_ref[:, :d] = (dm * pa_ref[...].astype(F32) * sa * (1.0 - sa)).astype(BF16)
        dg_ref[:, d:] = (dm * pb_ref[...].astype(F32) * sb * (1.0 - sb)).astype(BF16)

    row = pl.BlockSpec((tr, d), lambda i: (i, 0))
    out = jax.ShapeDtypeStruct((n, d), BF16)
    return pl.pallas_call(
        body, name=name, out_shape=(out, out, jax.ShapeDtypeStruct((n, dproj_cols), BF16)), grid=(n // tr,),
        in_specs=[row, pl.BlockSpec((d, d), lambda i: (0, 0)), pl.BlockSpec((tr, d), lambda i: (i, gate_blk)),
                  pl.BlockSpec((tr, d), lambda i: (i, gate_blk + 1)), row, row],
        out_specs=(row, row, pl.BlockSpec((tr, 2 * d), lambda i: (i, gate_blk // 2))),
        compiler_params=_cparams("parallel"),
    )(dx1, w_out, proj, proj, pa, pb)


def _outproj_fwd(mixed, w, x0, g, *, name, tr=512):
    n, d = x0.shape
    tr = _tile(n, tr, 8)

    def body(m_ref, w_ref, x_ref, g_ref, x1_ref, h_ref, r_ref):
        x1 = x_ref[...] + jnp.dot(m_ref[...], w_ref[...], preferred_element_type=F32)
        r = lax.rsqrt(jnp.mean(x1 * x1, axis=-1, keepdims=True) + RMS_EPS)
        x1_ref[...] = x1
        h_ref[...] = (x1 * r * g_ref[...]).astype(BF16)
        r_ref[...] = r

    row = pl.BlockSpec((tr, d), lambda i: (i, 0))
    return pl.pallas_call(
        body, name=name,
        out_shape=(jax.ShapeDtypeStruct((n, d), F32), jax.ShapeDtypeStruct((n, d), BF16),
                   jax.ShapeDtypeStruct((n, 1), F32)),
        grid=(n // tr,),
        in_specs=[row, pl.BlockSpec((d, d), lambda i: (0, 0)), row, pl.BlockSpec((1, d), lambda i: (0, 0))],
        out_specs=(row, row, pl.BlockSpec((tr, 1), lambda i: (i, 0))),
        compiler_params=_cparams("parallel"),
    )(mixed, w, x0, g.reshape(1, d))


def _adamw(w, g, m, v, *, name):
    shape = w.shape
    total = w.size
    if w.ndim == 3 and shape[1] % 8 == 0:
        lead, rows, cols = shape
    elif total % PACK_COLS == 0 and ((total // PACK_COLS) % 8 == 0 or total // PACK_COLS <= 512):
        lead, rows, cols = 1, total // PACK_COLS, PACK_COLS
    elif w.ndim >= 2:
        lead, rows, cols = 1, total // shape[-1], shape[-1]
    else:
        lead, rows, cols = 1, 1, total
    tr = _tile(rows, 512, 8)

    def body(w_ref, g_ref, m_ref, v_ref, d_ref, nm_ref, nv_ref):
        gv = g_ref[...]
        mn = ADAM_B1 * m_ref[...] + (1.0 - ADAM_B1) * gv
        vn = ADAM_B2 * v_ref[...] + (1.0 - ADAM_B2) * (gv * gv)
        m_hat = mn / (1.0 - ADAM_B1 ** ADAM_STEP)
        v_hat = vn / (1.0 - ADAM_B2 ** ADAM_STEP)
        d_ref[...] = -ADAM_LR * (m_hat / (jnp.sqrt(v_hat) + ADAM_EPS) + ADAM_WD * w_ref[...])
        nm_ref[...] = mn
        nv_ref[...] = vn

    blk = pl.BlockSpec((None, tr, cols), lambda l, i: (l, i, 0))
    out = jax.ShapeDtypeStruct((lead, rows, cols), F32)
    outs = pl.pallas_call(
        body, name=name, out_shape=(out, out, out), grid=(lead, rows // tr),
        in_specs=[blk] * 4, out_specs=(blk, blk, blk), compiler_params=_cparams("parallel", "parallel"),
    )(*[t.reshape(lead, rows, cols) for t in (w, g, m, v)])
    return tuple(o.reshape(shape) for o in outs)


def _all_gather(blocks, *, name):
    n = len(blocks)

    def body(*refs):
        x_refs, out_refs = refs[:n], refs[n:2 * n]
        send_sems, recv_sems, local_sems = refs[2 * n:]
        x, y, c = lax.axis_index("x"), lax.axis_index("y"), lax.axis_index("c")
        me, sibling = (x, y, c), (x, y, 1 - c)
        chips = [(1 - x, y), (x, 1 - y), (1 - x, 1 - y)]

        def slot(a, px, py, pc):
            return out_refs[a].at[4 * px + 2 * py + pc]

        def copy(a, k, block, to, src=None):
            return pltpu.make_async_remote_copy(
                src_ref=slot(a, *block) if src is None else src, dst_ref=slot(a, *block),
                send_sem=send_sems.at[7 * a + k], recv_sem=recv_sems.at[7 * a + k], device_id=to,
                device_id_type=MESH)

        started = []
        for a in range(n):
            mine = pltpu.make_async_copy(x_refs[a], slot(a, *me), local_sems.at[a])
            mine.start()
            started.append(mine)
        sends = []
        for a in range(n):
            first = [copy(a, 0, me, sibling, src=x_refs[a])]
            first += [copy(a, 1 + j, me, (*chip, c), src=x_refs[a]) for j, chip in enumerate(chips)]
            for cp in first:
                cp.start()
            sends += first
        for a in range(n):
            for j, chip in enumerate(chips):
                copy(a, 1 + j, (*chip, c), me).wait_recv()
                onward = copy(a, 4 + j, (*chip, c), sibling)
                onward.start()
                sends.append(onward)
        for a in range(n):
            copy(a, 0, sibling, me).wait_recv()
            for j, chip in enumerate(chips):
                copy(a, 4 + j, (*chip, 1 - c), me).wait_recv()
        for cp in sends:
            cp.wait_send()
        for mine in started:
            mine.wait()

    return pl.pallas_call(
        body, name=name, out_shape=[jax.ShapeDtypeStruct((N_DEV,) + b.shape, b.dtype) for b in blocks],
        in_specs=[HBM] * n, out_specs=[HBM] * n,
        scratch_shapes=[pltpu.SemaphoreType.DMA((7 * n,)), pltpu.SemaphoreType.DMA((7 * n,)),
                        pltpu.SemaphoreType.DMA((n,))],
    )(*blocks)


def _sum8(blocks, *, name, tr=SUM_ROWS):
    _, R, C = blocks.shape
    tr = _tile(R, tr, 16) if R % 16 == 0 else R

    def body(x_ref, o_ref):
        acc = x_ref[0].astype(F32)
        for i in range(1, N_DEV):
            acc = acc + x_ref[i].astype(F32)
        o_ref[...] = acc

    return pl.pallas_call(
        body, name=name, out_shape=jax.ShapeDtypeStruct((R, C), F32), grid=(R // tr,),
        in_specs=[pl.BlockSpec((N_DEV, tr, C), lambda i: (0, i, 0))],
        out_specs=pl.BlockSpec((tr, C), lambda i: (i, 0)), compiler_params=_cparams("parallel"),
    )(blocks)


def _pack(parts):
    flat = jnp.concatenate([p.astype(F32).reshape(-1) for p in parts])
    unit = 8 * PACK_COLS
    padded = -(-flat.size // unit) * unit
    return jnp.pad(flat, (0, padded - flat.size)).reshape(padded // PACK_COLS, PACK_COLS)


def _unpack(buf, like):
    flat, out, off = buf.reshape(-1), [], 0
    for p in like:
        out.append(flat[off:off + p.size].reshape(p.shape))
        off += p.size
    return out


def _ssm_discretise(lre, lim, logdt, bre, bim):
    lam = lax.complex(lre, lim)
    dt = jnp.exp(logdt)[:, None]
    lam_bar = jnp.exp(lam * dt)
    b_bar = ((lam_bar - 1.0) / lam)[:, :, None] * lax.complex(bre, bim)
    return lam_bar.real, lam_bar.imag, b_bar.real, b_bar.imag


def _block_diag(a, rows_first):
    ns, g, r, c = a.shape
    eye = jnp.eye(g, dtype=a.dtype)
    return jnp.einsum("sgrc,gh->sgrhc", a, eye).reshape(ns, g * r, g * c)


def _diag_blocks(m, r, c):
    ns = m.shape[0]
    g = SLAB_GROUPS
    return jnp.einsum("sgrhc,gh->sgrc", m.reshape(ns, g, r, g, c), jnp.eye(g, dtype=m.dtype))


def _to_tm(a, T):
    b, s, w = a.shape
    return a.reshape(b, s // T, T, w).transpose(0, 2, 1, 3)


def _from_tm(a):
    b, t, nc, w = a.shape
    return a.transpose(0, 2, 1, 3).reshape(b, nc * t, w)


WEIGHTS = ["norm_mix", "w_in", "b_forget", "ssm_lambda_re", "ssm_lambda_im", "ssm_log_dt", "ssm_b_re", "ssm_b_im",
           "ssm_c_re", "ssm_c_im", "ssm_d", "w_glu", "b_glu", "w_branch_a", "w_branch_b", "w_out", "norm_mlp",
           "w_mlp_up", "w_mlp_down", "norm_final"]
SHARDED = {"w_in": 2, "w_glu": 1, "w_branch_a": 2, "w_branch_b": 2, "w_out": 1, "w_mlp_up": 2, "w_mlp_down": 1}


REST = [n for n in SHARDED if n != "w_in"]
BY_COLUMNS = [n for n in SHARDED if SHARDED[n] == 2]


def _key(n):
    return n + "_t" if n in BY_COLUMNS else n


def _whole(seg):
    return seg.reshape(-1, seg.shape[-1])


def _blocks(g):
    return g.reshape(N_DEV, -1, g.shape[-1])


def _sum_blocks(n, got):
    return _sum8(got.reshape(N_DEV, -1, got.shape[-1]), name="sum_grads_" + n).reshape(got.shape[1:])


def kernel(x, norm_mix, w_in, b_forget, ssm_lambda_re, ssm_lambda_im, ssm_log_dt, ssm_b_re, ssm_b_im, ssm_c_re, ssm_c_im, ssm_d, w_glu, b_glu, w_branch_a, w_branch_b, w_out, norm_mlp, w_mlp_up, w_mlp_down, norm_final, loss_target, m_norm_mix, m_w_in, m_b_forget, m_ssm_lambda_re, m_ssm_lambda_im, m_ssm_log_dt, m_ssm_b_re, m_ssm_b_im, m_ssm_c_re, m_ssm_c_im, m_ssm_d, m_w_glu, m_b_glu, m_w_branch_a, m_w_branch_b, m_w_out, m_norm_mlp, m_w_mlp_up, m_w_mlp_down, m_norm_final, v_norm_mix, v_w_in, v_b_forget, v_ssm_lambda_re, v_ssm_lambda_im, v_ssm_log_dt, v_ssm_b_re, v_ssm_b_im, v_ssm_c_re, v_ssm_c_im, v_ssm_d, v_w_glu, v_b_glu, v_w_branch_a, v_w_branch_b, v_w_out, v_norm_mlp, v_w_mlp_up, v_w_mlp_down, v_norm_final):
    args = dict(locals())
    w = {n: args[n] for n in WEIGHTS}
    Bl, S, D = x.shape
    L, H = b_forget.shape
    G, P, C = ssm_b_re.shape[1:]
    AW, W, HP = H * HEAD_DIM, G * C, H // 2
    N = Bl * S
    T = SSM_CHUNK
    NS = G // SLAB_GROUPS
    SP = SLAB_GROUPS * P
    tq = min(ATTN_BLOCK, S)
    nq = S // tq
    u_off = 3 * AW
    gate_blk = (u_off + W) // D
    assert (u_off + W) % D == 0 and W % LANES == 0 and AW % LANES == 0 and S % T == 0

    tr_ = lambda a: jnp.swapaxes(a, 1, 2)
    shard = {n: tr_(w[n].astype(BF16)) if n in BY_COLUMNS else w[n].astype(BF16) for n in SHARDED}
    weights = [dict() for _ in range(L)]
    weights[0]["w_in_t"] = _whole(_all_gather([shard["w_in"][0]], name="gather_first")[0])

    ssm = []
    for l in range(L):
        disc, disc_vjp = jax.vjp(_ssm_discretise, ssm_lambda_re[l], ssm_lambda_im[l], ssm_log_dt[l],
                                 ssm_b_re[l], ssm_b_im[l])
        lbr, lbi, bbr, bbi = disc
        z = lax.complex(ssm_lambda_re[l], ssm_lambda_im[l]) * jnp.exp(ssm_log_dt[l])[:, None]
        powers = jnp.exp(z[None] * jnp.arange(1, T + 1, dtype=F32)[:, None, None])
        slabs = lambda a: a.reshape(NS, SP)
        lam = jnp.stack([slabs(lbr), slabs(lbi)], axis=1)
        lam_t = jnp.stack([slabs(powers[T - 1].real), slabs(powers[T - 1].imag)], axis=1)
        pw = jnp.stack([powers.real.reshape(T, NS, SP), powers.imag.reshape(T, NS, SP)], axis=0).transpose(2, 0, 1, 3)
        to_rows = lambda a: jnp.swapaxes(a.reshape(NS, SLAB_GROUPS, P, C), 2, 3)
        bmat = jnp.concatenate([_block_diag(to_rows(bbr), True), _block_diag(to_rows(bbi), True)], axis=2)
        cre = ssm_c_re[l].reshape(NS, SLAB_GROUPS, C, P)
        cim = ssm_c_im[l].reshape(NS, SLAB_GROUPS, C, P)
        cmat_t = jnp.concatenate([_block_diag(cre, True), -_block_diag(cim, True)], axis=2)
        ssm.append(dict(vjp=disc_vjp, lam=lam, lam_t=lam_t, pw=pw, bmat=bmat.astype(BF16),
                        bmat_t=tr_(bmat).astype(BF16), cmat=tr_(cmat_t).astype(BF16), cmat_t=cmat_t.astype(BF16),
                        d=ssm_d[l].reshape(1, W)))

    xcur = x.reshape(N, D)
    saved = []
    for l in range(L):
        s_, wl = ssm[l], weights[l]
        win_t = wl["w_in_t"]
        wl["wcat_t"] = jnp.concatenate([win_t[:3 * AW], win_t[3 * AW + H:]], axis=0)
        wl["wf_t"] = jnp.pad(win_t[3 * AW:3 * AW + H], ((0, LANES - H), (0, 0)))
        h, r0, fl = _rmsnorm_fwd(xcur, norm_mix[l], wl["wf_t"], name="norm_mix_fwd")
        proj = _mm(h, wl["wcat_t"], name="in_proj", tb=True, tm=WIDE_N, tn=WIDE_N)
        ft = fl[:, :H].reshape(Bl, S, H).transpose(0, 2, 1)
        F = _fox_gate_fwd(ft, b_forget[l], name="forget_gate_fwd")
        frow = F.reshape(Bl, HP, 2, nq, tq)
        proj3 = proj.reshape(Bl, S, -1)
        coming = [shard[n][l] for n in REST] + ([shard["w_in"][l + 1]] if l + 1 < L else [])
        ya, lse, got = _attn_fwd(proj3, frow, name="attn_fwd" if l + 1 < L else "attn_fwd_last", H=H, tq=tq,
                                 hosted=_Hosted(gather=coming))
        for n, seg in zip(REST, got):
            wl[_key(n)] = _whole(seg)
        if l + 1 < L:
            weights[l + 1]["w_in_t"] = _whole(got[-1])
        u_tm = _to_tm(proj3[:, :, u_off:u_off + W], T)
        ys = _from_tm(_ssm_fwd(u_tm, s_["bmat"], s_["cmat"], s_["lam"], s_["pw"], s_["lam_t"], s_["d"],
                               name="ssm_fwd")).reshape(N, W)
        yb2 = _glu_fwd(ys, wl["w_glu"], b_glu[l], name="glu_fwd")
        ya2 = ya.reshape(N, AW)
        mixed, pa, pb = _merge_fwd(ya2, yb2, wl["w_branch_a_t"], wl["w_branch_b_t"], proj, gate_blk,
                                   name="merge_fwd")
        x1, h2, r1 = _outproj_fwd(mixed, wl["w_out"], xcur, norm_mlp[l], name="out_proj")
        a = _mm(h2, wl["w_mlp_up_t"], name="mlp_up", tb=True, tm=WIDE_N, tn=WIDE_N)
        x2 = _mm(a, wl["w_mlp_down"], name="mlp_down", a_fn=_relu_sq, epi=lambda acc, res: acc + res,
                 extras=(x1,), out_dtype=F32, tk=LONG_K)
        saved.append(dict(x0=xcur, h=h, r0=r0, proj=proj, ft=ft, frow=frow, ya=ya, lse=lse, u_tm=u_tm,
                          ys=ys, yb2=yb2, mixed=mixed, pa=pa, pb=pb, x1=x1, h2=h2, r1=r1, a=a))
        xcur = x2

    dx, g_final, loss_row = _loss_head(xcur, norm_final, loss_target.reshape(N, D), name="loss_head")
    loss = lax.psum(loss_row[0, 0], MESH_AXES)

    big = {n: [None] * L for n in SHARDED}
    small = {n: [None] * L for n in WEIGHTS if n not in SHARDED and n != "norm_final"}
    small_sums = [None] * L
    win_grad_t = small_above = None
    for l in reversed(range(L)):
        sv, s_, wl = saved[l], ssm[l], weights[l]
        a = sv["a"]
        gw = {}
        d_a = _mm(dx, wl["w_mlp_down"], name="mlp_down_dx", tb=True, tn=WIDE_N,
                  epi=lambda acc, av: acc * (2.0 * jnp.maximum(av.astype(F32), 0.0)), extras=(a,))
        gw["w_mlp_down"] = _mm(a, dx, name="mlp_down_dw", ta=True, a_fn=_relu_sq, tk=LONG_K)
        gw["w_mlp_up"] = _mm(d_a, sv["h2"], name="mlp_up_dw", ta=True, tk=LONG_K)
        dx1, g = _mm_norm_bwd(d_a, wl["w_mlp_up_t"], sv["x1"], sv["r1"], norm_mlp[l], dx, name="mlp_up_dx_norm",
                              tk=LONG_K)
        small["norm_mlp"][l] = g[0]
        gw["w_out"] = _mm(sv["mixed"], dx1, name="out_proj_dw", ta=True, tk=LONG_K)
        ncat = wl["wcat_t"].shape[0]
        dpa, dpb, dproj = _merge_bwd(dx1, wl["w_out"], sv["proj"], sv["pa"], sv["pb"], gate_blk, ncat + LANES,
                                     name="merge_bwd")
        ya2 = sv["ya"].reshape(N, AW)
        gw["w_branch_a"] = _mm(dpa, ya2, name="branch_a_dw", ta=True, tk=LONG_K)
        dya = _mm(dpa, wl["w_branch_a_t"], name="branch_a_dx")
        gw["w_branch_b"] = _mm(dpb, sv["yb2"], name="branch_b_dw", ta=True, tk=LONG_K)
        dyb2 = _mm(dpb, wl["w_branch_b_t"], name="branch_b_dx")
        dys, dz, yb, g = _glu_bwd(sv["ys"], dyb2, wl["w_glu"], wl["w_glu"].T, b_glu[l], name="glu_bwd")
        small["b_glu"][l] = g[0]
        gw["w_glu"] = _mm(yb, dz, name="glu_dw", ta=True, tk=LONG_K)

        du_tm, g_bt, g_ct, g_lam, g_d = _ssm_bwd(
            sv["u_tm"], _to_tm(dys.reshape(Bl, S, W), T), s_["bmat"], s_["bmat_t"], s_["cmat_t"], s_["lam"],
            s_["pw"], s_["lam_t"], s_["d"], name="ssm_bwd")
        du = _from_tm(du_tm).reshape(N, W)
        g_b = _diag_blocks(jnp.swapaxes(g_bt, 1, 2).reshape(NS, LANES, 2, SP).transpose(2, 0, 1, 3).reshape(
            2 * NS, LANES, SP), C, P).reshape(2, G, C, P)
        g_bbar = jnp.swapaxes(g_b, 2, 3)
        g_c = _diag_blocks(g_ct.reshape(NS, LANES, 2, SP).transpose(2, 0, 1, 3).reshape(2 * NS, LANES, SP),
                           C, P).reshape(2, G, C, P)
        g_lbar = g_lam.transpose(1, 0, 2).reshape(2, G, P)
        g_lre, g_lim, g_ldt, g_bre, g_bim = s_["vjp"]((g_lbar[0], g_lbar[1], g_bbar[0], g_bbar[1]))
        small["ssm_lambda_re"][l], small["ssm_lambda_im"][l], small["ssm_log_dt"][l] = g_lre, g_lim, g_ldt
        small["ssm_b_re"][l], small["ssm_b_im"][l] = g_bre, g_bim
        small["ssm_c_re"][l], small["ssm_c_im"][l] = g_c[0], -g_c[1]
        small["ssm_d"][l] = g_d.reshape(W)

        proj3 = sv["proj"].reshape(Bl, S, -1)
        leaving = [_blocks(gw[n]) for n in REST] + ([_blocks(win_grad_t)] if l + 1 < L else [])
        (dq, dk, dv, dfk, dfq), got = _attn_bwd(
            proj3, sv["ya"], dya.reshape(Bl, S, AW), sv["lse"], sv["frow"],
            name="attn_bwd" if l + 1 < L else "attn_bwd_top", H=H, tq=tq,
            hosted=_Hosted(gather=[small_above] if l + 1 < L else [], exchange=leaving))
        if l + 1 < L:
            small_sums[l + 1] = _sum8(got[0], name="sum_small_grads")
            big["w_in"][l + 1] = _sum_blocks("w_in", got[-1]).T
            got = got[1:]
        for n, blocks in zip(REST, got):
            big[n][l] = _sum_blocks(n, blocks).T if n in BY_COLUMNS else _sum_blocks(n, blocks)
        dF = dfk.reshape(Bl, H, S) + dfq.transpose(0, 1, 3, 2).reshape(Bl, H, S)
        dft, g = _fox_gate_bwd(dF, sv["ft"], b_forget[l], name="forget_gate_bwd")
        small["b_forget"][l] = g[:, 0]
        dfl = jnp.pad(dft.transpose(0, 2, 1).reshape(N, H), ((0, 0), (0, LANES - H))).astype(BF16)
        for off, piece in ((0, dq.reshape(N, AW)), (AW, dk.reshape(N, AW)), (2 * AW, dv.reshape(N, AW)), (u_off, du),
                           (ncat, dfl)):
            dproj = lax.dynamic_update_slice(dproj, piece, (0, off))
        gcat_t = _mm(dproj, sv["h"], name="in_proj_dw", ta=True, tm=1408, tk=LONG_K)
        win_grad_t = jnp.concatenate([gcat_t[:3 * AW], gcat_t[ncat:ncat + H], gcat_t[3 * AW:ncat]], axis=0)
        wfull_t = jnp.concatenate([wl["wcat_t"], wl["wf_t"]], axis=0)
        if l > 0:
            dx, g = _mm_norm_bwd(dproj, wfull_t, sv["x0"], sv["r0"], norm_mix[l], dx1, name="in_proj_dx_norm",
                                 tk=1408)
        else:
            dh, got = _mm(dproj, wfull_t, name="in_proj_dx_bottom", out_dtype=F32, tk=1408,
                          hosted=_Hosted(exchange=[_blocks(win_grad_t)]))
            big["w_in"][0] = _sum_blocks("w_in", got[0]).T
            dx, g = _rmsnorm_bwd(dh, sv["x0"], sv["r0"], norm_mix[l], dx1, name="norm_mix_bwd")
        small["norm_mix"][l] = g[0]
        small_above = _pack([small[n][l] for n in small])

    last = [small[n][0] for n in small] + [g_final[0]]
    small_sums[0] = _sum8(_all_gather([_pack(last)], name="gather_small_grads")[0], name="sum_small_grads_last")
    grads = {n: jnp.stack(big[n]) for n in SHARDED}
    per_layer = [_unpack(small_sums[l], last if l == 0 else last[:-1]) for l in range(L)]
    for i, n in enumerate(small):
        grads[n] = jnp.stack([per_layer[l][i] for l in range(L)])
    grads["norm_final"] = per_layer[0][-1]

    deltas, new_m, new_v = {}, {}, {}
    for n in WEIGHTS:
        deltas[n], new_m[n], new_v[n] = _adamw(w[n], grads[n], args["m_" + n], args["v_" + n], name="adamw_" + n)
    return (loss, dx.reshape(Bl, S, D), *[grads[n] for n in WEIGHTS], *[deltas[n] for n in WEIGHTS],
            *[new_m[n] for n in WEIGHTS], *[new_v[n] for n in WEIGHTS])
```

```python
import functools

import jax
import jax.numpy as jnp
from jax import lax
from jax.experimental import pallas as pl
from jax.experimental.pallas import tpu as pltpu

F32 = jnp.float32
BF16 = jnp.bfloat16

N_DEV = 8
HEAD_DIM = 64
LANES = 128
SSM_CHUNK = 32
SLAB_GROUPS = 8
ATTN_BLOCK = 512
WIDE_KEYS = 2
LONG_K = 2048
WIDE_N = 2048
PACK_COLS = 1024
SUM_ROWS = 256
RMS_EPS = 1e-6
VMEM_LIMIT = 56 * 1024 * 1024
ADAM_LR, ADAM_B1, ADAM_B2, ADAM_EPS, ADAM_WD, ADAM_STEP = 0.001, 0.9, 0.999, 1e-08, 0.01, 10
MESH_AXES = ("x", "y", "c")
NEG = -1e30
NT = (((1,), (1,)), ((), ()))
TN = (((0,), (0,)), ((), ()))


def _cparams(*sem):
    return pltpu.CompilerParams(dimension_semantics=sem, vmem_limit_bytes=VMEM_LIMIT)


def _tile(dim, pref, unit=LANES):
    if dim <= pref:
        return dim
    best = None
    for t in range(unit, pref + 1, unit):
        if dim % t == 0:
            best = t
    assert best is not None, (dim, pref)
    return best


def _mm(a, b, *, name, ta=False, tb=False, a_fn=None, epi=None, extras=(), out_dtype=BF16, tm=1024, tn=1024,
        tk=1024, hosted=None):
    if ta:
        K, M = a.shape
    else:
        M, K = a.shape
    N, Kb = b.shape if tb else b.shape[::-1]
    assert K == Kb and not (ta and tb), (a.shape, b.shape)
    tm, tn, tk = _tile(M, tm), _tile(N, tn), _tile(K, tk)
    gm, gn, nk = M // tm, N // tn, K // tk
    ne = len(extras)
    nh = hosted.n if hosted else 0
    n_acc = 1 if nk > 1 else 0

    def body(a_ref, b_ref, *rest):
        e_refs, o_ref = rest[:ne], rest[ne + nh]
        acc_ref = rest[ne + 2 * nh + 1] if nk > 1 else None
        k = pl.program_id(2)
        if hosted:
            i, j = pl.program_id(0), pl.program_id(1)
            start, finish = hosted.run((i == 0) & (j == 0) & (k == 0), (i == gm - 1) & (j == gn - 1) & (k == nk - 1),
                                       rest[ne:ne + nh], rest[ne + nh + 1:ne + 2 * nh + 1],
                                       rest[ne + 2 * nh + 1 + n_acc:])
            start()
        av = a_ref[...]
        if a_fn is not None:
            av = a_fn(av)
        av = av.astype(BF16)
        bv = b_ref[...].astype(BF16)
        dims = TN if ta else NT if tb else (((1,), (0,)), ((), ()))
        part = lax.dot_general(av, bv, dims, preferred_element_type=F32)

        def finish_tile(r):
            if epi is not None:
                r = epi(r, *[e[...] for e in e_refs])
            o_ref[...] = r.astype(o_ref.dtype)

        if nk == 1:
            finish_tile(part)
        else:
            @pl.when(k == 0)
            def _():
                acc_ref[...] = part

            @pl.when(k > 0)
            def _():
                acc_ref[...] += part

            @pl.when(k == nk - 1)
            def _():
                finish_tile(acc_ref[...])

        if hosted:
            finish()

    a_spec = pl.BlockSpec((tk, tm), lambda i, j, k: (k, i)) if ta else pl.BlockSpec((tm, tk), lambda i, j, k: (i, k))
    outs = pl.pallas_call(
        body, name=name,
        out_shape=[jax.ShapeDtypeStruct((M, N), out_dtype)] + (hosted.out_shape if hosted else []),
        grid=(gm, gn, nk),
        in_specs=[a_spec, pl.BlockSpec((tn, tk), lambda i, j, k: (j, k)) if tb
                  else pl.BlockSpec((tk, tn), lambda i, j, k: (k, j))]
        + [pl.BlockSpec((tm, tn), lambda i, j, k: (i, j)) for _ in extras] + [HBM] * nh,
        out_specs=[pl.BlockSpec((tm, tn), lambda i, j, k: (i, j))] + [HBM] * nh,
        scratch_shapes=([pltpu.VMEM((tm, tn), F32)] if nk > 1 else []) + (hosted.scratch if hosted else []),
        compiler_params=_cparams(*(("arbitrary",) * 3 if hosted else ("parallel", "parallel", "arbitrary"))),
    )(a, b, *extras, *(hosted.arrays if hosted else []))
    return (outs[0], outs[1:]) if hosted else outs[0]


def _relu_sq(v):
    r = jnp.maximum(v.astype(F32), 0.0)
    return r * r


def _sigmoid(v):
    return 1.0 / (1.0 + jnp.exp(-v))


GELU_C = 0.7978845608028654
GELU_A = 0.044715


def _gelu(v):
    return 0.5 * v * (1.0 + jnp.tanh(GELU_C * (v + GELU_A * v * v * v)))


def _gelu_grad(v):
    t = jnp.tanh(GELU_C * (v + GELU_A * v * v * v))
    return 0.5 * (1.0 + t) + 0.5 * v * (1.0 - t * t) * GELU_C * (1.0 + 3.0 * GELU_A * v * v)


def _rmsnorm_fwd(x, g, wf_t, *, name, tr=512):
    n, d = x.shape
    nf = wf_t.shape[0]
    tr = _tile(n, tr, 8)

    def body(x_ref, g_ref, wf_ref, h_ref, r_ref, f_ref):
        xv = x_ref[...]
        r = lax.rsqrt(jnp.mean(xv * xv, axis=-1, keepdims=True) + RMS_EPS)
        h = (xv * r * g_ref[...]).astype(BF16)
        h_ref[...] = h
        r_ref[...] = r
        f_ref[...] = lax.dot_general(h, wf_ref[...], NT, preferred_element_type=F32)

    return pl.pallas_call(
        body, name=name,
        out_shape=(jax.ShapeDtypeStruct((n, d), BF16), jax.ShapeDtypeStruct((n, 1), F32),
                   jax.ShapeDtypeStruct((n, nf), F32)),
        grid=(n // tr,),
        in_specs=[pl.BlockSpec((tr, d), lambda i: (i, 0)), pl.BlockSpec((1, d), lambda i: (0, 0)),
                  pl.BlockSpec((nf, d), lambda i: (0, 0))],
        out_specs=(pl.BlockSpec((tr, d), lambda i: (i, 0)), pl.BlockSpec((tr, 1), lambda i: (i, 0)),
                   pl.BlockSpec((tr, nf), lambda i: (i, 0))),
        compiler_params=_cparams("parallel"),
    )(x, g.reshape(1, d), wf_t)


def _norm_bwd_tile(dh, x, r, g, dres):
    xh = x * r
    dxh = dh * g
    m = jnp.mean(dxh * xh, axis=-1, keepdims=True)
    return r * (dxh - xh * m) + dres, jnp.sum(dh * xh, axis=0, keepdims=True)


def _mm_norm_bwd(a, w, x, r, g, dres, *, name, tm=1024, tk=1024):
    M, K = a.shape
    D = w.shape[1]
    tm, tk = _tile(M, tm, 8), _tile(K, tk)
    nk = K // tk

    def body(a_ref, w_ref, x_ref, r_ref, g_ref, dres_ref, dx_ref, dg_ref, acc_ref):
        i, k = pl.program_id(0), pl.program_id(1)
        part = jnp.dot(a_ref[...].astype(BF16), w_ref[...], preferred_element_type=F32)

        @pl.when(k == 0)
        def _():
            acc_ref[...] = part

        @pl.when(k > 0)
        def _():
            acc_ref[...] += part

        @pl.when(k == nk - 1)
        def _():
            dx, dg = _norm_bwd_tile(acc_ref[...], x_ref[...], r_ref[...], g_ref[...], dres_ref[...])
            dx_ref[...] = dx

            @pl.when(i == 0)
            def _():
                dg_ref[...] = dg

            @pl.when(i > 0)
            def _():
                dg_ref[...] += dg

    row = pl.BlockSpec((tm, D), lambda i, k: (i, 0))
    vec = pl.BlockSpec((1, D), lambda i, k: (0, 0))
    return pl.pallas_call(
        body, name=name,
        out_shape=(jax.ShapeDtypeStruct((M, D), F32), jax.ShapeDtypeStruct((1, D), F32)),
        grid=(M // tm, nk),
        in_specs=[pl.BlockSpec((tm, tk), lambda i, k: (i, k)), pl.BlockSpec((tk, D), lambda i, k: (k, 0)),
                  row, pl.BlockSpec((tm, 1), lambda i, k: (i, 0)), vec, row],
        out_specs=(row, vec),
        scratch_shapes=[pltpu.VMEM((tm, D), F32)],
        compiler_params=_cparams("arbitrary", "arbitrary"),
    )(a, w, x, r, g.reshape(1, D), dres)


def _rmsnorm_bwd(dh, x, r, g, dres, *, name, tr=512):
    n, d = x.shape
    tr = _tile(n, tr, 8)

    def body(dh_ref, x_ref, r_ref, g_ref, dres_ref, dx_ref, dg_ref):
        i = pl.program_id(0)
        dx_ref[...], part = _norm_bwd_tile(dh_ref[...].astype(F32), x_ref[...], r_ref[...], g_ref[...],
                                           dres_ref[...])

        @pl.when(i == 0)
        def _():
            dg_ref[...] = part

        @pl.when(i > 0)
        def _():
            dg_ref[...] += part

    row = pl.BlockSpec((tr, d), lambda i: (i, 0))
    vec = pl.BlockSpec((1, d), lambda i: (0, 0))
    return pl.pallas_call(
        body, name=name,
        out_shape=(jax.ShapeDtypeStruct((n, d), F32), jax.ShapeDtypeStruct((1, d), F32)),
        grid=(n // tr,),
        in_specs=[row, row, pl.BlockSpec((tr, 1), lambda i: (i, 0)), vec, row],
        out_specs=(row, vec),
        compiler_params=_cparams("arbitrary"),
    )(dh, x, r, g.reshape(1, d), dres)


def _loss_head(x, g, target, *, name, tr=512):
    n, d = x.shape
    tr = _tile(n, tr, 8)

    def body(x_ref, g_ref, t_ref, dx_ref, dg_ref, loss_ref):
        i = pl.program_id(0)
        xv = x_ref[...]
        gv = g_ref[...]
        r = lax.rsqrt(jnp.mean(xv * xv, axis=-1, keepdims=True) + RMS_EPS)
        xh = xv * r
        err = xh * gv - t_ref[...]
        lpart = 0.5 * jnp.sum(jnp.mean(err * err, axis=-1, keepdims=True), axis=0, keepdims=True)
        dy = err * (1.0 / d)
        dxh = dy * gv
        m = jnp.mean(dxh * xh, axis=-1, keepdims=True)
        dx_ref[...] = r * (dxh - xh * m)
        gpart = jnp.sum(dy * xh, axis=0, keepdims=True)
        lrow = jnp.broadcast_to(lpart, (1, LANES))

        @pl.when(i == 0)
        def _():
            dg_ref[...] = gpart
            loss_ref[...] = lrow

        @pl.when(i > 0)
        def _():
            dg_ref[...] += gpart
            loss_ref[...] += lrow

    row = pl.BlockSpec((tr, d), lambda i: (i, 0))
    vec = pl.BlockSpec((1, d), lambda i: (0, 0))
    return pl.pallas_call(
        body, name=name,
        out_shape=(jax.ShapeDtypeStruct((n, d), F32), jax.ShapeDtypeStruct((1, d), F32),
                   jax.ShapeDtypeStruct((1, LANES), F32)),
        grid=(n // tr,),
        in_specs=[row, vec, row],
        out_specs=(row, vec, pl.BlockSpec((1, LANES), lambda i: (0, 0))),
        compiler_params=_cparams("arbitrary"),
    )(x, g.reshape(1, d), target)


def _tri_dot(v, tri):
    hi = v.astype(BF16)
    r1 = v - hi.astype(F32)
    mid = r1.astype(BF16)
    lo = (r1 - mid.astype(F32)).astype(BF16)
    d = functools.partial(jnp.dot, preferred_element_type=F32)
    return d(hi, tri) + d(mid, tri) + d(lo, tri)


def _fox_gate_fwd(ft, bf, *, name, blk=256):
    B, H, S = ft.shape
    blk = _tile(S, blk)
    nb = S // blk

    def body(f_ref, b_ref, o_ref):
        x = f_ref[0] + b_ref[...]
        logf = jnp.minimum(x, 0.0) - jnp.log(1.0 + jnp.exp(-jnp.abs(x)))
        rr = lax.broadcasted_iota(jnp.int32, (blk, blk), 0)
        cc = lax.broadcasted_iota(jnp.int32, (blk, blk), 1)
        tri = (rr <= cc).astype(BF16)
        carry = jnp.zeros((H, 1), F32)
        for n in range(nb):
            c = _tri_dot(logf[:, n * blk:(n + 1) * blk], tri) + carry
            o_ref[0, :, n * blk:(n + 1) * blk] = c
            carry = c[:, blk - 1:blk]

    return pl.pallas_call(
        body, name=name,
        out_shape=jax.ShapeDtypeStruct((B, H, S), F32),
        grid=(B,),
        in_specs=[pl.BlockSpec((1, H, S), lambda b: (b, 0, 0)), pl.BlockSpec((H, 1), lambda b: (0, 0))],
        out_specs=pl.BlockSpec((1, H, S), lambda b: (b, 0, 0)),
        compiler_params=_cparams("parallel"),
    )(ft, bf.reshape(H, 1))


def _fox_gate_bwd(dF, ft, bf, *, name, blk=256):
    B, H, S = ft.shape
    blk = _tile(S, blk)
    nb = S // blk

    def body(d_ref, f_ref, b_ref, o_ref, db_ref):
        b = pl.program_id(0)
        x = f_ref[0] + b_ref[...]
        sneg = 1.0 / (1.0 + jnp.exp(x))
        dv = d_ref[0]
        rr = lax.broadcasted_iota(jnp.int32, (blk, blk), 0)
        cc = lax.broadcasted_iota(jnp.int32, (blk, blk), 1)
        tri = (rr >= cc).astype(BF16)
        carry = jnp.zeros((H, 1), F32)
        tot = jnp.zeros((H, 1), F32)
        for n in reversed(range(nb)):
            sl = slice(n * blk, (n + 1) * blk)
            c = _tri_dot(dv[:, sl], tri) + carry
            g = c * sneg[:, sl]
            o_ref[0, :, sl] = g
            tot = tot + jnp.sum(g, axis=1, keepdims=True)
            carry = c[:, 0:1]

        @pl.when(b == 0)
        def _():
            db_ref[...] = tot

        @pl.when(b > 0)
        def _():
            db_ref[...] += tot

    blkspec = pl.BlockSpec((1, H, S), lambda b: (b, 0, 0))
    return pl.pallas_call(
        body, name=name,
        out_shape=(jax.ShapeDtypeStruct((B, H, S), F32), jax.ShapeDtypeStruct((H, 1), F32)),
        grid=(B,),
        in_specs=[blkspec, blkspec, pl.BlockSpec((H, 1), lambda b: (0, 0))],
        out_specs=(blkspec, pl.BlockSpec((H, 1), lambda b: (0, 0))),
        compiler_params=_cparams("arbitrary"),
    )(dF, ft, bf.reshape(H, 1))


def _head_masks():
    lane = lax.broadcasted_iota(jnp.int32, (1, LANES), 1)
    return [lane < HEAD_DIM, lane >= HEAD_DIM]


HBM = pl.BlockSpec(memory_space=pltpu.HBM)
MESH = pl.DeviceIdType.MESH


def _direct_copies(kinds, x_refs, out_refs, send_sems, recv_sems, local_sems):
    x, y, c = lax.axis_index("x"), lax.axis_index("y"), lax.axis_index("c")
    me = 4 * x + 2 * y + c
    copies = []
    for a, (kind, xr, outr) in enumerate(zip(kinds, x_refs, out_refs)):
        copies.append(pltpu.make_async_copy(xr if kind == "gather" else xr.at[me], outr.at[me], local_sems.at[a]))
    for k in range(1, N_DEV):
        px = 1 - x if (k >> 2) & 1 else x
        py = 1 - y if (k >> 1) & 1 else y
        pc = 1 - c if k & 1 else c
        for a, (kind, xr, outr) in enumerate(zip(kinds, x_refs, out_refs)):
            copies.append(pltpu.make_async_remote_copy(
                src_ref=xr if kind == "gather" else xr.at[4 * px + 2 * py + pc], dst_ref=outr.at[me],
                send_sem=send_sems.at[7 * a + k - 1], recv_sem=recv_sems.at[7 * a + k - 1],
                device_id=(px, py, pc), device_id_type=MESH))
    return copies


class _Hosted:
    def __init__(self, gather=(), exchange=()):
        self.arrays = list(gather) + list(exchange)
        self.kinds = ["gather"] * len(gather) + ["exchange"] * len(exchange)
        self.n = len(self.arrays)
        self.out_shape = [jax.ShapeDtypeStruct(((N_DEV,) if k == "gather" else ()) + a.shape, a.dtype)
                          for k, a in zip(self.kinds, self.arrays)]
        self.scratch = [pltpu.SemaphoreType.DMA((7 * self.n,)), pltpu.SemaphoreType.DMA((7 * self.n,)),
                        pltpu.SemaphoreType.DMA((self.n,))]

    def run(self, first, last, x_refs, out_refs, sems):
        def go(when, act):
            @pl.when(when)
            def _():
                for cp in _direct_copies(self.kinds, x_refs, out_refs, *sems):
                    act(cp)
        return (lambda: go(first, lambda cp: cp.start())), (lambda: go(last, lambda cp: cp.wait()))


def _stack_heads(x, masks):
    zero = jnp.zeros_like(x)
    return jnp.concatenate([jnp.where(masks[0], x, zero), jnp.where(masks[1], x, zero)], axis=0)


def _attn_fwd(proj, frow, *, name, H, tq, hosted=None):
    B, S, _ = proj.shape
    HP = H // 2
    nq = S // tq
    scale = HEAD_DIM ** -0.5
    nh = hosted.n if hosted else 0

    def body(*refs):
        q_ref, k_ref, v_ref, fk_ref = refs[:4]
        o_ref, lse_ref = refs[4 + nh:6 + nh]
        i = pl.program_id(2)
        if hosted:
            b, hp = pl.program_id(0), pl.program_id(1)
            start, finish = hosted.run((b == 0) & (hp == 0) & (i == 0), (b == B - 1) & (hp == HP - 1) & (i == nq - 1),
                                       refs[4:4 + nh], refs[6 + nh:6 + 2 * nh], refs[6 + 2 * nh:])
            start()
        masks = _head_masks()
        q2 = _stack_heads(q_ref[0], masks) * jnp.asarray(scale, BF16)
        rr = lax.broadcasted_iota(jnp.int32, (tq, tq), 0)
        cc = lax.broadcasted_iota(jnp.int32, (tq, tq), 1)
        causal = rr >= cc

        def block(j, carry, masked, span=1):
            rows = pl.ds(pl.multiple_of(j * (span * tq), span * tq), span * tq)
            kj = k_ref[0, rows, :]
            vj = v_ref[0, rows, :]
            s2 = lax.dot_general(q2, kj, NT, preferred_element_type=F32)
            new, ps = [], []
            for h in range(2):
                m, l, acc = carry[h]
                fk = jnp.concatenate([fk_ref[0, 0, h, pl.ds(j * span + n, 1), :] for n in range(span)], axis=1)
                s = s2[h * tq:(h + 1) * tq] - fk
                if masked:
                    s = jnp.where(causal, s, NEG)
                m_new = jnp.maximum(m, jnp.max(s, axis=-1, keepdims=True))
                alpha = jnp.exp(m - m_new)
                p = jnp.exp(s - m_new)
                new.append((m_new, alpha * l + jnp.sum(p, axis=-1, keepdims=True), alpha, acc))
                ps.append(p.astype(BF16))
            pv = jnp.dot(jnp.concatenate(ps, axis=0), vj, preferred_element_type=F32)
            return tuple((m, l, alpha * acc + pv[h * tq:(h + 1) * tq]) for h, (m, l, alpha, acc) in enumerate(new))

        one = (jnp.full((tq, 1), NEG, F32), jnp.zeros((tq, 1), F32), jnp.zeros((tq, LANES), F32))
        carry = lax.fori_loop(0, i // WIDE_KEYS, lambda j, c: block(j, c, False, WIDE_KEYS), (one, one))
        carry = lax.fori_loop((i // WIDE_KEYS) * WIDE_KEYS, i, lambda j, c: block(j, c, False), carry)
        (m0, l0, a0), (m1, l1, a1) = block(i, carry, True)
        o_ref[0] = jnp.where(masks[0], a0 / l0, a1 / l1).astype(BF16)
        two = lax.broadcasted_iota(jnp.int32, (1, 2), 1)
        lse_ref[0, 0] = jnp.where(two == 0, m0 + jnp.log(l0), m1 + jnp.log(l1))
        if hosted:
            finish()

    kv = lambda off: pl.BlockSpec((1, S, LANES), lambda b, hp, i: (b, 0, off + hp))
    outs = pl.pallas_call(
        body, name=name,
        out_shape=[jax.ShapeDtypeStruct((B, S, H * HEAD_DIM), BF16), jax.ShapeDtypeStruct((B, HP, S, 2), F32)]
        + (hosted.out_shape if hosted else []),
        grid=(B, HP, nq),
        in_specs=[pl.BlockSpec((1, tq, LANES), lambda b, hp, i: (b, i, hp)), kv(HP), kv(2 * HP),
                  pl.BlockSpec((1, 1, 2, nq, tq), lambda b, hp, i: (b, hp, 0, 0, 0))] + [HBM] * nh,
        out_specs=[pl.BlockSpec((1, tq, LANES), lambda b, hp, i: (b, i, hp)),
                   pl.BlockSpec((1, 1, tq, 2), lambda b, hp, i: (b, hp, i, 0))] + [HBM] * nh,
        scratch_shapes=hosted.scratch if hosted else [],
        compiler_params=_cparams("arbitrary", "arbitrary", "arbitrary"),
    )(proj, proj, proj, frow, *(hosted.arrays if hosted else []))
    return outs[0], outs[1], outs[2:]


def _attn_bwd(proj, ya, dya, lse, frow, *, name, H, tq, hosted=None):
    B, S, _ = proj.shape
    HP = H // 2
    nq = S // tq
    AW = H * HEAD_DIM
    scale = HEAD_DIM ** -0.5
    nh = hosted.n if hosted else 0

    def body(*refs):
        q_ref, k_ref, v_ref, o_ref, do_ref, lse_ref, fk_ref = refs[:7]
        dq_ref, dk_ref, dv_ref, dfk_ref, dfq_ref = refs[7 + nh:12 + nh]
        (q2_ref, do2_ref, lse2_ref, delta2_ref, dq2_acc, dfq2_acc, dk_acc, dv_acc,
         dfk_acc) = refs[12 + 2 * nh:21 + 2 * nh]
        if hosted:
            b, hp = pl.program_id(0), pl.program_id(1)
            start, finish = hosted.run((b == 0) & (hp == 0), (b == B - 1) & (hp == HP - 1),
                                       refs[7:7 + nh], refs[12 + nh:12 + 2 * nh], refs[21 + 2 * nh:])
            start()
        masks = _head_masks()
        rr = lax.broadcasted_iota(jnp.int32, (tq, tq), 0)
        cc = lax.broadcasted_iota(jnp.int32, (tq, tq), 1)
        causal = rr >= cc
        sc = jnp.asarray(scale, BF16)

        def stage(i, c):
            rows = pl.ds(pl.multiple_of(i * tq, tq), tq)
            dov = do_ref[0, rows, :]
            q2_ref[i] = _stack_heads(q_ref[0, rows, :], masks) * sc
            do2_ref[i] = _stack_heads(dov, masks)
            prod = dov.astype(F32) * o_ref[0, rows, :].astype(F32)
            delta2_ref[i] = jnp.concatenate(
                [jnp.sum(jnp.where(masks[h], prod, 0.0), axis=-1, keepdims=True) for h in range(2)], axis=0)
            lv = lse_ref[0, 0, rows, :]
            lse2_ref[i] = jnp.concatenate([lv[:, 0:1], lv[:, 1:2]], axis=0)
            return c

        lax.fori_loop(0, nq, stage, 0)
        dq2_acc[...] = jnp.zeros_like(dq2_acc)
        dfq2_acc[...] = jnp.zeros_like(dfq2_acc)

        def kv_block(j, carry):
            rows_j = pl.ds(pl.multiple_of(j * tq, tq), tq)
            kj = k_ref[0, rows_j, :]
            vj = v_ref[0, rows_j, :]
            ks = kj * sc
            dk_acc[...] = jnp.zeros_like(dk_acc)
            dv_acc[...] = jnp.zeros_like(dv_acc)
            dfk_acc[...] = jnp.zeros_like(dfk_acc)

            def logits(i):
                return (lax.dot_general(q2_ref[i], kj, NT, preferred_element_type=F32),
                        lax.dot_general(do2_ref[i], vj, NT, preferred_element_type=F32))

            def probs(i, s2, dp2, masked):
                lse2 = lse2_ref[i]
                delta2 = delta2_ref[i]
                ps, dss = [], []
                for h in range(2):
                    half = slice(h * tq, (h + 1) * tq)
                    p = jnp.exp(s2[half] - fk_ref[0, 0, h, pl.ds(j, 1), :] - lse2[half])
                    if masked:
                        p = jnp.where(causal, p, 0.0)
                    ds = p * (dp2[half] - delta2[half])
                    dfk_acc[h:h + 1, :] -= jnp.sum(ds, axis=0, keepdims=True)
                    dfq2_acc[i, half, :] += jnp.sum(ds, axis=1, keepdims=True)
                    ps.append(p.astype(BF16))
                    dss.append(ds.astype(BF16))
                return jnp.concatenate(ps, axis=0), jnp.concatenate(dss, axis=0)

            def grads(i, p2, ds2):
                dv_acc[...] += lax.dot_general(p2, do2_ref[i], TN, preferred_element_type=F32)
                dk_acc[...] += lax.dot_general(ds2, q2_ref[i], TN, preferred_element_type=F32)
                dq2_acc[i] += jnp.dot(ds2, ks, preferred_element_type=F32)

            grads(j, *probs(j, *logits(j), True))

            def rest(i, c):
                grads(i, *probs(i, *logits(i), False))
                return c

            lax.fori_loop(j + 1, nq, rest, 0)
            dk_ref[0, rows_j, :] = dk_acc[...].astype(BF16)
            dv_ref[0, rows_j, :] = dv_acc[...].astype(BF16)
            for h in range(2):
                dfk_ref[0, 0, h, pl.ds(j, 1), :] = dfk_acc[h:h + 1, :]
            return carry

        lax.fori_loop(0, nq, kv_block, 0)
        two = lax.broadcasted_iota(jnp.int32, (1, 2), 1)

        def finish_block(i, c):
            rows = pl.ds(pl.multiple_of(i * tq, tq), tq)
            dq2 = dq2_acc[i]
            dq_ref[0, rows, :] = jnp.where(masks[0], dq2[:tq], dq2[tq:]).astype(BF16)
            dfq2 = dfq2_acc[i]
            dfq_ref[0, 0, rows, :] = jnp.where(two == 0, dfq2[:tq], dfq2[tq:])
            return c

        lax.fori_loop(0, nq, finish_block, 0)
        if hosted:
            finish()

    col = lambda off: pl.BlockSpec((1, S, LANES), lambda b, hp: (b, 0, off + hp))
    stat = pl.BlockSpec((1, 1, S, 2), lambda b, hp: (b, hp, 0, 0))
    rowf = pl.BlockSpec((1, 1, 2, nq, tq), lambda b, hp: (b, hp, 0, 0, 0))
    grad = jax.ShapeDtypeStruct((B, S, AW), BF16)
    outs = pl.pallas_call(
        body, name=name,
        out_shape=[grad, grad, grad, jax.ShapeDtypeStruct((B, HP, 2, nq, tq), F32),
                   jax.ShapeDtypeStruct((B, HP, S, 2), F32)] + (hosted.out_shape if hosted else []),
        grid=(B, HP),
        in_specs=[col(0), col(HP), col(2 * HP), col(0), col(0), stat, rowf] + [HBM] * nh,
        out_specs=[col(0), col(0), col(0), rowf, stat] + [HBM] * nh,
        scratch_shapes=[pltpu.VMEM((nq, 2 * tq, LANES), BF16), pltpu.VMEM((nq, 2 * tq, LANES), BF16),
                        pltpu.VMEM((nq, 2 * tq, 1), F32), pltpu.VMEM((nq, 2 * tq, 1), F32),
                        pltpu.VMEM((nq, 2 * tq, LANES), F32), pltpu.VMEM((nq, 2 * tq, 1), F32),
                        pltpu.VMEM((tq, LANES), F32), pltpu.VMEM((tq, LANES), F32), pltpu.VMEM((2, tq), F32)]
        + (hosted.scratch if hosted else []),
        compiler_params=_cparams("arbitrary", "arbitrary"),
    )(proj, proj, proj, ya, dya, lse, frow, *(hosted.arrays if hosted else []))
    return outs[:5], outs[5:]


def _cmul(ar, ai, br, bi):
    return ar * br - ai * bi, ar * bi + ai * br


def _ssm_states(u_ref, bm, lam_ref, pw_ref, lamT_ref, hr_ref, hi_ref, inr_ref, ini_ref, T, NC, SP):
    lr, li = lam_ref[0, 0:1, :], lam_ref[0, 1:2, :]
    bu = jnp.dot(u_ref[0, 0], bm, preferred_element_type=F32)
    hr_ref[0] = bu[:, :SP]
    hi_ref[0] = bu[:, SP:]

    def step(t, c):
        bu = jnp.dot(u_ref[0, t], bm, preferred_element_type=F32)
        pr, pi = _cmul(hr_ref[t - 1], hi_ref[t - 1], lr, li)
        hr_ref[t] = pr + bu[:, :SP]
        hi_ref[t] = pi + bu[:, SP:]
        return c

    lax.fori_loop(1, T, step, 0, unroll=2)

    tr, ti = lamT_ref[0, 0:1, :], lamT_ref[0, 1:2, :]
    inr_ref[0:1, :] = jnp.zeros((1, SP), F32)
    ini_ref[0:1, :] = jnp.zeros((1, SP), F32)

    def chunk(n, c):
        prev = pl.ds(n - 1, 1)
        pr, pi = _cmul(inr_ref[prev, :], ini_ref[prev, :], tr, ti)
        inr_ref[pl.ds(n, 1), :] = pr + hr_ref[T - 1, prev, :]
        ini_ref[pl.ds(n, 1), :] = pi + hi_ref[T - 1, prev, :]
        return c

    lax.fori_loop(1, NC, chunk, 0)


def _ssm_entry_term(t, pw_ref, inr_ref, ini_ref):
    return _cmul(inr_ref[...], ini_ref[...], pw_ref[0, 0, pl.ds(t, 1), :], pw_ref[0, 1, pl.ds(t, 1), :])


def _ssm_fwd(u_tm, bmat, cmat, lam, pw, lamT, dskip, *, name):
    B, T, NC, W = u_tm.shape
    NS = W // LANES
    SP = bmat.shape[2] // 2

    def body(u_ref, b_ref, c_ref, lam_ref, pw_ref, lamT_ref, d_ref, y_ref, hr_ref, hi_ref, inr_ref, ini_ref):
        _ssm_states(u_ref, b_ref[0], lam_ref, pw_ref, lamT_ref, hr_ref, hi_ref, inr_ref, ini_ref, T, NC, SP)
        cm = c_ref[0]
        dv = d_ref[...]

        def out(t, c):
            cr, ci = _ssm_entry_term(t, pw_ref, inr_ref, ini_ref)
            hcat = jnp.concatenate([hr_ref[t] + cr, hi_ref[t] + ci], axis=1).astype(BF16)
            y_ref[0, t] = jnp.dot(hcat, cm, preferred_element_type=F32) + dv * u_ref[0, t].astype(F32)
            return c

        lax.fori_loop(0, T, out, 0, unroll=2)

    slab = lambda *shape: pl.BlockSpec((1,) + shape, lambda b, s: (s,) + (0,) * len(shape))
    tok = pl.BlockSpec((1, T, NC, LANES), lambda b, s: (b, 0, 0, s))
    return pl.pallas_call(
        body, name=name,
        out_shape=jax.ShapeDtypeStruct((B, T, NC, W), F32),
        grid=(B, NS),
        in_specs=[tok, slab(LANES, 2 * SP), slab(2 * SP, LANES), slab(2, SP), slab(2, T, SP), slab(2, SP),
                  pl.BlockSpec((1, LANES), lambda b, s: (0, s))],
        out_specs=tok,
        scratch_shapes=[pltpu.VMEM((T, NC, SP), F32), pltpu.VMEM((T, NC, SP), F32),
                        pltpu.VMEM((NC, SP), F32), pltpu.VMEM((NC, SP), F32)],
        compiler_params=_cparams("parallel", "parallel"),
    )(u_tm, bmat, cmat, lam, pw, lamT, dskip)


def _ssm_bwd(u_tm, dy_tm, bmat, bmat_t, cmat_t, lam, pw, lamT, dskip, *, name):
    B, T, NC, W = u_tm.shape
    NS = W // LANES
    SP = bmat.shape[2] // 2

    def body(u_ref, dy_ref, b_ref, bt_ref, ct_ref, lam_ref, pw_ref, lamT_ref, d_ref,
             du_ref, gb_ref, gc_ref, glam_ref, gd_ref,
             hr_ref, hi_ref, ar_ref, ai_ref, inr_ref, ini_ref, anr_ref, ani_ref):
        b = pl.program_id(1)
        _ssm_states(u_ref, b_ref[0], lam_ref, pw_ref, lamT_ref, hr_ref, hi_ref, inr_ref, ini_ref, T, NC, SP)
        lr, li = lam_ref[0, 0:1, :], lam_ref[0, 1:2, :]
        ct = ct_ref[0]
        bt = bt_ref[0]
        dv = d_ref[...]

        gh = jnp.dot(dy_ref[0, T - 1].astype(BF16), ct, preferred_element_type=F32)
        ar_ref[T - 1] = gh[:, :SP]
        ai_ref[T - 1] = gh[:, SP:]

        def back(k, c):
            t = T - 2 - k
            gh = jnp.dot(dy_ref[0, t].astype(BF16), ct, preferred_element_type=F32)
            pr, pi = _cmul(ar_ref[t + 1], ai_ref[t + 1], lr, -li)
            ar_ref[t] = pr + gh[:, :SP]
            ai_ref[t] = pi + gh[:, SP:]
            return c

        lax.fori_loop(0, T - 1, back, 0, unroll=2)

        tr, ti = lamT_ref[0, 0:1, :], lamT_ref[0, 1:2, :]
        anr_ref[NC - 1:NC, :] = jnp.zeros((1, SP), F32)
        ani_ref[NC - 1:NC, :] = jnp.zeros((1, SP), F32)

        def chunk(k, c):
            n = NC - 2 - k
            nxt = pl.ds(n + 1, 1)
            pr, pi = _cmul(anr_ref[nxt, :], ani_ref[nxt, :], tr, -ti)
            anr_ref[pl.ds(n, 1), :] = pr + ar_ref[0, nxt, :]
            ani_ref[pl.ds(n, 1), :] = pi + ai_ref[0, nxt, :]
            return c

        lax.fori_loop(0, NC - 1, chunk, 0)

        @pl.when(b == 0)
        def _():
            gb_ref[...] = jnp.zeros_like(gb_ref)
            gc_ref[...] = jnp.zeros_like(gc_ref)
            glam_ref[...] = jnp.zeros_like(glam_ref)
            gd_ref[...] = jnp.zeros_like(gd_ref)

        def final(t, hpr, hpi, gl):
            back_pow = pl.ds(T - 1 - t, 1)
            cr, ci = _cmul(anr_ref[...], ani_ref[...], pw_ref[0, 0, back_pow, :], -pw_ref[0, 1, back_pow, :])
            a_r = ar_ref[t] + cr
            a_i = ai_ref[t] + ci
            gl = (gl[0] + jnp.sum(a_r * hpr + a_i * hpi, axis=0, keepdims=True),
                  gl[1] + jnp.sum(a_i * hpr - a_r * hpi, axis=0, keepdims=True))
            acat = jnp.concatenate([a_r, a_i], axis=1).astype(BF16)
            ut = u_ref[0, t]
            dyt = dy_ref[0, t]
            du_ref[0, t] = (jnp.dot(acat, bt, preferred_element_type=F32) + dv * dyt).astype(BF16)
            gb_ref[0] += lax.dot_general(acat, ut, TN, preferred_element_type=F32)
            er, ei = _ssm_entry_term(t, pw_ref, inr_ref, ini_ref)
            h_r = hr_ref[t] + er
            h_i = hi_ref[t] + ei
            hr_ref[t] = h_r
            hi_ref[t] = h_i
            hcat = jnp.concatenate([h_r, h_i], axis=1).astype(BF16)
            gc_ref[0] += lax.dot_general(dyt.astype(BF16), hcat, TN, preferred_element_type=F32)
            gd_ref[0] += jnp.sum(dyt * ut.astype(F32), axis=0, keepdims=True)
            return gl

        zero = jnp.zeros((1, SP), F32)
        gl = final(0, inr_ref[...], ini_ref[...], (zero, zero))
        gl = lax.fori_loop(1, T, lambda t, gl: final(t, hr_ref[t - 1], hi_ref[t - 1], gl), gl)
        glam_ref[0, 0:1, :] += gl[0]
        glam_ref[0, 1:2, :] += gl[1]

    slab = lambda *shape: pl.BlockSpec((1,) + shape, lambda s, b: (s,) + (0,) * len(shape))
    tok = pl.BlockSpec((1, T, NC, LANES), lambda s, b: (b, 0, 0, s))
    big = pltpu.VMEM((T, NC, SP), F32)
    small = pltpu.VMEM((NC, SP), F32)
    return pl.pallas_call(
        body, name=name,
        out_shape=(jax.ShapeDtypeStruct((B, T, NC, W), BF16),
                   jax.ShapeDtypeStruct((NS, 2 * SP, LANES), F32), jax.ShapeDtypeStruct((NS, LANES, 2 * SP), F32),
                   jax.ShapeDtypeStruct((NS, 2, SP), F32), jax.ShapeDtypeStruct((NS, 1, LANES), F32)),
        grid=(NS, B),
        in_specs=[tok, tok, slab(LANES, 2 * SP), slab(2 * SP, LANES), slab(LANES, 2 * SP), slab(2, SP),
                  slab(2, T, SP), slab(2, SP), pl.BlockSpec((1, LANES), lambda s, b: (0, s))],
        out_specs=(tok, slab(2 * SP, LANES), slab(LANES, 2 * SP), slab(2, SP), slab(1, LANES)),
        scratch_shapes=[big, big, big, big, small, small, small, small],
        compiler_params=_cparams("parallel", "arbitrary"),
    )(u_tm, dy_tm, bmat, bmat_t, cmat_t, lam, pw, lamT, dskip)


def _glu_fwd(ys, w, b, *, name, tr=512):
    n, wd = ys.shape
    tr = _tile(n, tr, 8)

    def body(y_ref, w_ref, b_ref, o_ref):
        yb = _gelu(y_ref[...])
        z = jnp.dot(yb.astype(BF16), w_ref[...], preferred_element_type=F32) + b_ref[...]
        o_ref[...] = (yb * _sigmoid(z)).astype(BF16)

    row = pl.BlockSpec((tr, wd), lambda i: (i, 0))
    return pl.pallas_call(
        body, name=name, out_shape=jax.ShapeDtypeStruct((n, wd), BF16), grid=(n // tr,),
        in_specs=[row, pl.BlockSpec((wd, wd), lambda i: (0, 0)), pl.BlockSpec((1, wd), lambda i: (0, 0))],
        out_specs=row, compiler_params=_cparams("parallel"),
    )(ys, w, b.reshape(1, wd))


def _glu_bwd(ys, dyb2, w, w_t, b, *, name, tr=512):
    n, wd = ys.shape
    tr = _tile(n, tr, 8)

    def body(y_ref, d_ref, w_ref, wt_ref, b_ref, dys_ref, dz_ref, yb_ref, db_ref):
        i = pl.program_id(0)
        yv = y_ref[...]
        yb = _gelu(yv)
        ybb = yb.astype(BF16)
        sg = _sigmoid(jnp.dot(ybb, w_ref[...], preferred_element_type=F32) + b_ref[...])
        dv = d_ref[...].astype(F32)
        dz = dv * yb * sg * (1.0 - sg)
        dzb = dz.astype(BF16)
        dyb = dv * sg + jnp.dot(dzb, wt_ref[...], preferred_element_type=F32)
        dys_ref[...] = dyb * _gelu_grad(yv)
        dz_ref[...] = dzb
        yb_ref[...] = ybb
        part = jnp.sum(dz, axis=0, keepdims=True)

        @pl.when(i == 0)
        def _():
            db_ref[...] = part

        @pl.when(i > 0)
        def _():
            db_ref[...] += part

    row = pl.BlockSpec((tr, wd), lambda i: (i, 0))
    mat = pl.BlockSpec((wd, wd), lambda i: (0, 0))
    vec = pl.BlockSpec((1, wd), lambda i: (0, 0))
    return pl.pallas_call(
        body, name=name,
        out_shape=(jax.ShapeDtypeStruct((n, wd), F32), jax.ShapeDtypeStruct((n, wd), BF16),
                   jax.ShapeDtypeStruct((n, wd), BF16), jax.ShapeDtypeStruct((1, wd), F32)),
        grid=(n // tr,), in_specs=[row, row, mat, mat, vec], out_specs=(row, row, row, vec),
        compiler_params=_cparams("arbitrary"),
    )(ys, dyb2, w, w_t, b.reshape(1, wd))


def _merge_fwd(ya, yb2, wa_t, wb_t, proj, gate_blk, *, name, tr=512):
    n, aw = ya.shape
    d = wa_t.shape[0]
    tr = _tile(n, tr, 8)
    wa, wb = wa_t, wb_t

    def body(ya_ref, yb_ref, wa_ref, wb_ref, ga_ref, gb_ref, mix_ref, pa_ref, pb_ref):
        pa = lax.dot_general(ya_ref[...], wa_ref[...], NT, preferred_element_type=F32)
        pb = lax.dot_general(yb_ref[...], wb_ref[...], NT, preferred_element_type=F32)
        mix = _sigmoid(ga_ref[...].astype(F32)) * pa + _sigmoid(gb_ref[...].astype(F32)) * pb
        mix_ref[...] = mix.astype(BF16)
        pa_ref[...] = pa.astype(BF16)
        pb_ref[...] = pb.astype(BF16)

    row = lambda wdt: pl.BlockSpec((tr, wdt), lambda i: (i, 0))
    full = lambda r, c: pl.BlockSpec((r, c), lambda i: (0, 0))
    out = jax.ShapeDtypeStruct((n, d), BF16)
    return pl.pallas_call(
        body, name=name, out_shape=(out, out, out), grid=(n // tr,),
        in_specs=[row(aw), row(yb2.shape[1]), full(*wa.shape), full(*wb.shape),
                  pl.BlockSpec((tr, d), lambda i: (i, gate_blk)), pl.BlockSpec((tr, d), lambda i: (i, gate_blk + 1))],
        out_specs=(row(d), row(d), row(d)), compiler_params=_cparams("parallel"),
    )(ya, yb2, wa, wb, proj, proj)


def _merge_bwd(dx1, w_out, proj, pa, pb, gate_blk, dproj_cols, *, name, tr=512):
    n, d = dx1.shape
    tr = _tile(n, tr, 8)
    assert gate_blk % 2 == 0

    def body(dx_ref, w_ref, ga_ref, gb_ref, pa_ref, pb_ref, dpa_ref, dpb_ref, dg_ref):
        dm = lax.dot_general(dx_ref[...].astype(BF16), w_ref[...], NT, preferred_element_type=F32)
        sa = _sigmoid(ga_ref[...].astype(F32))
        sb = _sigmoid(gb_ref[...].astype(F32))
        dpa_ref[...] = (dm * sa).astype(BF16)
        dpb_ref[...] = (dm * sb).astype(BF16)
        dg_ref[:, :d] = (dm * pa_ref[...].astype(F32) * sa * (1.0 - sa)).astype(BF16)
        dg_ref[:, d:] = (dm * pb_ref[...].astype(F32) * sb * (1.0 - sb)).astype(BF16)

    row = pl.BlockSpec((tr, d), lambda i: (i, 0))
    out = jax.ShapeDtypeStruct((n, d), BF16)
    return pl.pallas_call(
        body, name=name, out_shape=(out, out, jax.ShapeDtypeStruct((n, dproj_cols), BF16)), grid=(n // tr,),
        in_specs=[row, pl.BlockSpec((d, d), lambda i: (0, 0)), pl.BlockSpec((tr, d), lambda i: (i, gate_blk)),
                  pl.BlockSpec((tr, d), lambda i: (i, gate_blk + 1)), row, row],
        out_specs=(row, row, pl.BlockSpec((tr, 2 * d), lambda i: (i, gate_blk // 2))),
        compiler_params=_cparams("parallel"),
    )(dx1, w_out, proj, proj, pa, pb)


def _outproj_fwd(mixed, w, x0, g, *, name, tr=512):
    n, d = x0.shape
    tr = _tile(n, tr, 8)

    def body(m_ref, w_ref, x_ref, g_ref, x1_ref, h_ref, r_ref):
        x1 = x_ref[...] + jnp.dot(m_ref[...], w_ref[...], preferred_element_type=F32)
        r = lax.rsqrt(jnp.mean(x1 * x1, axis=-1, keepdims=True) + RMS_EPS)
        x1_ref[...] = x1
        h_ref[...] = (x1 * r * g_ref[...]).astype(BF16)
        r_ref[...] = r

    row = pl.BlockSpec((tr, d), lambda i: (i, 0))
    return pl.pallas_call(
        body, name=name,
        out_shape=(jax.ShapeDtypeStruct((n, d), F32), jax.ShapeDtypeStruct((n, d), BF16),
                   jax.ShapeDtypeStruct((n, 1), F32)),
        grid=(n // tr,),
        in_specs=[row, pl.BlockSpec((d, d), lambda i: (0, 0)), row, pl.BlockSpec((1, d), lambda i: (0, 0))],
        out_specs=(row, row, pl.BlockSpec((tr, 1), lambda i: (i, 0))),
        compiler_params=_cparams("parallel"),
    )(mixed, w, x0, g.reshape(1, d))


def _adamw(w, g, m, v, *, name):
    shape = w.shape
    total = w.size
    if w.ndim == 3 and shape[1] % 8 == 0:
        lead, rows, cols = shape
    elif total % PACK_COLS == 0 and ((total // PACK_COLS) % 8 == 0 or total // PACK_COLS <= 512):
        lead, rows, cols = 1, total // PACK_COLS, PACK_COLS
    elif w.ndim >= 2:
        lead, rows, cols = 1, total // shape[-1], shape[-1]
    else:
        lead, rows, cols = 1, 1, total
    tr = _tile(rows, 512, 8)

    def body(w_ref, g_ref, m_ref, v_ref, d_ref, nm_ref, nv_ref):
        gv = g_ref[...]
        mn = ADAM_B1 * m_ref[...] + (1.0 - ADAM_B1) * gv
        vn = ADAM_B2 * v_ref[...] + (1.0 - ADAM_B2) * (gv * gv)
        m_hat = mn / (1.0 - ADAM_B1 ** ADAM_STEP)
        v_hat = vn / (1.0 - ADAM_B2 ** ADAM_STEP)
        d_ref[...] = -ADAM_LR * (m_hat / (jnp.sqrt(v_hat) + ADAM_EPS) + ADAM_WD * w_ref[...])
        nm_ref[...] = mn
        nv_ref[...] = vn

    blk = pl.BlockSpec((None, tr, cols), lambda l, i: (l, i, 0))
    out = jax.ShapeDtypeStruct((lead, rows, cols), F32)
    outs = pl.pallas_call(
        body, name=name, out_shape=(out, out, out), grid=(lead, rows // tr),
        in_specs=[blk] * 4, out_specs=(blk, blk, blk), compiler_params=_cparams("parallel", "parallel"),
    )(*[t.reshape(lead, rows, cols) for t in (w, g, m, v)])
    return tuple(o.reshape(shape) for o in outs)


def _all_gather(blocks, *, name):
    n = len(blocks)

    def body(*refs):
        x_refs, out_refs = refs[:n], refs[n:2 * n]
        send_sems, recv_sems, local_sems = refs[2 * n:]
        x, y, c = lax.axis_index("x"), lax.axis_index("y"), lax.axis_index("c")
        me, sibling = (x, y, c), (x, y, 1 - c)
        chips = [(1 - x, y), (x, 1 - y), (1 - x, 1 - y)]

        def slot(a, px, py, pc):
            return out_refs[a].at[4 * px + 2 * py + pc]

        def copy(a, k, block, to, src=None):
            return pltpu.make_async_remote_copy(
                src_ref=slot(a, *block) if src is None else src, dst_ref=slot(a, *block),
                send_sem=send_sems.at[7 * a + k], recv_sem=recv_sems.at[7 * a + k], device_id=to,
                device_id_type=MESH)

        started = []
        for a in range(n):
            mine = pltpu.make_async_copy(x_refs[a], slot(a, *me), local_sems.at[a])
            mine.start()
            started.append(mine)
        sends = []
        for a in range(n):
            first = [copy(a, 0, me, sibling, src=x_refs[a])]
            first += [copy(a, 1 + j, me, (*chip, c), src=x_refs[a]) for j, chip in enumerate(chips)]
            for cp in first:
                cp.start()
            sends += first
        for a in range(n):
            for j, chip in enumerate(chips):
                copy(a, 1 + j, (*chip, c), me).wait_recv()
                onward = copy(a, 4 + j, (*chip, c), sibling)
                onward.start()
                sends.append(onward)
        for a in range(n):
            copy(a, 0, sibling, me).wait_recv()
            for j, chip in enumerate(chips):
                copy(a, 4 + j, (*chip, 1 - c), me).wait_recv()
        for cp in sends:
            cp.wait_send()
        for mine in started:
            mine.wait()

    return pl.pallas_call(
        body, name=name, out_shape=[jax.ShapeDtypeStruct((N_DEV,) + b.shape, b.dtype) for b in blocks],
        in_specs=[HBM] * n, out_specs=[HBM] * n,
        scratch_shapes=[pltpu.SemaphoreType.DMA((7 * n,)), pltpu.SemaphoreType.DMA((7 * n,)),
                        pltpu.SemaphoreType.DMA((n,))],
    )(*blocks)


def _sum8(blocks, *, name, tr=SUM_ROWS):
    _, R, C = blocks.shape
    tr = _tile(R, tr, 16) if R % 16 == 0 else R

    def body(x_ref, o_ref):
        acc = x_ref[0].astype(F32)
        for i in range(1, N_DEV):
            acc = acc + x_ref[i].astype(F32)
        o_ref[...] = acc

    return pl.pallas_call(
        body, name=name, out_shape=jax.ShapeDtypeStruct((R, C), F32), grid=(R // tr,),
        in_specs=[pl.BlockSpec((N_DEV, tr, C), lambda i: (0, i, 0))],
        out_specs=pl.BlockSpec((tr, C), lambda i: (i, 0)), compiler_params=_cparams("parallel"),
    )(blocks)


def _pack(parts):
    flat = jnp.concatenate([p.astype(F32).reshape(-1) for p in parts])
    unit = 8 * PACK_COLS
    padded = -(-flat.size // unit) * unit
    return jnp.pad(flat, (0, padded - flat.size)).reshape(padded // PACK_COLS, PACK_COLS)


def _unpack(buf, like):
    flat, out, off = buf.reshape(-1), [], 0
    for p in like:
        out.append(flat[off:off + p.size].reshape(p.shape))
        off += p.size
    return out


def _ssm_discretise(lre, lim, logdt, bre, bim):
    lam = lax.complex(lre, lim)
    dt = jnp.exp(logdt)[:, None]
    lam_bar = jnp.exp(lam * dt)
    b_bar = ((lam_bar - 1.0) / lam)[:, :, None] * lax.complex(bre, bim)
    return lam_bar.real, lam_bar.imag, b_bar.real, b_bar.imag


def _block_diag(a, rows_first):
    ns, g, r, c = a.shape
    eye = jnp.eye(g, dtype=a.dtype)
    return jnp.einsum("sgrc,gh->sgrhc", a, eye).reshape(ns, g * r, g * c)


def _diag_blocks(m, r, c):
    ns = m.shape[0]
    g = SLAB_GROUPS
    return jnp.einsum("sgrhc,gh->sgrc", m.reshape(ns, g, r, g, c), jnp.eye(g, dtype=m.dtype))


def _to_tm(a, T):
    b, s, w = a.shape
    return a.reshape(b, s // T, T, w).transpose(0, 2, 1, 3)


def _from_tm(a):
    b, t, nc, w = a.shape
    return a.transpose(0, 2, 1, 3).reshape(b, nc * t, w)


WEIGHTS = ["norm_mix", "w_in", "b_forget", "ssm_lambda_re", "ssm_lambda_im", "ssm_log_dt", "ssm_b_re", "ssm_b_im",
           "ssm_c_re", "ssm_c_im", "ssm_d", "w_glu", "b_glu", "w_branch_a", "w_branch_b", "w_out", "norm_mlp",
           "w_mlp_up", "w_mlp_down", "norm_final"]
SHARDED = {"w_in": 2, "w_glu": 1, "w_branch_a": 2, "w_branch_b": 2, "w_out": 1, "w_mlp_up": 2, "w_mlp_down": 1}


REST = [n for n in SHARDED if n != "w_in"]
BY_COLUMNS = [n for n in SHARDED if SHARDED[n] == 2]


def _key(n):
    return n + "_t" if n in BY_COLUMNS else n


def _whole(seg):
    return seg.reshape(-1, seg.shape[-1])


def _blocks(g):
    return g.reshape(N_DEV, -1, g.shape[-1])


def _sum_blocks(n, got):
    return _sum8(got.reshape(N_DEV, -1, got.shape[-1]), name="sum_grads_" + n).reshape(got.shape[1:])


def kernel(x, norm_mix, w_in, b_forget, ssm_lambda_re, ssm_lambda_im, ssm_log_dt, ssm_b_re, ssm_b_im, ssm_c_re, ssm_c_im, ssm_d, w_glu, b_glu, w_branch_a, w_branch_b, w_out, norm_mlp, w_mlp_up, w_mlp_down, norm_final, loss_target, m_norm_mix, m_w_in, m_b_forget, m_ssm_lambda_re, m_ssm_lambda_im, m_ssm_log_dt, m_ssm_b_re, m_ssm_b_im, m_ssm_c_re, m_ssm_c_im, m_ssm_d, m_w_glu, m_b_glu, m_w_branch_a, m_w_branch_b, m_w_out, m_norm_mlp, m_w_mlp_up, m_w_mlp_down, m_norm_final, v_norm_mix, v_w_in, v_b_forget, v_ssm_lambda_re, v_ssm_lambda_im, v_ssm_log_dt, v_ssm_b_re, v_ssm_b_im, v_ssm_c_re, v_ssm_c_im, v_ssm_d, v_w_glu, v_b_glu, v_w_branch_a, v_w_branch_b, v_w_out, v_norm_mlp, v_w_mlp_up, v_w_mlp_down, v_norm_final):
    args = dict(locals())
    w = {n: args[n] for n in WEIGHTS}
    Bl, S, D = x.shape
    L, H = b_forget.shape
    G, P, C = ssm_b_re.shape[1:]
    AW, W, HP = H * HEAD_DIM, G * C, H // 2
    N = Bl * S
    T = SSM_CHUNK
    NS = G // SLAB_GROUPS
    SP = SLAB_GROUPS * P
    tq = min(ATTN_BLOCK, S)
    nq = S // tq
    u_off = 3 * AW
    gate_blk = (u_off + W) // D
    assert (u_off + W) % D == 0 and W % LANES == 0 and AW % LANES == 0 and S % T == 0

    tr_ = lambda a: jnp.swapaxes(a, 1, 2)
    shard = {n: tr_(w[n].astype(BF16)) if n in BY_COLUMNS else w[n].astype(BF16) for n in SHARDED}
    weights = [dict() for _ in range(L)]
    weights[0]["w_in_t"] = _whole(_all_gather([shard["w_in"][0]], name="gather_first")[0])

    ssm = []
    for l in range(L):
        disc, disc_vjp = jax.vjp(_ssm_discretise, ssm_lambda_re[l], ssm_lambda_im[l], ssm_log_dt[l],
                                 ssm_b_re[l], ssm_b_im[l])
        lbr, lbi, bbr, bbi = disc
        z = lax.complex(ssm_lambda_re[l], ssm_lambda_im[l]) * jnp.exp(ssm_log_dt[l])[:, None]
        powers = jnp.exp(z[None] * jnp.arange(1, T + 1, dtype=F32)[:, None, None])
        slabs = lambda a: a.reshape(NS, SP)
        lam = jnp.stack([slabs(lbr), slabs(lbi)], axis=1)
        lam_t = jnp.stack([slabs(powers[T - 1].real), slabs(powers[T - 1].imag)], axis=1)
        pw = jnp.stack([powers.real.reshape(T, NS, SP), powers.imag.reshape(T, NS, SP)], axis=0).transpose(2, 0, 1, 3)
        to_rows = lambda a: jnp.swapaxes(a.reshape(NS, SLAB_GROUPS, P, C), 2, 3)
        bmat = jnp.concatenate([_block_diag(to_rows(bbr), True), _block_diag(to_rows(bbi), True)], axis=2)
        cre = ssm_c_re[l].reshape(NS, SLAB_GROUPS, C, P)
        cim = ssm_c_im[l].reshape(NS, SLAB_GROUPS, C, P)
        cmat_t = jnp.concatenate([_block_diag(cre, True), -_block_diag(cim, True)], axis=2)
        ssm.append(dict(vjp=disc_vjp, lam=lam, lam_t=lam_t, pw=pw, bmat=bmat.astype(BF16),
                        bmat_t=tr_(bmat).astype(BF16), cmat=tr_(cmat_t).astype(BF16), cmat_t=cmat_t.astype(BF16),
                        d=ssm_d[l].reshape(1, W)))

    xcur = x.reshape(N, D)
    saved = []
    for l in range(L):
        s_, wl = ssm[l], weights[l]
        win_t = wl["w_in_t"]
        wl["wcat_t"] = jnp.concatenate([win_t[:3 * AW], win_t[3 * AW + H:]], axis=0)
        wl["wf_t"] = jnp.pad(win_t[3 * AW:3 * AW + H], ((0, LANES - H), (0, 0)))
        h, r0, fl = _rmsnorm_fwd(xcur, norm_mix[l], wl["wf_t"], name="norm_mix_fwd")
        proj = _mm(h, wl["wcat_t"], name="in_proj", tb=True, tm=WIDE_N, tn=WIDE_N)
        ft = fl[:, :H].reshape(Bl, S, H).transpose(0, 2, 1)
        F = _fox_gate_fwd(ft, b_forget[l], name="forget_gate_fwd")
        frow = F.reshape(Bl, HP, 2, nq, tq)
        proj3 = proj.reshape(Bl, S, -1)
        coming = [shard[n][l] for n in REST] + ([shard["w_in"][l + 1]] if l + 1 < L else [])
        ya, lse, got = _attn_fwd(proj3, frow, name="attn_fwd" if l + 1 < L else "attn_fwd_last", H=H, tq=tq,
                                 hosted=_Hosted(gather=coming))
        for n, seg in zip(REST, got):
            wl[_key(n)] = _whole(seg)
        if l + 1 < L:
            weights[l + 1]["w_in_t"] = _whole(got[-1])
        u_tm = _to_tm(proj3[:, :, u_off:u_off + W], T)
        ys = _from_tm(_ssm_fwd(u_tm, s_["bmat"], s_["cmat"], s_["lam"], s_["pw"], s_["lam_t"], s_["d"],
                               name="ssm_fwd")).reshape(N, W)
        yb2 = _glu_fwd(ys, wl["w_glu"], b_glu[l], name="glu_fwd")
        ya2 = ya.reshape(N, AW)
        mixed, pa, pb = _merge_fwd(ya2, yb2, wl["w_branch_a_t"], wl["w_branch_b_t"], proj, gate_blk,
                                   name="merge_fwd")
        x1, h2, r1 = _outproj_fwd(mixed, wl["w_out"], xcur, norm_mlp[l], name="out_proj")
        a = _mm(h2, wl["w_mlp_up_t"], name="mlp_up", tb=True, tm=WIDE_N, tn=WIDE_N)
        x2 = _mm(a, wl["w_mlp_down"], name="mlp_down", a_fn=_relu_sq, epi=lambda acc, res: acc + res,
                 extras=(x1,), out_dtype=F32, tk=LONG_K)
        saved.append(dict(x0=xcur, h=h, r0=r0, proj=proj, ft=ft, frow=frow, ya=ya, lse=lse, u_tm=u_tm,
                          ys=ys, yb2=yb2, mixed=mixed, pa=pa, pb=pb, x1=x1, h2=h2, r1=r1, a=a))
        xcur = x2

    dx, g_final, loss_row = _loss_head(xcur, norm_final, loss_target.reshape(N, D), name="loss_head")
    loss = lax.psum(loss_row[0, 0], MESH_AXES)

    big = {n: [None] * L for n in SHARDED}
    small = {n: [None] * L for n in WEIGHTS if n not in SHARDED and n != "norm_final"}
    small_sums = [None] * L
    win_grad_t = small_above = None
    for l in reversed(range(L)):
        sv, s_, wl = saved[l], ssm[l], weights[l]
        a = sv["a"]
        gw = {}
        d_a = _mm(dx, wl["w_mlp_down"], name="mlp_down_dx", tb=True, tn=WIDE_N,
                  epi=lambda acc, av: acc * (2.0 * jnp.maximum(av.astype(F32), 0.0)), extras=(a,))
        gw["w_mlp_down"] = _mm(a, dx, name="mlp_down_dw", ta=True, a_fn=_relu_sq, tk=LONG_K)
        gw["w_mlp_up"] = _mm(d_a, sv["h2"], name="mlp_up_dw", ta=True, tk=LONG_K)
        dx1, g = _mm_norm_bwd(d_a, wl["w_mlp_up_t"], sv["x1"], sv["r1"], norm_mlp[l], dx, name="mlp_up_dx_norm",
                              tk=LONG_K)
        small["norm_mlp"][l] = g[0]
        gw["w_out"] = _mm(sv["mixed"], dx1, name="out_proj_dw", ta=True, tk=LONG_K)
        ncat = wl["wcat_t"].shape[0]
        dpa, dpb, dproj = _merge_bwd(dx1, wl["w_out"], sv["proj"], sv["pa"], sv["pb"], gate_blk, ncat + LANES,
                                     name="merge_bwd")
        ya2 = sv["ya"].reshape(N, AW)
        gw["w_branch_a"] = _mm(dpa, ya2, name="branch_a_dw", ta=True, tk=LONG_K)
        dya = _mm(dpa, wl["w_branch_a_t"], name="branch_a_dx")
        gw["w_branch_b"] = _mm(dpb, sv["yb2"], name="branch_b_dw", ta=True, tk=LONG_K)
        dyb2 = _mm(dpb, wl["w_branch_b_t"], name="branch_b_dx")
        dys, dz, yb, g = _glu_bwd(sv["ys"], dyb2, wl["w_glu"], wl["w_glu"].T, b_glu[l], name="glu_bwd")
        small["b_glu"][l] = g[0]
        gw["w_glu"] = _mm(yb, dz, name="glu_dw", ta=True, tk=LONG_K)

        du_tm, g_bt, g_ct, g_lam, g_d = _ssm_bwd(
            sv["u_tm"], _to_tm(dys.reshape(Bl, S, W), T), s_["bmat"], s_["bmat_t"], s_["cmat_t"], s_["lam"],
            s_["pw"], s_["lam_t"], s_["d"], name="ssm_bwd")
        du = _from_tm(du_tm).reshape(N, W)
        g_b = _diag_blocks(jnp.swapaxes(g_bt, 1, 2).reshape(NS, LANES, 2, SP).transpose(2, 0, 1, 3).reshape(
            2 * NS, LANES, SP), C, P).reshape(2, G, C, P)
        g_bbar = jnp.swapaxes(g_b, 2, 3)
        g_c = _diag_blocks(g_ct.reshape(NS, LANES, 2, SP).transpose(2, 0, 1, 3).reshape(2 * NS, LANES, SP),
                           C, P).reshape(2, G, C, P)
        g_lbar = g_lam.transpose(1, 0, 2).reshape(2, G, P)
        g_lre, g_lim, g_ldt, g_bre, g_bim = s_["vjp"]((g_lbar[0], g_lbar[1], g_bbar[0], g_bbar[1]))
        small["ssm_lambda_re"][l], small["ssm_lambda_im"][l], small["ssm_log_dt"][l] = g_lre, g_lim, g_ldt
        small["ssm_b_re"][l], small["ssm_b_im"][l] = g_bre, g_bim
        small["ssm_c_re"][l], small["ssm_c_im"][l] = g_c[0], -g_c[1]
        small["ssm_d"][l] = g_d.reshape(W)

        proj3 = sv["proj"].reshape(Bl, S, -1)
        leaving = [_blocks(gw[n]) for n in REST] + ([_blocks(win_grad_t)] if l + 1 < L else [])
        (dq, dk, dv, dfk, dfq), got = _attn_bwd(
            proj3, sv["ya"], dya.reshape(Bl, S, AW), sv["lse"], sv["frow"],
            name="attn_bwd" if l + 1 < L else "attn_bwd_top", H=H, tq=tq,
            hosted=_Hosted(gather=[small_above] if l + 1 < L else [], exchange=leaving))
        if l + 1 < L:
            small_sums[l + 1] = _sum8(got[0], name="sum_small_grads")
            big["w_in"][l + 1] = _sum_blocks("w_in", got[-1]).T
            got = got[1:]
        for n, blocks in zip(REST, got):
            big[n][l] = _sum_blocks(n, blocks).T if n in BY_COLUMNS else _sum_blocks(n, blocks)
        dF = dfk.reshape(Bl, H, S) + dfq.transpose(0, 1, 3, 2).reshape(Bl, H, S)
        dft, g = _fox_gate_bwd(dF, sv["ft"], b_forget[l], name="forget_gate_bwd")
        small["b_forget"][l] = g[:, 0]
        dfl = jnp.pad(dft.transpose(0, 2, 1).reshape(N, H), ((0, 0), (0, LANES - H))).astype(BF16)
        for off, piece in ((0, dq.reshape(N, AW)), (AW, dk.reshape(N, AW)), (2 * AW, dv.reshape(N, AW)), (u_off, du),
                           (ncat, dfl)):
            dproj = lax.dynamic_update_slice(dproj, piece, (0, off))
        gcat_t = _mm(dproj, sv["h"], name="in_proj_dw", ta=True, tm=1408, tk=LONG_K)
        win_grad_t = jnp.concatenate([gcat_t[:3 * AW], gcat_t[ncat:ncat + H], gcat_t[3 * AW:ncat]], axis=0)
        wfull_t = jnp.concatenate([wl["wcat_t"], wl["wf_t"]], axis=0)
        if l > 0:
            dx, g = _mm_norm_bwd(dproj, wfull_t, sv["x0"], sv["r0"], norm_mix[l], dx1, name="in_proj_dx_norm",
                                 tk=1408)
        else:
            dh, got = _mm(dproj, wfull_t, name="in_proj_dx_bottom", out_dtype=F32, tk=1408,
                          hosted=_Hosted(exchange=[_blocks(win_grad_t)]))
            big["w_in"][0] = _sum_blocks("w_in", got[0]).T
            dx, g = _rmsnorm_bwd(dh, sv["x0"], sv["r0"], norm_mix[l], dx1, name="norm_mix_bwd")
        small["norm_mix"][l] = g[0]
        small_above = _pack([small[n][l] for n in small])

    last = [small[n][0] for n in small] + [g_final[0]]
    small_sums[0] = _sum8(_all_gather([_pack(last)], name="gather_small_grads")[0], name="sum_small_grads_last")
    grads = {n: jnp.stack(big[n]) for n in SHARDED}
    per_layer = [_unpack(small_sums[l], last if l == 0 else last[:-1]) for l in range(L)]
    for i, n in enumerate(small):
        grads[n] = jnp.stack([per_layer[l][i] for l in range(L)])
    grads["norm_final"] = per_layer[0][-1]

    deltas, new_m, new_v = {}, {}, {}
    for n in WEIGHTS:
        deltas[n], new_m[n], new_v[n] = _adamw(w[n], grads[n], args["m_" + n], args["v_" + n], name="adamw_" + n)
    return (loss, dx.reshape(Bl, S, D), *[grads[n] for n in WEIGHTS], *[deltas[n] for n in WEIGHTS],
            *[new_m[n] for n in WEIGHTS], *[new_v[n] for n in WEIGHTS])
```
